```python
import jax
import jax.numpy as jnp
from jax import lax
import numpy as np

D_MODEL = 2048
BATCH = 1
SEQ = 8192
DEPTH = 2

PLE_DIM = 256
RMS_EPS = 1e-6

MLSTM_HEADS = 4
MLSTM_HEAD_DIM = 256
MLSTM_W = MLSTM_HEADS * MLSTM_HEAD_DIM
MLSTM_CONV = 4
MLSTM_CHUNK = 64
GATE_SOFTCAP = 15.0

RWKV_HEADS = 8
RWKV_HEAD_DIM = 64
RWKV_W = RWKV_HEADS * RWKV_HEAD_DIM
DECAY_LORA = 96
AAA_LORA = 96
GATE_LORA = 256
RWKV_GN_EPS = 64e-5

S5_GROUP = 16
S5_GROUPS = 32
S5_W = S5_GROUP * S5_GROUPS
S5_STATE = 64

FFN_HIDDEN = ((8 * D_MODEL + 3 * 256 - 1) // (3 * 256)) * 256

M_IN = 4 * MLSTM_W + 2 * MLSTM_HEADS
RWKV_IN = 3 * RWKV_W + DECAY_LORA + AAA_LORA + GATE_LORA
N_BRANCH = 3
N_IN = M_IN + RWKV_IN + S5_W + N_BRANCH * D_MODEL
IN_SPLIT = (M_IN, M_IN + RWKV_IN, M_IN + RWKV_IN + S5_W)
RWKV_SPLIT = (RWKV_W, 2 * RWKV_W, 3 * RWKV_W, 3 * RWKV_W + DECAY_LORA, 3 * RWKV_W + DECAY_LORA + AAA_LORA)

kernel_name = 'hybrid_mlstm_rwkv7_s5_gated'


def _rmsnorm(x, g):
    xf = x.astype(jnp.float32)
    y = xf * lax.rsqrt(jnp.mean(xf * xf, axis=-1, keepdims=True) + RMS_EPS)
    return (y * g.astype(jnp.float32)).astype(x.dtype)


def _shift(x, n):
    return jnp.pad(x, ((0, 0), (n, 0), (0, 0)))[:, :x.shape[1]]


def _causal_dwconv(x, w):
    out = x * w[0]
    for j in range(1, w.shape[0]):
        out = out + w[j] * _shift(x, j)
    return out


def _token_shift(x, mu):
    return x + (_shift(x, 1) - x) * mu


def _softcap(z):
    return GATE_SOFTCAP * jnp.tanh(z / GATE_SOFTCAP)


def _mlstm(q, k, v, o, ig, fg, norm_g):
    B, T, _ = q.shape
    H, dh, L = MLSTM_HEADS, MLSTM_HEAD_DIM, MLSTM_CHUNK
    nc = T // L
    f32 = jnp.float32

    def to_chunks(z):
        return z.astype(f32).reshape(B, nc, L, H, dh).transpose(1, 0, 3, 2, 4)

    def gate_chunks(z):
        return z.reshape(B, nc, L, H).transpose(1, 0, 3, 2)

    qc = to_chunks(q) * (dh ** -0.5)
    kc = to_chunks(k)
    vc = to_chunks(v)
    ic = gate_chunks(_softcap(ig.astype(f32)))
    logf = jax.nn.log_sigmoid(_softcap(fg.astype(f32)))
    bc = jnp.cumsum(gate_chunks(logf), axis=-1)
    causal = jnp.tril(jnp.ones((L, L), dtype=bool))

    def step(carry, inp):
        C, n, m = carry
        qb, kb, vb, ib, bb = inp
        dmat = bb[..., :, None] - bb[..., None, :] + ib[..., None, :]
        dmat = jnp.where(causal, dmat, -jnp.inf)
        inter = bb + m[..., None]
        m_t = jnp.maximum(inter, jnp.max(dmat, axis=-1))
        s = jnp.einsum('bhtd,bhsd->bhts', qb, kb) * jnp.exp(dmat - m_t[..., None])
        inter_w = jnp.exp(inter - m_t)
        num = jnp.einsum('bhts,bhsd->bhtd', s, vb) + inter_w[..., None] * jnp.einsum('bhtk,bhvk->bhtv', qb, C)
        den = jnp.sum(s, axis=-1) + inter_w * jnp.einsum('bhtk,bhk->bht', qb, n)
        h = num / jnp.maximum(jnp.abs(den), jnp.exp(-m_t))[..., None]
        b_end = bb[..., -1]
        wlog = b_end[..., None] - bb + ib
        m_new = jnp.maximum(b_end + m, jnp.max(wlog, axis=-1))
        decay = jnp.exp(b_end + m - m_new)
        ws = jnp.exp(wlog - m_new[..., None])
        C_new = decay[..., None, None] * C + jnp.einsum('bhs,bhsv,bhsk->bhvk', ws, vb, kb)
        n_new = decay[..., None] * n + jnp.einsum('bhs,bhsk->bhk', ws, kb)
        return (C_new, n_new, m_new), h

    carry0 = (jnp.zeros((B, H, dh, dh), f32), jnp.zeros((B, H, dh), f32), jnp.zeros((B, H), f32))
    _, h = lax.scan(step, carry0, (qc, kc, vc, ic, bc))
    h = h.transpose(1, 0, 3, 2, 4).reshape(B, T, H, dh)
    h = h * lax.rsqrt(jnp.mean(h * h, axis=-1, keepdims=True) + RMS_EPS)
    h = h.reshape(B, T, MLSTM_W) * norm_g.astype(f32)
    return jax.nn.sigmoid(o.astype(f32)) * h


def _rwkv7(r, k, v, wl, al, gl, w0, w2, a0, a2, g2, k_k, k_a, r_k, ln_g, ln_b):
    B, T, _ = r.shape
    H, dh = RWKV_HEADS, RWKV_HEAD_DIM
    f32 = jnp.float32
    r, k, v = r.astype(f32), k.astype(f32), v.astype(f32)
    w = -jax.nn.softplus(-(w0 + jnp.tanh(wl.astype(f32)) @ w2)) - 0.5
    decay = jnp.exp(-jnp.exp(w))
    a = jax.nn.sigmoid(a0 + al.astype(f32) @ a2)
    g = jax.nn.sigmoid(gl.astype(f32)) @ g2
    kk = (k * k_k).reshape(B, T, H, dh)
    kk = kk / jnp.maximum(jnp.sqrt(jnp.sum(kk * kk, axis=-1, keepdims=True)), 1e-12)
    k = k * (1.0 + (a - 1.0) * k_a)

    def heads(z):
        return z.reshape(B, T, H, dh).transpose(1, 0, 2, 3)

    kk_t = kk.transpose(1, 0, 2, 3)

    def step(S, inp):
        r_t, w_t, k_t, v_t, kk_s, a_t = inp
        sa = jnp.einsum('bhvk,bhk->bhv', S, -kk_s)
        S = (S * w_t[:, :, None, :] + sa[..., None] * (kk_s * a_t)[:, :, None, :]
             + v_t[..., None] * k_t[:, :, None, :])
        return S, jnp.einsum('bhvk,bhk->bhv', S, r_t)

    S0 = jnp.zeros((B, H, dh, dh), f32)
    _, y = lax.scan(step, S0, (heads(r), heads(decay), heads(k), heads(v), kk_t, heads(a)))
    y = y.transpose(1, 0, 2, 3)
    mu = jnp.mean(y, axis=-1, keepdims=True)
    var = jnp.mean((y - mu) ** 2, axis=-1, keepdims=True)
    y = ((y - mu) * lax.rsqrt(var + RWKV_GN_EPS)).reshape(B, T, RWKV_W) * ln_g + ln_b
    bonus = jnp.sum((r * k * r_k).reshape(B, T, H, dh), axis=-1, keepdims=True) * v.reshape(B, T, H, dh)
    y = y + bonus.reshape(B, T, RWKV_W)
    return y * g


def _s5(u, a_re, a_im, log_dt, b_re, b_im, c_re, c_im, d, glu_w, glu_b):
    B, T, _ = u.shape
    G, P = S5_GROUPS, S5_GROUP
    uf = u.astype(jnp.float32).reshape(B, T, G, P)
    dt = jnp.exp(log_dt)[:, None]
    mag = jnp.exp(a_re * dt)
    ang = a_im * dt
    abar_re, abar_im = mag * jnp.cos(ang), mag * jnp.sin(ang)
    den = a_re * a_re + a_im * a_im
    nr, ni = abar_re - 1.0, abar_im
    coef_re = (nr * a_re + ni * a_im) / den
    coef_im = (ni * a_re - nr * a_im) / den
    bu_re = jnp.einsum('btgp,gnp->btgn', uf, b_re)
    bu_im = jnp.einsum('btgp,gnp->btgn', uf, b_im)
    x_re = coef_re * bu_re - coef_im * bu_im
    x_im = coef_re * bu_im + coef_im * bu_re
    ar = jnp.broadcast_to(abar_re, x_re.shape)
    ai = jnp.broadcast_to(abar_im, x_re.shape)

    def combine(e1, e2):
        a1r, a1i, b1r, b1i = e1
        a2r, a2i, b2r, b2i = e2
        return (a2r * a1r - a2i * a1i, a2r * a1i + a2i * a1r,
                a2r * b1r - a2i * b1i + b2r, a2r * b1i + a2i * b1r + b2i)

    _, _, s_re, s_im = lax.associative_scan(combine, (ar, ai, x_re, x_im), axis=1)
    y = (jnp.einsum('btgn,gpn->btgp', s_re, c_re) - jnp.einsum('btgn,gpn->btgp', s_im, c_im)
         + d.reshape(G, P) * uf)
    y = jax.nn.gelu(y.reshape(B, T, S5_W))
    return y * jax.nn.sigmoid(y @ glu_w + glu_b)


def setup_inputs(seed: int = 0) -> dict:
    key = jax.random.key(seed)
    ks = iter(jax.random.split(key, 48))
    L = DEPTH
    f32 = jnp.float32

    def nrm(shape, fan_in, scale=1.0):
        return jax.random.normal(next(ks), shape, f32) * (scale * fan_in ** -0.5)

    def gain(shape, center=1.0):
        return center + 0.02 * jax.random.normal(next(ks), shape, f32)

    def unif(shape, lo, hi):
        return jax.random.uniform(next(ks), shape, f32, minval=lo, maxval=hi)

    inputs = {
        'x': jax.random.normal(next(ks), (BATCH, SEQ, D_MODEL), f32),
        'p': jax.random.normal(next(ks), (DEPTH, BATCH, SEQ, PLE_DIM), f32),
        'norm_mix_g': gain((L, D_MODEL)),
        'w_in': nrm((L, D_MODEL, N_IN), D_MODEL),
        'mlstm_conv': nrm((L, MLSTM_CONV, 2 * MLSTM_W), MLSTM_CONV),
        'mlstm_ib': -2.0 + 0.5 * jax.random.normal(next(ks), (L, MLSTM_HEADS), f32),
        'mlstm_fb': unif((L, MLSTM_HEADS), 3.0, 6.0),
        'mlstm_norm_g': gain((L, MLSTM_W)),
        'rwkv_mu': unif((L, RWKV_IN), 0.0, 1.0),
        'rwkv_w0': unif((L, RWKV_W), -6.0, -1.0),
        'rwkv_w2': nrm((L, DECAY_LORA, RWKV_W), DECAY_LORA, 0.1),
        'rwkv_a0': 0.1 * jax.random.normal(next(ks), (L, RWKV_W), f32),
        'rwkv_a2': nrm((L, AAA_LORA, RWKV_W), AAA_LORA, 0.1),
        'rwkv_g2': nrm((L, GATE_LORA, RWKV_W), GATE_LORA),
        'rwkv_kk': gain((L, RWKV_W), 0.85),
        'rwkv_ka': gain((L, RWKV_W)),
        'rwkv_rk': 0.1 * jax.random.normal(next(ks), (L, RWKV_W), f32),
        'rwkv_ln_g': gain((L, RWKV_W)),
        'rwkv_ln_b': 0.01 * jax.random.normal(next(ks), (L, RWKV_W), f32),
        's5_a_re': -0.5 + 0.01 * jax.random.normal(next(ks), (L, S5_GROUPS, S5_STATE), f32),
        's5_a_im': (jnp.pi * jnp.arange(S5_STATE, dtype=f32)
                    + 0.01 * jax.random.normal(next(ks), (L, S5_GROUPS, S5_STATE), f32)),
        's5_log_dt': unif((L, S5_GROUPS), float(np.log(1e-3)), float(np.log(1e-1))),
        's5_b_re': nrm((L, S5_GROUPS, S5_STATE, S5_GROUP), 2 * S5_GROUP),
        's5_b_im': nrm((L, S5_GROUPS, S5_STATE, S5_GROUP), 2 * S5_GROUP),
        's5_c_re': nrm((L, S5_GROUPS, S5_GROUP, S5_STATE), 2 * S5_STATE),
        's5_c_im': nrm((L, S5_GROUPS, S5_GROUP, S5_STATE), 2 * S5_STATE),
        's5_d': jax.random.normal(next(ks), (L, S5_W), f32),
        's5_glu_w': nrm((L, S5_W, S5_W), S5_W),
        's5_glu_b': 0.01 * jax.random.normal(next(ks), (L, S5_W), f32),
        'w_up_m': nrm((L, MLSTM_W, D_MODEL), MLSTM_W),
        'w_up_r': nrm((L, RWKV_W, D_MODEL), RWKV_W),
        'w_up_s': nrm((L, S5_W, D_MODEL), S5_W),
        'w_out': nrm((L, D_MODEL, D_MODEL), D_MODEL),
        'norm_ffn_g': gain((L, D_MODEL)),
        'ffn_w_gate': nrm((L, D_MODEL, FFN_HIDDEN), D_MODEL),
        'ffn_w_up': nrm((L, D_MODEL, FFN_HIDDEN), D_MODEL),
        'ffn_w_down': nrm((L, FFN_HIDDEN, D_MODEL), FFN_HIDDEN),
        'norm_ple_g': gain((L, D_MODEL)),
        'ple_w_gate': nrm((L, D_MODEL, D_MODEL), D_MODEL),
        'ple_w_proj': nrm((L, PLE_DIM, D_MODEL), PLE_DIM),
        'final_norm_g': gain((D_MODEL,)),
    }
    return inputs


def reference(x, p, norm_mix_g, w_in, mlstm_conv, mlstm_ib, mlstm_fb, mlstm_norm_g,
              rwkv_mu, rwkv_w0, rwkv_w2, rwkv_a0, rwkv_a2, rwkv_g2, rwkv_kk, rwkv_ka, rwkv_rk,
              rwkv_ln_g, rwkv_ln_b, s5_a_re, s5_a_im, s5_log_dt, s5_b_re, s5_b_im, s5_c_re,
              s5_c_im, s5_d, s5_glu_w, s5_glu_b, w_up_m, w_up_r, w_up_s, w_out, norm_ffn_g,
              ffn_w_gate, ffn_w_up, ffn_w_down, norm_ple_g, ple_w_gate, ple_w_proj, final_norm_g):
    h = x
    dt = x.dtype
    for i in range(DEPTH):
        xn = _rmsnorm(h, norm_mix_g[i])
        z = xn @ w_in[i]
        zm, zr, zs, zg = jnp.split(z, IN_SPLIT, axis=-1)

        qk = jax.nn.silu(_causal_dwconv(zm[..., :2 * MLSTM_W], mlstm_conv[i]))
        q, k = jnp.split(qk, 2, axis=-1)
        v = zm[..., 2 * MLSTM_W:3 * MLSTM_W]
        o = zm[..., 3 * MLSTM_W:4 * MLSTM_W]
        ig = zm[..., 4 * MLSTM_W:4 * MLSTM_W + MLSTM_HEADS] + mlstm_ib[i]
        fg = zm[..., 4 * MLSTM_W + MLSTM_HEADS:] + mlstm_fb[i]
        y_m = _mlstm(q, k, v, o, ig, fg, mlstm_norm_g[i])

        zr = _token_shift(zr, rwkv_mu[i])
        r, kr, vr, wl, al, gl = jnp.split(zr, RWKV_SPLIT, axis=-1)
        y_r = _rwkv7(r, kr, vr, wl, al, gl, rwkv_w0[i], rwkv_w2[i], rwkv_a0[i], rwkv_a2[i],
                     rwkv_g2[i], rwkv_kk[i], rwkv_ka[i], rwkv_rk[i], rwkv_ln_g[i], rwkv_ln_b[i])

        y_s = _s5(zs, s5_a_re[i], s5_a_im[i], s5_log_dt[i], s5_b_re[i], s5_b_im[i],
                  s5_c_re[i], s5_c_im[i], s5_d[i], s5_glu_w[i], s5_glu_b[i])

        g_m, g_r, g_s = jnp.split(jax.nn.sigmoid(zg), N_BRANCH, axis=-1)
        mixed = (g_m * (y_m.astype(dt) @ w_up_m[i]) + g_r * (y_r.astype(dt) @ w_up_r[i])
                 + g_s * (y_s.astype(dt) @ w_up_s[i]))
        h = h + mixed @ w_out[i]

        hn = _rmsnorm(h, norm_ffn_g[i])
        h = h + (jax.nn.silu(hn @ ffn_w_gate[i]) * (hn @ ffn_w_up[i])) @ ffn_w_down[i]

        hp = _rmsnorm(h, norm_ple_g[i])
        h = h + (p[i] @ ple_w_proj[i]) * jax.nn.sigmoid(hp @ ple_w_gate[i])
    return _rmsnorm(h, final_norm_g)
```

```python
import functools

import jax
import jax.numpy as jnp
import numpy as np
from jax import lax
from jax.experimental import pallas as pl
from jax.experimental.pallas import tpu as pltpu

f32 = jnp.float32
bf16 = jnp.bfloat16
HI = lax.Precision.HIGHEST

D_MODEL = 2048
PLE_DIM = 256
RMS_EPS = 1e-6
MLSTM_HEADS = 4
MLSTM_HEAD_DIM = 256
MLSTM_W = 1024
MLSTM_CHUNK = 256
GATE_SOFTCAP = 15.0
RWKV_HEADS = 8
RWKV_HEAD_DIM = 64
RWKV_W = 512
DECAY_LORA = 96
AAA_LORA = 96
GATE_LORA = 256
LORA_PAD = 128
RWKV_GN_EPS = 64e-5
RWKV_CHUNK = 32
S5_GROUP = 16
S5_GROUPS = 32
S5_W = 512
S5_STATE = 64
S5_CHUNK = 16
FFN_HIDDEN = 5632
M_IN = 4 * MLSTM_W + 2 * MLSTM_HEADS
RWKV_IN = 3 * RWKV_W + DECAY_LORA + AAA_LORA + GATE_LORA
RWKV_PAD = 3 * RWKV_W + 2 * LORA_PAD + GATE_LORA
Z_MAIN = 4 * MLSTM_W + RWKV_PAD + 3 * D_MODEL
Z_TAIL = S5_W + 128
VMEM_LIMIT = 56 * 1024 * 1024


def _cp(sem):
    return pltpu.CompilerParams(dimension_semantics=sem, vmem_limit_bytes=VMEM_LIMIT)


def _nt(a, b, **kw):
    return lax.dot_general(a, b, (((1,), (1,)), ((), ())), preferred_element_type=f32, **kw)


def _tn(a, b, **kw):
    return lax.dot_general(a, b, (((0,), (0,)), ((), ())), preferred_element_type=f32, **kw)


def _mm(a, b, **kw):
    return jnp.dot(a, b, preferred_element_type=f32, **kw)


def _bmm(a, b):
    return jnp.dot(a.astype(bf16), b.astype(bf16), preferred_element_type=f32)


def _sigmoid(x):
    return 1.0 / (1.0 + jnp.exp(-x))


def _rms(x, g):
    return x * lax.rsqrt(jnp.mean(x * x, axis=-1, keepdims=True) + RMS_EPS) * g


def _norm_kernel(x_ref, g_ref, o_ref):
    o_ref[...] = _rms(x_ref[...], g_ref[...]).astype(o_ref.dtype)


def _rmsnorm(x, g, out_dtype):
    T, D = x.shape
    tm = min(512, T)
    return pl.pallas_call(
        _norm_kernel,
        grid=(T // tm,),
        in_specs=[pl.BlockSpec((tm, D), lambda i: (i, 0)), pl.BlockSpec((1, D), lambda i: (0, 0))],
        out_specs=pl.BlockSpec((tm, D), lambda i: (i, 0)),
        out_shape=jax.ShapeDtypeStruct((T, D), out_dtype),
        compiler_params=_cp(("parallel",)),
        name="rmsnorm",
    )(x, g.reshape(1, D))


def _proj_kernel(a_ref, w_ref, o_ref):
    o_ref[...] = _mm(a_ref[...], w_ref[...]).astype(o_ref.dtype)


def _proj(a, w, tn, out_dtype=f32):
    T, K = a.shape
    N = w.shape[1]
    tm = min(1024, T)
    return pl.pallas_call(
        _proj_kernel,
        grid=(N // tn, T // tm),
        in_specs=[pl.BlockSpec((tm, K), lambda j, i: (i, 0)), pl.BlockSpec((K, tn), lambda j, i: (0, j))],
        out_specs=pl.BlockSpec((tm, tn), lambda j, i: (i, j)),
        out_shape=jax.ShapeDtypeStruct((T, N), out_dtype),
        compiler_params=_cp(("parallel", "parallel")),
        name="in_proj",
    )(a, w)


def _merge_kernel(ym_ref, yr_ref, ys_ref, gm_ref, gr_ref, gs_ref, um_ref, ur_ref, us_ref, o_ref):
    acc = _sigmoid(gm_ref[...]) * _mm(ym_ref[...], um_ref[...])
    acc += _sigmoid(gr_ref[...]) * _mm(yr_ref[...], ur_ref[...])
    acc += _sigmoid(gs_ref[...]) * _mm(ys_ref[...], us_ref[...])
    o_ref[...] = acc.astype(o_ref.dtype)


def _merge(ym, yr, ys, z_main, um, ur, us):
    T = ym.shape[0]
    tm, tn = min(512, T), 1024
    nj = D_MODEL // tn
    g0 = (4 * MLSTM_W + RWKV_PAD) // tn

    def gate_spec(b):
        return pl.BlockSpec((tm, tn), lambda j, i, b=b: (i, g0 + b * nj + j))

    def y_spec(w):
        return pl.BlockSpec((tm, w), lambda j, i: (i, 0))

    def u_spec(w):
        return pl.BlockSpec((w, tn), lambda j, i: (0, j))

    return pl.pallas_call(
        _merge_kernel,
        grid=(nj, T // tm),
        in_specs=[y_spec(MLSTM_W), y_spec(RWKV_W), y_spec(S5_W), gate_spec(0), gate_spec(1), gate_spec(2),
                  u_spec(MLSTM_W), u_spec(RWKV_W), u_spec(S5_W)],
        out_specs=pl.BlockSpec((tm, tn), lambda j, i: (i, j)),
        out_shape=jax.ShapeDtypeStruct((T, D_MODEL), bf16),
        compiler_params=_cp(("parallel", "parallel")),
        name="gated_merge",
    )(ym, yr, ys, z_main, z_main, z_main, um, ur, us)


def _resid_norm_kernel(a_ref, w_ref, h_ref, g_ref, ho_ref, no_ref, acc_ref):
    k = pl.program_id(1)

    @pl.when(k == 0)
    def _():
        acc_ref[...] = h_ref[...]

    acc_ref[...] += _mm(a_ref[...], w_ref[...])

    @pl.when(k == pl.num_programs(1) - 1)
    def _():
        hn = acc_ref[...]
        ho_ref[...] = hn
        no_ref[...] = _rms(hn, g_ref[...]).astype(no_ref.dtype)


def _resid_norm(a, w, h, g, tk, norm_dtype=bf16):
    T, K = a.shape
    tm = min(512, T)
    return pl.pallas_call(
        _resid_norm_kernel,
        grid=(T // tm, K // tk),
        in_specs=[pl.BlockSpec((tm, tk), lambda i, k: (i, k)), pl.BlockSpec((tk, D_MODEL), lambda i, k: (k, 0)),
                  pl.BlockSpec((tm, D_MODEL), lambda i, k: (i, 0)), pl.BlockSpec((1, D_MODEL), lambda i, k: (0, 0))],
        out_specs=[pl.BlockSpec((tm, D_MODEL), lambda i, k: (i, 0)), pl.BlockSpec((tm, D_MODEL), lambda i, k: (i, 0))],
        out_shape=[jax.ShapeDtypeStruct((T, D_MODEL), f32), jax.ShapeDtypeStruct((T, D_MODEL), norm_dtype)],
        scratch_shapes=[pltpu.VMEM((tm, D_MODEL), f32)],
        compiler_params=_cp(("parallel", "arbitrary")),
        name="resid_matmul_norm",
    )(a, w, h, g.reshape(1, D_MODEL))


def _ffn_up_kernel(a_ref, wg_ref, wu_ref, o_ref):
    a = a_ref[...]
    gt = _mm(a, wg_ref[...])
    up = _mm(a, wu_ref[...])
    o_ref[...] = (gt * _sigmoid(gt) * up).astype(o_ref.dtype)


def _ffn_up(a, wg, wu):
    T, K = a.shape
    N = wg.shape[1]
    tm, tn = min(1024, T), 512
    return pl.pallas_call(
        _ffn_up_kernel,
        grid=(N // tn, T // tm),
        in_specs=[pl.BlockSpec((tm, K), lambda j, i: (i, 0)), pl.BlockSpec((K, tn), lambda j, i: (0, j)),
                  pl.BlockSpec((K, tn), lambda j, i: (0, j))],
        out_specs=pl.BlockSpec((tm, tn), lambda j, i: (i, j)),
        out_shape=jax.ShapeDtypeStruct((T, N), bf16),
        compiler_params=_cp(("parallel", "parallel")),
        name="ffn_up",
    )(a, wg, wu)


def _ple_kernel(hp_ref, p_ref, wg_ref, wp_ref, h_ref, g_ref, ho_ref, no_ref):
    gate = _sigmoid(_mm(hp_ref[...], wg_ref[...]))
    hn = h_ref[...] + _mm(p_ref[...], wp_ref[...]) * gate
    ho_ref[...] = hn
    no_ref[...] = _rms(hn, g_ref[...]).astype(no_ref.dtype)


def _ple(hp, p, wg, wp, h, g, norm_dtype):
    T = h.shape[0]
    tm = min(512, T)
    row = lambda w: pl.BlockSpec((tm, w), lambda i: (i, 0))
    full = lambda s: pl.BlockSpec(s, lambda i: (0, 0))
    return pl.pallas_call(
        _ple_kernel,
        grid=(T // tm,),
        in_specs=[row(D_MODEL), row(PLE_DIM), full((D_MODEL, D_MODEL)), full((PLE_DIM, D_MODEL)), row(D_MODEL),
                  full((1, D_MODEL))],
        out_specs=[row(D_MODEL), row(D_MODEL)],
        out_shape=[jax.ShapeDtypeStruct((T, D_MODEL), f32), jax.ShapeDtypeStruct((T, D_MODEL), norm_dtype)],
        compiler_params=_cp(("parallel",)),
        name="ple_norm",
    )(hp, p, wg, wp, h, g.reshape(1, D_MODEL))


def _mlstm_kernel(zq_ref, zk_ref, v_ref, o_ref, gz_ref, cq_ref, ck_ref, gb_ref, ng_ref, y_ref,
                  ct_ref, n_ref, qbuf_ref, kbuf_ref):
    h = pl.program_id(0)
    c = pl.program_id(1)
    L, dh = zq_ref.shape

    @pl.when(c == 0)
    def _():
        ct_ref[...] = jnp.zeros_like(ct_ref)
        n_ref[...] = jnp.zeros_like(n_ref)
        qbuf_ref[0:8, :] = jnp.zeros((8, dh), f32)
        kbuf_ref[0:8, :] = jnp.zeros((8, dh), f32)

    def conv_silu(z_ref, w_ref, buf_ref):
        x = z_ref[...]
        buf_ref[8:, :] = x
        w = w_ref[...]
        acc = x * w[0:1, :]
        for j in range(1, w.shape[0]):
            acc = acc + buf_ref[pl.ds(8 - j, L), :] * w[j:j + 1, :]
        buf_ref[0:8, :] = x[L - 8:, :]
        return acc * _sigmoid(acc)

    q = conv_silu(zq_ref, cq_ref, qbuf_ref) * (dh ** -0.5)
    k = conv_silu(zk_ref, ck_ref, kbuf_ref)
    v = v_ref[...]

    g = gz_ref[...] + gb_ref[...]
    sc = GATE_SOFTCAP * jnp.tanh(g / GATE_SOFTCAP)
    logf = -jnp.log(1.0 + jnp.exp(-sc))
    row = lax.broadcasted_iota(jnp.int32, (L, L), 0)
    col = lax.broadcasted_iota(jnp.int32, (L, L), 1)
    causal = row >= col
    bcs = _mm(causal.astype(f32), logf, precision=HI)
    lane = lax.broadcasted_iota(jnp.int32, g.shape, 1)
    b_sel = jnp.where(lane == MLSTM_HEADS + h, bcs, 0.0)
    e_sel = jnp.where(lane == h, sc, 0.0) - b_sel
    b_col = jnp.sum(b_sel, axis=1, keepdims=True)
    e_col = jnp.sum(e_sel, axis=1, keepdims=True)
    e_row = _nt(jnp.ones_like(e_sel), e_sel, precision=HI)
    wmat = jnp.where(causal, jnp.exp(jnp.where(causal, b_col + e_row, 0.0)), 0.0)
    b_end = b_col[L - 1:L, :]
    eb = jnp.exp(b_col)

    qb = q.astype(bf16)
    kb = k.astype(bf16)
    vb = v.astype(bf16)
    s = _nt(qb, kb) * wmat
    ct = ct_ref[...]
    n = n_ref[...]
    num = _mm(s.astype(bf16), vb) + eb * _mm(qb, ct.astype(bf16))
    den = jnp.sum(s, axis=1, keepdims=True) + eb * jnp.sum(q * n, axis=1, keepdims=True)
    hh = num / jnp.maximum(jnp.abs(den), 1.0)
    hh = hh * lax.rsqrt(jnp.mean(hh * hh, axis=1, keepdims=True) + RMS_EPS) * ng_ref[...]
    y_ref[...] = (_sigmoid(o_ref[...]) * hh).astype(y_ref.dtype)

    kw = k * jnp.exp(b_end + e_col)
    decay = jnp.exp(b_end)
    ct_ref[...] = decay * ct + _tn(kw.astype(bf16), vb)
    n_ref[...] = decay * n + jnp.sum(kw, axis=0, keepdims=True)


def _mlstm(z_main, z_tail, conv_w, gate_b, norm_g):
    T = z_main.shape[0]
    L, dh, H = min(MLSTM_CHUNK, T), MLSTM_HEAD_DIM, MLSTM_HEADS
    blk = lambda off: pl.BlockSpec((L, dh), lambda h, c, off=off: (c, off + h))
    return pl.pallas_call(
        _mlstm_kernel,
        grid=(H, T // L),
        in_specs=[blk(0), blk(H), blk(2 * H), blk(3 * H),
                  pl.BlockSpec((L, 128), lambda h, c: (c, S5_W // 128)),
                  pl.BlockSpec((4, dh), lambda h, c: (0, h)), pl.BlockSpec((4, dh), lambda h, c: (0, H + h)),
                  pl.BlockSpec((1, 128), lambda h, c: (0, 0)), pl.BlockSpec((1, dh), lambda h, c: (0, h))],
        out_specs=pl.BlockSpec((L, dh), lambda h, c: (c, h)),
        out_shape=jax.ShapeDtypeStruct((T, MLSTM_W), bf16),
        scratch_shapes=[pltpu.VMEM((dh, dh), f32), pltpu.VMEM((1, dh), f32),
                        pltpu.VMEM((L + 8, dh), f32), pltpu.VMEM((L + 8, dh), f32)],
        compiler_params=_cp(("arbitrary", "arbitrary")),
        name="mlstm",
    )(z_main, z_main, z_main, z_main, z_tail, conv_w, conv_w, gate_b, norm_g.reshape(1, MLSTM_W))


def _iota_div(shape, dim, width):
    return lax.shift_right_logical(lax.broadcasted_iota(jnp.int32, shape, dim), int(np.log2(width)))


def _head_ones(n, width):
    return (_iota_div((n, n), 0, width) == _iota_div((n, n), 1, width)).astype(f32)


def _rwkv_prep_kernel(z_ref, mu_ref, w0_ref, a0_ref, kkg_ref, ka_ref, rk_ref, w2_ref, a2_ref, g2_ref,
                      r_ref, ld_ref, k_ref, v_ref, kk_ref, b_ref, g_ref, bonus_ref, buf_ref):
    i = pl.program_id(0)
    tm = z_ref.shape[0]
    W = RWKV_W

    @pl.when(i == 0)
    def _():
        buf_ref[0:8, :] = jnp.zeros((8, buf_ref.shape[1]), f32)

    x = z_ref[...]
    buf_ref[8:, :] = x
    xs = x + (buf_ref[pl.ds(7, tm), :] - x) * mu_ref[...]
    buf_ref[0:8, :] = x[tm - 8:, :]
    r = xs[:, 0:W]
    k = xs[:, W:2 * W]
    v = xs[:, 2 * W:3 * W]
    wl = xs[:, 3 * W:3 * W + LORA_PAD]
    al = xs[:, 3 * W + LORA_PAD:3 * W + 2 * LORA_PAD]
    gl = xs[:, 3 * W + 2 * LORA_PAD:]
    t = w0_ref[...] + _bmm(jnp.tanh(wl), w2_ref[...])
    w = -(jnp.maximum(-t, 0.0) + jnp.log(1.0 + jnp.exp(-jnp.abs(t)))) - 0.5
    a = _sigmoid(a0_ref[...] + _bmm(al, a2_ref[...]))
    g = _bmm(_sigmoid(gl), g2_ref[...])
    ones = _head_ones(W, RWKV_HEAD_DIM)
    kk = k * kkg_ref[...]
    ss = _mm(kk * kk, ones, precision=HI)
    kk = kk / jnp.maximum(jnp.sqrt(ss), 1e-12)
    k2 = k * (1.0 + (a - 1.0) * ka_ref[...])
    bonus = _mm(r * k2 * rk_ref[...], ones, precision=HI) * v
    r_ref[...] = r
    ld_ref[...] = -jnp.exp(w)
    k_ref[...] = k2
    v_ref[...] = v
    kk_ref[...] = kk
    b_ref[...] = kk * a
    g_ref[...] = g
    bonus_ref[...] = bonus


def _rwkv_prep(z_main, mu, w0, a0, kkg, ka, rk, w2, a2, g2):
    T = z_main.shape[0]
    tm = min(256, T)
    W = RWKV_W
    vec = lambda n: pl.BlockSpec((1, n), lambda i: (0, 0))
    mat = lambda s: pl.BlockSpec(s, lambda i: (0, 0))
    out = pl.BlockSpec((tm, W), lambda i: (i, 0))
    return pl.pallas_call(
        _rwkv_prep_kernel,
        grid=(T // tm,),
        in_specs=[pl.BlockSpec((tm, RWKV_PAD), lambda i: (i, 4 * MLSTM_W // RWKV_PAD)), vec(RWKV_PAD), vec(W), vec(W),
                  vec(W), vec(W), vec(W), mat((LORA_PAD, W)), mat((LORA_PAD, W)), mat((GATE_LORA, W))],
        out_specs=[out] * 8,
        out_shape=[jax.ShapeDtypeStruct((T, W), f32)] * 8,
        scratch_shapes=[pltpu.VMEM((tm + 8, RWKV_PAD), f32)],
        compiler_params=_cp(("arbitrary",)),
        name="rwkv_prep",
    )(z_main, mu, w0, a0, kkg, ka, rk, w2, a2, g2)


def _rwkv_scan_kernel(r_ref, ld_ref, k_ref, v_ref, kk_ref, b_ref, g_ref, bonus_ref, lng_ref, lnb_ref, y_ref, s_ref):
    c = pl.program_id(0)
    L, W = r_ref.shape
    H, dh = RWKV_HEADS, RWKV_HEAD_DIM
    R = H * L

    @pl.when(c == 0)
    def _():
        s_ref[...] = jnp.zeros_like(s_ref)

    ld = ld_ref[...]
    tr = lax.broadcasted_iota(jnp.int32, (L, L), 0)
    tc = lax.broadcasted_iota(jnp.int32, (L, L), 1)
    lg = _mm((tr >= tc).astype(f32), ld, precision=HI)
    g_end = lg[L - 1:L, :]
    dec_in = jnp.exp(lg)
    dec_out = jnp.exp(-lg)
    dec_tail = jnp.exp(g_end - lg)
    kk = kk_ref[...]
    k = k_ref[...]
    b = b_ref[...]
    v = v_ref[...]
    kap = kk * jnp.exp(lg - ld)
    rt = r_ref[...] * dec_in
    kt = k * dec_out
    bt = b * dec_out

    head_mask = _iota_div((R, W), 0, L) == _iota_div((R, W), 1, dh)

    def tile(x):
        return jnp.concatenate([x] * H, axis=0)

    def stack(x):
        return jnp.where(head_mask, tile(x), 0.0)

    def unstack(x):
        acc = x[0:L, :]
        for hh in range(1, H):
            acc = acc + x[hh * L:(hh + 1) * L, :]
        return acc

    kap_s = stack(kap).astype(bf16)
    rt_s = stack(rt).astype(bf16)
    v_s = stack(v).astype(bf16)
    kt_t = tile(kt).astype(bf16)
    bt_t = tile(bt).astype(bf16)
    rr = lax.broadcasted_iota(jnp.int32, (R, R), 0)
    cc = lax.broadcasted_iota(jnp.int32, (R, R), 1)
    same = _iota_div((R, R), 0, L) == _iota_div((R, R), 1, L)
    strict = same & (rr > cc)
    incl = same & (rr >= cc)
    a_kb = jnp.where(strict, _nt(kap_s, bt_t), 0.0)
    a_kk = jnp.where(strict, _nt(kap_s, kt_t), 0.0)
    a_rk = jnp.where(incl, _nt(rt_s, kt_t), 0.0)
    a_rb = jnp.where(incl, _nt(rt_s, bt_t), 0.0)

    p = -a_kb
    tinv = jnp.where(rr == cc, 1.0, 0.0) + p
    step = 2
    while step < L:
        p = _bmm(p, p)
        tinv = tinv + _bmm(tinv, p)
        step *= 2

    s_mat = s_ref[...]
    s_b = s_mat.astype(bf16)
    pk = _nt(kap.astype(bf16), s_b)
    pr = _nt(rt.astype(bf16), s_b)
    y_s = stack(pk) + _bmm(a_kk, v_s)
    u_s = _bmm(tinv, y_s)
    o_s = stack(pr) + _bmm(a_rk, v_s) - _bmm(a_rb, u_s)
    o = unstack(o_s)
    u = unstack(u_s)

    khat = (k * dec_tail).astype(bf16)
    bhat = (b * dec_tail).astype(bf16)
    blk = _head_ones(W, dh)
    upd = _tn(v.astype(bf16), khat) - _tn(u.astype(bf16), bhat)
    s_ref[...] = s_mat * jnp.exp(g_end) + upd * blk

    mean = _mm(o, blk, precision=HI) * (1.0 / dh)
    d = o - mean
    var = _mm(d * d, blk, precision=HI) * (1.0 / dh)
    yn = d * lax.rsqrt(var + RWKV_GN_EPS) * lng_ref[...] + lnb_ref[...] + bonus_ref[...]
    y_ref[...] = (yn * g_ref[...]).astype(y_ref.dtype)


def _rwkv_scan(parts, ln_g, ln_b):
    r, ld, k, v, kk, b, g, bonus = parts
    T, W = r.shape
    L = RWKV_CHUNK
    blk = pl.BlockSpec((L, W), lambda c: (c, 0))
    vec = pl.BlockSpec((1, W), lambda c: (0, 0))
    return pl.pallas_call(
        _rwkv_scan_kernel,
        grid=(T // L,),
        in_specs=[blk] * 8 + [vec, vec],
        out_specs=blk,
        out_shape=jax.ShapeDtypeStruct((T, W), bf16),
        scratch_shapes=[pltpu.VMEM((W, W), f32)],
        compiler_params=_cp(("arbitrary",)),
        name="rwkv_scan",
    )(r, ld, k, v, kk, b, g, bonus, ln_g.reshape(1, W), ln_b.reshape(1, W))


def _s5_in_kernel(u_ref, kt_ref, br_ref, bi_ref, y_ref, xr_ref, xi_ref):
    for j in range(2):
        y_ref[j] = _mm(u_ref[j], kt_ref[j], precision=HI)
    u2 = jnp.concatenate([u_ref[0], u_ref[1]], axis=1)
    xr_ref[...] = _mm(u2, br_ref[0], precision=HI)
    xi_ref[...] = _mm(u2, bi_ref[0], precision=HI)


def _s5_scan_kernel(xr_ref, xi_ref, ar_ref, ai_ref, sr_ref, si_ref):
    nc = xr_ref.shape[0]
    ar = ar_ref[...]
    ai = ai_ref[...]

    def body(c, carry):
        sr, si = carry
        sr_ref[pl.ds(c, 1), :] = sr
        si_ref[pl.ds(c, 1), :] = si
        xr = xr_ref[pl.ds(c, 1), :]
        xi = xi_ref[pl.ds(c, 1), :]
        return ar * sr - ai * si + xr, ar * si + ai * sr + xi

    z = jnp.zeros_like(ar)
    lax.fori_loop(0, nc, body, (z, z))


def _s5_out_kernel(y_ref, sr_ref, si_ref, cr_ref, ci_ref, u_ref, d_ref, o_ref):
    sr = sr_ref[...]
    si = si_ref[...]
    for j in range(2):
        y = y_ref[j] + _mm(sr, cr_ref[j], precision=HI) + _mm(si, ci_ref[j], precision=HI) + d_ref[j] * u_ref[j]
        o_ref[j] = 0.5 * y * (1.0 + jnp.tanh(0.7978845608028654 * (y + 0.044715 * (y * y * y))))


def _glu_kernel(y_ref, w_ref, b_ref, o_ref):
    y = y_ref[...]
    o_ref[...] = (y * _sigmoid(_bmm(y, w_ref[...]) + b_ref[...])).astype(o_ref.dtype)


def _s5_tables(a_re, a_im, log_dt, b_re, b_im, c_re, c_im, d):
    G, N, P, L = S5_GROUPS, S5_STATE, S5_GROUP, S5_CHUNK
    dt = jnp.exp(log_dt)[:, None]
    lags = jnp.arange(L + 1, dtype=f32)[:, None, None]
    mag = jnp.exp(a_re * dt * lags)
    ang = a_im * dt * lags
    pw_re, pw_im = mag * jnp.cos(ang), mag * jnp.sin(ang)
    den = a_re * a_re + a_im * a_im
    nr, ni = pw_re[1] - 1.0, pw_im[1]
    coef_re = (nr * a_re + ni * a_im) / den
    coef_im = (ni * a_re - nr * a_im) / den
    bb_re = coef_re[:, :, None] * b_re - coef_im[:, :, None] * b_im
    bb_im = coef_re[:, :, None] * b_im + coef_im[:, :, None] * b_re
    cb_re = jnp.einsum('gpn,dgn->dgpn', c_re, pw_re[:L]) - jnp.einsum('gpn,dgn->dgpn', c_im, pw_im[:L])
    cb_im = jnp.einsum('gpn,dgn->dgpn', c_re, pw_im[:L]) + jnp.einsum('gpn,dgn->dgpn', c_im, pw_re[:L])
    kd = jnp.einsum('dgpn,gnq->dgpq', cb_re, bb_re, precision=HI) - jnp.einsum('dgpn,gnq->dgpq', cb_im, bb_im, precision=HI)
    t = np.arange(L)
    lag = t[None, :] - t[:, None]
    kt = jnp.where((lag >= 0)[:, :, None, None, None], kd[np.clip(lag, 0, L - 1)], 0.0)
    kt = kt.transpose(2, 0, 4, 1, 3).reshape(G, L * P, L * P)
    e_re, e_im = pw_re[L - 1 - t], pw_im[L - 1 - t]
    bs_re = jnp.einsum('sgn,gnq->gsqn', e_re, bb_re) - jnp.einsum('sgn,gnq->gsqn', e_im, bb_im)
    bs_im = jnp.einsum('sgn,gnq->gsqn', e_re, bb_im) + jnp.einsum('sgn,gnq->gsqn', e_im, bb_re)
    def pair_in(bs):
        bs = bs.reshape(G // 2, 2, L * P, N)
        zz = jnp.zeros((G // 2, L * P, N), f32)
        return jnp.concatenate([jnp.concatenate([bs[:, 0], zz], axis=2), jnp.concatenate([zz, bs[:, 1]], axis=2)], axis=1)

    bsr, bsi = pair_in(bs_re.reshape(G, L * P, N)), pair_in(bs_im.reshape(G, L * P, N))
    f_re, f_im = pw_re[t + 1], pw_im[t + 1]
    cs_re = (jnp.einsum('gpn,tgn->gntp', c_re, f_re) - jnp.einsum('gpn,tgn->gntp', c_im, f_im)).reshape(G, N, L * P)
    cs_im = -(jnp.einsum('gpn,tgn->gntp', c_re, f_im) + jnp.einsum('gpn,tgn->gntp', c_im, f_re)).reshape(G, N, L * P)
    z = jnp.zeros((G // 2, N, L * P), f32)

    def pair(cs):
        cs = cs.reshape(G // 2, 2, N, L * P)
        top = jnp.concatenate([cs[:, 0], z], axis=1)[:, None]
        bot = jnp.concatenate([z, cs[:, 1]], axis=1)[:, None]
        return jnp.concatenate([top, bot], axis=1).reshape(G, 2 * N, L * P)

    return dict(kt=kt, bsr=bsr, bsi=bsi, cr=pair(cs_re), ci=pair(cs_im), al_re=pw_re[L].reshape(1, G * N),
                al_im=pw_im[L].reshape(1, G * N), d=jnp.broadcast_to(d.reshape(G, 1, P), (G, L, P)).reshape(G, 1, L * P))


def _s5(z_tail, tab, glu_w, glu_b):
    T = z_tail.shape[0]
    G, N, P, L = S5_GROUPS, S5_STATE, S5_GROUP, S5_CHUNK
    nc, LP = T // L, L * P
    u = z_tail[:, :S5_W].reshape(nc, L, G, P).transpose(2, 0, 1, 3).reshape(G, nc, LP)
    grp = lambda s: pl.BlockSpec((2,) + s, lambda j: (j, 0, 0))
    lanes = pl.BlockSpec((nc, 2 * N), lambda j: (0, j))
    y_in, xr, xi = pl.pallas_call(
        _s5_in_kernel,
        grid=(G // 2,),
        in_specs=[grp((nc, LP)), grp((LP, LP)), pl.BlockSpec((1, 2 * LP, 2 * N), lambda j: (j, 0, 0)),
                  pl.BlockSpec((1, 2 * LP, 2 * N), lambda j: (j, 0, 0))],
        out_specs=[grp((nc, LP)), lanes, lanes],
        out_shape=[jax.ShapeDtypeStruct((G, nc, LP), f32), jax.ShapeDtypeStruct((nc, G * N), f32),
                   jax.ShapeDtypeStruct((nc, G * N), f32)],
        compiler_params=_cp(("parallel",)),
        name="s5_in",
    )(u, tab["kt"], tab["bsr"], tab["bsi"])
    whole = lambda s: pl.BlockSpec(s, lambda: (0,) * len(s))
    sr, si = pl.pallas_call(
        _s5_scan_kernel,
        in_specs=[whole((nc, G * N)), whole((nc, G * N)), whole((1, G * N)), whole((1, G * N))],
        out_specs=[whole((nc, G * N)), whole((nc, G * N))],
        out_shape=[jax.ShapeDtypeStruct((nc, G * N), f32)] * 2,
        compiler_params=pltpu.CompilerParams(vmem_limit_bytes=VMEM_LIMIT),
        name="s5_scan",
    )(xr, xi, tab["al_re"], tab["al_im"])
    y = pl.pallas_call(
        _s5_out_kernel,
        grid=(G // 2,),
        in_specs=[grp((nc, LP)), lanes, lanes, grp((2 * N, LP)), grp((2 * N, LP)), grp((nc, LP)), grp((1, LP))],
        out_specs=grp((nc, LP)),
        out_shape=jax.ShapeDtypeStruct((G, nc, LP), f32),
        compiler_params=_cp(("parallel",)),
        name="s5_out",
    )(y_in, sr, si, tab["cr"], tab["ci"], u, tab["d"])
    y = y.reshape(G, nc, L, P).transpose(1, 2, 0, 3).reshape(T, S5_W)
    tm = min(1024, T)
    return pl.pallas_call(
        _glu_kernel,
        grid=(T // tm,),
        in_specs=[pl.BlockSpec((tm, S5_W), lambda i: (i, 0)), pl.BlockSpec((S5_W, S5_W), lambda i: (0, 0)),
                  pl.BlockSpec((1, S5_W), lambda i: (0, 0))],
        out_specs=pl.BlockSpec((tm, S5_W), lambda i: (i, 0)),
        out_shape=jax.ShapeDtypeStruct((T, S5_W), bf16),
        compiler_params=_cp(("parallel",)),
        name="s5_glu",
    )(y, glu_w, glu_b.reshape(1, S5_W))


def _pad_cols(w, n):
    return jnp.pad(w, ((0, 0), (0, n - w.shape[1])))


def _pad_rows(w, n):
    return jnp.pad(w, ((0, n - w.shape[0]), (0, 0)))


def _split_in_proj(w):
    r0 = M_IN
    l0 = r0 + 3 * RWKV_W
    s0 = M_IN + RWKV_IN
    g0 = s0 + S5_W
    main = jnp.concatenate([
        w[:, :4 * MLSTM_W], w[:, r0:l0],
        _pad_cols(w[:, l0:l0 + DECAY_LORA], LORA_PAD),
        _pad_cols(w[:, l0 + DECAY_LORA:l0 + DECAY_LORA + AAA_LORA], LORA_PAD),
        w[:, l0 + DECAY_LORA + AAA_LORA:s0], w[:, g0:]], axis=1).astype(bf16)
    tail = _pad_cols(jnp.concatenate([w[:, s0:g0], w[:, 4 * MLSTM_W:M_IN]], axis=1), Z_TAIL).astype(bf16)
    return main, tail


def _split_mu(mu):
    l0 = 3 * RWKV_W
    return jnp.concatenate([
        mu[:l0], jnp.pad(mu[l0:l0 + DECAY_LORA], (0, LORA_PAD - DECAY_LORA)),
        jnp.pad(mu[l0 + DECAY_LORA:l0 + DECAY_LORA + AAA_LORA], (0, LORA_PAD - AAA_LORA)),
        mu[l0 + DECAY_LORA + AAA_LORA:]]).reshape(1, RWKV_PAD)


def kernel(x, p, norm_mix_g, w_in, mlstm_conv, mlstm_ib, mlstm_fb, mlstm_norm_g, rwkv_mu, rwkv_w0, rwkv_w2, rwkv_a0, rwkv_a2, rwkv_g2, rwkv_kk, rwkv_ka, rwkv_rk, rwkv_ln_g, rwkv_ln_b, s5_a_re, s5_a_im, s5_log_dt, s5_b_re, s5_b_im, s5_c_re, s5_c_im, s5_d, s5_glu_w, s5_glu_b, w_up_m, w_up_r, w_up_s, w_out, norm_ffn_g, ffn_w_gate, ffn_w_up, ffn_w_down, norm_ple_g, ple_w_gate, ple_w_proj, final_norm_g):
    B, T, D = x.shape
    depth = w_in.shape[0]
    outs = []
    for bi in range(B):
        h = x[bi]
        xn = _rmsnorm(h, norm_mix_g[0], bf16)
        for i in range(depth):
            w_main, w_tail = _split_in_proj(w_in[i])
            z_main = _proj(xn, w_main, 1024)
            z_tail = _proj(xn, w_tail, Z_TAIL)

            gate_b = jnp.pad(jnp.concatenate([mlstm_ib[i], mlstm_fb[i]]), (0, 128 - 2 * MLSTM_HEADS)).reshape(1, 128)
            y_m = _mlstm(z_main, z_tail, mlstm_conv[i], gate_b, mlstm_norm_g[i])

            vec = lambda a: a.reshape(1, RWKV_W)
            parts = _rwkv_prep(z_main, _split_mu(rwkv_mu[i]), vec(rwkv_w0[i]), vec(rwkv_a0[i]), vec(rwkv_kk[i]),
                               vec(rwkv_ka[i]), vec(rwkv_rk[i]), _pad_rows(rwkv_w2[i], LORA_PAD).astype(bf16),
                               _pad_rows(rwkv_a2[i], LORA_PAD).astype(bf16), rwkv_g2[i].astype(bf16))
            y_r = _rwkv_scan(parts, rwkv_ln_g[i], rwkv_ln_b[i])

            tab = _s5_tables(s5_a_re[i], s5_a_im[i], s5_log_dt[i], s5_b_re[i], s5_b_im[i], s5_c_re[i], s5_c_im[i], s5_d[i])
            y_s = _s5(z_tail, tab, s5_glu_w[i].astype(bf16), s5_glu_b[i])

            mixed = _merge(y_m, y_r, y_s, z_main, w_up_m[i].astype(bf16), w_up_r[i].astype(bf16), w_up_s[i].astype(bf16))
            h, hn = _resid_norm(mixed, w_out[i].astype(bf16), h, norm_ffn_g[i], D_MODEL)
            act = _ffn_up(hn, ffn_w_gate[i].astype(bf16), ffn_w_up[i].astype(bf16))
            h, hp = _resid_norm(act, ffn_w_down[i].astype(bf16), h, norm_ple_g[i], FFN_HIDDEN // 4)
            last = i == depth - 1
            g_next = final_norm_g if last else norm_mix_g[i + 1]
            h, xn = _ple(hp, p[i, bi].astype(bf16), ple_w_gate[i].astype(bf16), ple_w_proj[i].astype(bf16), h, g_next,
                         f32 if last else bf16)
        outs.append(xn)
    return jnp.stack(outs)
```

```python
import functools

import jax
import jax.numpy as jnp
import numpy as np
from jax import lax
from jax.experimental import pallas as pl
from jax.experimental.pallas import tpu as pltpu

f32 = jnp.float32
bf16 = jnp.bfloat16
HI = lax.Precision.HIGHEST

D_MODEL = 2048
PLE_DIM = 256
RMS_EPS = 1e-6
MLSTM_HEADS = 4
MLSTM_HEAD_DIM = 256
MLSTM_W = 1024
MLSTM_CHUNK = 256
GATE_SOFTCAP = 15.0
RWKV_HEADS = 8
RWKV_HEAD_DIM = 64
RWKV_W = 512
DECAY_LORA = 96
AAA_LORA = 96
GATE_LORA = 256
LORA_PAD = 128
RWKV_GN_EPS = 64e-5
RWKV_CHUNK = 32
S5_GROUP = 16
S5_GROUPS = 32
S5_W = 512
S5_STATE = 64
S5_CHUNK = 16
FFN_HIDDEN = 5632
M_IN = 4 * MLSTM_W + 2 * MLSTM_HEADS
RWKV_IN = 3 * RWKV_W + DECAY_LORA + AAA_LORA + GATE_LORA
RWKV_PAD = 3 * RWKV_W + 2 * LORA_PAD + GATE_LORA
Z_MAIN = 4 * MLSTM_W + RWKV_PAD + 3 * D_MODEL
Z_TAIL = S5_W + 128
VMEM_LIMIT = 56 * 1024 * 1024


def _cp(sem):
    return pltpu.CompilerParams(dimension_semantics=sem, vmem_limit_bytes=VMEM_LIMIT)


def _nt(a, b, **kw):
    return lax.dot_general(a, b, (((1,), (1,)), ((), ())), preferred_element_type=f32, **kw)


def _tn(a, b, **kw):
    return lax.dot_general(a, b, (((0,), (0,)), ((), ())), preferred_element_type=f32, **kw)


def _mm(a, b, **kw):
    return jnp.dot(a, b, preferred_element_type=f32, **kw)


def _bmm(a, b):
    return jnp.dot(a.astype(bf16), b.astype(bf16), preferred_element_type=f32)


def _sigmoid(x):
    return 1.0 / (1.0 + jnp.exp(-x))


def _rms(x, g):
    return x * lax.rsqrt(jnp.mean(x * x, axis=-1, keepdims=True) + RMS_EPS) * g


def _norm_kernel(x_ref, g_ref, o_ref):
    o_ref[...] = _rms(x_ref[...], g_ref[...]).astype(o_ref.dtype)


def _rmsnorm(x, g, out_dtype):
    T, D = x.shape
    tm = min(512, T)
    return pl.pallas_call(
        _norm_kernel,
        grid=(T // tm,),
        in_specs=[pl.BlockSpec((tm, D), lambda i: (i, 0)), pl.BlockSpec((1, D), lambda i: (0, 0))],
        out_specs=pl.BlockSpec((tm, D), lambda i: (i, 0)),
        out_shape=jax.ShapeDtypeStruct((T, D), out_dtype),
        compiler_params=_cp(("parallel",)),
        name="rmsnorm",
    )(x, g.reshape(1, D))


def _proj_kernel(a_ref, w_ref, o_ref):
    o_ref[...] = _mm(a_ref[...], w_ref[...]).astype(o_ref.dtype)


def _proj(a, w, tn, out_dtype=f32):
    T, K = a.shape
    N = w.shape[1]
    tm = min(1024, T)
    return pl.pallas_call(
        _proj_kernel,
        grid=(N // tn, T // tm),
        in_specs=[pl.BlockSpec((tm, K), lambda j, i: (i, 0)), pl.BlockSpec((K, tn), lambda j, i: (0, j))],
        out_specs=pl.BlockSpec((tm, tn), lambda j, i: (i, j)),
        out_shape=jax.ShapeDtypeStruct((T, N), out_dtype),
        compiler_params=_cp(("parallel", "parallel")),
        name="in_proj",
    )(a, w)


def _merge_kernel(ym_ref, yr_ref, ys_ref, gm_ref, gr_ref, gs_ref, um_ref, ur_ref, us_ref, o_ref, umb_ref, urb_ref, usb_ref):
    @pl.when(pl.program_id(1) == 0)
    def _():
        umb_ref[...] = um_ref[...].astype(bf16)
        urb_ref[...] = ur_ref[...].astype(bf16)
        usb_ref[...] = us_ref[...].astype(bf16)

    acc = _sigmoid(gm_ref[...]) * _mm(ym_ref[...], umb_ref[...])
    acc += _sigmoid(gr_ref[...]) * _mm(yr_ref[...], urb_ref[...])
    acc += _sigmoid(gs_ref[...]) * _mm(ys_ref[...], usb_ref[...])
    o_ref[...] = acc.astype(o_ref.dtype)


def _merge(ym, yr, ys, z_main, um, ur, us):
    T = ym.shape[0]
    tm, tn = min(512, T), 1024
    nj = D_MODEL // tn
    g0 = (4 * MLSTM_W + RWKV_PAD) // tn

    def gate_spec(b):
        return pl.BlockSpec((tm, tn), lambda j, i, b=b: (i, g0 + b * nj + j))

    def y_spec(w):
        return pl.BlockSpec((tm, w), lambda j, i: (i, 0))

    def u_spec(w):
        return pl.BlockSpec((w, tn), lambda j, i: (0, j))

    return pl.pallas_call(
        _merge_kernel,
        grid=(nj, T // tm),
        in_specs=[y_spec(MLSTM_W), y_spec(RWKV_W), y_spec(S5_W), gate_spec(0), gate_spec(1), gate_spec(2),
                  u_spec(MLSTM_W), u_spec(RWKV_W), u_spec(S5_W)],
        out_specs=pl.BlockSpec((tm, tn), lambda j, i: (i, j)),
        out_shape=jax.ShapeDtypeStruct((T, D_MODEL), bf16),
        scratch_shapes=[pltpu.VMEM((MLSTM_W, tn), bf16), pltpu.VMEM((RWKV_W, tn), bf16), pltpu.VMEM((S5_W, tn), bf16)],
        compiler_params=_cp(("parallel", "arbitrary")),
        name="gated_merge",
    )(ym, yr, ys, z_main, z_main, z_main, um, ur, us)


def _resid_norm_kernel(a_ref, w_ref, h_ref, g_ref, ho_ref, no_ref, acc_ref):
    k = pl.program_id(1)

    @pl.when(k == 0)
    def _():
        acc_ref[...] = h_ref[...]

    acc_ref[...] += _mm(a_ref[...], w_ref[...])

    @pl.when(k == pl.num_programs(1) - 1)
    def _():
        hn = acc_ref[...]
        ho_ref[...] = hn
        no_ref[...] = _rms(hn, g_ref[...]).astype(no_ref.dtype)


def _resid_norm(a, w, h, g, tk, norm_dtype=bf16):
    T, K = a.shape
    tm = min(512, T)
    return pl.pallas_call(
        _resid_norm_kernel,
        grid=(T // tm, K // tk),
        in_specs=[pl.BlockSpec((tm, tk), lambda i, k: (i, k)), pl.BlockSpec((tk, D_MODEL), lambda i, k: (k, 0)),
                  pl.BlockSpec((tm, D_MODEL), lambda i, k: (i, 0)), pl.BlockSpec((1, D_MODEL), lambda i, k: (0, 0))],
        out_specs=[pl.BlockSpec((tm, D_MODEL), lambda i, k: (i, 0)), pl.BlockSpec((tm, D_MODEL), lambda i, k: (i, 0))],
        out_shape=[jax.ShapeDtypeStruct((T, D_MODEL), f32), jax.ShapeDtypeStruct((T, D_MODEL), norm_dtype)],
        scratch_shapes=[pltpu.VMEM((tm, D_MODEL), f32)],
        compiler_params=_cp(("parallel", "arbitrary")),
        name="resid_matmul_norm",
    )(a, w, h, g.reshape(1, D_MODEL))


def _ffn_up_kernel(a_ref, wg_ref, wu_ref, o_ref, wgb_ref, wub_ref):
    @pl.when(pl.program_id(1) == 0)
    def _():
        wgb_ref[...] = wg_ref[...].astype(bf16)
        wub_ref[...] = wu_ref[...].astype(bf16)

    a = a_ref[...]
    gt = _mm(a, wgb_ref[...])
    up = _mm(a, wub_ref[...])
    o_ref[...] = (gt * _sigmoid(gt) * up).astype(o_ref.dtype)


def _ffn_up(a, wg, wu):
    T, K = a.shape
    N = wg.shape[1]
    tm, tn = min(1024, T), 512
    return pl.pallas_call(
        _ffn_up_kernel,
        grid=(N // tn, T // tm),
        in_specs=[pl.BlockSpec((tm, K), lambda j, i: (i, 0)), pl.BlockSpec((K, tn), lambda j, i: (0, j)),
                  pl.BlockSpec((K, tn), lambda j, i: (0, j))],
        out_specs=pl.BlockSpec((tm, tn), lambda j, i: (i, j)),
        out_shape=jax.ShapeDtypeStruct((T, N), bf16),
        scratch_shapes=[pltpu.VMEM((K, tn), bf16), pltpu.VMEM((K, tn), bf16)],
        compiler_params=_cp(("parallel", "arbitrary")),
        name="ffn_up",
    )(a, wg, wu)


def _ple_kernel(hp_ref, p_ref, wg_ref, wp_ref, h_ref, g_ref, ho_ref, no_ref):
    gate = _sigmoid(_mm(hp_ref[...], wg_ref[...]))
    hn = h_ref[...] + _mm(p_ref[...], wp_ref[...]) * gate
    ho_ref[...] = hn
    no_ref[...] = _rms(hn, g_ref[...]).astype(no_ref.dtype)


def _ple(hp, p, wg, wp, h, g, norm_dtype):
    T = h.shape[0]
    tm = min(512, T)
    row = lambda w: pl.BlockSpec((tm, w), lambda i: (i, 0))
    full = lambda s: pl.BlockSpec(s, lambda i: (0, 0))
    return pl.pallas_call(
        _ple_kernel,
        grid=(T // tm,),
        in_specs=[row(D_MODEL), row(PLE_DIM), full((D_MODEL, D_MODEL)), full((PLE_DIM, D_MODEL)), row(D_MODEL),
                  full((1, D_MODEL))],
        out_specs=[row(D_MODEL), row(D_MODEL)],
        out_shape=[jax.ShapeDtypeStruct((T, D_MODEL), f32), jax.ShapeDtypeStruct((T, D_MODEL), norm_dtype)],
        compiler_params=_cp(("parallel",)),
        name="ple_norm",
    )(hp, p, wg, wp, h, g.reshape(1, D_MODEL))


def _mlstm_kernel(zq_ref, zk_ref, v_ref, o_ref, gz_ref, cq_ref, ck_ref, gb_ref, ng_ref, y_ref,
                  ct_ref, n_ref, qbuf_ref, kbuf_ref):
    h = pl.program_id(0)
    c = pl.program_id(1)
    L, dh = zq_ref.shape

    @pl.when(c == 0)
    def _():
        ct_ref[...] = jnp.zeros_like(ct_ref)
        n_ref[...] = jnp.zeros_like(n_ref)
        qbuf_ref[0:8, :] = jnp.zeros((8, dh), f32)
        kbuf_ref[0:8, :] = jnp.zeros((8, dh), f32)

    def conv_silu(z_ref, w_ref, buf_ref):
        x = z_ref[...]
        buf_ref[8:, :] = x
        w = w_ref[...]
        acc = x * w[0:1, :]
        for j in range(1, w.shape[0]):
            acc = acc + buf_ref[pl.ds(8 - j, L), :] * w[j:j + 1, :]
        buf_ref[0:8, :] = x[L - 8:, :]
        return acc * _sigmoid(acc)

    q = conv_silu(zq_ref, cq_ref, qbuf_ref) * (dh ** -0.5)
    k = conv_silu(zk_ref, ck_ref, kbuf_ref)
    v = v_ref[...]

    g = gz_ref[...] + gb_ref[...]
    sc = GATE_SOFTCAP * jnp.tanh(g / GATE_SOFTCAP)
    logf = -jnp.log(1.0 + jnp.exp(-sc))
    row = lax.broadcasted_iota(jnp.int32, (L, L), 0)
    col = lax.broadcasted_iota(jnp.int32, (L, L), 1)
    causal = row >= col
    bcs = _mm(causal.astype(f32), logf, precision=HI)
    lane = lax.broadcasted_iota(jnp.int32, g.shape, 1)
    b_sel = jnp.where(lane == MLSTM_HEADS + h, bcs, 0.0)
    e_sel = jnp.where(lane == h, sc, 0.0) - b_sel
    b_col = jnp.sum(b_sel, axis=1, keepdims=True)
    e_col = jnp.sum(e_sel, axis=1, keepdims=True)
    e_row = _nt(jnp.ones_like(e_sel), e_sel, precision=HI)
    wmat = jnp.where(causal, jnp.exp(jnp.where(causal, b_col + e_row, 0.0)), 0.0)
    b_end = b_col[L - 1:L, :]
    eb = jnp.exp(b_col)

    qb = q.astype(bf16)
    kb = k.astype(bf16)
    vb = v.astype(bf16)
    s = _nt(qb, kb) * wmat
    ct = ct_ref[...]
    n = n_ref[...]
    num = _mm(s.astype(bf16), vb) + eb * _mm(qb, ct.astype(bf16))
    den = jnp.sum(s, axis=1, keepdims=True) + eb * jnp.sum(q * n, axis=1, keepdims=True)
    hh = num / jnp.maximum(jnp.abs(den), 1.0)
    hh = hh * lax.rsqrt(jnp.mean(hh * hh, axis=1, keepdims=True) + RMS_EPS) * ng_ref[...]
    y_ref[...] = (_sigmoid(o_ref[...]) * hh).astype(y_ref.dtype)

    kw = k * jnp.exp(b_end + e_col)
    decay = jnp.exp(b_end)
    ct_ref[...] = decay * ct + _tn(kw.astype(bf16), vb)
    n_ref[...] = decay * n + jnp.sum(kw, axis=0, keepdims=True)


def _mlstm(z_main, z_tail, conv_w, gate_b, norm_g):
    T = z_main.shape[0]
    L, dh, H = min(MLSTM_CHUNK, T), MLSTM_HEAD_DIM, MLSTM_HEADS
    blk = lambda off: pl.BlockSpec((L, dh), lambda h, c, off=off: (c, off + h))
    return pl.pallas_call(
        _mlstm_kernel,
        grid=(H, T // L),
        in_specs=[blk(0), blk(H), blk(2 * H), blk(3 * H),
                  pl.BlockSpec((L, 128), lambda h, c: (c, S5_W // 128)),
                  pl.BlockSpec((4, dh), lambda h, c: (0, h)), pl.BlockSpec((4, dh), lambda h, c: (0, H + h)),
                  pl.BlockSpec((1, 128), lambda h, c: (0, 0)), pl.BlockSpec((1, dh), lambda h, c: (0, h))],
        out_specs=pl.BlockSpec((L, dh), lambda h, c: (c, h)),
        out_shape=jax.ShapeDtypeStruct((T, MLSTM_W), bf16),
        scratch_shapes=[pltpu.VMEM((dh, dh), f32), pltpu.VMEM((1, dh), f32),
                        pltpu.VMEM((L + 8, dh), f32), pltpu.VMEM((L + 8, dh), f32)],
        compiler_params=_cp(("arbitrary", "arbitrary")),
        name="mlstm",
    )(z_main, z_main, z_main, z_main, z_tail, conv_w, conv_w, gate_b, norm_g.reshape(1, MLSTM_W))


def _iota_div(shape, dim, width):
    return lax.shift_right_logical(lax.broadcasted_iota(jnp.int32, shape, dim), int(np.log2(width)))


def _head_ones(n, width):
    return (_iota_div((n, n), 0, width) == _iota_div((n, n), 1, width)).astype(f32)


def _rwkv_prep_kernel(z_ref, mu_ref, w0_ref, a0_ref, kkg_ref, ka_ref, rk_ref, w2_ref, a2_ref, g2_ref,
                      r_ref, ld_ref, k_ref, v_ref, kk_ref, b_ref, g_ref, bonus_ref, buf_ref):
    i = pl.program_id(0)
    tm = z_ref.shape[0]
    W = RWKV_W

    @pl.when(i == 0)
    def _():
        buf_ref[0:8, :] = jnp.zeros((8, buf_ref.shape[1]), f32)

    x = z_ref[...]
    buf_ref[8:, :] = x
    xs = x + (buf_ref[pl.ds(7, tm), :] - x) * mu_ref[...]
    buf_ref[0:8, :] = x[tm - 8:, :]
    r = xs[:, 0:W]
    k = xs[:, W:2 * W]
    v = xs[:, 2 * W:3 * W]
    wl = xs[:, 3 * W:3 * W + LORA_PAD]
    al = xs[:, 3 * W + LORA_PAD:3 * W + 2 * LORA_PAD]
    gl = xs[:, 3 * W + 2 * LORA_PAD:]
    t = w0_ref[...] + _bmm(jnp.tanh(wl), w2_ref[...])
    w = -(jnp.maximum(-t, 0.0) + jnp.log(1.0 + jnp.exp(-jnp.abs(t)))) - 0.5
    a = _sigmoid(a0_ref[...] + _bmm(al, a2_ref[...]))
    g = _bmm(_sigmoid(gl), g2_ref[...])
    ones = _head_ones(W, RWKV_HEAD_DIM)
    kk = k * kkg_ref[...]
    ss = _mm(kk * kk, ones, precision=HI)
    kk = kk / jnp.maximum(jnp.sqrt(ss), 1e-12)
    k2 = k * (1.0 + (a - 1.0) * ka_ref[...])
    bonus = _mm(r * k2 * rk_ref[...], ones, precision=HI) * v
    r_ref[...] = r
    ld_ref[...] = -jnp.exp(w)
    k_ref[...] = k2
    v_ref[...] = v
    kk_ref[...] = kk
    b_ref[...] = kk * a
    g_ref[...] = g
    bonus_ref[...] = bonus


def _rwkv_prep(z_main, mu, w0, a0, kkg, ka, rk, w2, a2, g2):
    T = z_main.shape[0]
    tm = min(256, T)
    W = RWKV_W
    vec = lambda n: pl.BlockSpec((1, n), lambda i: (0, 0))
    mat = lambda s: pl.BlockSpec(s, lambda i: (0, 0))
    out = pl.BlockSpec((tm, W), lambda i: (i, 0))
    return pl.pallas_call(
        _rwkv_prep_kernel,
        grid=(T // tm,),
        in_specs=[pl.BlockSpec((tm, RWKV_PAD), lambda i: (i, 4 * MLSTM_W // RWKV_PAD)), vec(RWKV_PAD), vec(W), vec(W),
                  vec(W), vec(W), vec(W), mat((LORA_PAD, W)), mat((LORA_PAD, W)), mat((GATE_LORA, W))],
        out_specs=[out] * 8,
        out_shape=[jax.ShapeDtypeStruct((T, W), f32)] * 8,
        scratch_shapes=[pltpu.VMEM((tm + 8, RWKV_PAD), f32)],
        compiler_params=_cp(("arbitrary",)),
        name="rwkv_prep",
    )(z_main, mu, w0, a0, kkg, ka, rk, w2, a2, g2)


def _rwkv_scan_kernel(r_ref, ld_ref, k_ref, v_ref, kk_ref, b_ref, g_ref, bonus_ref, lng_ref, lnb_ref, y_ref, s_ref):
    c = pl.program_id(0)
    L, W = r_ref.shape
    H, dh = RWKV_HEADS, RWKV_HEAD_DIM
    R = H * L

    @pl.when(c == 0)
    def _():
        s_ref[...] = jnp.zeros_like(s_ref)

    ld = ld_ref[...]
    tr = lax.broadcasted_iota(jnp.int32, (L, L), 0)
    tc = lax.broadcasted_iota(jnp.int32, (L, L), 1)
    lg = _mm((tr >= tc).astype(f32), ld, precision=HI)
    g_end = lg[L - 1:L, :]
    dec_in = jnp.exp(lg)
    dec_out = jnp.exp(-lg)
    dec_tail = jnp.exp(g_end - lg)
    kk = kk_ref[...]
    k = k_ref[...]
    b = b_ref[...]
    v = v_ref[...]
    kap = kk * jnp.exp(lg - ld)
    rt = r_ref[...] * dec_in
    kt = k * dec_out
    bt = b * dec_out

    head_mask = _iota_div((R, W), 0, L) == _iota_div((R, W), 1, dh)

    def tile(x):
        return jnp.concatenate([x] * H, axis=0)

    def stack(x):
        return jnp.where(head_mask, tile(x), 0.0)

    def unstack(x):
        acc = x[0:L, :]
        for hh in range(1, H):
            acc = acc + x[hh * L:(hh + 1) * L, :]
        return acc

    kap_s = stack(kap).astype(bf16)
    rt_s = stack(rt).astype(bf16)
    v_s = stack(v).astype(bf16)
    kt_t = tile(kt).astype(bf16)
    bt_t = tile(bt).astype(bf16)
    rr = lax.broadcasted_iota(jnp.int32, (R, R), 0)
    cc = lax.broadcasted_iota(jnp.int32, (R, R), 1)
    same = _iota_div((R, R), 0, L) == _iota_div((R, R), 1, L)
    strict = same & (rr > cc)
    incl = same & (rr >= cc)
    a_kb = jnp.where(strict, _nt(kap_s, bt_t), 0.0)
    a_kk = jnp.where(strict, _nt(kap_s, kt_t), 0.0)
    a_rk = jnp.where(incl, _nt(rt_s, kt_t), 0.0)
    a_rb = jnp.where(incl, _nt(rt_s, bt_t), 0.0)

    p = -a_kb
    tinv = jnp.where(rr == cc, 1.0, 0.0) + p
    step = 2
    while step < L:
        p = _bmm(p, p)
        tinv = tinv + _bmm(tinv, p)
        step *= 2

    s_mat = s_ref[...]
    s_b = s_mat.astype(bf16)
    pk = _nt(kap.astype(bf16), s_b)
    pr = _nt(rt.astype(bf16), s_b)
    y_s = stack(pk) + _bmm(a_kk, v_s)
    u_s = _bmm(tinv, y_s)
    o_s = stack(pr) + _bmm(a_rk, v_s) - _bmm(a_rb, u_s)
    o = unstack(o_s)
    u = unstack(u_s)

    khat = (k * dec_tail).astype(bf16)
    bhat = (b * dec_tail).astype(bf16)
    blk = _head_ones(W, dh)
    upd = _tn(v.astype(bf16), khat) - _tn(u.astype(bf16), bhat)
    s_ref[...] = s_mat * jnp.exp(g_end) + upd * blk

    mean = _mm(o, blk, precision=HI) * (1.0 / dh)
    d = o - mean
    var = _mm(d * d, blk, precision=HI) * (1.0 / dh)
    yn = d * lax.rsqrt(var + RWKV_GN_EPS) * lng_ref[...] + lnb_ref[...] + bonus_ref[...]
    y_ref[...] = (yn * g_ref[...]).astype(y_ref.dtype)


def _rwkv_scan(parts, ln_g, ln_b):
    r, ld, k, v, kk, b, g, bonus = parts
    T, W = r.shape
    L = RWKV_CHUNK
    blk = pl.BlockSpec((L, W), lambda c: (c, 0))
    vec = pl.BlockSpec((1, W), lambda c: (0, 0))
    return pl.pallas_call(
        _rwkv_scan_kernel,
        grid=(T // L,),
        in_specs=[blk] * 8 + [vec, vec],
        out_specs=blk,
        out_shape=jax.ShapeDtypeStruct((T, W), bf16),
        scratch_shapes=[pltpu.VMEM((W, W), f32)],
        compiler_params=_cp(("arbitrary",)),
        name="rwkv_scan",
    )(r, ld, k, v, kk, b, g, bonus, ln_g.reshape(1, W), ln_b.reshape(1, W))


def _s5_dot(a, b):
    return jnp.dot(a.astype(bf16), b.astype(bf16), preferred_element_type=f32)


def _s5_split_kernel(z_ref, u_ref):
    x = z_ref[...]
    for g in range(S5_GROUPS):
        u_ref[g] = x[:, g * S5_GROUP:(g + 1) * S5_GROUP]


def _toeplitz(krow):
    P, LP = krow.shape
    lane = lax.broadcasted_iota(jnp.int32, (P, LP), 1)
    blocks = [krow]
    for s in range(1, LP // P):
        blocks.append(jnp.where(lane >= P * s, pltpu.roll(krow, P * s, 1), 0.0))
    return jnp.concatenate(blocks, axis=0)


def _s5_in_kernel(u_ref, krow_ref, br_ref, bi_ref, y_ref, xr_ref, xi_ref):
    for j in range(2):
        y_ref[j] = _s5_dot(u_ref[j], _toeplitz(krow_ref[j]))
    u2 = jnp.concatenate([u_ref[0], u_ref[1]], axis=1)
    xr_ref[...] = _s5_dot(u2, br_ref[0])
    xi_ref[...] = _s5_dot(u2, bi_ref[0])


def _s5_scan_kernel(xr_ref, xi_ref, ar_ref, ai_ref, sr_ref, si_ref):
    nc = xr_ref.shape[0]
    ar = ar_ref[...]
    ai = ai_ref[...]

    def body(c, carry):
        sr, si = carry
        sr_ref[pl.ds(c, 1), :] = sr
        si_ref[pl.ds(c, 1), :] = si
        xr = xr_ref[pl.ds(c, 1), :]
        xi = xi_ref[pl.ds(c, 1), :]
        return ar * sr - ai * si + xr, ar * si + ai * sr + xi

    z = jnp.zeros_like(ar)
    lax.fori_loop(0, nc, body, (z, z))


def _s5_out_kernel(y_ref, sr_ref, si_ref, cr_ref, ci_ref, u_ref, d_ref, o_ref):
    sr = sr_ref[...]
    si = si_ref[...]
    for j in range(2):
        y = y_ref[j] + _s5_dot(sr, cr_ref[j]) + _s5_dot(si, ci_ref[j]) + d_ref[j] * u_ref[j]
        o_ref[j] = 0.5 * y * (1.0 + jnp.tanh(0.7978845608028654 * (y + 0.044715 * (y * y * y))))


def _glu_kernel(y_ref, w_ref, b_ref, o_ref):
    y = jnp.concatenate([y_ref[g] for g in range(S5_GROUPS)], axis=1)
    o_ref[...] = (y * _sigmoid(_bmm(y, w_ref[...]) + b_ref[...])).astype(o_ref.dtype)


def _s5_tables(a_re, a_im, log_dt, b_re, b_im, c_re, c_im, d):
    G, N, P, L = S5_GROUPS, S5_STATE, S5_GROUP, S5_CHUNK
    D = a_re.shape[0]
    dt = jnp.exp(log_dt)[:, None, :, None]
    lags = jnp.arange(L + 1, dtype=f32)[None, :, None, None]
    mag = jnp.exp(a_re[:, None] * dt * lags)
    ang = a_im[:, None] * dt * lags
    pw_re, pw_im = mag * jnp.cos(ang), mag * jnp.sin(ang)
    den = a_re * a_re + a_im * a_im
    nr, ni = pw_re[:, 1] - 1.0, pw_im[:, 1]
    coef_re = ((nr * a_re + ni * a_im) / den)[..., None]
    coef_im = ((ni * a_re - nr * a_im) / den)[..., None]
    bb_re = coef_re * b_re - coef_im * b_im
    bb_im = coef_re * b_im + coef_im * b_re
    cb_re = c_re[:, None] * pw_re[:, :L, :, None, :] - c_im[:, None] * pw_im[:, :L, :, None, :]
    cb_im = c_re[:, None] * pw_im[:, :L, :, None, :] + c_im[:, None] * pw_re[:, :L, :, None, :]
    krow = (jnp.einsum('dtgpn,dgnq->dgqtp', cb_re, bb_re, precision=HI)
            - jnp.einsum('dtgpn,dgnq->dgqtp', cb_im, bb_im, precision=HI)).reshape(D, G, P, L * P)
    e_re, e_im = pw_re[:, L - 1::-1][:, :L], pw_im[:, L - 1::-1][:, :L]
    e_re, e_im = e_re.transpose(0, 2, 1, 3)[:, :, :, None, :], e_im.transpose(0, 2, 1, 3)[:, :, :, None, :]
    bq_re, bq_im = bb_re.transpose(0, 1, 3, 2)[:, :, None], bb_im.transpose(0, 1, 3, 2)[:, :, None]
    bs_re = (e_re * bq_re - e_im * bq_im).reshape(D, G // 2, 2, L * P, N)
    bs_im = (e_re * bq_im + e_im * bq_re).reshape(D, G // 2, 2, L * P, N)

    def pair_in(bs):
        zz = jnp.zeros_like(bs[:, :, 0])
        return jnp.concatenate([jnp.concatenate([bs[:, :, 0], zz], axis=3), jnp.concatenate([zz, bs[:, :, 1]], axis=3)], axis=2)

    f_re = pw_re[:, 1:].transpose(0, 2, 3, 1)[..., None]
    f_im = pw_im[:, 1:].transpose(0, 2, 3, 1)[..., None]
    cn_re, cn_im = c_re.transpose(0, 1, 3, 2)[:, :, :, None, :], c_im.transpose(0, 1, 3, 2)[:, :, :, None, :]
    cs_re = (cn_re * f_re - cn_im * f_im).reshape(D, G // 2, 2, N, L * P)
    cs_im = (-(cn_re * f_im + cn_im * f_re)).reshape(D, G // 2, 2, N, L * P)

    def pair_out(cs):
        zz = jnp.zeros_like(cs[:, :, 0])
        top = jnp.concatenate([cs[:, :, 0], zz], axis=2)[:, :, None]
        bot = jnp.concatenate([zz, cs[:, :, 1]], axis=2)[:, :, None]
        return jnp.concatenate([top, bot], axis=2).reshape(D, G, 2 * N, L * P)

    dd = jnp.broadcast_to(d.reshape(D, G, 1, 1, P), (D, G, 1, L, P)).reshape(D, G, 1, L * P)
    return dict(krow=krow, bsr=pair_in(bs_re), bsi=pair_in(bs_im), cr=pair_out(cs_re), ci=pair_out(cs_im),
                al_re=pw_re[:, L].reshape(D, 1, G * N), al_im=pw_im[:, L].reshape(D, 1, G * N), d=dd)


def _s5(z_tail, tab, glu_w, glu_b):
    T = z_tail.shape[0]
    G, N, P, L = S5_GROUPS, S5_STATE, S5_GROUP, S5_CHUNK
    nc, LP = T // L, L * P
    tm = min(512, T)
    u = pl.pallas_call(
        _s5_split_kernel,
        grid=(T // tm,),
        in_specs=[pl.BlockSpec((tm, S5_W), lambda i: (i, 0))],
        out_specs=pl.BlockSpec((G, tm, P), lambda i: (0, i, 0)),
        out_shape=jax.ShapeDtypeStruct((G, T, P), f32),
        compiler_params=_cp(("parallel",)),
        name="s5_split",
    )(z_tail).reshape(G, nc, LP)
    grp = lambda s: pl.BlockSpec((2,) + s, lambda j: (j, 0, 0))
    one = lambda s: pl.BlockSpec((1,) + s, lambda j: (j, 0, 0))
    lanes = pl.BlockSpec((nc, 2 * N), lambda j: (0, j))
    y_in, xr, xi = pl.pallas_call(
        _s5_in_kernel,
        grid=(G // 2,),
        in_specs=[grp((nc, LP)), grp((P, LP)), one((2 * LP, 2 * N)), one((2 * LP, 2 * N))],
        out_specs=[grp((nc, LP)), lanes, lanes],
        out_shape=[jax.ShapeDtypeStruct((G, nc, LP), f32), jax.ShapeDtypeStruct((nc, G * N), f32),
                   jax.ShapeDtypeStruct((nc, G * N), f32)],
        compiler_params=_cp(("parallel",)),
        name="s5_in",
    )(u, tab["krow"], tab["bsr"], tab["bsi"])
    whole = lambda s: pl.BlockSpec(s, lambda: (0,) * len(s))
    sr, si = pl.pallas_call(
        _s5_scan_kernel,
        in_specs=[whole((nc, G * N)), whole((nc, G * N)), whole((1, G * N)), whole((1, G * N))],
        out_specs=[whole((nc, G * N)), whole((nc, G * N))],
        out_shape=[jax.ShapeDtypeStruct((nc, G * N), f32)] * 2,
        compiler_params=pltpu.CompilerParams(vmem_limit_bytes=VMEM_LIMIT),
        name="s5_scan",
    )(xr, xi, tab["al_re"], tab["al_im"])
    y = pl.pallas_call(
        _s5_out_kernel,
        grid=(G // 2,),
        in_specs=[grp((nc, LP)), lanes, lanes, grp((2 * N, LP)), grp((2 * N, LP)), grp((nc, LP)), grp((1, LP))],
        out_specs=grp((nc, LP)),
        out_shape=jax.ShapeDtypeStruct((G, nc, LP), f32),
        compiler_params=_cp(("parallel",)),
        name="s5_out",
    )(y_in, sr, si, tab["cr"], tab["ci"], u, tab["d"])
    return pl.pallas_call(
        _glu_kernel,
        grid=(T // tm,),
        in_specs=[pl.BlockSpec((G, tm, P), lambda i: (0, i, 0)), pl.BlockSpec((S5_W, S5_W), lambda i: (0, 0)),
                  pl.BlockSpec((1, S5_W), lambda i: (0, 0))],
        out_specs=pl.BlockSpec((tm, S5_W), lambda i: (i, 0)),
        out_shape=jax.ShapeDtypeStruct((T, S5_W), bf16),
        compiler_params=_cp(("parallel",)),
        name="s5_glu",
    )(y.reshape(G, T, P), glu_w, glu_b.reshape(1, S5_W))


def _pad_cols(w, n):
    return jnp.pad(w, ((0, 0), (0, n - w.shape[1])))


def _pad_rows(w, n):
    return jnp.pad(w, ((0, n - w.shape[0]), (0, 0)))


def _split_in_proj(w):
    r0 = M_IN
    l0 = r0 + 3 * RWKV_W
    s0 = M_IN + RWKV_IN
    g0 = s0 + S5_W
    main = jnp.concatenate([
        w[:, :4 * MLSTM_W], w[:, r0:l0],
        _pad_cols(w[:, l0:l0 + DECAY_LORA], LORA_PAD),
        _pad_cols(w[:, l0 + DECAY_LORA:l0 + DECAY_LORA + AAA_LORA], LORA_PAD),
        w[:, l0 + DECAY_LORA + AAA_LORA:s0], w[:, g0:]], axis=1).astype(bf16)
    tail = _pad_cols(jnp.concatenate([w[:, s0:g0], w[:, 4 * MLSTM_W:M_IN]], axis=1), Z_TAIL).astype(bf16)
    return main, tail


def _split_mu(mu):
    l0 = 3 * RWKV_W
    return jnp.concatenate([
        mu[:l0], jnp.pad(mu[l0:l0 + DECAY_LORA], (0, LORA_PAD - DECAY_LORA)),
        jnp.pad(mu[l0 + DECAY_LORA:l0 + DECAY_LORA + AAA_LORA], (0, LORA_PAD - AAA_LORA)),
        mu[l0 + DECAY_LORA + AAA_LORA:]]).reshape(1, RWKV_PAD)


def kernel(x, p, norm_mix_g, w_in, mlstm_conv, mlstm_ib, mlstm_fb, mlstm_norm_g, rwkv_mu, rwkv_w0, rwkv_w2, rwkv_a0, rwkv_a2, rwkv_g2, rwkv_kk, rwkv_ka, rwkv_rk, rwkv_ln_g, rwkv_ln_b, s5_a_re, s5_a_im, s5_log_dt, s5_b_re, s5_b_im, s5_c_re, s5_c_im, s5_d, s5_glu_w, s5_glu_b, w_up_m, w_up_r, w_up_s, w_out, norm_ffn_g, ffn_w_gate, ffn_w_up, ffn_w_down, norm_ple_g, ple_w_gate, ple_w_proj, final_norm_g):
    B, T, D = x.shape
    depth = w_in.shape[0]
    s5_tab = _s5_tables(s5_a_re, s5_a_im, s5_log_dt, s5_b_re, s5_b_im, s5_c_re, s5_c_im, s5_d)
    outs = []
    for bi in range(B):
        h = x[bi]
        xn = _rmsnorm(h, norm_mix_g[0], bf16)
        for i in range(depth):
            w_main, w_tail = _split_in_proj(w_in[i])
            z_main = _proj(xn, w_main, 1024)
            z_tail = _proj(xn, w_tail, Z_TAIL)

            gate_b = jnp.pad(jnp.concatenate([mlstm_ib[i], mlstm_fb[i]]), (0, 128 - 2 * MLSTM_HEADS)).reshape(1, 128)
            y_m = _mlstm(z_main, z_tail, mlstm_conv[i], gate_b, mlstm_norm_g[i])

            vec = lambda a: a.reshape(1, RWKV_W)
            parts = _rwkv_prep(z_main, _split_mu(rwkv_mu[i]), vec(rwkv_w0[i]), vec(rwkv_a0[i]), vec(rwkv_kk[i]),
                               vec(rwkv_ka[i]), vec(rwkv_rk[i]), _pad_rows(rwkv_w2[i], LORA_PAD).astype(bf16),
                               _pad_rows(rwkv_a2[i], LORA_PAD).astype(bf16), rwkv_g2[i].astype(bf16))
            y_r = _rwkv_scan(parts, rwkv_ln_g[i], rwkv_ln_b[i])

            y_s = _s5(z_tail, {k: v[i] for k, v in s5_tab.items()}, s5_glu_w[i].astype(bf16), s5_glu_b[i])

            mixed = _merge(y_m, y_r, y_s, z_main, w_up_m[i], w_up_r[i], w_up_s[i])
            h, hn = _resid_norm(mixed, w_out[i].astype(bf16), h, norm_ffn_g[i], D_MODEL)
            act = _ffn_up(hn, ffn_w_gate[i], ffn_w_up[i])
            h, hp = _resid_norm(act, ffn_w_down[i].astype(bf16), h, norm_ple_g[i], FFN_HIDDEN // 4)
            last = i == depth - 1
            g_next = final_norm_g if last else norm_mix_g[i + 1]
            h, xn = _ple(hp, p[i, bi].astype(bf16), ple_w_gate[i].astype(bf16), ple_w_proj[i].astype(bf16), h, g_next,
                         f32 if last else bf16)
        outs.append(xn)
    return jnp.stack(outs)
```

```python
import functools

import jax
import jax.numpy as jnp
import numpy as np
from jax import lax
from jax.experimental import pallas as pl
from jax.experimental.pallas import tpu as pltpu

f32 = jnp.float32
bf16 = jnp.bfloat16
HI = lax.Precision.HIGHEST

D_MODEL = 2048
PLE_DIM = 256
RMS_EPS = 1e-6
MLSTM_HEADS = 4
MLSTM_HEAD_DIM = 256
MLSTM_W = 1024
MLSTM_CHUNK = 256
GATE_SOFTCAP = 15.0
RWKV_HEADS = 8
RWKV_HEAD_DIM = 64
RWKV_W = 512
DECAY_LORA = 96
AAA_LORA = 96
GATE_LORA = 256
LORA_PAD = 128
RWKV_GN_EPS = 64e-5
RWKV_CHUNK = 32
S5_GROUP = 16
S5_GROUPS = 32
S5_W = 512
S5_STATE = 64
S5_CHUNK = 16
FFN_HIDDEN = 5632
M_IN = 4 * MLSTM_W + 2 * MLSTM_HEADS
RWKV_IN = 3 * RWKV_W + DECAY_LORA + AAA_LORA + GATE_LORA
RWKV_PAD = 3 * RWKV_W + 2 * LORA_PAD + GATE_LORA
Z_MAIN = 4 * MLSTM_W + RWKV_PAD + 3 * D_MODEL
Z_TAIL = S5_W + 128
VMEM_LIMIT = 56 * 1024 * 1024


def _cp(sem):
    return pltpu.CompilerParams(dimension_semantics=sem, vmem_limit_bytes=VMEM_LIMIT)


def _nt(a, b, **kw):
    return lax.dot_general(a, b, (((1,), (1,)), ((), ())), preferred_element_type=f32, **kw)


def _tn(a, b, **kw):
    return lax.dot_general(a, b, (((0,), (0,)), ((), ())), preferred_element_type=f32, **kw)


def _mm(a, b, **kw):
    return jnp.dot(a, b, preferred_element_type=f32, **kw)


def _bmm(a, b):
    return jnp.dot(a.astype(bf16), b.astype(bf16), preferred_element_type=f32)


def _sigmoid(x):
    return 1.0 / (1.0 + jnp.exp(-x))


def _rms(x, g):
    return x * lax.rsqrt(jnp.mean(x * x, axis=-1, keepdims=True) + RMS_EPS) * g


def _norm_kernel(x_ref, g_ref, o_ref):
    o_ref[...] = _rms(x_ref[...], g_ref[...]).astype(o_ref.dtype)


def _rmsnorm(x, g, out_dtype):
    T, D = x.shape
    tm = min(512, T)
    return pl.pallas_call(
        _norm_kernel,
        grid=(T // tm,),
        in_specs=[pl.BlockSpec((tm, D), lambda i: (i, 0)), pl.BlockSpec((1, D), lambda i: (0, 0))],
        out_specs=pl.BlockSpec((tm, D), lambda i: (i, 0)),
        out_shape=jax.ShapeDtypeStruct((T, D), out_dtype),
        compiler_params=_cp(("parallel",)),
        name="rmsnorm",
    )(x, g.reshape(1, D))


def _proj_kernel(a_ref, w_ref, o_ref):
    o_ref[...] = _mm(a_ref[...], w_ref[...]).astype(o_ref.dtype)


def _proj(a, w, tn, out_dtype=f32):
    T, K = a.shape
    N = w.shape[1]
    tm = min(1024, T)
    return pl.pallas_call(
        _proj_kernel,
        grid=(N // tn, T // tm),
        in_specs=[pl.BlockSpec((tm, K), lambda j, i: (i, 0)), pl.BlockSpec((K, tn), lambda j, i: (0, j))],
        out_specs=pl.BlockSpec((tm, tn), lambda j, i: (i, j)),
        out_shape=jax.ShapeDtypeStruct((T, N), out_dtype),
        compiler_params=_cp(("parallel", "parallel")),
        name="in_proj",
    )(a, w)


def _merge_kernel(ym_ref, yr_ref, ys_ref, gm_ref, gr_ref, gs_ref, um_ref, ur_ref, us_ref, o_ref, umb_ref, urb_ref, usb_ref):
    @pl.when(pl.program_id(1) == 0)
    def _():
        umb_ref[...] = um_ref[...].astype(bf16)
        urb_ref[...] = ur_ref[...].astype(bf16)
        usb_ref[...] = us_ref[...].astype(bf16)

    acc = _sigmoid(gm_ref[...]) * _mm(ym_ref[...], umb_ref[...])
    acc += _sigmoid(gr_ref[...]) * _mm(yr_ref[...], urb_ref[...])
    acc += _sigmoid(gs_ref[...]) * _mm(ys_ref[...], usb_ref[...])
    o_ref[...] = acc.astype(o_ref.dtype)


def _merge(ym, yr, ys, z_main, um, ur, us, layer):
    T = ym.shape[0]
    tm, tn = min(512, T), 1024
    nj = D_MODEL // tn
    g0 = (4 * MLSTM_W + RWKV_PAD) // tn

    def gate_spec(b):
        return pl.BlockSpec((tm, tn), lambda j, i, b=b: (i, g0 + b * nj + j))

    def y_spec(w):
        return pl.BlockSpec((tm, w), lambda j, i: (i, 0))

    def u_spec(w):
        return pl.BlockSpec((None, w, tn), lambda j, i: (layer, 0, j))

    return pl.pallas_call(
        _merge_kernel,
        grid=(nj, T // tm),
        in_specs=[y_spec(MLSTM_W), y_spec(RWKV_W), y_spec(S5_W), gate_spec(0), gate_spec(1), gate_spec(2),
                  u_spec(MLSTM_W), u_spec(RWKV_W), u_spec(S5_W)],
        out_specs=pl.BlockSpec((tm, tn), lambda j, i: (i, j)),
        out_shape=jax.ShapeDtypeStruct((T, D_MODEL), bf16),
        scratch_shapes=[pltpu.VMEM((MLSTM_W, tn), bf16), pltpu.VMEM((RWKV_W, tn), bf16), pltpu.VMEM((S5_W, tn), bf16)],
        compiler_params=_cp(("parallel", "arbitrary")),
        name="gated_merge",
    )(ym, yr, ys, z_main, z_main, z_main, um, ur, us)


def _resid_norm_kernel(a_ref, w_ref, h_ref, g_ref, ho_ref, no_ref, acc_ref):
    k = pl.program_id(1)

    @pl.when(k == 0)
    def _():
        acc_ref[...] = h_ref[...]

    acc_ref[...] += _mm(a_ref[...], w_ref[...])

    @pl.when(k == pl.num_programs(1) - 1)
    def _():
        hn = acc_ref[...]
        ho_ref[...] = hn
        no_ref[...] = _rms(hn, g_ref[...]).astype(no_ref.dtype)


def _resid_norm(a, w, h, g, tk, norm_dtype=bf16):
    T, K = a.shape
    tm = min(512, T)
    return pl.pallas_call(
        _resid_norm_kernel,
        grid=(T // tm, K // tk),
        in_specs=[pl.BlockSpec((tm, tk), lambda i, k: (i, k)), pl.BlockSpec((tk, D_MODEL), lambda i, k: (k, 0)),
                  pl.BlockSpec((tm, D_MODEL), lambda i, k: (i, 0)), pl.BlockSpec((1, D_MODEL), lambda i, k: (0, 0))],
        out_specs=[pl.BlockSpec((tm, D_MODEL), lambda i, k: (i, 0)), pl.BlockSpec((tm, D_MODEL), lambda i, k: (i, 0))],
        out_shape=[jax.ShapeDtypeStruct((T, D_MODEL), f32), jax.ShapeDtypeStruct((T, D_MODEL), norm_dtype)],
        scratch_shapes=[pltpu.VMEM((tm, D_MODEL), f32)],
        compiler_params=_cp(("parallel", "arbitrary")),
        name="resid_matmul_norm",
    )(a, w, h, g.reshape(1, D_MODEL))


def _ffn_up_kernel(a_ref, wg_ref, wu_ref, o_ref, wgb_ref, wub_ref):
    @pl.when(pl.program_id(1) == 0)
    def _():
        wgb_ref[...] = wg_ref[...].astype(bf16)
        wub_ref[...] = wu_ref[...].astype(bf16)

    a = a_ref[...]
    gt = _mm(a, wgb_ref[...])
    up = _mm(a, wub_ref[...])
    o_ref[...] = (gt * _sigmoid(gt) * up).astype(o_ref.dtype)


def _ffn_up(a, wg, wu, layer):
    T, K = a.shape
    N = wg.shape[2]
    tm, tn = min(1024, T), 512
    return pl.pallas_call(
        _ffn_up_kernel,
        grid=(N // tn, T // tm),
        in_specs=[pl.BlockSpec((tm, K), lambda j, i: (i, 0)), pl.BlockSpec((None, K, tn), lambda j, i: (layer, 0, j)),
                  pl.BlockSpec((None, K, tn), lambda j, i: (layer, 0, j))],
        out_specs=pl.BlockSpec((tm, tn), lambda j, i: (i, j)),
        out_shape=jax.ShapeDtypeStruct((T, N), bf16),
        scratch_shapes=[pltpu.VMEM((K, tn), bf16), pltpu.VMEM((K, tn), bf16)],
        compiler_params=_cp(("parallel", "arbitrary")),
        name="ffn_up",
    )(a, wg, wu)


def _ple_kernel(hp_ref, p_ref, wg_ref, wp_ref, h_ref, g_ref, ho_ref, no_ref):
    gate = _sigmoid(_mm(hp_ref[...], wg_ref[...]))
    hn = h_ref[...] + _mm(p_ref[...], wp_ref[...]) * gate
    ho_ref[...] = hn
    no_ref[...] = _rms(hn, g_ref[...]).astype(no_ref.dtype)


def _ple(hp, p, wg, wp, h, g, norm_dtype):
    T = h.shape[0]
    tm = min(512, T)
    row = lambda w: pl.BlockSpec((tm, w), lambda i: (i, 0))
    full = lambda s: pl.BlockSpec(s, lambda i: (0, 0))
    return pl.pallas_call(
        _ple_kernel,
        grid=(T // tm,),
        in_specs=[row(D_MODEL), row(PLE_DIM), full((D_MODEL, D_MODEL)), full((PLE_DIM, D_MODEL)), row(D_MODEL),
                  full((1, D_MODEL))],
        out_specs=[row(D_MODEL), row(D_MODEL)],
        out_shape=[jax.ShapeDtypeStruct((T, D_MODEL), f32), jax.ShapeDtypeStruct((T, D_MODEL), norm_dtype)],
        compiler_params=_cp(("parallel",)),
        name="ple_norm",
    )(hp, p, wg, wp, h, g.reshape(1, D_MODEL))


def _mlstm_kernel(zq_ref, zk_ref, v_ref, o_ref, gz_ref, cq_ref, ck_ref, gb_ref, ng_ref, y_ref,
                  ct_ref, n_ref, qbuf_ref, kbuf_ref):
    h = pl.program_id(0)
    c = pl.program_id(1)
    L, dh = zq_ref.shape

    @pl.when(c == 0)
    def _():
        ct_ref[...] = jnp.zeros_like(ct_ref)
        n_ref[...] = jnp.zeros_like(n_ref)
        qbuf_ref[0:8, :] = jnp.zeros((8, dh), f32)
        kbuf_ref[0:8, :] = jnp.zeros((8, dh), f32)

    def conv_silu(z_ref, w_ref, buf_ref):
        x = z_ref[...]
        buf_ref[8:, :] = x
        w = w_ref[...]
        acc = x * w[0:1, :]
        for j in range(1, w.shape[0]):
            acc = acc + buf_ref[pl.ds(8 - j, L), :] * w[j:j + 1, :]
        buf_ref[0:8, :] = x[L - 8:, :]
        return acc * _sigmoid(acc)

    q = conv_silu(zq_ref, cq_ref, qbuf_ref) * (dh ** -0.5)
    k = conv_silu(zk_ref, ck_ref, kbuf_ref)
    v = v_ref[...]

    g = gz_ref[...] + gb_ref[...]
    sc = GATE_SOFTCAP * jnp.tanh(g / GATE_SOFTCAP)
    logf = -jnp.log(1.0 + jnp.exp(-sc))
    row = lax.broadcasted_iota(jnp.int32, (L, L), 0)
    col = lax.broadcasted_iota(jnp.int32, (L, L), 1)
    causal = row >= col
    bcs = _mm(causal.astype(f32), logf, precision=HI)
    lane = lax.broadcasted_iota(jnp.int32, g.shape, 1)
    b_sel = jnp.where(lane == MLSTM_HEADS + h, bcs, 0.0)
    e_sel = jnp.where(lane == h, sc, 0.0) - b_sel
    b_col = jnp.sum(b_sel, axis=1, keepdims=True)
    e_col = jnp.sum(e_sel, axis=1, keepdims=True)
    e_row = _nt(jnp.ones_like(e_sel), e_sel, precision=HI)
    wmat = jnp.where(causal, jnp.exp(jnp.where(causal, b_col + e_row, 0.0)), 0.0)
    b_end = b_col[L - 1:L, :]
    eb = jnp.exp(b_col)

    qb = q.astype(bf16)
    kb = k.astype(bf16)
    vb = v.astype(bf16)
    s = _nt(qb, kb) * wmat
    ct = ct_ref[...]
    n = n_ref[...]
    num = _mm(s.astype(bf16), vb) + eb * _mm(qb, ct.astype(bf16))
    den = jnp.sum(s, axis=1, keepdims=True) + eb * jnp.sum(q * n, axis=1, keepdims=True)
    hh = num / jnp.maximum(jnp.abs(den), 1.0)
    hh = hh * lax.rsqrt(jnp.mean(hh * hh, axis=1, keepdims=True) + RMS_EPS) * ng_ref[...]
    y_ref[...] = (_sigmoid(o_ref[...]) * hh).astype(y_ref.dtype)

    kw = k * jnp.exp(b_end + e_col)
    decay = jnp.exp(b_end)
    ct_ref[...] = decay * ct + _tn(kw.astype(bf16), vb)
    n_ref[...] = decay * n + jnp.sum(kw, axis=0, keepdims=True)


def _mlstm(z_main, z_tail, conv_w, gate_b, norm_g):
    T = z_main.shape[0]
    L, dh, H = min(MLSTM_CHUNK, T), MLSTM_HEAD_DIM, MLSTM_HEADS
    blk = lambda off: pl.BlockSpec((L, dh), lambda h, c, off=off: (c, off + h))
    return pl.pallas_call(
        _mlstm_kernel,
        grid=(H, T // L),
        in_specs=[blk(0), blk(H), blk(2 * H), blk(3 * H),
                  pl.BlockSpec((L, 128), lambda h, c: (c, S5_W // 128)),
                  pl.BlockSpec((4, dh), lambda h, c: (0, h)), pl.BlockSpec((4, dh), lambda h, c: (0, H + h)),
                  pl.BlockSpec((1, 128), lambda h, c: (0, 0)), pl.BlockSpec((1, dh), lambda h, c: (0, h))],
        out_specs=pl.BlockSpec((L, dh), lambda h, c: (c, h)),
        out_shape=jax.ShapeDtypeStruct((T, MLSTM_W), bf16),
        scratch_shapes=[pltpu.VMEM((dh, dh), f32), pltpu.VMEM((1, dh), f32),
                        pltpu.VMEM((L + 8, dh), f32), pltpu.VMEM((L + 8, dh), f32)],
        compiler_params=_cp(("arbitrary", "arbitrary")),
        name="mlstm",
    )(z_main, z_main, z_main, z_main, z_tail, conv_w, conv_w, gate_b, norm_g.reshape(1, MLSTM_W))


def _iota_div(shape, dim, width):
    return lax.shift_right_logical(lax.broadcasted_iota(jnp.int32, shape, dim), int(np.log2(width)))


def _head_ones(n, width):
    return (_iota_div((n, n), 0, width) == _iota_div((n, n), 1, width)).astype(f32)


def _rwkv_prep_kernel(z_ref, mu_ref, w0_ref, a0_ref, kkg_ref, ka_ref, rk_ref, w2_ref, a2_ref, g2_ref,
                      r_ref, ld_ref, k_ref, v_ref, kk_ref, b_ref, g_ref, bonus_ref, buf_ref):
    i = pl.program_id(0)
    tm = z_ref.shape[0]
    W = RWKV_W

    @pl.when(i == 0)
    def _():
        buf_ref[0:8, :] = jnp.zeros((8, buf_ref.shape[1]), f32)

    x = z_ref[...]
    buf_ref[8:, :] = x
    xs = x + (buf_ref[pl.ds(7, tm), :] - x) * mu_ref[...]
    buf_ref[0:8, :] = x[tm - 8:, :]
    r = xs[:, 0:W]
    k = xs[:, W:2 * W]
    v = xs[:, 2 * W:3 * W]
    wl = xs[:, 3 * W:3 * W + LORA_PAD]
    al = xs[:, 3 * W + LORA_PAD:3 * W + 2 * LORA_PAD]
    gl = xs[:, 3 * W + 2 * LORA_PAD:]
    t = w0_ref[...] + _bmm(jnp.tanh(wl), w2_ref[...])
    w = -(jnp.maximum(-t, 0.0) + jnp.log(1.0 + jnp.exp(-jnp.abs(t)))) - 0.5
    a = _sigmoid(a0_ref[...] + _bmm(al, a2_ref[...]))
    g = _bmm(_sigmoid(gl), g2_ref[...])
    ones = _head_ones(W, RWKV_HEAD_DIM)
    kk = k * kkg_ref[...]
    ss = _mm(kk * kk, ones, precision=HI)
    kk = kk / jnp.maximum(jnp.sqrt(ss), 1e-12)
    k2 = k * (1.0 + (a - 1.0) * ka_ref[...])
    bonus = _mm(r * k2 * rk_ref[...], ones, precision=HI) * v
    r_ref[...] = r
    ld_ref[...] = -jnp.exp(w)
    k_ref[...] = k2
    v_ref[...] = v
    kk_ref[...] = kk
    b_ref[...] = kk * a
    g_ref[...] = g
    bonus_ref[...] = bonus


def _rwkv_prep(z_main, mu, w0, a0, kkg, ka, rk, w2, a2, g2):
    T = z_main.shape[0]
    tm = min(256, T)
    W = RWKV_W
    vec = lambda n: pl.BlockSpec((1, n), lambda i: (0, 0))
    mat = lambda s: pl.BlockSpec(s, lambda i: (0, 0))
    out = pl.BlockSpec((tm, W), lambda i: (i, 0))
    return pl.pallas_call(
        _rwkv_prep_kernel,
        grid=(T // tm,),
        in_specs=[pl.BlockSpec((tm, RWKV_PAD), lambda i: (i, 4 * MLSTM_W // RWKV_PAD)), vec(RWKV_PAD), vec(W), vec(W),
                  vec(W), vec(W), vec(W), mat((LORA_PAD, W)), mat((LORA_PAD, W)), mat((GATE_LORA, W))],
        out_specs=[out] * 8,
        out_shape=[jax.ShapeDtypeStruct((T, W), f32)] * 8,
        scratch_shapes=[pltpu.VMEM((tm + 8, RWKV_PAD), f32)],
        compiler_params=_cp(("arbitrary",)),
        name="rwkv_prep",
    )(z_main, mu, w0, a0, kkg, ka, rk, w2, a2, g2)


def _rwkv_scan_kernel(r_ref, ld_ref, k_ref, v_ref, kk_ref, b_ref, g_ref, bonus_ref, lng_ref, lnb_ref, y_ref, s_ref):
    c = pl.program_id(0)
    L, W = r_ref.shape
    H, dh = RWKV_HEADS, RWKV_HEAD_DIM
    R = H * L

    @pl.when(c == 0)
    def _():
        s_ref[...] = jnp.zeros_like(s_ref)

    ld = ld_ref[...]
    tr = lax.broadcasted_iota(jnp.int32, (L, L), 0)
    tc = lax.broadcasted_iota(jnp.int32, (L, L), 1)
    lg = _mm((tr >= tc).astype(f32), ld, precision=HI)
    g_end = lg[L - 1:L, :]
    dec_in = jnp.exp(lg)
    dec_out = jnp.exp(-lg)
    dec_tail = jnp.exp(g_end - lg)
    kk = kk_ref[...]
    k = k_ref[...]
    b = b_ref[...]
    v = v_ref[...]
    kap = kk * jnp.exp(lg - ld)
    rt = r_ref[...] * dec_in
    kt = k * dec_out
    bt = b * dec_out

    head_mask = _iota_div((R, W), 0, L) == _iota_div((R, W), 1, dh)

    def tile(x):
        return jnp.concatenate([x] * H, axis=0)

    def stack(x):
        return jnp.where(head_mask, tile(x), 0.0)

    def unstack(x):
        acc = x[0:L, :]
        for hh in range(1, H):
            acc = acc + x[hh * L:(hh + 1) * L, :]
        return acc

    kap_s = stack(kap).astype(bf16)
    rt_s = stack(rt).astype(bf16)
    v_s = stack(v).astype(bf16)
    kt_t = tile(kt).astype(bf16)
    bt_t = tile(bt).astype(bf16)
    rr = lax.broadcasted_iota(jnp.int32, (R, R), 0)
    cc = lax.broadcasted_iota(jnp.int32, (R, R), 1)
    same = _iota_div((R, R), 0, L) == _iota_div((R, R), 1, L)
    strict = same & (rr > cc)
    incl = same & (rr >= cc)
    a_kb = jnp.where(strict, _nt(kap_s, bt_t), 0.0)
    a_kk = jnp.where(strict, _nt(kap_s, kt_t), 0.0)
    a_rk = jnp.where(incl, _nt(rt_s, kt_t), 0.0)
    a_rb = jnp.where(incl, _nt(rt_s, bt_t), 0.0)

    p = -a_kb
    tinv = jnp.where(rr == cc, 1.0, 0.0) + p
    step = 2
    while step < L:
        p = _bmm(p, p)
        tinv = tinv + _bmm(tinv, p)
        step *= 2

    s_mat = s_ref[...]
    s_b = s_mat.astype(bf16)
    pk = _nt(kap.astype(bf16), s_b)
    pr = _nt(rt.astype(bf16), s_b)
    y_s = stack(pk) + _bmm(a_kk, v_s)
    u_s = _bmm(tinv, y_s)
    o_s = stack(pr) + _bmm(a_rk, v_s) - _bmm(a_rb, u_s)
    o = unstack(o_s)
    u = unstack(u_s)

    khat = (k * dec_tail).astype(bf16)
    bhat = (b * dec_tail).astype(bf16)
    blk = _head_ones(W, dh)
    upd = _tn(v.astype(bf16), khat) - _tn(u.astype(bf16), bhat)
    s_ref[...] = s_mat * jnp.exp(g_end) + upd * blk

    mean = _mm(o, blk, precision=HI) * (1.0 / dh)
    d = o - mean
    var = _mm(d * d, blk, precision=HI) * (1.0 / dh)
    yn = d * lax.rsqrt(var + RWKV_GN_EPS) * lng_ref[...] + lnb_ref[...] + bonus_ref[...]
    y_ref[...] = (yn * g_ref[...]).astype(y_ref.dtype)


def _rwkv_scan(parts, ln_g, ln_b):
    r, ld, k, v, kk, b, g, bonus = parts
    T, W = r.shape
    L = RWKV_CHUNK
    blk = pl.BlockSpec((L, W), lambda c: (c, 0))
    vec = pl.BlockSpec((1, W), lambda c: (0, 0))
    return pl.pallas_call(
        _rwkv_scan_kernel,
        grid=(T // L,),
        in_specs=[blk] * 8 + [vec, vec],
        out_specs=blk,
        out_shape=jax.ShapeDtypeStruct((T, W), bf16),
        scratch_shapes=[pltpu.VMEM((W, W), f32)],
        compiler_params=_cp(("arbitrary",)),
        name="rwkv_scan",
    )(r, ld, k, v, kk, b, g, bonus, ln_g.reshape(1, W), ln_b.reshape(1, W))


def _s5_dot(a, b):
    return jnp.dot(a.astype(bf16), b.astype(bf16), preferred_element_type=f32)


def _transpose_pieces(w):
    w = list(w)
    lane = lax.broadcasted_iota(jnp.int32, w[0].shape, 1)
    for d in (4, 2, 1):
        width = S5_GROUP * d
        low = (lane & width) == 0
        for i in range(8):
            if i & d == 0:
                a, b = w[i], w[i + d]
                w[i] = jnp.where(low, a, pltpu.roll(b, width, 1))
                w[i + d] = jnp.where(low, pltpu.roll(a, 128 - width, 1), b)
    return w


def _s5_split_kernel(*refs):
    z_refs, u_ref = refs[:-1], refs[-1]
    ncb = u_ref.shape[1]
    for j, z_ref in enumerate(z_refs):
        for b in range(S5_CHUNK // 8):
            w = [z_ref[pl.ds(8 * b + i, ncb, stride=S5_CHUNK), :] for i in range(8)]
            o = _transpose_pieces(w)
            for a in range(8):
                u_ref[8 * j + a, :, 128 * b:128 * (b + 1)] = o[a]


def _toeplitz(krow):
    P, LP = krow.shape
    lane = lax.broadcasted_iota(jnp.int32, (P, LP), 1)
    blocks = [krow]
    for s in range(1, LP // P):
        blocks.append(jnp.where(lane >= P * s, pltpu.roll(krow, P * s, 1), 0.0))
    return jnp.concatenate(blocks, axis=0)


def _s5_in_kernel(u_ref, krow_ref, br_ref, bi_ref, y_ref, xr_ref, xi_ref):
    for j in range(2):
        y_ref[j] = _s5_dot(u_ref[j], _toeplitz(krow_ref[j]))
    u2 = jnp.concatenate([u_ref[0], u_ref[1]], axis=1)
    xr_ref[...] = _s5_dot(u2, br_ref[0])
    xi_ref[...] = _s5_dot(u2, bi_ref[0])


def _s5_scan_kernel(xr_ref, xi_ref, ar_ref, ai_ref, sr_ref, si_ref):
    nc = xr_ref.shape[0]
    ar = ar_ref[...]
    ai = ai_ref[...]

    def body(c, carry):
        sr, si = carry
        sr_ref[pl.ds(c, 1), :] = sr
        si_ref[pl.ds(c, 1), :] = si
        xr = xr_ref[pl.ds(c, 1), :]
        xi = xi_ref[pl.ds(c, 1), :]
        return ar * sr - ai * si + xr, ar * si + ai * sr + xi

    z = jnp.zeros_like(ar)
    lax.fori_loop(0, nc, body, (z, z))


def _s5_out_kernel(y_ref, sr_ref, si_ref, cr_ref, ci_ref, u_ref, d_ref, o_ref):
    sr = sr_ref[...]
    si = si_ref[...]
    for j in range(2):
        y = y_ref[j] + _s5_dot(sr, cr_ref[j]) + _s5_dot(si, ci_ref[j]) + d_ref[j] * u_ref[j]
        o_ref[j] = 0.5 * y * (1.0 + jnp.tanh(0.7978845608028654 * (y + 0.044715 * (y * y * y))))


def _glu_kernel(y_ref, w_ref, b_ref, o_ref, *nat_refs):
    ncb = y_ref.shape[1]
    for j, nat_ref in enumerate(nat_refs):
        for b in range(S5_CHUNK // 8):
            o = [y_ref[8 * j + a, :, 128 * b:128 * (b + 1)] for a in range(8)]
            w = _transpose_pieces(o)
            for i in range(8):
                nat_ref[pl.ds(8 * b + i, ncb, stride=S5_CHUNK), :] = w[i]
    y = jnp.concatenate([r[...] for r in nat_refs], axis=1)
    o_ref[...] = (y * _sigmoid(_bmm(y, w_ref[...]) + b_ref[...])).astype(o_ref.dtype)


def _s5_tables(a_re, a_im, log_dt, b_re, b_im, c_re, c_im, d):
    G, N, P, L = S5_GROUPS, S5_STATE, S5_GROUP, S5_CHUNK
    D = a_re.shape[0]
    dt = jnp.exp(log_dt)[:, None, :, None]
    lags = jnp.arange(L + 1, dtype=f32)[None, :, None, None]
    mag = jnp.exp(a_re[:, None] * dt * lags)
    ang = a_im[:, None] * dt * lags
    pw_re, pw_im = mag * jnp.cos(ang), mag * jnp.sin(ang)
    den = a_re * a_re + a_im * a_im
    nr, ni = pw_re[:, 1] - 1.0, pw_im[:, 1]
    coef_re = ((nr * a_re + ni * a_im) / den)[..., None]
    coef_im = ((ni * a_re - nr * a_im) / den)[..., None]
    bb_re = coef_re * b_re - coef_im * b_im
    bb_im = coef_re * b_im + coef_im * b_re
    rep = lambda a, axis: jnp.repeat(a, P, axis=axis)
    pl_re, pl_im = rep(pw_re[:, :L].transpose(0, 2, 1, 3), 2), rep(pw_im[:, :L].transpose(0, 2, 1, 3), 2)
    ct_re, ct_im = jnp.tile(c_re, (1, 1, L, 1)), jnp.tile(c_im, (1, 1, L, 1))
    cb_re, cb_im = ct_re * pl_re - ct_im * pl_im, ct_re * pl_im + ct_im * pl_re
    krow = (jnp.einsum('dgxn,dgnq->dgqx', cb_re, bb_re, precision=HI)
            - jnp.einsum('dgxn,dgnq->dgqx', cb_im, bb_im, precision=HI))
    e_re = rep(pw_re[:, L - 1::-1][:, :L].transpose(0, 2, 1, 3), 2)
    e_im = rep(pw_im[:, L - 1::-1][:, :L].transpose(0, 2, 1, 3), 2)
    bq_re = jnp.tile(bb_re.transpose(0, 1, 3, 2), (1, 1, L, 1))
    bq_im = jnp.tile(bb_im.transpose(0, 1, 3, 2), (1, 1, L, 1))
    bs_re = (e_re * bq_re - e_im * bq_im).reshape(D, G // 2, 2, L * P, N)
    bs_im = (e_re * bq_im + e_im * bq_re).reshape(D, G // 2, 2, L * P, N)

    def pair_in(bs):
        zz = jnp.zeros_like(bs[:, :, 0])
        return jnp.concatenate([jnp.concatenate([bs[:, :, 0], zz], axis=3), jnp.concatenate([zz, bs[:, :, 1]], axis=3)], axis=2)

    f_re = rep(pw_re[:, 1:].transpose(0, 2, 3, 1), 3)
    f_im = rep(pw_im[:, 1:].transpose(0, 2, 3, 1), 3)
    cn_re = jnp.tile(c_re.transpose(0, 1, 3, 2), (1, 1, 1, L))
    cn_im = jnp.tile(c_im.transpose(0, 1, 3, 2), (1, 1, 1, L))
    cs_re = (cn_re * f_re - cn_im * f_im).reshape(D, G // 2, 2, N, L * P)
    cs_im = (-(cn_re * f_im + cn_im * f_re)).reshape(D, G // 2, 2, N, L * P)

    def pair_out(cs):
        zz = jnp.zeros_like(cs[:, :, 0])
        top = jnp.concatenate([cs[:, :, 0], zz], axis=2)[:, :, None]
        bot = jnp.concatenate([zz, cs[:, :, 1]], axis=2)[:, :, None]
        return jnp.concatenate([top, bot], axis=2).reshape(D, G, 2 * N, L * P)

    dd = jnp.tile(d.reshape(D, G, 1, P), (1, 1, 1, L))
    return dict(krow=krow, bsr=pair_in(bs_re), bsi=pair_in(bs_im), cr=pair_out(cs_re), ci=pair_out(cs_im),
                al_re=pw_re[:, L].reshape(D, 1, G * N), al_im=pw_im[:, L].reshape(D, 1, G * N), d=dd)


def _s5(z_tail, tab, glu_w, glu_b):
    T = z_tail.shape[0]
    G, N, P, L = S5_GROUPS, S5_STATE, S5_GROUP, S5_CHUNK
    nc, LP = T // L, L * P
    tm = min(1024, T)
    u = pl.pallas_call(
        _s5_split_kernel,
        grid=(T // tm,),
        in_specs=[pl.BlockSpec((tm, 128), lambda i, j=j: (i, j)) for j in range(S5_W // 128)],
        out_specs=pl.BlockSpec((G, tm // L, LP), lambda i: (0, i, 0)),
        out_shape=jax.ShapeDtypeStruct((G, nc, LP), f32),
        compiler_params=_cp(("parallel",)),
        name="s5_split",
    )(*[z_tail] * (S5_W // 128))
    grp = lambda s: pl.BlockSpec((2,) + s, lambda j: (j, 0, 0))
    one = lambda s: pl.BlockSpec((1,) + s, lambda j: (j, 0, 0))
    lanes = pl.BlockSpec((nc, 2 * N), lambda j: (0, j))
    y_in, xr, xi = pl.pallas_call(
        _s5_in_kernel,
        grid=(G // 2,),
        in_specs=[grp((nc, LP)), grp((P, LP)), one((2 * LP, 2 * N)), one((2 * LP, 2 * N))],
        out_specs=[grp((nc, LP)), lanes, lanes],
        out_shape=[jax.ShapeDtypeStruct((G, nc, LP), f32), jax.ShapeDtypeStruct((nc, G * N), f32),
                   jax.ShapeDtypeStruct((nc, G * N), f32)],
        compiler_params=_cp(("parallel",)),
        name="s5_in",
    )(u, tab["krow"], tab["bsr"], tab["bsi"])
    whole = lambda s: pl.BlockSpec(s, lambda: (0,) * len(s))
    sr, si = pl.pallas_call(
        _s5_scan_kernel,
        in_specs=[whole((nc, G * N)), whole((nc, G * N)), whole((1, G * N)), whole((1, G * N))],
        out_specs=[whole((nc, G * N)), whole((nc, G * N))],
        out_shape=[jax.ShapeDtypeStruct((nc, G * N), f32)] * 2,
        compiler_params=pltpu.CompilerParams(vmem_limit_bytes=VMEM_LIMIT),
        name="s5_scan",
    )(xr, xi, tab["al_re"], tab["al_im"])
    y = pl.pallas_call(
        _s5_out_kernel,
        grid=(G // 2,),
        in_specs=[grp((nc, LP)), lanes, lanes, grp((2 * N, LP)), grp((2 * N, LP)), grp((nc, LP)), grp((1, LP))],
        out_specs=grp((nc, LP)),
        out_shape=jax.ShapeDtypeStruct((G, nc, LP), f32),
        compiler_params=_cp(("parallel",)),
        name="s5_out",
    )(y_in, sr, si, tab["cr"], tab["ci"], u, tab["d"])
    return pl.pallas_call(
        _glu_kernel,
        grid=(T // tm,),
        in_specs=[pl.BlockSpec((G, tm // L, LP), lambda i: (0, i, 0)), pl.BlockSpec((S5_W, S5_W), lambda i: (0, 0)),
                  pl.BlockSpec((1, S5_W), lambda i: (0, 0))],
        out_specs=pl.BlockSpec((tm, S5_W), lambda i: (i, 0)),
        out_shape=jax.ShapeDtypeStruct((T, S5_W), bf16),
        scratch_shapes=[pltpu.VMEM((tm, 128), f32)] * (S5_W // 128),
        compiler_params=_cp(("parallel",)),
        name="s5_glu",
    )(y, glu_w, glu_b.reshape(1, S5_W))


def _pad_cols(w, n):
    return jnp.pad(w, ((0, 0), (0, n - w.shape[1])))


def _pad_rows(w, n):
    return jnp.pad(w, ((0, n - w.shape[0]), (0, 0)))


def _split_in_proj(w):
    r0 = M_IN
    l0 = r0 + 3 * RWKV_W
    s0 = M_IN + RWKV_IN
    g0 = s0 + S5_W
    main = jnp.concatenate([
        w[:, :4 * MLSTM_W], w[:, r0:l0],
        _pad_cols(w[:, l0:l0 + DECAY_LORA], LORA_PAD),
        _pad_cols(w[:, l0 + DECAY_LORA:l0 + DECAY_LORA + AAA_LORA], LORA_PAD),
        w[:, l0 + DECAY_LORA + AAA_LORA:s0], w[:, g0:]], axis=1).astype(bf16)
    tail = _pad_cols(jnp.concatenate([w[:, s0:g0], w[:, 4 * MLSTM_W:M_IN]], axis=1), Z_TAIL).astype(bf16)
    return main, tail


def _split_mu(mu):
    l0 = 3 * RWKV_W
    return jnp.concatenate([
        mu[:l0], jnp.pad(mu[l0:l0 + DECAY_LORA], (0, LORA_PAD - DECAY_LORA)),
        jnp.pad(mu[l0 + DECAY_LORA:l0 + DECAY_LORA + AAA_LORA], (0, LORA_PAD - AAA_LORA)),
        mu[l0 + DECAY_LORA + AAA_LORA:]]).reshape(1, RWKV_PAD)


def kernel(x, p, norm_mix_g, w_in, mlstm_conv, mlstm_ib, mlstm_fb, mlstm_norm_g, rwkv_mu, rwkv_w0, rwkv_w2, rwkv_a0, rwkv_a2, rwkv_g2, rwkv_kk, rwkv_ka, rwkv_rk, rwkv_ln_g, rwkv_ln_b, s5_a_re, s5_a_im, s5_log_dt, s5_b_re, s5_b_im, s5_c_re, s5_c_im, s5_d, s5_glu_w, s5_glu_b, w_up_m, w_up_r, w_up_s, w_out, norm_ffn_g, ffn_w_gate, ffn_w_up, ffn_w_down, norm_ple_g, ple_w_gate, ple_w_proj, final_norm_g):
    B, T, D = x.shape
    depth = w_in.shape[0]
    s5_tab = _s5_tables(s5_a_re, s5_a_im, s5_log_dt, s5_b_re, s5_b_im, s5_c_re, s5_c_im, s5_d)
    outs = []
    for bi in range(B):
        h = x[bi]
        xn = _rmsnorm(h, norm_mix_g[0], bf16)
        for i in range(depth):
            w_main, w_tail = _split_in_proj(w_in[i])
            z_main = _proj(xn, w_main, 1024)
            z_tail = _proj(xn, w_tail, Z_TAIL)

            gate_b = jnp.pad(jnp.concatenate([mlstm_ib[i], mlstm_fb[i]]), (0, 128 - 2 * MLSTM_HEADS)).reshape(1, 128)
            y_m = _mlstm(z_main, z_tail, mlstm_conv[i], gate_b, mlstm_norm_g[i])

            vec = lambda a: a.reshape(1, RWKV_W)
            parts = _rwkv_prep(z_main, _split_mu(rwkv_mu[i]), vec(rwkv_w0[i]), vec(rwkv_a0[i]), vec(rwkv_kk[i]),
                               vec(rwkv_ka[i]), vec(rwkv_rk[i]), _pad_rows(rwkv_w2[i], LORA_PAD).astype(bf16),
                               _pad_rows(rwkv_a2[i], LORA_PAD).astype(bf16), rwkv_g2[i].astype(bf16))
            y_r = _rwkv_scan(parts, rwkv_ln_g[i], rwkv_ln_b[i])

            y_s = _s5(z_tail, {k: v[i] for k, v in s5_tab.items()}, s5_glu_w[i].astype(bf16), s5_glu_b[i])

            mixed = _merge(y_m, y_r, y_s, z_main, w_up_m, w_up_r, w_up_s, i)
            h, hn = _resid_norm(mixed, w_out[i].astype(bf16), h, norm_ffn_g[i], D_MODEL)
            act = _ffn_up(hn, ffn_w_gate, ffn_w_up, i)
            h, hp = _resid_norm(act, ffn_w_down[i].astype(bf16), h, norm_ple_g[i], FFN_HIDDEN // 4)
            last = i == depth - 1
            g_next = final_norm_g if last else norm_mix_g[i + 1]
            h, xn = _ple(hp, p[i, bi].astype(bf16), ple_w_gate[i].astype(bf16), ple_w_proj[i].astype(bf16), h, g_next,
                         f32 if last else bf16)
        outs.append(xn)
    return jnp.stack(outs)
```

```python
import functools

import jax
import jax.numpy as jnp
import numpy as np
from jax import lax
from jax.experimental import pallas as pl
from jax.experimental.pallas import tpu as pltpu

f32 = jnp.float32
bf16 = jnp.bfloat16
HI = lax.Precision.HIGHEST

D_MODEL = 2048
PLE_DIM = 256
RMS_EPS = 1e-6
MLSTM_HEADS = 4
MLSTM_HEAD_DIM = 256
MLSTM_W = 1024
MLSTM_CHUNK = 256
GATE_SOFTCAP = 15.0
RWKV_HEADS = 8
RWKV_HEAD_DIM = 64
RWKV_W = 512
DECAY_LORA = 96
AAA_LORA = 96
GATE_LORA = 256
LORA_PAD = 128
RWKV_GN_EPS = 64e-5
RWKV_CHUNK = 64
S5_GROUP = 16
S5_GROUPS = 32
S5_W = 512
S5_STATE = 64
S5_CHUNK = 16
FFN_HIDDEN = 5632
M_IN = 4 * MLSTM_W + 2 * MLSTM_HEADS
RWKV_IN = 3 * RWKV_W + DECAY_LORA + AAA_LORA + GATE_LORA
RWKV_PAD = 3 * RWKV_W + 2 * LORA_PAD + GATE_LORA
Z_MAIN = 4 * MLSTM_W + RWKV_PAD + 3 * D_MODEL
Z_TAIL = S5_W + 128
VMEM_LIMIT = 56 * 1024 * 1024


def _cp(sem):
    return pltpu.CompilerParams(dimension_semantics=sem, vmem_limit_bytes=VMEM_LIMIT)


def _nt(a, b, **kw):
    return lax.dot_general(a, b, (((1,), (1,)), ((), ())), preferred_element_type=f32, **kw)


def _tn(a, b, **kw):
    return lax.dot_general(a, b, (((0,), (0,)), ((), ())), preferred_element_type=f32, **kw)


def _mm(a, b, **kw):
    return jnp.dot(a, b, preferred_element_type=f32, **kw)


def _bmm(a, b):
    return jnp.dot(a.astype(bf16), b.astype(bf16), preferred_element_type=f32)


def _sigmoid(x):
    return 1.0 / (1.0 + jnp.exp(-x))


def _rms(x, g):
    return x * lax.rsqrt(jnp.mean(x * x, axis=-1, keepdims=True) + RMS_EPS) * g


def _norm_kernel(x_ref, g_ref, o_ref):
    o_ref[...] = _rms(x_ref[...], g_ref[...]).astype(o_ref.dtype)


def _rmsnorm(x, g, out_dtype):
    T, D = x.shape
    tm = min(512, T)
    return pl.pallas_call(
        _norm_kernel,
        grid=(T // tm,),
        in_specs=[pl.BlockSpec((tm, D), lambda i: (i, 0)), pl.BlockSpec((1, D), lambda i: (0, 0))],
        out_specs=pl.BlockSpec((tm, D), lambda i: (i, 0)),
        out_shape=jax.ShapeDtypeStruct((T, D), out_dtype),
        compiler_params=_cp(("parallel",)),
        name="rmsnorm",
    )(x, g.reshape(1, D))


def _proj_kernel(a_ref, w_ref, o_ref):
    o_ref[...] = _mm(a_ref[...], w_ref[...]).astype(o_ref.dtype)


def _proj(a, w, tn, out_dtype=f32):
    T, K = a.shape
    N = w.shape[1]
    tm = min(1024, T)
    return pl.pallas_call(
        _proj_kernel,
        grid=(N // tn, T // tm),
        in_specs=[pl.BlockSpec((tm, K), lambda j, i: (i, 0)), pl.BlockSpec((K, tn), lambda j, i: (0, j))],
        out_specs=pl.BlockSpec((tm, tn), lambda j, i: (i, j)),
        out_shape=jax.ShapeDtypeStruct((T, N), out_dtype),
        compiler_params=_cp(("parallel", "parallel")),
        name="in_proj",
    )(a, w)


def _merge_kernel(ym_ref, yr_ref, ys_ref, gm_ref, gr_ref, gs_ref, um_ref, ur_ref, us_ref, o_ref, umb_ref, urb_ref, usb_ref):
    @pl.when(pl.program_id(1) == 0)
    def _():
        umb_ref[...] = um_ref[...].astype(bf16)
        urb_ref[...] = ur_ref[...].astype(bf16)
        usb_ref[...] = us_ref[...].astype(bf16)

    acc = _sigmoid(gm_ref[...]) * _mm(ym_ref[...], umb_ref[...])
    acc += _sigmoid(gr_ref[...]) * _mm(yr_ref[...], urb_ref[...])
    acc += _sigmoid(gs_ref[...]) * _mm(ys_ref[...], usb_ref[...])
    o_ref[...] = acc.astype(o_ref.dtype)


def _merge(ym, yr, ys, z_main, um, ur, us, layer):
    T = ym.shape[0]
    tm, tn = min(512, T), 1024
    nj = D_MODEL // tn
    g0 = (4 * MLSTM_W + RWKV_PAD) // tn

    def gate_spec(b):
        return pl.BlockSpec((tm, tn), lambda j, i, b=b: (i, g0 + b * nj + j))

    def y_spec(w):
        return pl.BlockSpec((tm, w), lambda j, i: (i, 0))

    def u_spec(w):
        return pl.BlockSpec((None, w, tn), lambda j, i: (layer, 0, j))

    return pl.pallas_call(
        _merge_kernel,
        grid=(nj, T // tm),
        in_specs=[y_spec(MLSTM_W), y_spec(RWKV_W), y_spec(S5_W), gate_spec(0), gate_spec(1), gate_spec(2),
                  u_spec(MLSTM_W), u_spec(RWKV_W), u_spec(S5_W)],
        out_specs=pl.BlockSpec((tm, tn), lambda j, i: (i, j)),
        out_shape=jax.ShapeDtypeStruct((T, D_MODEL), bf16),
        scratch_shapes=[pltpu.VMEM((MLSTM_W, tn), bf16), pltpu.VMEM((RWKV_W, tn), bf16), pltpu.VMEM((S5_W, tn), bf16)],
        compiler_params=_cp(("parallel", "arbitrary")),
        name="gated_merge",
    )(ym, yr, ys, z_main, z_main, z_main, um, ur, us)


def _resid_norm_kernel(a_ref, w_ref, h_ref, g_ref, ho_ref, no_ref, acc_ref):
    k = pl.program_id(1)

    @pl.when(k == 0)
    def _():
        acc_ref[...] = h_ref[...]

    acc_ref[...] += _mm(a_ref[...], w_ref[...])

    @pl.when(k == pl.num_programs(1) - 1)
    def _():
        hn = acc_ref[...]
        ho_ref[...] = hn
        no_ref[...] = _rms(hn, g_ref[...]).astype(no_ref.dtype)


def _resid_norm(a, w, h, g, tk, norm_dtype=bf16):
    T, K = a.shape
    tm = min(512, T)
    return pl.pallas_call(
        _resid_norm_kernel,
        grid=(T // tm, K // tk),
        in_specs=[pl.BlockSpec((tm, tk), lambda i, k: (i, k)), pl.BlockSpec((tk, D_MODEL), lambda i, k: (k, 0)),
                  pl.BlockSpec((tm, D_MODEL), lambda i, k: (i, 0)), pl.BlockSpec((1, D_MODEL), lambda i, k: (0, 0))],
        out_specs=[pl.BlockSpec((tm, D_MODEL), lambda i, k: (i, 0)), pl.BlockSpec((tm, D_MODEL), lambda i, k: (i, 0))],
        out_shape=[jax.ShapeDtypeStruct((T, D_MODEL), f32), jax.ShapeDtypeStruct((T, D_MODEL), norm_dtype)],
        scratch_shapes=[pltpu.VMEM((tm, D_MODEL), f32)],
        compiler_params=_cp(("parallel", "arbitrary")),
        name="resid_matmul_norm",
    )(a, w, h, g.reshape(1, D_MODEL))


def _ffn_up_kernel(a_ref, wg_ref, wu_ref, o_ref, wgb_ref, wub_ref):
    @pl.when(pl.program_id(1) == 0)
    def _():
        wgb_ref[...] = wg_ref[...].astype(bf16)
        wub_ref[...] = wu_ref[...].astype(bf16)

    a = a_ref[...]
    gt = _mm(a, wgb_ref[...])
    up = _mm(a, wub_ref[...])
    o_ref[...] = (gt * _sigmoid(gt) * up).astype(o_ref.dtype)


def _ffn_up(a, wg, wu, layer):
    T, K = a.shape
    N = wg.shape[2]
    tm, tn = min(1024, T), 512
    return pl.pallas_call(
        _ffn_up_kernel,
        grid=(N // tn, T // tm),
        in_specs=[pl.BlockSpec((tm, K), lambda j, i: (i, 0)), pl.BlockSpec((None, K, tn), lambda j, i: (layer, 0, j)),
                  pl.BlockSpec((None, K, tn), lambda j, i: (layer, 0, j))],
        out_specs=pl.BlockSpec((tm, tn), lambda j, i: (i, j)),
        out_shape=jax.ShapeDtypeStruct((T, N), bf16),
        scratch_shapes=[pltpu.VMEM((K, tn), bf16), pltpu.VMEM((K, tn), bf16)],
        compiler_params=_cp(("parallel", "arbitrary")),
        name="ffn_up",
    )(a, wg, wu)


def _ple_kernel(hp_ref, p_ref, wg_ref, wp_ref, h_ref, g_ref, ho_ref, no_ref):
    gate = _sigmoid(_mm(hp_ref[...], wg_ref[...]))
    hn = h_ref[...] + _mm(p_ref[...], wp_ref[...]) * gate
    ho_ref[...] = hn
    no_ref[...] = _rms(hn, g_ref[...]).astype(no_ref.dtype)


def _ple(hp, p, wg, wp, h, g, norm_dtype):
    T = h.shape[0]
    tm = min(512, T)
    row = lambda w: pl.BlockSpec((tm, w), lambda i: (i, 0))
    full = lambda s: pl.BlockSpec(s, lambda i: (0, 0))
    return pl.pallas_call(
        _ple_kernel,
        grid=(T // tm,),
        in_specs=[row(D_MODEL), row(PLE_DIM), full((D_MODEL, D_MODEL)), full((PLE_DIM, D_MODEL)), row(D_MODEL),
                  full((1, D_MODEL))],
        out_specs=[row(D_MODEL), row(D_MODEL)],
        out_shape=[jax.ShapeDtypeStruct((T, D_MODEL), f32), jax.ShapeDtypeStruct((T, D_MODEL), norm_dtype)],
        compiler_params=_cp(("parallel",)),
        name="ple_norm",
    )(hp, p, wg, wp, h, g.reshape(1, D_MODEL))


def _mlstm_kernel(zq_ref, zk_ref, v_ref, o_ref, gz_ref, cq_ref, ck_ref, gb_ref, ng_ref, y_ref,
                  ct_ref, n_ref, qbuf_ref, kbuf_ref):
    h = pl.program_id(0)
    c = pl.program_id(1)
    L, dh = zq_ref.shape

    @pl.when(c == 0)
    def _():
        ct_ref[...] = jnp.zeros_like(ct_ref)
        n_ref[...] = jnp.zeros_like(n_ref)
        qbuf_ref[0:8, :] = jnp.zeros((8, dh), f32)
        kbuf_ref[0:8, :] = jnp.zeros((8, dh), f32)

    def conv_silu(z_ref, w_ref, buf_ref):
        x = z_ref[...]
        buf_ref[8:, :] = x
        w = w_ref[...]
        acc = x * w[0:1, :]
        for j in range(1, w.shape[0]):
            acc = acc + buf_ref[pl.ds(8 - j, L), :] * w[j:j + 1, :]
        buf_ref[0:8, :] = x[L - 8:, :]
        return acc * _sigmoid(acc)

    q = conv_silu(zq_ref, cq_ref, qbuf_ref) * (dh ** -0.5)
    k = conv_silu(zk_ref, ck_ref, kbuf_ref)
    v = v_ref[...]

    g = gz_ref[...] + gb_ref[...]
    sc = GATE_SOFTCAP * jnp.tanh(g / GATE_SOFTCAP)
    logf = -jnp.log(1.0 + jnp.exp(-sc))
    row = lax.broadcasted_iota(jnp.int32, (L, L), 0)
    col = lax.broadcasted_iota(jnp.int32, (L, L), 1)
    causal = row >= col
    bcs = _mm(causal.astype(f32), logf, precision=HI)
    lane = lax.broadcasted_iota(jnp.int32, g.shape, 1)
    b_sel = jnp.where(lane == MLSTM_HEADS + h, bcs, 0.0)
    e_sel = jnp.where(lane == h, sc, 0.0) - b_sel
    b_col = jnp.sum(b_sel, axis=1, keepdims=True)
    e_col = jnp.sum(e_sel, axis=1, keepdims=True)
    e_row = _nt(jnp.ones_like(e_sel), e_sel, precision=HI)
    wmat = jnp.where(causal, jnp.exp(jnp.where(causal, b_col + e_row, 0.0)), 0.0)
    b_end = b_col[L - 1:L, :]
    eb = jnp.exp(b_col)

    qb = q.astype(bf16)
    kb = k.astype(bf16)
    vb = v.astype(bf16)
    s = _nt(qb, kb) * wmat
    ct = ct_ref[...]
    n = n_ref[...]
    num = _mm(s.astype(bf16), vb) + eb * _mm(qb, ct.astype(bf16))
    den = jnp.sum(s, axis=1, keepdims=True) + eb * jnp.sum(q * n, axis=1, keepdims=True)
    hh = num / jnp.maximum(jnp.abs(den), 1.0)
    hh = hh * lax.rsqrt(jnp.mean(hh * hh, axis=1, keepdims=True) + RMS_EPS) * ng_ref[...]
    y_ref[...] = (_sigmoid(o_ref[...]) * hh).astype(y_ref.dtype)

    kw = k * jnp.exp(b_end + e_col)
    decay = jnp.exp(b_end)
    ct_ref[...] = decay * ct + _tn(kw.astype(bf16), vb)
    n_ref[...] = decay * n + jnp.sum(kw, axis=0, keepdims=True)


def _mlstm(z_main, z_tail, conv_w, gate_b, norm_g):
    T = z_main.shape[0]
    L, dh, H = min(MLSTM_CHUNK, T), MLSTM_HEAD_DIM, MLSTM_HEADS
    blk = lambda off: pl.BlockSpec((L, dh), lambda h, c, off=off: (c, off + h))
    return pl.pallas_call(
        _mlstm_kernel,
        grid=(H, T // L),
        in_specs=[blk(0), blk(H), blk(2 * H), blk(3 * H),
                  pl.BlockSpec((L, 128), lambda h, c: (c, S5_W // 128)),
                  pl.BlockSpec((4, dh), lambda h, c: (0, h)), pl.BlockSpec((4, dh), lambda h, c: (0, H + h)),
                  pl.BlockSpec((1, 128), lambda h, c: (0, 0)), pl.BlockSpec((1, dh), lambda h, c: (0, h))],
        out_specs=pl.BlockSpec((L, dh), lambda h, c: (c, h)),
        out_shape=jax.ShapeDtypeStruct((T, MLSTM_W), bf16),
        scratch_shapes=[pltpu.VMEM((dh, dh), f32), pltpu.VMEM((1, dh), f32),
                        pltpu.VMEM((L + 8, dh), f32), pltpu.VMEM((L + 8, dh), f32)],
        compiler_params=_cp(("arbitrary", "arbitrary")),
        name="mlstm",
    )(z_main, z_main, z_main, z_main, z_tail, conv_w, conv_w, gate_b, norm_g.reshape(1, MLSTM_W))


def _iota_div(shape, dim, width):
    return lax.shift_right_logical(lax.broadcasted_iota(jnp.int32, shape, dim), int(np.log2(width)))


def _head_ones(n, width):
    return (_iota_div((n, n), 0, width) == _iota_div((n, n), 1, width)).astype(f32)


def _rwkv_prep_kernel(z_ref, mu_ref, w0_ref, a0_ref, kkg_ref, ka_ref, rk_ref, w2_ref, a2_ref, g2_ref,
                      r_ref, ld_ref, k_ref, v_ref, kk_ref, b_ref, g_ref, bonus_ref, buf_ref):
    i = pl.program_id(0)
    tm = z_ref.shape[0]
    W = RWKV_W

    @pl.when(i == 0)
    def _():
        buf_ref[0:8, :] = jnp.zeros((8, buf_ref.shape[1]), f32)

    x = z_ref[...]
    buf_ref[8:, :] = x
    xs = x + (buf_ref[pl.ds(7, tm), :] - x) * mu_ref[...]
    buf_ref[0:8, :] = x[tm - 8:, :]
    r = xs[:, 0:W]
    k = xs[:, W:2 * W]
    v = xs[:, 2 * W:3 * W]
    wl = xs[:, 3 * W:3 * W + LORA_PAD]
    al = xs[:, 3 * W + LORA_PAD:3 * W + 2 * LORA_PAD]
    gl = xs[:, 3 * W + 2 * LORA_PAD:]
    t = w0_ref[...] + _bmm(jnp.tanh(wl), w2_ref[...])
    w = -(jnp.maximum(-t, 0.0) + jnp.log(1.0 + jnp.exp(-jnp.abs(t)))) - 0.5
    a = _sigmoid(a0_ref[...] + _bmm(al, a2_ref[...]))
    g = _bmm(_sigmoid(gl), g2_ref[...])
    ones = _head_ones(W, RWKV_HEAD_DIM)
    kk = k * kkg_ref[...]
    ss = _mm(kk * kk, ones, precision=HI)
    kk = kk / jnp.maximum(jnp.sqrt(ss), 1e-12)
    k2 = k * (1.0 + (a - 1.0) * ka_ref[...])
    bonus = _mm(r * k2 * rk_ref[...], ones, precision=HI) * v
    r_ref[...] = r
    ld_ref[...] = -jnp.exp(w)
    k_ref[...] = k2
    v_ref[...] = v
    kk_ref[...] = kk
    b_ref[...] = kk * a
    g_ref[...] = g
    bonus_ref[...] = bonus


def _rwkv_prep(z_main, mu, w0, a0, kkg, ka, rk, w2, a2, g2):
    T = z_main.shape[0]
    tm = min(256, T)
    W = RWKV_W
    vec = lambda n: pl.BlockSpec((1, n), lambda i: (0, 0))
    mat = lambda s: pl.BlockSpec(s, lambda i: (0, 0))
    out = pl.BlockSpec((tm, W), lambda i: (i, 0))
    return pl.pallas_call(
        _rwkv_prep_kernel,
        grid=(T // tm,),
        in_specs=[pl.BlockSpec((tm, RWKV_PAD), lambda i: (i, 4 * MLSTM_W // RWKV_PAD)), vec(RWKV_PAD), vec(W), vec(W),
                  vec(W), vec(W), vec(W), mat((LORA_PAD, W)), mat((LORA_PAD, W)), mat((GATE_LORA, W))],
        out_specs=[out] * 8,
        out_shape=[jax.ShapeDtypeStruct((T, W), f32)] * 8,
        scratch_shapes=[pltpu.VMEM((tm + 8, RWKV_PAD), f32)],
        compiler_params=_cp(("arbitrary",)),
        name="rwkv_prep",
    )(z_main, mu, w0, a0, kkg, ka, rk, w2, a2, g2)


def _rwkv_pre_kernel(r_ref, ld_ref, k_ref, v_ref, kk_ref, b_ref, rh_ref, oh_ref, nt_ref, ds_ref, ge_ref):
    L, dh = RWKV_CHUNK, RWKV_HEAD_DIM
    GW = 4 * dh
    n_chunks = r_ref.shape[0] // L
    rr = lax.broadcasted_iota(jnp.int32, (GW, GW), 0)
    cc = lax.broadcasted_iota(jnp.int32, (GW, GW), 1)
    same = _iota_div((GW, GW), 0, L) == _iota_div((GW, GW), 1, dh)
    strict = same & (rr > cc)
    incl = same & (rr >= cc)
    eye = jnp.where(rr == cc, 1.0, 0.0)
    pair_blk = _head_ones(2 * dh, dh)
    tr = lax.broadcasted_iota(jnp.int32, (L, L), 0)
    tc = lax.broadcasted_iota(jnp.int32, (L, L), 1)
    tril = (tr >= tc).astype(f32)

    def unstack(x):
        acc = x[0:L, :]
        for hh in range(1, GW // L):
            acc = acc + x[hh * L:(hh + 1) * L, :]
        return acc

    for ci in range(n_chunks):
        rows = slice(ci * L, (ci + 1) * L)
        ld = ld_ref[rows, :]
        lg = _mm(tril, ld, precision=HI)
        g_end = lg[L - 1:L, :]
        ge_ref[ci] = g_end
        dec_out = jnp.exp(-lg)
        dec_tail = jnp.exp(g_end - lg)
        k = k_ref[rows, :]
        b = b_ref[rows, :]
        v = v_ref[rows, :]
        kap = kk_ref[rows, :] * jnp.exp(lg - ld)
        rt = r_ref[rows, :] * jnp.exp(lg)
        kt = k * dec_out
        bt = b * dec_out
        khat = (k * dec_tail).astype(bf16)
        bhat = (b * dec_tail).astype(bf16)
        vb = v.astype(bf16)
        for gi in range(RWKV_W // GW):
            lanes = slice(gi * GW, (gi + 1) * GW)

            def tile(x):
                return jnp.concatenate([x[:, lanes]] * (GW // L), axis=0)

            def stack(x):
                return jnp.where(same, tile(x), 0.0)

            kap_s = stack(kap)
            rt_s = stack(rt)
            kap_b = kap_s.astype(bf16)
            rt_b = rt_s.astype(bf16)
            v_s = stack(v).astype(bf16)
            kt_t = tile(kt).astype(bf16)
            bt_t = tile(bt).astype(bf16)
            a_kb = jnp.where(strict, _nt(kap_b, bt_t), 0.0)
            a_kk = jnp.where(strict, _nt(kap_b, kt_t), 0.0)
            a_rk = jnp.where(incl, _nt(rt_b, kt_t), 0.0)
            a_rb = jnp.where(incl, _nt(rt_b, bt_t), 0.0)
            p = -a_kb
            tinv = eye + p
            step = 2
            while step < L:
                p = _bmm(p, p)
                tinv = tinv + _bmm(tinv, p)
                step *= 2
            w1 = _bmm(a_kk, v_s)
            ku = _bmm(tinv, jnp.concatenate([kap_s, w1], axis=1))
            ru = _bmm(a_rb, ku)
            rh_ref[rows, lanes] = unstack(rt_s - ru[:, :GW]).astype(rh_ref.dtype)
            oh_ref[rows, lanes] = unstack(_bmm(a_rk, v_s) - ru[:, GW:])
            kh = unstack(ku[:, :GW]).astype(bf16)
            uh = unstack(ku[:, GW:]).astype(bf16)
            for pi in range(GW // (2 * dh)):
                q = gi * (GW // (2 * dh)) + pi
                loc = slice(pi * 2 * dh, (pi + 1) * 2 * dh)
                glo = slice(q * 2 * dh, (q + 1) * 2 * dh)
                nt_ref[ci, q] = (_tn(kh[:, loc], bhat[:, glo]) * pair_blk).astype(nt_ref.dtype)
                ds_ref[ci, q] = (_tn(vb[:, glo], khat[:, glo]) - _tn(uh[:, loc], bhat[:, glo])) * pair_blk


def _rwkv_seq_kernel(rh_ref, oh_ref, nt_ref, ds_ref, ge_ref, o_ref, s_ref):
    L, PW = RWKV_CHUNK, 2 * RWKV_HEAD_DIM
    n_chunks = rh_ref.shape[0] // L

    @pl.when(pl.program_id(0) == 0)
    def _():
        s_ref[...] = jnp.zeros_like(s_ref)

    for q in range(RWKV_W // PW):
        lanes = slice(q * PW, (q + 1) * PW)
        s_mat = s_ref[q]
        for ci in range(n_chunks):
            rows = slice(ci * L, (ci + 1) * L)
            s_b = s_mat.astype(bf16)
            o_ref[rows, lanes] = _nt(rh_ref[rows, lanes], s_b) + oh_ref[rows, lanes]
            s_mat = s_mat * jnp.exp(ge_ref[ci][:, lanes]) - _mm(s_b, nt_ref[ci, q]) + ds_ref[ci, q]
        s_ref[q] = s_mat


def _rwkv_post_kernel(o_ref, g_ref, bonus_ref, lng_ref, lnb_ref, y_ref):
    dh = RWKV_HEAD_DIM
    blk = _head_ones(RWKV_W, dh).astype(bf16)

    def head_mean(x):
        hi = x.astype(bf16)
        lo = (x - hi.astype(f32)).astype(bf16)
        return (_mm(hi, blk) + _mm(lo, blk)) * (1.0 / dh)

    o = o_ref[...]
    d = o - head_mean(o)
    var = head_mean(d * d)
    yn = d * lax.rsqrt(var + RWKV_GN_EPS) * lng_ref[...] + lnb_ref[...] + bonus_ref[...]
    y_ref[...] = (yn * g_ref[...]).astype(y_ref.dtype)


def _rwkv_scan(parts, ln_g, ln_b):
    r, ld, k, v, kk, b, g, bonus = parts
    T, W = r.shape
    L, PW = RWKV_CHUNK, 2 * RWKV_HEAD_DIM
    nc, npair = T // L, W // PW
    pre_rows = min(2 * L, T)
    seq_rows = min(4 * L, T)

    def specs(rows):
        n = rows // L
        row = pl.BlockSpec((rows, W), lambda c: (c, 0))
        mat = pl.BlockSpec((n, npair, PW, PW), lambda c: (c, 0, 0, 0))
        vec = pl.BlockSpec((n, 1, W), lambda c: (c, 0, 0))
        return row, mat, vec

    row, mat, vec = specs(pre_rows)
    rh, oh, nt, ds, ge = pl.pallas_call(
        _rwkv_pre_kernel,
        grid=(T // pre_rows,),
        in_specs=[row] * 6,
        out_specs=[row, row, mat, mat, vec],
        out_shape=[jax.ShapeDtypeStruct((T, W), bf16), jax.ShapeDtypeStruct((T, W), f32),
                   jax.ShapeDtypeStruct((nc, npair, PW, PW), bf16), jax.ShapeDtypeStruct((nc, npair, PW, PW), f32),
                   jax.ShapeDtypeStruct((nc, 1, W), f32)],
        compiler_params=_cp(("parallel",)),
        name="rwkv_pre",
    )(r, ld, k, v, kk, b)
    row, mat, vec = specs(seq_rows)
    o = pl.pallas_call(
        _rwkv_seq_kernel,
        grid=(T // seq_rows,),
        in_specs=[row, row, mat, mat, vec],
        out_specs=row,
        out_shape=jax.ShapeDtypeStruct((T, W), f32),
        scratch_shapes=[pltpu.VMEM((npair, PW, PW), f32)],
        compiler_params=_cp(("arbitrary",)),
        name="rwkv_seq",
    )(rh, oh, nt, ds, ge)
    tm = min(512, T)
    blk = pl.BlockSpec((tm, W), lambda i: (i, 0))
    one = pl.BlockSpec((1, W), lambda i: (0, 0))
    return pl.pallas_call(
        _rwkv_post_kernel,
        grid=(T // tm,),
        in_specs=[blk, blk, blk, one, one],
        out_specs=blk,
        out_shape=jax.ShapeDtypeStruct((T, W), bf16),
        compiler_params=_cp(("parallel",)),
        name="rwkv_post",
    )(o, g, bonus, ln_g.reshape(1, W), ln_b.reshape(1, W))


def _s5_dot(a, b):
    return jnp.dot(a.astype(bf16), b.astype(bf16), preferred_element_type=f32)


def _transpose_pieces(w):
    w = list(w)
    lane = lax.broadcasted_iota(jnp.int32, w[0].shape, 1)
    for d in (4, 2, 1):
        width = S5_GROUP * d
        low = (lane & width) == 0
        for i in range(8):
            if i & d == 0:
                a, b = w[i], w[i + d]
                w[i] = jnp.where(low, a, pltpu.roll(b, width, 1))
                w[i + d] = jnp.where(low, pltpu.roll(a, 128 - width, 1), b)
    return w


def _s5_split_kernel(*refs):
    z_refs, u_ref = refs[:-1], refs[-1]
    ncb = u_ref.shape[1]
    for j, z_ref in enumerate(z_refs):
        for b in range(S5_CHUNK // 8):
            w = [z_ref[pl.ds(8 * b + i, ncb, stride=S5_CHUNK), :] for i in range(8)]
            o = _transpose_pieces(w)
            for a in range(8):
                u_ref[8 * j + a, :, 128 * b:128 * (b + 1)] = o[a]


def _toeplitz(krow):
    P, LP = krow.shape
    lane = lax.broadcasted_iota(jnp.int32, (P, LP), 1)
    blocks = [krow]
    for s in range(1, LP // P):
        blocks.append(jnp.where(lane >= P * s, pltpu.roll(krow, P * s, 1), 0.0))
    return jnp.concatenate(blocks, axis=0)


def _s5_in_kernel(u_ref, krow_ref, br_ref, bi_ref, y_ref, xr_ref, xi_ref):
    for j in range(2):
        y_ref[j] = _s5_dot(u_ref[j], _toeplitz(krow_ref[j]))
    u2 = jnp.concatenate([u_ref[0], u_ref[1]], axis=1)
    xr_ref[...] = _s5_dot(u2, br_ref[0])
    xi_ref[...] = _s5_dot(u2, bi_ref[0])


def _s5_scan_kernel(xr_ref, xi_ref, ar_ref, ai_ref, sr_ref, si_ref):
    nc = xr_ref.shape[0]
    ar = ar_ref[...]
    ai = ai_ref[...]

    def body(c, carry):
        sr, si = carry
        sr_ref[pl.ds(c, 1), :] = sr
        si_ref[pl.ds(c, 1), :] = si
        xr = xr_ref[pl.ds(c, 1), :]
        xi = xi_ref[pl.ds(c, 1), :]
        return ar * sr - ai * si + xr, ar * si + ai * sr + xi

    z = jnp.zeros_like(ar)
    lax.fori_loop(0, nc, body, (z, z))


def _s5_out_kernel(y_ref, sr_ref, si_ref, cr_ref, ci_ref, u_ref, d_ref, o_ref):
    sr = sr_ref[...]
    si = si_ref[...]
    for j in range(2):
        y = y_ref[j] + _s5_dot(sr, cr_ref[j]) + _s5_dot(si, ci_ref[j]) + d_ref[j] * u_ref[j]
        o_ref[j] = 0.5 * y * (1.0 + jnp.tanh(0.7978845608028654 * (y + 0.044715 * (y * y * y))))


def _glu_kernel(y_ref, w_ref, b_ref, o_ref, *nat_refs):
    ncb = y_ref.shape[1]
    for j, nat_ref in enumerate(nat_refs):
        for b in range(S5_CHUNK // 8):
            o = [y_ref[8 * j + a, :, 128 * b:128 * (b + 1)] for a in range(8)]
            w = _transpose_pieces(o)
            for i in range(8):
                nat_ref[pl.ds(8 * b + i, ncb, stride=S5_CHUNK), :] = w[i]
    y = jnp.concatenate([r[...] for r in nat_refs], axis=1)
    o_ref[...] = (y * _sigmoid(_bmm(y, w_ref[...]) + b_ref[...])).astype(o_ref.dtype)


def _s5_tables(a_re, a_im, log_dt, b_re, b_im, c_re, c_im, d):
    G, N, P, L = S5_GROUPS, S5_STATE, S5_GROUP, S5_CHUNK
    D = a_re.shape[0]
    dt = jnp.exp(log_dt)[:, None, :, None]
    lags = jnp.arange(L + 1, dtype=f32)[None, :, None, None]
    mag = jnp.exp(a_re[:, None] * dt * lags)
    ang = a_im[:, None] * dt * lags
    pw_re, pw_im = mag * jnp.cos(ang), mag * jnp.sin(ang)
    den = a_re * a_re + a_im * a_im
    nr, ni = pw_re[:, 1] - 1.0, pw_im[:, 1]
    coef_re = ((nr * a_re + ni * a_im) / den)[..., None]
    coef_im = ((ni * a_re - nr * a_im) / den)[..., None]
    bb_re = coef_re * b_re - coef_im * b_im
    bb_im = coef_re * b_im + coef_im * b_re
    rep = lambda a, axis: jnp.repeat(a, P, axis=axis)
    pl_re, pl_im = rep(pw_re[:, :L].transpose(0, 2, 1, 3), 2), rep(pw_im[:, :L].transpose(0, 2, 1, 3), 2)
    ct_re, ct_im = jnp.tile(c_re, (1, 1, L, 1)), jnp.tile(c_im, (1, 1, L, 1))
    cb_re, cb_im = ct_re * pl_re - ct_im * pl_im, ct_re * pl_im + ct_im * pl_re
    krow = (jnp.einsum('dgxn,dgnq->dgqx', cb_re, bb_re, precision=HI)
            - jnp.einsum('dgxn,dgnq->dgqx', cb_im, bb_im, precision=HI))
    e_re = rep(pw_re[:, L - 1::-1][:, :L].transpose(0, 2, 1, 3), 2)
    e_im = rep(pw_im[:, L - 1::-1][:, :L].transpose(0, 2, 1, 3), 2)
    bq_re = jnp.tile(bb_re.transpose(0, 1, 3, 2), (1, 1, L, 1))
    bq_im = jnp.tile(bb_im.transpose(0, 1, 3, 2), (1, 1, L, 1))
    bs_re = (e_re * bq_re - e_im * bq_im).reshape(D, G // 2, 2, L * P, N)
    bs_im = (e_re * bq_im + e_im * bq_re).reshape(D, G // 2, 2, L * P, N)

    def pair_in(bs):
        zz = jnp.zeros_like(bs[:, :, 0])
        return jnp.concatenate([jnp.concatenate([bs[:, :, 0], zz], axis=3), jnp.concatenate([zz, bs[:, :, 1]], axis=3)], axis=2)

    f_re = rep(pw_re[:, 1:].transpose(0, 2, 3, 1), 3)
    f_im = rep(pw_im[:, 1:].transpose(0, 2, 3, 1), 3)
    cn_re = jnp.tile(c_re.transpose(0, 1, 3, 2), (1, 1, 1, L))
    cn_im = jnp.tile(c_im.transpose(0, 1, 3, 2), (1, 1, 1, L))
    cs_re = (cn_re * f_re - cn_im * f_im).reshape(D, G // 2, 2, N, L * P)
    cs_im = (-(cn_re * f_im + cn_im * f_re)).reshape(D, G // 2, 2, N, L * P)

    def pair_out(cs):
        zz = jnp.zeros_like(cs[:, :, 0])
        top = jnp.concatenate([cs[:, :, 0], zz], axis=2)[:, :, None]
        bot = jnp.concatenate([zz, cs[:, :, 1]], axis=2)[:, :, None]
        return jnp.concatenate([top, bot], axis=2).reshape(D, G, 2 * N, L * P)

    dd = jnp.tile(d.reshape(D, G, 1, P), (1, 1, 1, L))
    return dict(krow=krow, bsr=pair_in(bs_re), bsi=pair_in(bs_im), cr=pair_out(cs_re), ci=pair_out(cs_im),
                al_re=pw_re[:, L].reshape(D, 1, G * N), al_im=pw_im[:, L].reshape(D, 1, G * N), d=dd)


def _s5(z_tail, tab, glu_w, glu_b):
    T = z_tail.shape[0]
    G, N, P, L = S5_GROUPS, S5_STATE, S5_GROUP, S5_CHUNK
    nc, LP = T // L, L * P
    tm = min(1024, T)
    u = pl.pallas_call(
        _s5_split_kernel,
        grid=(T // tm,),
        in_specs=[pl.BlockSpec((tm, 128), lambda i, j=j: (i, j)) for j in range(S5_W // 128)],
        out_specs=pl.BlockSpec((G, tm // L, LP), lambda i: (0, i, 0)),
        out_shape=jax.ShapeDtypeStruct((G, nc, LP), f32),
        compiler_params=_cp(("parallel",)),
        name="s5_split",
    )(*[z_tail] * (S5_W // 128))
    grp = lambda s: pl.BlockSpec((2,) + s, lambda j: (j, 0, 0))
    one = lambda s: pl.BlockSpec((1,) + s, lambda j: (j, 0, 0))
    lanes = pl.BlockSpec((nc, 2 * N), lambda j: (0, j))
    y_in, xr, xi = pl.pallas_call(
        _s5_in_kernel,
        grid=(G // 2,),
        in_specs=[grp((nc, LP)), grp((P, LP)), one((2 * LP, 2 * N)), one((2 * LP, 2 * N))],
        out_specs=[grp((nc, LP)), lanes, lanes],
        out_shape=[jax.ShapeDtypeStruct((G, nc, LP), f32), jax.ShapeDtypeStruct((nc, G * N), f32),
                   jax.ShapeDtypeStruct((nc, G * N), f32)],
        compiler_params=_cp(("parallel",)),
        name="s5_in",
    )(u, tab["krow"], tab["bsr"], tab["bsi"])
    whole = lambda s: pl.BlockSpec(s, lambda: (0,) * len(s))
    sr, si = pl.pallas_call(
        _s5_scan_kernel,
        in_specs=[whole((nc, G * N)), whole((nc, G * N)), whole((1, G * N)), whole((1, G * N))],
        out_specs=[whole((nc, G * N)), whole((nc, G * N))],
        out_shape=[jax.ShapeDtypeStruct((nc, G * N), f32)] * 2,
        compiler_params=pltpu.CompilerParams(vmem_limit_bytes=VMEM_LIMIT),
        name="s5_scan",
    )(xr, xi, tab["al_re"], tab["al_im"])
    y = pl.pallas_call(
        _s5_out_kernel,
        grid=(G // 2,),
        in_specs=[grp((nc, LP)), lanes, lanes, grp((2 * N, LP)), grp((2 * N, LP)), grp((nc, LP)), grp((1, LP))],
        out_specs=grp((nc, LP)),
        out_shape=jax.ShapeDtypeStruct((G, nc, LP), f32),
        compiler_params=_cp(("parallel",)),
        name="s5_out",
    )(y_in, sr, si, tab["cr"], tab["ci"], u, tab["d"])
    return pl.pallas_call(
        _glu_kernel,
        grid=(T // tm,),
        in_specs=[pl.BlockSpec((G, tm // L, LP), lambda i: (0, i, 0)), pl.BlockSpec((S5_W, S5_W), lambda i: (0, 0)),
                  pl.BlockSpec((1, S5_W), lambda i: (0, 0))],
        out_specs=pl.BlockSpec((tm, S5_W), lambda i: (i, 0)),
        out_shape=jax.ShapeDtypeStruct((T, S5_W), bf16),
        scratch_shapes=[pltpu.VMEM((tm, 128), f32)] * (S5_W // 128),
        compiler_params=_cp(("parallel",)),
        name="s5_glu",
    )(y, glu_w, glu_b.reshape(1, S5_W))


def _pad_cols(w, n):
    return jnp.pad(w, ((0, 0), (0, n - w.shape[1])))


def _pad_rows(w, n):
    return jnp.pad(w, ((0, n - w.shape[0]), (0, 0)))


def _split_in_proj(w):
    r0 = M_IN
    l0 = r0 + 3 * RWKV_W
    s0 = M_IN + RWKV_IN
    g0 = s0 + S5_W
    main = jnp.concatenate([
        w[:, :4 * MLSTM_W], w[:, r0:l0],
        _pad_cols(w[:, l0:l0 + DECAY_LORA], LORA_PAD),
        _pad_cols(w[:, l0 + DECAY_LORA:l0 + DECAY_LORA + AAA_LORA], LORA_PAD),
        w[:, l0 + DECAY_LORA + AAA_LORA:s0], w[:, g0:]], axis=1).astype(bf16)
    tail = _pad_cols(jnp.concatenate([w[:, s0:g0], w[:, 4 * MLSTM_W:M_IN]], axis=1), Z_TAIL).astype(bf16)
    return main, tail


def _split_mu(mu):
    l0 = 3 * RWKV_W
    return jnp.concatenate([
        mu[:l0], jnp.pad(mu[l0:l0 + DECAY_LORA], (0, LORA_PAD - DECAY_LORA)),
        jnp.pad(mu[l0 + DECAY_LORA:l0 + DECAY_LORA + AAA_LORA], (0, LORA_PAD - AAA_LORA)),
        mu[l0 + DECAY_LORA + AAA_LORA:]]).reshape(1, RWKV_PAD)


def kernel(x, p, norm_mix_g, w_in, mlstm_conv, mlstm_ib, mlstm_fb, mlstm_norm_g, rwkv_mu, rwkv_w0, rwkv_w2, rwkv_a0, rwkv_a2, rwkv_g2, rwkv_kk, rwkv_ka, rwkv_rk, rwkv_ln_g, rwkv_ln_b, s5_a_re, s5_a_im, s5_log_dt, s5_b_re, s5_b_im, s5_c_re, s5_c_im, s5_d, s5_glu_w, s5_glu_b, w_up_m, w_up_r, w_up_s, w_out, norm_ffn_g, ffn_w_gate, ffn_w_up, ffn_w_down, norm_ple_g, ple_w_gate, ple_w_proj, final_norm_g):
    B, T, D = x.shape
    depth = w_in.shape[0]
    s5_tab = _s5_tables(s5_a_re, s5_a_im, s5_log_dt, s5_b_re, s5_b_im, s5_c_re, s5_c_im, s5_d)
    outs = []
    for bi in range(B):
        h = x[bi]
        xn = _rmsnorm(h, norm_mix_g[0], bf16)
        for i in range(depth):
            w_main, w_tail = _split_in_proj(w_in[i])
            z_main = _proj(xn, w_main, 1024)
            z_tail = _proj(xn, w_tail, Z_TAIL)

            gate_b = jnp.pad(jnp.concatenate([mlstm_ib[i], mlstm_fb[i]]), (0, 128 - 2 * MLSTM_HEADS)).reshape(1, 128)
            y_m = _mlstm(z_main, z_tail, mlstm_conv[i], gate_b, mlstm_norm_g[i])

            vec = lambda a: a.reshape(1, RWKV_W)
            parts = _rwkv_prep(z_main, _split_mu(rwkv_mu[i]), vec(rwkv_w0[i]), vec(rwkv_a0[i]), vec(rwkv_kk[i]),
                               vec(rwkv_ka[i]), vec(rwkv_rk[i]), _pad_rows(rwkv_w2[i], LORA_PAD).astype(bf16),
                               _pad_rows(rwkv_a2[i], LORA_PAD).astype(bf16), rwkv_g2[i].astype(bf16))
            y_r = _rwkv_scan(parts, rwkv_ln_g[i], rwkv_ln_b[i])

            y_s = _s5(z_tail, {k: v[i] for k, v in s5_tab.items()}, s5_glu_w[i].astype(bf16), s5_glu_b[i])

            mixed = _merge(y_m, y_r, y_s, z_main, w_up_m, w_up_r, w_up_s, i)
            h, hn = _resid_norm(mixed, w_out[i].astype(bf16), h, norm_ffn_g[i], D_MODEL)
            act = _ffn_up(hn, ffn_w_gate, ffn_w_up, i)
            h, hp = _resid_norm(act, ffn_w_down[i].astype(bf16), h, norm_ple_g[i], FFN_HIDDEN // 4)
            last = i == depth - 1
            g_next = final_norm_g if last else norm_mix_g[i + 1]
            h, xn = _ple(hp, p[i, bi].astype(bf16), ple_w_gate[i].astype(bf16), ple_w_proj[i].astype(bf16), h, g_next,
                         f32 if last else bf16)
        outs.append(xn)
    return jnp.stack(outs)
```

```python
import functools

import jax
import jax.numpy as jnp
import numpy as np
from jax import lax
from jax.experimental import pallas as pl
from jax.experimental.pallas import tpu as pltpu

f32 = jnp.float32
bf16 = jnp.bfloat16
HI = lax.Precision.HIGHEST

D_MODEL = 2048
PLE_DIM = 256
RMS_EPS = 1e-6
MLSTM_HEADS = 4
MLSTM_HEAD_DIM = 256
MLSTM_W = 1024
MLSTM_CHUNK = 256
GATE_SOFTCAP = 15.0
RWKV_HEADS = 8
RWKV_HEAD_DIM = 64
RWKV_W = 512
DECAY_LORA = 96
AAA_LORA = 96
GATE_LORA = 256
LORA_PAD = 128
RWKV_GN_EPS = 64e-5
RWKV_CHUNK = 64
S5_GROUP = 16
S5_GROUPS = 32
S5_W = 512
S5_STATE = 64
S5_CHUNK = 16
FFN_HIDDEN = 5632
M_IN = 4 * MLSTM_W + 2 * MLSTM_HEADS
RWKV_IN = 3 * RWKV_W + DECAY_LORA + AAA_LORA + GATE_LORA
RWKV_PAD = 3 * RWKV_W + 2 * LORA_PAD + GATE_LORA
Z_TAIL = S5_W + 128
VMEM_LIMIT = 56 * 1024 * 1024


def _cp(sem):
    return pltpu.CompilerParams(dimension_semantics=sem, vmem_limit_bytes=VMEM_LIMIT)


def _nt(a, b, **kw):
    return lax.dot_general(a, b, (((1,), (1,)), ((), ())), preferred_element_type=f32, **kw)


def _tn(a, b, **kw):
    return lax.dot_general(a, b, (((0,), (0,)), ((), ())), preferred_element_type=f32, **kw)


def _mm(a, b, **kw):
    return jnp.dot(a, b, preferred_element_type=f32, **kw)


def _bmm(a, b):
    return jnp.dot(a.astype(bf16), b.astype(bf16), preferred_element_type=f32)


def _sigmoid(x):
    return 1.0 / (1.0 + jnp.exp(-x))


def _rms(x, g):
    return x * lax.rsqrt(jnp.mean(x * x, axis=-1, keepdims=True) + RMS_EPS) * g


def _norm_kernel(x_ref, g_ref, o_ref):
    o_ref[...] = _rms(x_ref[...], g_ref[...]).astype(o_ref.dtype)


def _rmsnorm(x, g, out_dtype):
    T, D = x.shape
    tm = min(512, T)
    return pl.pallas_call(
        _norm_kernel,
        grid=(T // tm,),
        in_specs=[pl.BlockSpec((tm, D), lambda i: (i, 0)), pl.BlockSpec((1, D), lambda i: (0, 0))],
        out_specs=pl.BlockSpec((tm, D), lambda i: (i, 0)),
        out_shape=jax.ShapeDtypeStruct((T, D), out_dtype),
        compiler_params=_cp(("parallel",)),
        name="rmsnorm",
    )(x, g.reshape(1, D))


def _proj_kernel(a_ref, w_ref, o_ref, *, gate):
    z = _mm(a_ref[...], w_ref[...])
    o_ref[...] = (_sigmoid(z) if gate else z).astype(o_ref.dtype)


def _proj(a, w, tn, out_dtype=f32, gate=False):
    T, K = a.shape
    N = w.shape[1]
    tm = min(1024, T)
    return pl.pallas_call(
        functools.partial(_proj_kernel, gate=gate),
        grid=(N // tn, T // tm),
        in_specs=[pl.BlockSpec((tm, K), lambda j, i: (i, 0)), pl.BlockSpec((K, tn), lambda j, i: (0, j))],
        out_specs=pl.BlockSpec((tm, tn), lambda j, i: (i, j)),
        out_shape=jax.ShapeDtypeStruct((T, N), out_dtype),
        compiler_params=_cp(("parallel", "parallel")),
        name="in_proj",
    )(a, w)


def _merge_kernel(ym_ref, yr_ref, ys_ref, gm_ref, gr_ref, gs_ref, um_ref, ur_ref, us_ref, o_ref, umb_ref, urb_ref, usb_ref):
    @pl.when(pl.program_id(1) == 0)
    def _():
        umb_ref[...] = um_ref[...].astype(bf16)
        urb_ref[...] = ur_ref[...].astype(bf16)
        usb_ref[...] = us_ref[...].astype(bf16)

    acc = gm_ref[...] * _mm(ym_ref[...], umb_ref[...])
    acc += gr_ref[...] * _mm(yr_ref[...], urb_ref[...])
    acc += gs_ref[...] * _mm(ys_ref[...], usb_ref[...])
    o_ref[...] = acc.astype(o_ref.dtype)


def _merge(ym, yr, ys, gates, um, ur, us, layer):
    T = ym.shape[0]
    tm, tn = min(512, T), 1024
    nj = D_MODEL // tn

    def gate_spec(b):
        return pl.BlockSpec((tm, tn), lambda j, i, b=b: (i, b * nj + j))

    def y_spec(w):
        return pl.BlockSpec((tm, w), lambda j, i: (i, 0))

    def u_spec(w):
        return pl.BlockSpec((None, w, tn), lambda j, i: (layer, 0, j))

    return pl.pallas_call(
        _merge_kernel,
        grid=(nj, T // tm),
        in_specs=[y_spec(MLSTM_W), y_spec(RWKV_W), y_spec(S5_W), gate_spec(0), gate_spec(1), gate_spec(2),
                  u_spec(MLSTM_W), u_spec(RWKV_W), u_spec(S5_W)],
        out_specs=pl.BlockSpec((tm, tn), lambda j, i: (i, j)),
        out_shape=jax.ShapeDtypeStruct((T, D_MODEL), bf16),
        scratch_shapes=[pltpu.VMEM((MLSTM_W, tn), bf16), pltpu.VMEM((RWKV_W, tn), bf16), pltpu.VMEM((S5_W, tn), bf16)],
        compiler_params=_cp(("parallel", "arbitrary")),
        name="gated_merge",
    )(ym, yr, ys, gates, gates, gates, um, ur, us)


def _resid_norm_kernel(a_ref, w_ref, h_ref, g_ref, ho_ref, no_ref, acc_ref):
    k = pl.program_id(1)

    @pl.when(k == 0)
    def _():
        acc_ref[...] = h_ref[...]

    acc_ref[...] += _mm(a_ref[...], w_ref[...])

    @pl.when(k == pl.num_programs(1) - 1)
    def _():
        hn = acc_ref[...]
        ho_ref[...] = hn
        no_ref[...] = _rms(hn, g_ref[...]).astype(no_ref.dtype)


def _resid_norm(a, w, h, g, tk, norm_dtype=bf16):
    T, K = a.shape
    tm = min(512, T)
    return pl.pallas_call(
        _resid_norm_kernel,
        grid=(T // tm, K // tk),
        in_specs=[pl.BlockSpec((tm, tk), lambda i, k: (i, k)), pl.BlockSpec((tk, D_MODEL), lambda i, k: (k, 0)),
                  pl.BlockSpec((tm, D_MODEL), lambda i, k: (i, 0)), pl.BlockSpec((1, D_MODEL), lambda i, k: (0, 0))],
        out_specs=[pl.BlockSpec((tm, D_MODEL), lambda i, k: (i, 0)), pl.BlockSpec((tm, D_MODEL), lambda i, k: (i, 0))],
        out_shape=[jax.ShapeDtypeStruct((T, D_MODEL), f32), jax.ShapeDtypeStruct((T, D_MODEL), norm_dtype)],
        scratch_shapes=[pltpu.VMEM((tm, D_MODEL), f32)],
        compiler_params=_cp(("parallel", "arbitrary")),
        name="resid_matmul_norm",
    )(a, w, h, g.reshape(1, D_MODEL))


def _ffn_up_kernel(a_ref, wg_ref, wu_ref, o_ref, wgb_ref, wub_ref):
    @pl.when(pl.program_id(1) == 0)
    def _():
        wgb_ref[...] = wg_ref[...].astype(bf16)
        wub_ref[...] = wu_ref[...].astype(bf16)

    a = a_ref[...]
    gt = _mm(a, wgb_ref[...])
    up = _mm(a, wub_ref[...])
    o_ref[...] = (gt * _sigmoid(gt) * up).astype(o_ref.dtype)


def _ffn_up(a, wg, wu, layer):
    T, K = a.shape
    N = wg.shape[2]
    tm, tn = min(1024, T), 512
    return pl.pallas_call(
        _ffn_up_kernel,
        grid=(N // tn, T // tm),
        in_specs=[pl.BlockSpec((tm, K), lambda j, i: (i, 0)), pl.BlockSpec((None, K, tn), lambda j, i: (layer, 0, j)),
                  pl.BlockSpec((None, K, tn), lambda j, i: (layer, 0, j))],
        out_specs=pl.BlockSpec((tm, tn), lambda j, i: (i, j)),
        out_shape=jax.ShapeDtypeStruct((T, N), bf16),
        scratch_shapes=[pltpu.VMEM((K, tn), bf16), pltpu.VMEM((K, tn), bf16)],
        compiler_params=_cp(("parallel", "arbitrary")),
        name="ffn_up",
    )(a, wg, wu)


def _resid_mm_kernel(a_ref, w_ref, h_ref, o_ref, wb_ref):
    @pl.when(pl.program_id(1) == 0)
    def _():
        wb_ref[...] = w_ref[...].astype(bf16)

    o_ref[...] = h_ref[...] + _mm(a_ref[...], wb_ref[...])


def _resid_mm(a, w, h, layer):
    T, K = a.shape
    N = w.shape[2]
    tm, tn = min(512, T), 512
    return pl.pallas_call(
        _resid_mm_kernel,
        grid=(N // tn, T // tm),
        in_specs=[pl.BlockSpec((tm, K), lambda j, i: (i, 0)), pl.BlockSpec((None, K, tn), lambda j, i: (layer, 0, j)),
                  pl.BlockSpec((tm, tn), lambda j, i: (i, j))],
        out_specs=pl.BlockSpec((tm, tn), lambda j, i: (i, j)),
        out_shape=jax.ShapeDtypeStruct((T, N), f32),
        scratch_shapes=[pltpu.VMEM((K, tn), bf16)],
        compiler_params=_cp(("parallel", "arbitrary")),
        name="resid_matmul",
    )(a, w, h)


def _ple_kernel(h_ref, p_ref, wg_ref, wp_ref, gp_ref, g_ref, ho_ref, no_ref):
    h = h_ref[...]
    hp = _rms(h, gp_ref[...]).astype(bf16)
    gate = _sigmoid(_mm(hp, wg_ref[...]))
    hn = h + _mm(p_ref[...], wp_ref[...]) * gate
    ho_ref[...] = hn
    no_ref[...] = _rms(hn, g_ref[...]).astype(no_ref.dtype)


def _ple(h, p, wg, wp, g_ple, g, norm_dtype):
    T = h.shape[0]
    tm = min(512, T)
    row = lambda w: pl.BlockSpec((tm, w), lambda i: (i, 0))
    full = lambda s: pl.BlockSpec(s, lambda i: (0, 0))
    return pl.pallas_call(
        _ple_kernel,
        grid=(T // tm,),
        in_specs=[row(D_MODEL), row(PLE_DIM), full((D_MODEL, D_MODEL)), full((PLE_DIM, D_MODEL)), full((1, D_MODEL)),
                  full((1, D_MODEL))],
        out_specs=[row(D_MODEL), row(D_MODEL)],
        out_shape=[jax.ShapeDtypeStruct((T, D_MODEL), f32), jax.ShapeDtypeStruct((T, D_MODEL), norm_dtype)],
        compiler_params=_cp(("parallel",)),
        name="ple_norm",
    )(h, p, wg, wp, g_ple.reshape(1, D_MODEL), g.reshape(1, D_MODEL))


def _mlstm_kernel(zq_ref, zk_ref, v_ref, o_ref, gz_ref, cq_ref, ck_ref, gb_ref, ng_ref, y_ref,
                  ct_ref, n_ref, qbuf_ref, kbuf_ref):
    c = pl.program_id(0)
    L, W = zq_ref.shape
    H, dh = MLSTM_HEADS, MLSTM_HEAD_DIM
    heads = range(H)

    @pl.when(c == 0)
    def _():
        ct_ref[...] = jnp.zeros_like(ct_ref)
        n_ref[...] = jnp.zeros_like(n_ref)
        qbuf_ref[0:8, :] = jnp.zeros((8, W), f32)
        kbuf_ref[0:8, :] = jnp.zeros((8, W), f32)

    def conv_silu(z_ref, w_ref, buf_ref):
        x = z_ref[...]
        buf_ref[8:, :] = x
        w = w_ref[...]
        acc = x * w[0:1, :]
        for j in range(1, w.shape[0]):
            acc = acc + buf_ref[pl.ds(8 - j, L), :] * w[j:j + 1, :]
        buf_ref[0:8, :] = x[L - 8:, :]
        return acc * _sigmoid(acc)

    q = conv_silu(zq_ref, cq_ref, qbuf_ref) * (dh ** -0.5)
    k = conv_silu(zk_ref, ck_ref, kbuf_ref)

    g = gz_ref[...] + gb_ref[...]
    sc = GATE_SOFTCAP * jnp.tanh(g / GATE_SOFTCAP)
    logf = -jnp.log(1.0 + jnp.exp(-sc))
    row = lax.broadcasted_iota(jnp.int32, (L, L), 0)
    col = lax.broadcasted_iota(jnp.int32, (L, L), 1)
    causal = row >= col
    bcs = _mm(causal.astype(f32), logf, precision=HI)
    b128 = pltpu.roll(bcs, 128 - H, 1)
    e128 = sc - b128
    e_t = e128.T

    sl = [slice(h * dh, (h + 1) * dh) for h in heads]
    b_col = [b128[:, h:h + 1] for h in heads]
    e_col = [e128[:, h:h + 1] for h in heads]
    wmat = [jnp.where(causal, jnp.exp(jnp.where(causal, b_col[h] + e_t[h:h + 1, :], 0.0)), 0.0) for h in heads]
    b_end = [b_col[h][L - 1:L, :] for h in heads]
    eb = [jnp.exp(b_col[h]) for h in heads]
    qh = [q[:, sl[h]] for h in heads]
    kh = [k[:, sl[h]] for h in heads]
    qb = [x.astype(bf16) for x in qh]
    kb = [x.astype(bf16) for x in kh]
    vb = [v_ref[:, sl[h]].astype(bf16) for h in heads]
    ct = [ct_ref[h] for h in heads]
    nn = [n_ref[:, sl[h]] for h in heads]
    s = [_nt(qb[h], kb[h]) * wmat[h] for h in heads]
    inter = [_mm(qb[h], ct[h].astype(bf16)) for h in heads]
    num = [_mm(s[h].astype(bf16), vb[h]) + eb[h] * inter[h] for h in heads]
    ng = ng_ref[...]
    for h in heads:
        den = jnp.sum(s[h], axis=1, keepdims=True) + eb[h] * jnp.sum(qh[h] * nn[h], axis=1, keepdims=True)
        hh = num[h] / jnp.maximum(jnp.abs(den), 1.0)
        hh = hh * lax.rsqrt(jnp.mean(hh * hh, axis=1, keepdims=True) + RMS_EPS) * ng[:, sl[h]]
        y_ref[:, sl[h]] = (_sigmoid(o_ref[:, sl[h]]) * hh).astype(y_ref.dtype)
    kw = [kh[h] * jnp.exp(b_end[h] + e_col[h]) for h in heads]
    upd = [_tn(kw[h].astype(bf16), vb[h]) for h in heads]
    for h in heads:
        decay = jnp.exp(b_end[h])
        ct_ref[h] = decay * ct[h] + upd[h]
        n_ref[:, sl[h]] = decay * nn[h] + jnp.sum(kw[h], axis=0, keepdims=True)


def _mlstm(z_main, z_tail, conv_w, gate_b, norm_g):
    T = z_main.shape[0]
    L, dh, H, W = min(MLSTM_CHUNK, T), MLSTM_HEAD_DIM, MLSTM_HEADS, MLSTM_W
    blk = lambda j: pl.BlockSpec((L, W), lambda c, j=j: (c, j))
    return pl.pallas_call(
        _mlstm_kernel,
        grid=(T // L,),
        in_specs=[blk(0), blk(1), blk(2), blk(3),
                  pl.BlockSpec((L, 128), lambda c: (c, S5_W // 128)),
                  pl.BlockSpec((4, W), lambda c: (0, 0)), pl.BlockSpec((4, W), lambda c: (0, 1)),
                  pl.BlockSpec((1, 128), lambda c: (0, 0)), pl.BlockSpec((1, W), lambda c: (0, 0))],
        out_specs=pl.BlockSpec((L, W), lambda c: (c, 0)),
        out_shape=jax.ShapeDtypeStruct((T, W), bf16),
        scratch_shapes=[pltpu.VMEM((H, dh, dh), f32), pltpu.VMEM((1, W), f32),
                        pltpu.VMEM((L + 8, W), f32), pltpu.VMEM((L + 8, W), f32)],
        compiler_params=_cp(("arbitrary",)),
        name="mlstm",
    )(z_main, z_main, z_main, z_main, z_tail, conv_w, conv_w, gate_b, norm_g.reshape(1, W))


def _iota_div(shape, dim, width):
    return lax.shift_right_logical(lax.broadcasted_iota(jnp.int32, shape, dim), int(np.log2(width)))


def _head_ones(n, width):
    return (_iota_div((n, n), 0, width) == _iota_div((n, n), 1, width)).astype(f32)


def _head_sum(x):
    blk = _head_ones(RWKV_W, RWKV_HEAD_DIM).astype(bf16)
    hi = x.astype(bf16)
    r1 = x - hi.astype(f32)
    mid = r1.astype(bf16)
    lo = (r1 - mid.astype(f32)).astype(bf16)
    return _mm(hi, blk) + _mm(mid, blk) + _mm(lo, blk)


def _rwkv_prep_kernel(z_ref, mu_ref, w0_ref, a0_ref, kkg_ref, ka_ref, rk_ref, w2_ref, a2_ref, g2_ref,
                      r_ref, ld_ref, k_ref, v_ref, kk_ref, b_ref, g_ref, bonus_ref, buf_ref):
    i = pl.program_id(0)
    tm = z_ref.shape[0]
    W = RWKV_W

    @pl.when(i == 0)
    def _():
        buf_ref[0:8, :] = jnp.zeros((8, buf_ref.shape[1]), f32)

    x = z_ref[...]
    buf_ref[8:, :] = x
    xs = x + (buf_ref[pl.ds(7, tm), :] - x) * mu_ref[...]
    buf_ref[0:8, :] = x[tm - 8:, :]
    r = xs[:, 0:W]
    k = xs[:, W:2 * W]
    v = xs[:, 2 * W:3 * W]
    wl = xs[:, 3 * W:3 * W + LORA_PAD]
    al = xs[:, 3 * W + LORA_PAD:3 * W + 2 * LORA_PAD]
    gl = xs[:, 3 * W + 2 * LORA_PAD:]
    t = w0_ref[...] + _bmm(jnp.tanh(wl), w2_ref[...])
    w = -(jnp.maximum(-t, 0.0) + jnp.log(1.0 + jnp.exp(-jnp.abs(t)))) - 0.5
    a = _sigmoid(a0_ref[...] + _bmm(al, a2_ref[...]))
    g = _bmm(_sigmoid(gl), g2_ref[...])
    kk = k * kkg_ref[...]
    ss = _head_sum(kk * kk)
    kk = kk / jnp.maximum(jnp.sqrt(ss), 1e-12)
    k2 = k * (1.0 + (a - 1.0) * ka_ref[...])
    bonus = _head_sum(r * k2 * rk_ref[...]) * v
    r_ref[...] = r
    ld_ref[...] = -jnp.exp(w)
    k_ref[...] = k2
    v_ref[...] = v
    kk_ref[...] = kk
    b_ref[...] = kk * a
    g_ref[...] = g
    bonus_ref[...] = bonus


def _rwkv_prep(z_main, mu, w0, a0, kkg, ka, rk, w2, a2, g2):
    T = z_main.shape[0]
    tm = min(256, T)
    W = RWKV_W
    vec = lambda n: pl.BlockSpec((1, n), lambda i: (0, 0))
    mat = lambda s: pl.BlockSpec(s, lambda i: (0, 0))
    out = pl.BlockSpec((tm, W), lambda i: (i, 0))
    return pl.pallas_call(
        _rwkv_prep_kernel,
        grid=(T // tm,),
        in_specs=[pl.BlockSpec((tm, RWKV_PAD), lambda i: (i, 4 * MLSTM_W // RWKV_PAD)), vec(RWKV_PAD), vec(W), vec(W),
                  vec(W), vec(W), vec(W), mat((LORA_PAD, W)), mat((LORA_PAD, W)), mat((GATE_LORA, W))],
        out_specs=[out] * 8,
        out_shape=[jax.ShapeDtypeStruct((T, W), f32)] * 8,
        scratch_shapes=[pltpu.VMEM((tm + 8, RWKV_PAD), f32)],
        compiler_params=_cp(("arbitrary",)),
        name="rwkv_prep",
    )(z_main, mu, w0, a0, kkg, ka, rk, w2, a2, g2)


def _rwkv_pre_kernel(r_ref, ld_ref, k_ref, v_ref, kk_ref, b_ref, rh_ref, oh_ref, nt_ref, ds_ref, ge_ref):
    L, dh = RWKV_CHUNK, RWKV_HEAD_DIM
    GW = 4 * dh
    PW = 2 * dh
    n_chunks = r_ref.shape[0] // L
    rr = lax.broadcasted_iota(jnp.int32, (GW, GW), 0)
    cc = lax.broadcasted_iota(jnp.int32, (GW, GW), 1)
    same = _iota_div((GW, GW), 0, L) == _iota_div((GW, GW), 1, dh)
    strict = same & (rr > cc)
    incl = same & (rr >= cc)
    eye = jnp.where(rr == cc, 1.0, 0.0)
    pair_blk = _head_ones(PW, dh)
    tr = lax.broadcasted_iota(jnp.int32, (L, L), 0)
    tc = lax.broadcasted_iota(jnp.int32, (L, L), 1)
    tril = (tr >= tc).astype(f32)

    def unstack(x):
        acc = x[0:L, :]
        for hh in range(1, GW // L):
            acc = acc + x[hh * L:(hh + 1) * L, :]
        return acc

    def tile(x):
        return jnp.concatenate([x] * (GW // L), axis=0)

    def stack(x):
        return jnp.where(same, tile(x), 0.0)

    tiles = [(ci, gi) for ci in range(n_chunks) for gi in range(RWKV_W // GW)]
    ch = []
    for ci in range(n_chunks):
        rows = slice(ci * L, (ci + 1) * L)
        ld = ld_ref[rows, :]
        lg = _mm(tril, ld, precision=HI)
        g_end = lg[L - 1:L, :]
        ge_ref[ci] = g_end
        dec_out = jnp.exp(-lg)
        dec_tail = jnp.exp(g_end - lg)
        k = k_ref[rows, :]
        b = b_ref[rows, :]
        v = v_ref[rows, :]
        ch.append(dict(rows=rows, v=v, kap=kk_ref[rows, :] * jnp.exp(lg - ld), rt=r_ref[rows, :] * jnp.exp(lg),
                       kt=k * dec_out, bt=b * dec_out, khat=(k * dec_tail).astype(bf16),
                       bhat=(b * dec_tail).astype(bf16), vb=v.astype(bf16)))
    ops = []
    for ci, gi in tiles:
        c = ch[ci]
        lanes = slice(gi * GW, (gi + 1) * GW)
        kap_s = stack(c["kap"][:, lanes])
        rt_s = stack(c["rt"][:, lanes])
        ops.append(dict(lanes=lanes, kap_s=kap_s, rt_s=rt_s, kap_b=kap_s.astype(bf16), rt_b=rt_s.astype(bf16),
                        v_s=stack(c["v"][:, lanes]).astype(bf16), kt_t=tile(c["kt"][:, lanes]).astype(bf16),
                        bt_t=tile(c["bt"][:, lanes]).astype(bf16)))
    for o in ops:
        o["a_kb"] = jnp.where(strict, _nt(o["kap_b"], o["bt_t"]), 0.0)
    for o in ops:
        o["a_kk"] = jnp.where(strict, _nt(o["kap_b"], o["kt_t"]), 0.0).astype(bf16)
        o["a_rk"] = jnp.where(incl, _nt(o["rt_b"], o["kt_t"]), 0.0).astype(bf16)
        o["a_rb"] = jnp.where(incl, _nt(o["rt_b"], o["bt_t"]), 0.0).astype(bf16)
    for o in ops:
        o["p"] = -o["a_kb"]
        o["tinv"] = eye + o["p"]
    step = 2
    while step < L:
        for o in ops:
            pb = o["p"].astype(bf16)
            o["p"] = _mm(pb, pb)
        for o in ops:
            o["tinv"] = o["tinv"] + _bmm(o["tinv"], o["p"])
        step *= 2
    for o in ops:
        o["w1"] = _mm(o["a_kk"], o["v_s"])
        o["w2"] = _mm(o["a_rk"], o["v_s"])
    for o in ops:
        o["ku"] = _bmm(o["tinv"], jnp.concatenate([o["kap_s"], o["w1"]], axis=1))
    for o in ops:
        o["ru"] = _mm(o["a_rb"], o["ku"].astype(bf16))
    for (ci, gi), o in zip(tiles, ops):
        c = ch[ci]
        rows, lanes = c["rows"], o["lanes"]
        rh_ref[rows, lanes] = unstack(o["rt_s"] - o["ru"][:, :GW]).astype(rh_ref.dtype)
        oh_ref[rows, lanes] = unstack(o["w2"] - o["ru"][:, GW:])
        kh = unstack(o["ku"][:, :GW]).astype(bf16)
        uh = unstack(o["ku"][:, GW:]).astype(bf16)
        for pi in range(GW // PW):
            q = gi * (GW // PW) + pi
            loc = slice(pi * PW, (pi + 1) * PW)
            glo = slice(q * PW, (q + 1) * PW)
            nt_ref[ci, q] = (_tn(kh[:, loc], c["bhat"][:, glo]) * pair_blk).astype(nt_ref.dtype)
            ds_ref[ci, q] = (_tn(c["vb"][:, glo], c["khat"][:, glo]) - _tn(uh[:, loc], c["bhat"][:, glo])) * pair_blk


def _rwkv_seq_kernel(rh_ref, oh_ref, nt_ref, ds_ref, ge_ref, o_ref, s_ref):
    L, PW = RWKV_CHUNK, 2 * RWKV_HEAD_DIM
    n_chunks = rh_ref.shape[0] // L

    @pl.when(pl.program_id(0) == 0)
    def _():
        s_ref[...] = jnp.zeros_like(s_ref)

    pairs = range(RWKV_W // PW)
    lanes = [slice(q * PW, (q + 1) * PW) for q in pairs]
    s_mat = [s_ref[q] for q in pairs]
    for ci in range(n_chunks):
        rows = slice(ci * L, (ci + 1) * L)
        s_b = [s.astype(bf16) for s in s_mat]
        decay = jnp.exp(ge_ref[ci])
        for q in pairs:
            s_mat[q] = s_mat[q] * decay[:, lanes[q]] - _mm(s_b[q], nt_ref[ci, q]) + ds_ref[ci, q]
        for q in pairs:
            o_ref[rows, lanes[q]] = _nt(rh_ref[rows, lanes[q]], s_b[q]) + oh_ref[rows, lanes[q]]
    for q in pairs:
        s_ref[q] = s_mat[q]


def _rwkv_post_kernel(o_ref, g_ref, bonus_ref, lng_ref, lnb_ref, y_ref):
    inv = 1.0 / RWKV_HEAD_DIM
    o = o_ref[...]
    d = o - _head_sum(o) * inv
    var = _head_sum(d * d) * inv
    yn = d * lax.rsqrt(var + RWKV_GN_EPS) * lng_ref[...] + lnb_ref[...] + bonus_ref[...]
    y_ref[...] = (yn * g_ref[...]).astype(y_ref.dtype)


def _rwkv_scan(parts, ln_g, ln_b):
    r, ld, k, v, kk, b, g, bonus = parts
    T, W = r.shape
    L, PW = RWKV_CHUNK, 2 * RWKV_HEAD_DIM
    nc, npair = T // L, W // PW
    pre_rows = min(2 * L, T)
    seq_rows = min(4 * L, T)

    def specs(rows):
        n = rows // L
        row = pl.BlockSpec((rows, W), lambda c: (c, 0))
        mat = pl.BlockSpec((n, npair, PW, PW), lambda c: (c, 0, 0, 0))
        vec = pl.BlockSpec((n, 1, W), lambda c: (c, 0, 0))
        return row, mat, vec

    row, mat, vec = specs(pre_rows)
    rh, oh, nt, ds, ge = pl.pallas_call(
        _rwkv_pre_kernel,
        grid=(T // pre_rows,),
        in_specs=[row] * 6,
        out_specs=[row, row, mat, mat, vec],
        out_shape=[jax.ShapeDtypeStruct((T, W), bf16), jax.ShapeDtypeStruct((T, W), f32),
                   jax.ShapeDtypeStruct((nc, npair, PW, PW), bf16), jax.ShapeDtypeStruct((nc, npair, PW, PW), f32),
                   jax.ShapeDtypeStruct((nc, 1, W), f32)],
        compiler_params=_cp(("parallel",)),
        name="rwkv_pre",
    )(r, ld, k, v, kk, b)
    row, mat, vec = specs(seq_rows)
    o = pl.pallas_call(
        _rwkv_seq_kernel,
        grid=(T // seq_rows,),
        in_specs=[row, row, mat, mat, vec],
        out_specs=row,
        out_shape=jax.ShapeDtypeStruct((T, W), f32),
        scratch_shapes=[pltpu.VMEM((npair, PW, PW), f32)],
        compiler_params=_cp(("arbitrary",)),
        name="rwkv_seq",
    )(rh, oh, nt, ds, ge)
    tm = min(512, T)
    blk = pl.BlockSpec((tm, W), lambda i: (i, 0))
    one = pl.BlockSpec((1, W), lambda i: (0, 0))
    return pl.pallas_call(
        _rwkv_post_kernel,
        grid=(T // tm,),
        in_specs=[blk, blk, blk, one, one],
        out_specs=blk,
        out_shape=jax.ShapeDtypeStruct((T, W), bf16),
        compiler_params=_cp(("parallel",)),
        name="rwkv_post",
    )(o, g, bonus, ln_g.reshape(1, W), ln_b.reshape(1, W))


def _s5_dot(a, b):
    return jnp.dot(a.astype(bf16), b.astype(bf16), preferred_element_type=f32)


def _transpose_pieces(w):
    w = list(w)
    lane = lax.broadcasted_iota(jnp.int32, w[0].shape, 1)
    for d in (4, 2, 1):
        width = S5_GROUP * d
        low = (lane & width) == 0
        for i in range(8):
            if i & d == 0:
                a, b = w[i], w[i + d]
                w[i] = jnp.where(low, a, pltpu.roll(b, width, 1))
                w[i + d] = jnp.where(low, pltpu.roll(a, 128 - width, 1), b)
    return w


def _s5_split_kernel(*refs):
    z_refs, u_ref = refs[:-1], refs[-1]
    ncb = u_ref.shape[1]
    for j, z_ref in enumerate(z_refs):
        for b in range(S5_CHUNK // 8):
            w = [z_ref[pl.ds(8 * b + i, ncb, stride=S5_CHUNK), :] for i in range(8)]
            o = _transpose_pieces(w)
            for a in range(8):
                u_ref[8 * j + a, :, 128 * b:128 * (b + 1)] = o[a]


def _toeplitz(krow):
    P, LP = krow.shape
    lane = lax.broadcasted_iota(jnp.int32, (P, LP), 1)
    blocks = [krow]
    for s in range(1, LP // P):
        blocks.append(jnp.where(lane >= P * s, pltpu.roll(krow, P * s, 1), 0.0))
    return jnp.concatenate(blocks, axis=0)


def _s5_in_kernel(u_ref, krow_ref, br_ref, bi_ref, y_ref, xr_ref, xi_ref):
    for j in range(2):
        y_ref[j] = _s5_dot(u_ref[j], _toeplitz(krow_ref[j]))
    u2 = jnp.concatenate([u_ref[0], u_ref[1]], axis=1)
    xr_ref[...] = _s5_dot(u2, br_ref[0])
    xi_ref[...] = _s5_dot(u2, bi_ref[0])


def _s5_scan_kernel(xr_ref, xi_ref, ar_ref, ai_ref, sr_ref, si_ref):
    nc = xr_ref.shape[0]
    ar = ar_ref[...]
    ai = ai_ref[...]

    def body(c, carry):
        sr, si = carry
        sr_ref[pl.ds(c, 1), :] = sr
        si_ref[pl.ds(c, 1), :] = si
        xr = xr_ref[pl.ds(c, 1), :]
        xi = xi_ref[pl.ds(c, 1), :]
        return ar * sr - ai * si + xr, ar * si + ai * sr + xi

    z = jnp.zeros_like(ar)
    lax.fori_loop(0, nc, body, (z, z))


def _s5_out_kernel(y_ref, sr_ref, si_ref, cr_ref, ci_ref, u_ref, d_ref, o_ref):
    sr = sr_ref[...]
    si = si_ref[...]
    for j in range(2):
        y = y_ref[j] + _s5_dot(sr, cr_ref[j]) + _s5_dot(si, ci_ref[j]) + d_ref[j] * u_ref[j]
        o_ref[j] = 0.5 * y * (1.0 + jnp.tanh(0.7978845608028654 * (y + 0.044715 * (y * y * y))))


def _glu_kernel(y_ref, w_ref, b_ref, o_ref, *nat_refs):
    ncb = y_ref.shape[1]
    for j, nat_ref in enumerate(nat_refs):
        for b in range(S5_CHUNK // 8):
            o = [y_ref[8 * j + a, :, 128 * b:128 * (b + 1)] for a in range(8)]
            w = _transpose_pieces(o)
            for i in range(8):
                nat_ref[pl.ds(8 * b + i, ncb, stride=S5_CHUNK), :] = w[i]
    y = jnp.concatenate([r[...] for r in nat_refs], axis=1)
    o_ref[...] = (y * _sigmoid(_bmm(y, w_ref[...]) + b_ref[...])).astype(o_ref.dtype)


def _s5_tables(a_re, a_im, log_dt, b_re, b_im, c_re, c_im, d):
    G, N, P, L = S5_GROUPS, S5_STATE, S5_GROUP, S5_CHUNK
    D = a_re.shape[0]
    dt = jnp.exp(log_dt)[:, None, :, None]
    lags = jnp.arange(L + 1, dtype=f32)[None, :, None, None]
    mag = jnp.exp(a_re[:, None] * dt * lags)
    ang = a_im[:, None] * dt * lags
    pw_re, pw_im = mag * jnp.cos(ang), mag * jnp.sin(ang)
    den = a_re * a_re + a_im * a_im
    nr, ni = pw_re[:, 1] - 1.0, pw_im[:, 1]
    coef_re = ((nr * a_re + ni * a_im) / den)[..., None]
    coef_im = ((ni * a_re - nr * a_im) / den)[..., None]
    bb_re = coef_re * b_re - coef_im * b_im
    bb_im = coef_re * b_im + coef_im * b_re
    rep = lambda a, axis: jnp.repeat(a, P, axis=axis)
    pl_re, pl_im = rep(pw_re[:, :L].transpose(0, 2, 1, 3), 2), rep(pw_im[:, :L].transpose(0, 2, 1, 3), 2)
    ct_re, ct_im = jnp.tile(c_re, (1, 1, L, 1)), jnp.tile(c_im, (1, 1, L, 1))
    cb_re, cb_im = ct_re * pl_re - ct_im * pl_im, ct_re * pl_im + ct_im * pl_re
    krow = (jnp.einsum('dgxn,dgnq->dgqx', cb_re, bb_re, precision=HI)
            - jnp.einsum('dgxn,dgnq->dgqx', cb_im, bb_im, precision=HI))
    e_re = rep(pw_re[:, L - 1::-1][:, :L].transpose(0, 2, 1, 3), 2)
    e_im = rep(pw_im[:, L - 1::-1][:, :L].transpose(0, 2, 1, 3), 2)
    bq_re = jnp.tile(bb_re.transpose(0, 1, 3, 2), (1, 1, L, 1))
    bq_im = jnp.tile(bb_im.transpose(0, 1, 3, 2), (1, 1, L, 1))
    bs_re = (e_re * bq_re - e_im * bq_im).reshape(D, G // 2, 2, L * P, N)
    bs_im = (e_re * bq_im + e_im * bq_re).reshape(D, G // 2, 2, L * P, N)

    def pair_in(bs):
        zz = jnp.zeros_like(bs[:, :, 0])
        return jnp.concatenate([jnp.concatenate([bs[:, :, 0], zz], axis=3), jnp.concatenate([zz, bs[:, :, 1]], axis=3)], axis=2)

    f_re = rep(pw_re[:, 1:].transpose(0, 2, 3, 1), 3)
    f_im = rep(pw_im[:, 1:].transpose(0, 2, 3, 1), 3)
    cn_re = jnp.tile(c_re.transpose(0, 1, 3, 2), (1, 1, 1, L))
    cn_im = jnp.tile(c_im.transpose(0, 1, 3, 2), (1, 1, 1, L))
    cs_re = (cn_re * f_re - cn_im * f_im).reshape(D, G // 2, 2, N, L * P)
    cs_im = (-(cn_re * f_im + cn_im * f_re)).reshape(D, G // 2, 2, N, L * P)

    def pair_out(cs):
        zz = jnp.zeros_like(cs[:, :, 0])
        top = jnp.concatenate([cs[:, :, 0], zz], axis=2)[:, :, None]
        bot = jnp.concatenate([zz, cs[:, :, 1]], axis=2)[:, :, None]
        return jnp.concatenate([top, bot], axis=2).reshape(D, G, 2 * N, L * P)

    dd = jnp.tile(d.reshape(D, G, 1, P), (1, 1, 1, L))
    return dict(krow=krow, bsr=pair_in(bs_re), bsi=pair_in(bs_im), cr=pair_out(cs_re), ci=pair_out(cs_im),
                al_re=pw_re[:, L].reshape(D, 1, G * N), al_im=pw_im[:, L].reshape(D, 1, G * N), d=dd)


def _s5(z_tail, tab, glu_w, glu_b):
    T = z_tail.shape[0]
    G, N, P, L = S5_GROUPS, S5_STATE, S5_GROUP, S5_CHUNK
    nc, LP = T // L, L * P
    tm = min(1024, T)
    u = pl.pallas_call(
        _s5_split_kernel,
        grid=(T // tm,),
        in_specs=[pl.BlockSpec((tm, 128), lambda i, j=j: (i, j)) for j in range(S5_W // 128)],
        out_specs=pl.BlockSpec((G, tm // L, LP), lambda i: (0, i, 0)),
        out_shape=jax.ShapeDtypeStruct((G, nc, LP), f32),
        compiler_params=_cp(("parallel",)),
        name="s5_split",
    )(*[z_tail] * (S5_W // 128))
    grp = lambda s: pl.BlockSpec((2,) + s, lambda j: (j, 0, 0))
    one = lambda s: pl.BlockSpec((1,) + s, lambda j: (j, 0, 0))
    lanes = pl.BlockSpec((nc, 2 * N), lambda j: (0, j))
    y_in, xr, xi = pl.pallas_call(
        _s5_in_kernel,
        grid=(G // 2,),
        in_specs=[grp((nc, LP)), grp((P, LP)), one((2 * LP, 2 * N)), one((2 * LP, 2 * N))],
        out_specs=[grp((nc, LP)), lanes, lanes],
        out_shape=[jax.ShapeDtypeStruct((G, nc, LP), f32), jax.ShapeDtypeStruct((nc, G * N), f32),
                   jax.ShapeDtypeStruct((nc, G * N), f32)],
        compiler_params=_cp(("parallel",)),
        name="s5_in",
    )(u, tab["krow"], tab["bsr"], tab["bsi"])
    whole = lambda s: pl.BlockSpec(s, lambda: (0,) * len(s))
    sr, si = pl.pallas_call(
        _s5_scan_kernel,
        in_specs=[whole((nc, G * N)), whole((nc, G * N)), whole((1, G * N)), whole((1, G * N))],
        out_specs=[whole((nc, G * N)), whole((nc, G * N))],
        out_shape=[jax.ShapeDtypeStruct((nc, G * N), f32)] * 2,
        compiler_params=pltpu.CompilerParams(vmem_limit_bytes=VMEM_LIMIT),
        name="s5_scan",
    )(xr, xi, tab["al_re"], tab["al_im"])
    y = pl.pallas_call(
        _s5_out_kernel,
        grid=(G // 2,),
        in_specs=[grp((nc, LP)), lanes, lanes, grp((2 * N, LP)), grp((2 * N, LP)), grp((nc, LP)), grp((1, LP))],
        out_specs=grp((nc, LP)),
        out_shape=jax.ShapeDtypeStruct((G, nc, LP), f32),
        compiler_params=_cp(("parallel",)),
        name="s5_out",
    )(y_in, sr, si, tab["cr"], tab["ci"], u, tab["d"])
    return pl.pallas_call(
        _glu_kernel,
        grid=(T // tm,),
        in_specs=[pl.BlockSpec((G, tm // L, LP), lambda i: (0, i, 0)), pl.BlockSpec((S5_W, S5_W), lambda i: (0, 0)),
                  pl.BlockSpec((1, S5_W), lambda i: (0, 0))],
        out_specs=pl.BlockSpec((tm, S5_W), lambda i: (i, 0)),
        out_shape=jax.ShapeDtypeStruct((T, S5_W), bf16),
        scratch_shapes=[pltpu.VMEM((tm, 128), f32)] * (S5_W // 128),
        compiler_params=_cp(("parallel",)),
        name="s5_glu",
    )(y, glu_w, glu_b.reshape(1, S5_W))


def _pad_cols(w, n):
    return jnp.pad(w, ((0, 0), (0, n - w.shape[1])))


def _pad_rows(w, n):
    return jnp.pad(w, ((0, n - w.shape[0]), (0, 0)))


def _split_in_proj(w):
    r0 = M_IN
    l0 = r0 + 3 * RWKV_W
    s0 = M_IN + RWKV_IN
    g0 = s0 + S5_W
    main = jnp.concatenate([
        w[:, :4 * MLSTM_W], w[:, r0:l0],
        _pad_cols(w[:, l0:l0 + DECAY_LORA], LORA_PAD),
        _pad_cols(w[:, l0 + DECAY_LORA:l0 + DECAY_LORA + AAA_LORA], LORA_PAD),
        w[:, l0 + DECAY_LORA + AAA_LORA:s0]], axis=1).astype(bf16)
    gates = w[:, g0:].astype(bf16)
    tail = _pad_cols(jnp.concatenate([w[:, s0:g0], w[:, 4 * MLSTM_W:M_IN]], axis=1), Z_TAIL).astype(bf16)
    return main, gates, tail


def _split_mu(mu):
    l0 = 3 * RWKV_W
    return jnp.concatenate([
        mu[:l0], jnp.pad(mu[l0:l0 + DECAY_LORA], (0, LORA_PAD - DECAY_LORA)),
        jnp.pad(mu[l0 + DECAY_LORA:l0 + DECAY_LORA + AAA_LORA], (0, LORA_PAD - AAA_LORA)),
        mu[l0 + DECAY_LORA + AAA_LORA:]]).reshape(1, RWKV_PAD)


def kernel(x, p, norm_mix_g, w_in, mlstm_conv, mlstm_ib, mlstm_fb, mlstm_norm_g, rwkv_mu, rwkv_w0, rwkv_w2, rwkv_a0, rwkv_a2, rwkv_g2, rwkv_kk, rwkv_ka, rwkv_rk, rwkv_ln_g, rwkv_ln_b, s5_a_re, s5_a_im, s5_log_dt, s5_b_re, s5_b_im, s5_c_re, s5_c_im, s5_d, s5_glu_w, s5_glu_b, w_up_m, w_up_r, w_up_s, w_out, norm_ffn_g, ffn_w_gate, ffn_w_up, ffn_w_down, norm_ple_g, ple_w_gate, ple_w_proj, final_norm_g):
    B, T, D = x.shape
    depth = w_in.shape[0]
    s5_tab = _s5_tables(s5_a_re, s5_a_im, s5_log_dt, s5_b_re, s5_b_im, s5_c_re, s5_c_im, s5_d)
    outs = []
    for bi in range(B):
        h = x[bi]
        xn = _rmsnorm(h, norm_mix_g[0], bf16)
        for i in range(depth):
            w_main, w_gates, w_tail = _split_in_proj(w_in[i])
            z_main = _proj(xn, w_main, 1024)
            z_gate = _proj(xn, w_gates, 1024, bf16, gate=True)
            z_tail = _proj(xn, w_tail, Z_TAIL)

            gate_b = jnp.pad(jnp.concatenate([mlstm_ib[i], mlstm_fb[i]]), (0, 128 - 2 * MLSTM_HEADS)).reshape(1, 128)
            y_m = _mlstm(z_main, z_tail, mlstm_conv[i], gate_b, mlstm_norm_g[i])

            vec = lambda a: a.reshape(1, RWKV_W)
            parts = _rwkv_prep(z_main, _split_mu(rwkv_mu[i]), vec(rwkv_w0[i]), vec(rwkv_a0[i]), vec(rwkv_kk[i]),
                               vec(rwkv_ka[i]), vec(rwkv_rk[i]), _pad_rows(rwkv_w2[i], LORA_PAD).astype(bf16),
                               _pad_rows(rwkv_a2[i], LORA_PAD).astype(bf16), rwkv_g2[i].astype(bf16))
            y_r = _rwkv_scan(parts, rwkv_ln_g[i], rwkv_ln_b[i])

            y_s = _s5(z_tail, {k: v[i] for k, v in s5_tab.items()}, s5_glu_w[i].astype(bf16), s5_glu_b[i])

            mixed = _merge(y_m, y_r, y_s, z_gate, w_up_m, w_up_r, w_up_s, i)
            h, hn = _resid_norm(mixed, w_out[i].astype(bf16), h, norm_ffn_g[i], D_MODEL)
            act = _ffn_up(hn, ffn_w_gate, ffn_w_up, i)
            h = _resid_mm(act, ffn_w_down, h, i)
            last = i == depth - 1
            g_next = final_norm_g if last else norm_mix_g[i + 1]
            h, xn = _ple(h, p[i, bi].astype(bf16), ple_w_gate[i].astype(bf16), ple_w_proj[i].astype(bf16), norm_ple_g[i],
                         g_next, f32 if last else bf16)
        outs.append(xn)
    return jnp.stack(outs)
```

```python
import functools

import jax
import jax.numpy as jnp
import numpy as np
from jax import lax
from jax.experimental import pallas as pl
from jax.experimental.pallas import tpu as pltpu

f32 = jnp.float32
bf16 = jnp.bfloat16
HI = lax.Precision.HIGHEST

D_MODEL = 2048
PLE_DIM = 256
RMS_EPS = 1e-6
MLSTM_HEADS = 4
MLSTM_HEAD_DIM = 256
MLSTM_W = 1024
MLSTM_CHUNK = 256
GATE_SOFTCAP = 15.0
RWKV_HEADS = 8
RWKV_HEAD_DIM = 64
RWKV_W = 512
DECAY_LORA = 96
AAA_LORA = 96
GATE_LORA = 256
LORA_PAD = 128
RWKV_GN_EPS = 64e-5
RWKV_CHUNK = 64
S5_GROUP = 16
S5_GROUPS = 32
S5_W = 512
S5_STATE = 64
S5_CHUNK = 16
FFN_HIDDEN = 5632
M_IN = 4 * MLSTM_W + 2 * MLSTM_HEADS
RWKV_IN = 3 * RWKV_W + DECAY_LORA + AAA_LORA + GATE_LORA
RWKV_PAD = 3 * RWKV_W + 2 * LORA_PAD + GATE_LORA
Z_TAIL = S5_W + 128
VMEM_LIMIT = 56 * 1024 * 1024


def _cp(sem):
    return pltpu.CompilerParams(dimension_semantics=sem, vmem_limit_bytes=VMEM_LIMIT)


def _nt(a, b, **kw):
    return lax.dot_general(a, b, (((1,), (1,)), ((), ())), preferred_element_type=f32, **kw)


def _tn(a, b, **kw):
    return lax.dot_general(a, b, (((0,), (0,)), ((), ())), preferred_element_type=f32, **kw)


def _mm(a, b, **kw):
    return jnp.dot(a, b, preferred_element_type=f32, **kw)


def _bmm(a, b):
    return jnp.dot(a.astype(bf16), b.astype(bf16), preferred_element_type=f32)


def _sigmoid(x):
    return 1.0 / (1.0 + jnp.exp(-x))


def _rms(x, g):
    return x * lax.rsqrt(jnp.mean(x * x, axis=-1, keepdims=True) + RMS_EPS) * g


def _norm_kernel(x_ref, g_ref, o_ref):
    o_ref[...] = _rms(x_ref[...], g_ref[...]).astype(o_ref.dtype)


def _rmsnorm(x, g, out_dtype):
    T, D = x.shape
    tm = min(512, T)
    return pl.pallas_call(
        _norm_kernel,
        grid=(T // tm,),
        in_specs=[pl.BlockSpec((tm, D), lambda i: (i, 0)), pl.BlockSpec((1, D), lambda i: (0, 0))],
        out_specs=pl.BlockSpec((tm, D), lambda i: (i, 0)),
        out_shape=jax.ShapeDtypeStruct((T, D), out_dtype),
        compiler_params=_cp(("parallel",)),
        name="rmsnorm",
    )(x, g.reshape(1, D))


def _proj_kernel(a_ref, w_ref, o_ref, *, gate):
    z = _mm(a_ref[...], w_ref[...])
    o_ref[...] = (0.5 * jnp.tanh(0.5 * z) + 0.5 if gate else z).astype(o_ref.dtype)


def _proj(a, w, layer, tn, out_dtype=f32, gate=False):
    T, K = a.shape
    N = w.shape[2]
    tm = min(1024, T)
    return pl.pallas_call(
        functools.partial(_proj_kernel, gate=gate),
        grid=(N // tn, T // tm),
        in_specs=[pl.BlockSpec((tm, K), lambda j, i: (i, 0)), pl.BlockSpec((None, K, tn), lambda j, i: (layer, 0, j))],
        out_specs=pl.BlockSpec((tm, tn), lambda j, i: (i, j)),
        out_shape=jax.ShapeDtypeStruct((T, N), out_dtype),
        compiler_params=_cp(("parallel", "parallel")),
        name="in_proj",
    )(a, w)


def _merge_kernel(ym_ref, yr_ref, ys_ref, gm_ref, gr_ref, gs_ref, um_ref, ur_ref, us_ref, o_ref, umb_ref, urb_ref, usb_ref):
    @pl.when(pl.program_id(1) == 0)
    def _():
        umb_ref[...] = um_ref[...].astype(bf16)
        urb_ref[...] = ur_ref[...].astype(bf16)
        usb_ref[...] = us_ref[...].astype(bf16)

    acc = gm_ref[...] * _mm(ym_ref[...], umb_ref[...])
    acc += gr_ref[...] * _mm(yr_ref[...], urb_ref[...])
    acc += gs_ref[...] * _mm(ys_ref[...], usb_ref[...])
    o_ref[...] = acc.astype(o_ref.dtype)


def _merge(ym, yr, ys, gates, um, ur, us, layer):
    T = ym.shape[0]
    tm, tn = min(512, T), 1024
    nj = D_MODEL // tn

    def gate_spec(b):
        return pl.BlockSpec((tm, tn), lambda j, i, b=b: (i, b * nj + j))

    def y_spec(w):
        return pl.BlockSpec((tm, w), lambda j, i: (i, 0))

    def u_spec(w):
        return pl.BlockSpec((None, w, tn), lambda j, i: (layer, 0, j))

    return pl.pallas_call(
        _merge_kernel,
        grid=(nj, T // tm),
        in_specs=[y_spec(MLSTM_W), y_spec(RWKV_W), y_spec(S5_W), gate_spec(0), gate_spec(1), gate_spec(2),
                  u_spec(MLSTM_W), u_spec(RWKV_W), u_spec(S5_W)],
        out_specs=pl.BlockSpec((tm, tn), lambda j, i: (i, j)),
        out_shape=jax.ShapeDtypeStruct((T, D_MODEL), bf16),
        scratch_shapes=[pltpu.VMEM((MLSTM_W, tn), bf16), pltpu.VMEM((RWKV_W, tn), bf16), pltpu.VMEM((S5_W, tn), bf16)],
        compiler_params=_cp(("parallel", "arbitrary")),
        name="gated_merge",
    )(ym, yr, ys, gates, gates, gates, um, ur, us)


def _resid_norm_kernel(a_ref, w_ref, h_ref, g_ref, ho_ref, no_ref, acc_ref):
    k = pl.program_id(1)

    @pl.when(k == 0)
    def _():
        acc_ref[...] = h_ref[...]

    acc_ref[...] += _mm(a_ref[...], w_ref[...])

    @pl.when(k == pl.num_programs(1) - 1)
    def _():
        hn = acc_ref[...]
        ho_ref[...] = hn
        no_ref[...] = _rms(hn, g_ref[...]).astype(no_ref.dtype)


def _resid_norm(a, w, h, g, tk, norm_dtype=bf16):
    T, K = a.shape
    tm = min(512, T)
    return pl.pallas_call(
        _resid_norm_kernel,
        grid=(T // tm, K // tk),
        in_specs=[pl.BlockSpec((tm, tk), lambda i, k: (i, k)), pl.BlockSpec((tk, D_MODEL), lambda i, k: (k, 0)),
                  pl.BlockSpec((tm, D_MODEL), lambda i, k: (i, 0)), pl.BlockSpec((1, D_MODEL), lambda i, k: (0, 0))],
        out_specs=[pl.BlockSpec((tm, D_MODEL), lambda i, k: (i, 0)), pl.BlockSpec((tm, D_MODEL), lambda i, k: (i, 0))],
        out_shape=[jax.ShapeDtypeStruct((T, D_MODEL), f32), jax.ShapeDtypeStruct((T, D_MODEL), norm_dtype)],
        scratch_shapes=[pltpu.VMEM((tm, D_MODEL), f32)],
        compiler_params=_cp(("parallel", "arbitrary")),
        name="resid_matmul_norm",
    )(a, w, h, g.reshape(1, D_MODEL))


def _ffn_up_kernel(a_ref, wg_ref, wu_ref, o_ref, wgb_ref, wub_ref):
    @pl.when(pl.program_id(1) == 0)
    def _():
        wgb_ref[...] = wg_ref[...].astype(bf16)
        wub_ref[...] = wu_ref[...].astype(bf16)

    a = a_ref[...]
    gt = _mm(a, wgb_ref[...])
    up = _mm(a, wub_ref[...])
    o_ref[...] = (gt * _sigmoid(gt) * up).astype(o_ref.dtype)


def _ffn_up(a, wg, wu, layer):
    T, K = a.shape
    N = wg.shape[2]
    tm, tn = min(1024, T), 512
    return pl.pallas_call(
        _ffn_up_kernel,
        grid=(N // tn, T // tm),
        in_specs=[pl.BlockSpec((tm, K), lambda j, i: (i, 0)), pl.BlockSpec((None, K, tn), lambda j, i: (layer, 0, j)),
                  pl.BlockSpec((None, K, tn), lambda j, i: (layer, 0, j))],
        out_specs=pl.BlockSpec((tm, tn), lambda j, i: (i, j)),
        out_shape=jax.ShapeDtypeStruct((T, N), bf16),
        scratch_shapes=[pltpu.VMEM((K, tn), bf16), pltpu.VMEM((K, tn), bf16)],
        compiler_params=_cp(("parallel", "arbitrary")),
        name="ffn_up",
    )(a, wg, wu)


def _resid_mm_kernel(a_ref, w_ref, h_ref, o_ref, wb_ref):
    @pl.when(pl.program_id(1) == 0)
    def _():
        wb_ref[...] = w_ref[...].astype(bf16)

    o_ref[...] = h_ref[...] + _mm(a_ref[...], wb_ref[...])


def _resid_mm(a, w, h, layer):
    T, K = a.shape
    N = w.shape[2]
    tm, tn = min(512, T), 512
    return pl.pallas_call(
        _resid_mm_kernel,
        grid=(N // tn, T // tm),
        in_specs=[pl.BlockSpec((tm, K), lambda j, i: (i, 0)), pl.BlockSpec((None, K, tn), lambda j, i: (layer, 0, j)),
                  pl.BlockSpec((tm, tn), lambda j, i: (i, j))],
        out_specs=pl.BlockSpec((tm, tn), lambda j, i: (i, j)),
        out_shape=jax.ShapeDtypeStruct((T, N), f32),
        scratch_shapes=[pltpu.VMEM((K, tn), bf16)],
        compiler_params=_cp(("parallel", "arbitrary")),
        name="resid_matmul",
    )(a, w, h)


def _ple_kernel(h_ref, p_ref, wg_ref, wp_ref, gp_ref, g_ref, ho_ref, no_ref):
    tm = h_ref.shape[0]
    parts = [slice(0, tm // 2), slice(tm // 2, tm)]
    hs = [h_ref[r, :] for r in parts]
    hps = [_rms(h, gp_ref[...]).astype(bf16) for h in hs]
    gates = [_sigmoid(_mm(hp, wg_ref[...])) for hp in hps]
    projs = [_mm(p_ref[r, :], wp_ref[...]) for r in parts]
    for r, h, gate, proj in zip(parts, hs, gates, projs):
        hn = h + proj * gate
        ho_ref[r, :] = hn
        no_ref[r, :] = _rms(hn, g_ref[...]).astype(no_ref.dtype)


def _ple(h, p, wg, wp, g_ple, g, norm_dtype):
    T = h.shape[0]
    tm = min(512, T)
    row = lambda w: pl.BlockSpec((tm, w), lambda i: (i, 0))
    full = lambda s: pl.BlockSpec(s, lambda i: (0, 0))
    return pl.pallas_call(
        _ple_kernel,
        grid=(T // tm,),
        in_specs=[row(D_MODEL), row(PLE_DIM), full((D_MODEL, D_MODEL)), full((PLE_DIM, D_MODEL)), full((1, D_MODEL)),
                  full((1, D_MODEL))],
        out_specs=[row(D_MODEL), row(D_MODEL)],
        out_shape=[jax.ShapeDtypeStruct((T, D_MODEL), f32), jax.ShapeDtypeStruct((T, D_MODEL), norm_dtype)],
        compiler_params=_cp(("parallel",)),
        name="ple_norm",
    )(h, p, wg, wp, g_ple.reshape(1, D_MODEL), g.reshape(1, D_MODEL))


def _mlstm_kernel(zq_ref, zk_ref, v_ref, o_ref, gz_ref, cq_ref, ck_ref, gb_ref, ng_ref, y_ref,
                  ct_ref, n_ref, qbuf_ref, kbuf_ref):
    c = pl.program_id(0)
    L, W = zq_ref.shape
    H, dh = MLSTM_HEADS, MLSTM_HEAD_DIM
    heads = range(H)

    @pl.when(c == 0)
    def _():
        ct_ref[...] = jnp.zeros_like(ct_ref)
        n_ref[...] = jnp.zeros_like(n_ref)
        qbuf_ref[0:8, :] = jnp.zeros((8, W), f32)
        kbuf_ref[0:8, :] = jnp.zeros((8, W), f32)

    def conv_silu(z_ref, w_ref, buf_ref):
        x = z_ref[...]
        buf_ref[8:, :] = x
        w = w_ref[...]
        acc = x * w[0:1, :]
        for j in range(1, w.shape[0]):
            acc = acc + buf_ref[pl.ds(8 - j, L), :] * w[j:j + 1, :]
        buf_ref[0:8, :] = x[L - 8:, :]
        return acc * _sigmoid(acc)

    q = conv_silu(zq_ref, cq_ref, qbuf_ref) * (dh ** -0.5)
    k = conv_silu(zk_ref, ck_ref, kbuf_ref)

    g = gz_ref[...] + gb_ref[...]
    sc = GATE_SOFTCAP * jnp.tanh(g / GATE_SOFTCAP)
    logf = -jnp.log(1.0 + jnp.exp(-sc))
    row = lax.broadcasted_iota(jnp.int32, (L, L), 0)
    col = lax.broadcasted_iota(jnp.int32, (L, L), 1)
    causal = row >= col
    bcs = _mm(causal.astype(f32), logf, precision=HI)
    b128 = pltpu.roll(bcs, 128 - H, 1)
    e128 = sc - b128
    e_t = e128.T

    sl = [slice(h * dh, (h + 1) * dh) for h in heads]
    b_col = [b128[:, h:h + 1] for h in heads]
    e_col = [e128[:, h:h + 1] for h in heads]
    wmat = [jnp.where(causal, jnp.exp(jnp.where(causal, b_col[h] + e_t[h:h + 1, :], 0.0)), 0.0) for h in heads]
    b_end = [b_col[h][L - 1:L, :] for h in heads]
    eb = [jnp.exp(b_col[h]) for h in heads]
    qh = [q[:, sl[h]] for h in heads]
    kh = [k[:, sl[h]] for h in heads]
    qb = [x.astype(bf16) for x in qh]
    kb = [x.astype(bf16) for x in kh]
    vb = [v_ref[:, sl[h]].astype(bf16) for h in heads]
    ct = [ct_ref[h] for h in heads]
    nn = [n_ref[:, sl[h]] for h in heads]
    s = [_nt(qb[h], kb[h]) * wmat[h] for h in heads]
    inter = [_mm(qb[h], ct[h].astype(bf16)) for h in heads]
    num = [_mm(s[h].astype(bf16), vb[h]) + eb[h] * inter[h] for h in heads]
    ng = ng_ref[...]
    for h in heads:
        den = jnp.sum(s[h], axis=1, keepdims=True) + eb[h] * jnp.sum(qh[h] * nn[h], axis=1, keepdims=True)
        hh = num[h] / jnp.maximum(jnp.abs(den), 1.0)
        hh = hh * lax.rsqrt(jnp.mean(hh * hh, axis=1, keepdims=True) + RMS_EPS) * ng[:, sl[h]]
        y_ref[:, sl[h]] = (_sigmoid(o_ref[:, sl[h]]) * hh).astype(y_ref.dtype)
    kw = [kh[h] * jnp.exp(b_end[h] + e_col[h]) for h in heads]
    upd = [_tn(kw[h].astype(bf16), vb[h]) for h in heads]
    for h in heads:
        decay = jnp.exp(b_end[h])
        ct_ref[h] = decay * ct[h] + upd[h]
        n_ref[:, sl[h]] = decay * nn[h] + jnp.sum(kw[h], axis=0, keepdims=True)


def _mlstm(z_main, z_tail, conv_w, gate_b, norm_g):
    T = z_main.shape[0]
    L, dh, H, W = min(MLSTM_CHUNK, T), MLSTM_HEAD_DIM, MLSTM_HEADS, MLSTM_W
    blk = lambda j: pl.BlockSpec((L, W), lambda c, j=j: (c, j))
    return pl.pallas_call(
        _mlstm_kernel,
        grid=(T // L,),
        in_specs=[blk(0), blk(1), blk(2), blk(3),
                  pl.BlockSpec((L, 128), lambda c: (c, S5_W // 128)),
                  pl.BlockSpec((4, W), lambda c: (0, 0)), pl.BlockSpec((4, W), lambda c: (0, 1)),
                  pl.BlockSpec((1, 128), lambda c: (0, 0)), pl.BlockSpec((1, W), lambda c: (0, 0))],
        out_specs=pl.BlockSpec((L, W), lambda c: (c, 0)),
        out_shape=jax.ShapeDtypeStruct((T, W), bf16),
        scratch_shapes=[pltpu.VMEM((H, dh, dh), f32), pltpu.VMEM((1, W), f32),
                        pltpu.VMEM((L + 8, W), f32), pltpu.VMEM((L + 8, W), f32)],
        compiler_params=_cp(("arbitrary",)),
        name="mlstm",
    )(z_main, z_main, z_main, z_main, z_tail, conv_w, conv_w, gate_b, norm_g.reshape(1, W))


def _iota_div(shape, dim, width):
    return lax.shift_right_logical(lax.broadcasted_iota(jnp.int32, shape, dim), int(np.log2(width)))


def _head_ones(n, width):
    return (_iota_div((n, n), 0, width) == _iota_div((n, n), 1, width)).astype(f32)


def _head_sum(x):
    blk = _head_ones(RWKV_W, RWKV_HEAD_DIM).astype(bf16)
    hi = x.astype(bf16)
    r1 = x - hi.astype(f32)
    mid = r1.astype(bf16)
    lo = (r1 - mid.astype(f32)).astype(bf16)
    return _mm(hi, blk) + _mm(mid, blk) + _mm(lo, blk)


def _rwkv_prep_kernel(z_ref, mu_ref, w0_ref, a0_ref, kkg_ref, ka_ref, rk_ref, w2_ref, a2_ref, g2_ref,
                      r_ref, ld_ref, k_ref, v_ref, kk_ref, b_ref, g_ref, bonus_ref, buf_ref):
    i = pl.program_id(0)
    tm = z_ref.shape[0]
    W = RWKV_W

    @pl.when(i == 0)
    def _():
        buf_ref[0:8, :] = jnp.zeros((8, buf_ref.shape[1]), f32)

    x = z_ref[...]
    buf_ref[8:, :] = x
    xs = x + (buf_ref[pl.ds(7, tm), :] - x) * mu_ref[...]
    buf_ref[0:8, :] = x[tm - 8:, :]
    r = xs[:, 0:W]
    k = xs[:, W:2 * W]
    v = xs[:, 2 * W:3 * W]
    wl = xs[:, 3 * W:3 * W + LORA_PAD]
    al = xs[:, 3 * W + LORA_PAD:3 * W + 2 * LORA_PAD]
    gl = xs[:, 3 * W + 2 * LORA_PAD:]
    t = w0_ref[...] + _bmm(jnp.tanh(wl), w2_ref[...])
    w = -(jnp.maximum(-t, 0.0) + jnp.log(1.0 + jnp.exp(-jnp.abs(t)))) - 0.5
    a = _sigmoid(a0_ref[...] + _bmm(al, a2_ref[...]))
    g = _bmm(_sigmoid(gl), g2_ref[...])
    kk = k * kkg_ref[...]
    ss = _head_sum(kk * kk)
    kk = kk / jnp.maximum(jnp.sqrt(ss), 1e-12)
    k2 = k * (1.0 + (a - 1.0) * ka_ref[...])
    bonus = _head_sum(r * k2 * rk_ref[...]) * v
    r_ref[...] = r
    ld_ref[...] = -jnp.exp(w)
    k_ref[...] = k2
    v_ref[...] = v
    kk_ref[...] = kk
    b_ref[...] = kk * a
    g_ref[...] = g
    bonus_ref[...] = bonus


def _rwkv_prep(z_main, mu, w0, a0, kkg, ka, rk, w2, a2, g2):
    T = z_main.shape[0]
    tm = min(256, T)
    W = RWKV_W
    vec = lambda n: pl.BlockSpec((1, n), lambda i: (0, 0))
    mat = lambda s: pl.BlockSpec(s, lambda i: (0, 0))
    out = pl.BlockSpec((tm, W), lambda i: (i, 0))
    return pl.pallas_call(
        _rwkv_prep_kernel,
        grid=(T // tm,),
        in_specs=[pl.BlockSpec((tm, RWKV_PAD), lambda i: (i, 4 * MLSTM_W // RWKV_PAD)), vec(RWKV_PAD), vec(W), vec(W),
                  vec(W), vec(W), vec(W), mat((LORA_PAD, W)), mat((LORA_PAD, W)), mat((GATE_LORA, W))],
        out_specs=[out] * 8,
        out_shape=[jax.ShapeDtypeStruct((T, W), f32)] * 8,
        scratch_shapes=[pltpu.VMEM((tm + 8, RWKV_PAD), f32)],
        compiler_params=_cp(("arbitrary",)),
        name="rwkv_prep",
    )(z_main, mu, w0, a0, kkg, ka, rk, w2, a2, g2)


def _rwkv_pre_kernel(r_ref, ld_ref, k_ref, v_ref, kk_ref, b_ref, rh_ref, oh_ref, nt_ref, ds_ref, ge_ref):
    L, dh = RWKV_CHUNK, RWKV_HEAD_DIM
    GW = 4 * dh
    PW = 2 * dh
    n_chunks = r_ref.shape[0] // L
    rr = lax.broadcasted_iota(jnp.int32, (GW, GW), 0)
    cc = lax.broadcasted_iota(jnp.int32, (GW, GW), 1)
    same = _iota_div((GW, GW), 0, L) == _iota_div((GW, GW), 1, dh)
    strict = same & (rr > cc)
    incl = same & (rr >= cc)
    eye = jnp.where(rr == cc, 1.0, 0.0)
    pair_blk = _head_ones(PW, dh)
    tr = lax.broadcasted_iota(jnp.int32, (L, L), 0)
    tc = lax.broadcasted_iota(jnp.int32, (L, L), 1)
    tril = (tr >= tc).astype(f32)

    def unstack(x):
        acc = x[0:L, :]
        for hh in range(1, GW // L):
            acc = acc + x[hh * L:(hh + 1) * L, :]
        return acc

    def tile(x):
        return jnp.concatenate([x] * (GW // L), axis=0)

    def stack(x):
        return jnp.where(same, tile(x), 0.0)

    tiles = [(ci, gi) for ci in range(n_chunks) for gi in range(RWKV_W // GW)]
    ch = []
    for ci in range(n_chunks):
        rows = slice(ci * L, (ci + 1) * L)
        ld = ld_ref[rows, :]
        lg = _mm(tril, ld, precision=HI)
        g_end = lg[L - 1:L, :]
        ge_ref[ci] = g_end
        dec_out = jnp.exp(-lg)
        dec_tail = jnp.exp(g_end - lg)
        k = k_ref[rows, :]
        b = b_ref[rows, :]
        v = v_ref[rows, :]
        ch.append(dict(rows=rows, v=v, kap=kk_ref[rows, :] * jnp.exp(lg - ld), rt=r_ref[rows, :] * jnp.exp(lg),
                       kt=k * dec_out, bt=b * dec_out, khat=(k * dec_tail).astype(bf16),
                       bhat=(b * dec_tail).astype(bf16), vb=v.astype(bf16)))
    ops = []
    for ci, gi in tiles:
        c = ch[ci]
        lanes = slice(gi * GW, (gi + 1) * GW)
        kap_s = stack(c["kap"][:, lanes])
        rt_s = stack(c["rt"][:, lanes])
        ops.append(dict(lanes=lanes, kap_s=kap_s, rt_s=rt_s, kap_b=kap_s.astype(bf16), rt_b=rt_s.astype(bf16),
                        v_s=stack(c["v"][:, lanes]).astype(bf16), kt_t=tile(c["kt"][:, lanes]).astype(bf16),
                        bt_t=tile(c["bt"][:, lanes]).astype(bf16)))
    for o in ops:
        o["a_kb"] = jnp.where(strict, _nt(o["kap_b"], o["bt_t"]), 0.0)
    for o in ops:
        o["a_kk"] = jnp.where(strict, _nt(o["kap_b"], o["kt_t"]), 0.0).astype(bf16)
        o["a_rk"] = jnp.where(incl, _nt(o["rt_b"], o["kt_t"]), 0.0).astype(bf16)
        o["a_rb"] = jnp.where(incl, _nt(o["rt_b"], o["bt_t"]), 0.0).astype(bf16)
    for o in ops:
        o["p"] = -o["a_kb"]
        o["tinv"] = eye + o["p"]
    step = 2
    while step < L:
        for o in ops:
            pb = o["p"].astype(bf16)
            o["p"] = _mm(pb, pb)
        for o in ops:
            o["tinv"] = o["tinv"] + _bmm(o["tinv"], o["p"])
        step *= 2
    for o in ops:
        o["w1"] = _mm(o["a_kk"], o["v_s"])
        o["w2"] = _mm(o["a_rk"], o["v_s"])
    for o in ops:
        o["ku"] = _bmm(o["tinv"], jnp.concatenate([o["kap_s"], o["w1"]], axis=1))
    for o in ops:
        o["ru"] = _mm(o["a_rb"], o["ku"].astype(bf16))
    for (ci, gi), o in zip(tiles, ops):
        c = ch[ci]
        rows, lanes = c["rows"], o["lanes"]
        rh_ref[rows, lanes] = unstack(o["rt_s"] - o["ru"][:, :GW]).astype(rh_ref.dtype)
        oh_ref[rows, lanes] = unstack(o["w2"] - o["ru"][:, GW:])
        kh = unstack(o["ku"][:, :GW]).astype(bf16)
        uh = unstack(o["ku"][:, GW:]).astype(bf16)
        for pi in range(GW // PW):
            q = gi * (GW // PW) + pi
            loc = slice(pi * PW, (pi + 1) * PW)
            glo = slice(q * PW, (q + 1) * PW)
            nt_ref[ci, q] = (_tn(kh[:, loc], c["bhat"][:, glo]) * pair_blk).astype(nt_ref.dtype)
            ds_ref[ci, q] = (_tn(c["vb"][:, glo], c["khat"][:, glo]) - _tn(uh[:, loc], c["bhat"][:, glo])) * pair_blk


def _rwkv_seq_kernel(rh_ref, oh_ref, nt_ref, ds_ref, ge_ref, o_ref, s_ref):
    L, PW = RWKV_CHUNK, 2 * RWKV_HEAD_DIM
    n_chunks = rh_ref.shape[0] // L

    @pl.when(pl.program_id(0) == 0)
    def _():
        s_ref[...] = jnp.zeros_like(s_ref)

    pairs = range(RWKV_W // PW)
    lanes = [slice(q * PW, (q + 1) * PW) for q in pairs]
    s_mat = [s_ref[q] for q in pairs]
    for ci in range(n_chunks):
        rows = slice(ci * L, (ci + 1) * L)
        s_b = [s.astype(bf16) for s in s_mat]
        decay = jnp.exp(ge_ref[ci])
        for q in pairs:
            s_mat[q] = s_mat[q] * decay[:, lanes[q]] - _mm(s_b[q], nt_ref[ci, q]) + ds_ref[ci, q]
        for q in pairs:
            o_ref[rows, lanes[q]] = _nt(rh_ref[rows, lanes[q]], s_b[q]) + oh_ref[rows, lanes[q]]
    for q in pairs:
        s_ref[q] = s_mat[q]


def _rwkv_post_kernel(o_ref, g_ref, bonus_ref, lng_ref, lnb_ref, y_ref):
    inv = 1.0 / RWKV_HEAD_DIM
    o = o_ref[...]
    d = o - _head_sum(o) * inv
    var = _head_sum(d * d) * inv
    yn = d * lax.rsqrt(var + RWKV_GN_EPS) * lng_ref[...] + lnb_ref[...] + bonus_ref[...]
    y_ref[...] = (yn * g_ref[...]).astype(y_ref.dtype)


def _rwkv_scan(parts, ln_g, ln_b):
    r, ld, k, v, kk, b, g, bonus = parts
    T, W = r.shape
    L, PW = RWKV_CHUNK, 2 * RWKV_HEAD_DIM
    nc, npair = T // L, W // PW
    pre_rows = min(2 * L, T)
    seq_rows = min(4 * L, T)

    def specs(rows):
        n = rows // L
        row = pl.BlockSpec((rows, W), lambda c: (c, 0))
        mat = pl.BlockSpec((n, npair, PW, PW), lambda c: (c, 0, 0, 0))
        vec = pl.BlockSpec((n, 1, W), lambda c: (c, 0, 0))
        return row, mat, vec

    row, mat, vec = specs(pre_rows)
    rh, oh, nt, ds, ge = pl.pallas_call(
        _rwkv_pre_kernel,
        grid=(T // pre_rows,),
        in_specs=[row] * 6,
        out_specs=[row, row, mat, mat, vec],
        out_shape=[jax.ShapeDtypeStruct((T, W), bf16), jax.ShapeDtypeStruct((T, W), f32),
                   jax.ShapeDtypeStruct((nc, npair, PW, PW), bf16), jax.ShapeDtypeStruct((nc, npair, PW, PW), f32),
                   jax.ShapeDtypeStruct((nc, 1, W), f32)],
        compiler_params=_cp(("parallel",)),
        name="rwkv_pre",
    )(r, ld, k, v, kk, b)
    row, mat, vec = specs(seq_rows)
    o = pl.pallas_call(
        _rwkv_seq_kernel,
        grid=(T // seq_rows,),
        in_specs=[row, row, mat, mat, vec],
        out_specs=row,
        out_shape=jax.ShapeDtypeStruct((T, W), f32),
        scratch_shapes=[pltpu.VMEM((npair, PW, PW), f32)],
        compiler_params=_cp(("arbitrary",)),
        name="rwkv_seq",
    )(rh, oh, nt, ds, ge)
    tm = min(512, T)
    blk = pl.BlockSpec((tm, W), lambda i: (i, 0))
    one = pl.BlockSpec((1, W), lambda i: (0, 0))
    return pl.pallas_call(
        _rwkv_post_kernel,
        grid=(T // tm,),
        in_specs=[blk, blk, blk, one, one],
        out_specs=blk,
        out_shape=jax.ShapeDtypeStruct((T, W), bf16),
        compiler_params=_cp(("parallel",)),
        name="rwkv_post",
    )(o, g, bonus, ln_g.reshape(1, W), ln_b.reshape(1, W))


def _s5_dot(a, b):
    return jnp.dot(a.astype(bf16), b.astype(bf16), preferred_element_type=f32)


def _transpose_pieces(w):
    w = list(w)
    lane = lax.broadcasted_iota(jnp.int32, w[0].shape, 1)
    for d in (4, 2, 1):
        width = S5_GROUP * d
        low = (lane & width) == 0
        for i in range(8):
            if i & d == 0:
                a, b = w[i], w[i + d]
                w[i] = jnp.where(low, a, pltpu.roll(b, width, 1))
                w[i + d] = jnp.where(low, pltpu.roll(a, 128 - width, 1), b)
    return w


def _s5_split_kernel(*refs):
    z_refs, u_ref = refs[:-1], refs[-1]
    ncb = u_ref.shape[1]
    for j, z_ref in enumerate(z_refs):
        for b in range(S5_CHUNK // 8):
            w = [z_ref[pl.ds(8 * b + i, ncb, stride=S5_CHUNK), :] for i in range(8)]
            o = _transpose_pieces(w)
            for a in range(8):
                u_ref[8 * j + a, :, 128 * b:128 * (b + 1)] = o[a]


def _toeplitz(krow):
    P, LP = krow.shape
    lane = lax.broadcasted_iota(jnp.int32, (P, LP), 1)
    blocks = [krow]
    for s in range(1, LP // P):
        blocks.append(jnp.where(lane >= P * s, pltpu.roll(krow, P * s, 1), 0.0))
    return jnp.concatenate(blocks, axis=0)


def _s5_in_kernel(u_ref, krow_ref, br_ref, bi_ref, y_ref, xr_ref, xi_ref):
    for j in range(2):
        y_ref[j] = _s5_dot(u_ref[j], _toeplitz(krow_ref[j]))
    u2 = jnp.concatenate([u_ref[0], u_ref[1]], axis=1)
    xr_ref[...] = _s5_dot(u2, br_ref[0])
    xi_ref[...] = _s5_dot(u2, bi_ref[0])


def _s5_scan_kernel(xr_ref, xi_ref, ar_ref, ai_ref, sr_ref, si_ref):
    nc = xr_ref.shape[0]
    ar = ar_ref[...]
    ai = ai_ref[...]

    def body(c, carry):
        sr, si = carry
        sr_ref[pl.ds(c, 1), :] = sr
        si_ref[pl.ds(c, 1), :] = si
        xr = xr_ref[pl.ds(c, 1), :]
        xi = xi_ref[pl.ds(c, 1), :]
        return ar * sr - ai * si + xr, ar * si + ai * sr + xi

    z = jnp.zeros_like(ar)
    lax.fori_loop(0, nc, body, (z, z))


def _s5_out_kernel(y_ref, sr_ref, si_ref, cr_ref, ci_ref, u_ref, d_ref, o_ref):
    sr = sr_ref[...]
    si = si_ref[...]
    for j in range(2):
        y = y_ref[j] + _s5_dot(sr, cr_ref[j]) + _s5_dot(si, ci_ref[j]) + d_ref[j] * u_ref[j]
        o_ref[j] = 0.5 * y * (1.0 + jnp.tanh(0.7978845608028654 * (y + 0.044715 * (y * y * y))))


def _glu_kernel(y_ref, w_ref, b_ref, o_ref, *nat_refs):
    ncb = y_ref.shape[1]
    for j, nat_ref in enumerate(nat_refs):
        for b in range(S5_CHUNK // 8):
            o = [y_ref[8 * j + a, :, 128 * b:128 * (b + 1)] for a in range(8)]
            w = _transpose_pieces(o)
            for i in range(8):
                nat_ref[pl.ds(8 * b + i, ncb, stride=S5_CHUNK), :] = w[i]
    y = jnp.concatenate([r[...] for r in nat_refs], axis=1)
    o_ref[...] = (y * _sigmoid(_bmm(y, w_ref[...]) + b_ref[...])).astype(o_ref.dtype)


def _s5_tables(a_re, a_im, log_dt, b_re, b_im, c_re, c_im, d):
    G, N, P, L = S5_GROUPS, S5_STATE, S5_GROUP, S5_CHUNK
    D = a_re.shape[0]
    dt = jnp.exp(log_dt)[:, None, :, None]
    lags = jnp.arange(L + 1, dtype=f32)[None, :, None, None]
    mag = jnp.exp(a_re[:, None] * dt * lags)
    ang = a_im[:, None] * dt * lags
    pw_re, pw_im = mag * jnp.cos(ang), mag * jnp.sin(ang)
    den = a_re * a_re + a_im * a_im
    nr, ni = pw_re[:, 1] - 1.0, pw_im[:, 1]
    coef_re = ((nr * a_re + ni * a_im) / den)[..., None]
    coef_im = ((ni * a_re - nr * a_im) / den)[..., None]
    bb_re = coef_re * b_re - coef_im * b_im
    bb_im = coef_re * b_im + coef_im * b_re
    rep = lambda a, axis: jnp.repeat(a, P, axis=axis)
    pl_re, pl_im = rep(pw_re[:, :L].transpose(0, 2, 1, 3), 2), rep(pw_im[:, :L].transpose(0, 2, 1, 3), 2)
    ct_re, ct_im = jnp.tile(c_re, (1, 1, L, 1)), jnp.tile(c_im, (1, 1, L, 1))
    cb_re, cb_im = ct_re * pl_re - ct_im * pl_im, ct_re * pl_im + ct_im * pl_re
    krow = (jnp.einsum('dgxn,dgnq->dgqx', cb_re, bb_re, precision=HI)
            - jnp.einsum('dgxn,dgnq->dgqx', cb_im, bb_im, precision=HI))
    e_re = rep(pw_re[:, L - 1::-1][:, :L].transpose(0, 2, 1, 3), 2)
    e_im = rep(pw_im[:, L - 1::-1][:, :L].transpose(0, 2, 1, 3), 2)
    bq_re = jnp.tile(bb_re.transpose(0, 1, 3, 2), (1, 1, L, 1))
    bq_im = jnp.tile(bb_im.transpose(0, 1, 3, 2), (1, 1, L, 1))
    bs_re = (e_re * bq_re - e_im * bq_im).reshape(D, G // 2, 2, L * P, N)
    bs_im = (e_re * bq_im + e_im * bq_re).reshape(D, G // 2, 2, L * P, N)

    def pair_in(bs):
        zz = jnp.zeros_like(bs[:, :, 0])
        return jnp.concatenate([jnp.concatenate([bs[:, :, 0], zz], axis=3), jnp.concatenate([zz, bs[:, :, 1]], axis=3)], axis=2)

    f_re = rep(pw_re[:, 1:].transpose(0, 2, 3, 1), 3)
    f_im = rep(pw_im[:, 1:].transpose(0, 2, 3, 1), 3)
    cn_re = jnp.tile(c_re.transpose(0, 1, 3, 2), (1, 1, 1, L))
    cn_im = jnp.tile(c_im.transpose(0, 1, 3, 2), (1, 1, 1, L))
    cs_re = (cn_re * f_re - cn_im * f_im).reshape(D, G // 2, 2, N, L * P)
    cs_im = (-(cn_re * f_im + cn_im * f_re)).reshape(D, G // 2, 2, N, L * P)

    def pair_out(cs):
        zz = jnp.zeros_like(cs[:, :, 0])
        top = jnp.concatenate([cs[:, :, 0], zz], axis=2)[:, :, None]
        bot = jnp.concatenate([zz, cs[:, :, 1]], axis=2)[:, :, None]
        return jnp.concatenate([top, bot], axis=2).reshape(D, G, 2 * N, L * P)

    dd = jnp.tile(d.reshape(D, G, 1, P), (1, 1, 1, L))
    return dict(krow=krow, bsr=pair_in(bs_re), bsi=pair_in(bs_im), cr=pair_out(cs_re), ci=pair_out(cs_im),
                al_re=pw_re[:, L].reshape(D, 1, G * N), al_im=pw_im[:, L].reshape(D, 1, G * N), d=dd)


def _s5(z_tail, tab, layer, glu_w, glu_b):
    T = z_tail.shape[0]
    G, N, P, L = S5_GROUPS, S5_STATE, S5_GROUP, S5_CHUNK
    nc, LP = T // L, L * P
    tm = min(1024, T)
    u = pl.pallas_call(
        _s5_split_kernel,
        grid=(T // tm,),
        in_specs=[pl.BlockSpec((tm, 128), lambda i, j=j: (i, j)) for j in range(S5_W // 128)],
        out_specs=pl.BlockSpec((G, tm // L, LP), lambda i: (0, i, 0)),
        out_shape=jax.ShapeDtypeStruct((G, nc, LP), f32),
        compiler_params=_cp(("parallel",)),
        name="s5_split",
    )(*[z_tail] * (S5_W // 128))
    grp = lambda s: pl.BlockSpec((2,) + s, lambda j: (j, 0, 0))
    tgrp = lambda s: pl.BlockSpec((None, 2) + s, lambda j: (layer, j, 0, 0))
    tone = lambda s: pl.BlockSpec((None, 1) + s, lambda j: (layer, j, 0, 0))
    lanes = pl.BlockSpec((nc, 2 * N), lambda j: (0, j))
    y_in, xr, xi = pl.pallas_call(
        _s5_in_kernel,
        grid=(G // 2,),
        in_specs=[grp((nc, LP)), tgrp((P, LP)), tone((2 * LP, 2 * N)), tone((2 * LP, 2 * N))],
        out_specs=[grp((nc, LP)), lanes, lanes],
        out_shape=[jax.ShapeDtypeStruct((G, nc, LP), f32), jax.ShapeDtypeStruct((nc, G * N), f32),
                   jax.ShapeDtypeStruct((nc, G * N), f32)],
        compiler_params=_cp(("parallel",)),
        name="s5_in",
    )(u, tab["krow"], tab["bsr"], tab["bsi"])
    whole = lambda s: pl.BlockSpec(s, lambda i: (0,) * len(s))
    sr, si = pl.pallas_call(
        _s5_scan_kernel,
        grid=(1,),
        in_specs=[whole((nc, G * N)), whole((nc, G * N)), pl.BlockSpec((None, 1, G * N), lambda i: (layer, 0, 0)),
                  pl.BlockSpec((None, 1, G * N), lambda i: (layer, 0, 0))],
        out_specs=[whole((nc, G * N)), whole((nc, G * N))],
        out_shape=[jax.ShapeDtypeStruct((nc, G * N), f32)] * 2,
        compiler_params=_cp(("arbitrary",)),
        name="s5_scan",
    )(xr, xi, tab["al_re"], tab["al_im"])
    y = pl.pallas_call(
        _s5_out_kernel,
        grid=(G // 2,),
        in_specs=[grp((nc, LP)), lanes, lanes, tgrp((2 * N, LP)), tgrp((2 * N, LP)), grp((nc, LP)), tgrp((1, LP))],
        out_specs=grp((nc, LP)),
        out_shape=jax.ShapeDtypeStruct((G, nc, LP), f32),
        compiler_params=_cp(("parallel",)),
        name="s5_out",
    )(y_in, sr, si, tab["cr"], tab["ci"], u, tab["d"])
    return pl.pallas_call(
        _glu_kernel,
        grid=(T // tm,),
        in_specs=[pl.BlockSpec((G, tm // L, LP), lambda i: (0, i, 0)), pl.BlockSpec((S5_W, S5_W), lambda i: (0, 0)),
                  pl.BlockSpec((1, S5_W), lambda i: (0, 0))],
        out_specs=pl.BlockSpec((tm, S5_W), lambda i: (i, 0)),
        out_shape=jax.ShapeDtypeStruct((T, S5_W), bf16),
        scratch_shapes=[pltpu.VMEM((tm, 128), f32)] * (S5_W // 128),
        compiler_params=_cp(("parallel",)),
        name="s5_glu",
    )(y, glu_w, glu_b.reshape(1, S5_W))


def _pad_cols(w, n):
    return jnp.pad(w, [(0, 0)] * (w.ndim - 1) + [(0, n - w.shape[-1])])


def _pad_rows(w, n):
    return jnp.pad(w, ((0, n - w.shape[0]), (0, 0)))


def _split_in_proj(w):
    r0 = M_IN
    l0 = r0 + 3 * RWKV_W
    s0 = M_IN + RWKV_IN
    g0 = s0 + S5_W
    main = jnp.concatenate([
        w[..., :4 * MLSTM_W], w[..., r0:l0],
        _pad_cols(w[..., l0:l0 + DECAY_LORA], LORA_PAD),
        _pad_cols(w[..., l0 + DECAY_LORA:l0 + DECAY_LORA + AAA_LORA], LORA_PAD),
        w[..., l0 + DECAY_LORA + AAA_LORA:s0]], axis=-1).astype(bf16)
    gates = w[..., g0:].astype(bf16)
    tail = _pad_cols(jnp.concatenate([w[..., s0:g0], w[..., 4 * MLSTM_W:M_IN]], axis=-1), Z_TAIL).astype(bf16)
    return main, gates, tail


def _split_mu(mu):
    l0 = 3 * RWKV_W
    return jnp.concatenate([
        mu[:l0], jnp.pad(mu[l0:l0 + DECAY_LORA], (0, LORA_PAD - DECAY_LORA)),
        jnp.pad(mu[l0 + DECAY_LORA:l0 + DECAY_LORA + AAA_LORA], (0, LORA_PAD - AAA_LORA)),
        mu[l0 + DECAY_LORA + AAA_LORA:]]).reshape(1, RWKV_PAD)


def kernel(x, p, norm_mix_g, w_in, mlstm_conv, mlstm_ib, mlstm_fb, mlstm_norm_g, rwkv_mu, rwkv_w0, rwkv_w2, rwkv_a0, rwkv_a2, rwkv_g2, rwkv_kk, rwkv_ka, rwkv_rk, rwkv_ln_g, rwkv_ln_b, s5_a_re, s5_a_im, s5_log_dt, s5_b_re, s5_b_im, s5_c_re, s5_c_im, s5_d, s5_glu_w, s5_glu_b, w_up_m, w_up_r, w_up_s, w_out, norm_ffn_g, ffn_w_gate, ffn_w_up, ffn_w_down, norm_ple_g, ple_w_gate, ple_w_proj, final_norm_g):
    B, T, D = x.shape
    depth = w_in.shape[0]
    w_main, w_gates, w_tail = _split_in_proj(w_in)
    s5_tab = _s5_tables(s5_a_re, s5_a_im, s5_log_dt, s5_b_re, s5_b_im, s5_c_re, s5_c_im, s5_d)
    outs = []
    for bi in range(B):
        h = x[bi]
        xn = _rmsnorm(h, norm_mix_g[0], bf16)
        for i in range(depth):
            z_main = _proj(xn, w_main, i, 1024)
            z_gate = _proj(xn, w_gates, i, 1024, bf16, gate=True)
            z_tail = _proj(xn, w_tail, i, Z_TAIL)

            gate_b = jnp.pad(jnp.concatenate([mlstm_ib[i], mlstm_fb[i]]), (0, 128 - 2 * MLSTM_HEADS)).reshape(1, 128)
            y_m = _mlstm(z_main, z_tail, mlstm_conv[i], gate_b, mlstm_norm_g[i])

            vec = lambda a: a.reshape(1, RWKV_W)
            parts = _rwkv_prep(z_main, _split_mu(rwkv_mu[i]), vec(rwkv_w0[i]), vec(rwkv_a0[i]), vec(rwkv_kk[i]),
                               vec(rwkv_ka[i]), vec(rwkv_rk[i]), _pad_rows(rwkv_w2[i], LORA_PAD).astype(bf16),
                               _pad_rows(rwkv_a2[i], LORA_PAD).astype(bf16), rwkv_g2[i].astype(bf16))
            y_r = _rwkv_scan(parts, rwkv_ln_g[i], rwkv_ln_b[i])

            y_s = _s5(z_tail, s5_tab, i, s5_glu_w[i].astype(bf16), s5_glu_b[i])

            mixed = _merge(y_m, y_r, y_s, z_gate, w_up_m, w_up_r, w_up_s, i)
            h, hn = _resid_norm(mixed, w_out[i].astype(bf16), h, norm_ffn_g[i], D_MODEL)
            act = _ffn_up(hn, ffn_w_gate, ffn_w_up, i)
            h = _resid_mm(act, ffn_w_down, h, i)
            last = i == depth - 1
            g_next = final_norm_g if last else norm_mix_g[i + 1]
            h, xn = _ple(h, p[i, bi].astype(bf16), ple_w_gate[i].astype(bf16), ple_w_proj[i].astype(bf16), norm_ple_g[i],
                         g_next, f32 if last else bf16)
        outs.append(xn)
    return jnp.stack(outs)
```

```python
import functools

import jax
import jax.numpy as jnp
import numpy as np
from jax import lax
from jax.experimental import pallas as pl
from jax.experimental.pallas import tpu as pltpu

f32 = jnp.float32
bf16 = jnp.bfloat16
HI = lax.Precision.HIGHEST

D_MODEL = 2048
PLE_DIM = 256
RMS_EPS = 1e-6
MLSTM_HEADS = 4
MLSTM_HEAD_DIM = 256
MLSTM_W = 1024
MLSTM_CHUNK = 256
GATE_SOFTCAP = 15.0
RWKV_HEADS = 8
RWKV_HEAD_DIM = 64
RWKV_W = 512
DECAY_LORA = 96
AAA_LORA = 96
GATE_LORA = 256
LORA_PAD = 128
RWKV_GN_EPS = 64e-5
RWKV_CHUNK = 64
S5_GROUP = 16
S5_GROUPS = 32
S5_W = 512
S5_STATE = 64
S5_CHUNK = 16
FFN_HIDDEN = 5632
M_IN = 4 * MLSTM_W + 2 * MLSTM_HEADS
RWKV_IN = 3 * RWKV_W + DECAY_LORA + AAA_LORA + GATE_LORA
RWKV_PAD = 3 * RWKV_W + 2 * LORA_PAD + GATE_LORA
Z_TAIL = S5_W + 128
VMEM_LIMIT = 56 * 1024 * 1024


def _cp(sem):
    return pltpu.CompilerParams(dimension_semantics=sem, vmem_limit_bytes=VMEM_LIMIT)


def _nt(a, b, **kw):
    return lax.dot_general(a, b, (((1,), (1,)), ((), ())), preferred_element_type=f32, **kw)


def _tn(a, b, **kw):
    return lax.dot_general(a, b, (((0,), (0,)), ((), ())), preferred_element_type=f32, **kw)


def _mm(a, b, **kw):
    return jnp.dot(a, b, preferred_element_type=f32, **kw)


def _bmm(a, b):
    return jnp.dot(a.astype(bf16), b.astype(bf16), preferred_element_type=f32)


def _sigmoid(x):
    return 1.0 / (1.0 + jnp.exp(-x))


def _rms(x, g):
    return x * lax.rsqrt(jnp.mean(x * x, axis=-1, keepdims=True) + RMS_EPS) * g


def _norm_kernel(x_ref, g_ref, o_ref):
    o_ref[...] = _rms(x_ref[...], g_ref[...]).astype(o_ref.dtype)


def _rmsnorm(x, g, out_dtype):
    T, D = x.shape
    tm = min(512, T)
    return pl.pallas_call(
        _norm_kernel,
        grid=(T // tm,),
        in_specs=[pl.BlockSpec((tm, D), lambda i: (i, 0)), pl.BlockSpec((1, D), lambda i: (0, 0))],
        out_specs=pl.BlockSpec((tm, D), lambda i: (i, 0)),
        out_shape=jax.ShapeDtypeStruct((T, D), out_dtype),
        compiler_params=_cp(("parallel",)),
        name="rmsnorm",
    )(x, g.reshape(1, D))


def _load_wt(w, rows):
    parts = [w[r[0]:r[1], :] if isinstance(r, tuple) else jnp.zeros((r, w.shape[1]), f32) for r in rows]
    w = parts[0] if len(parts) == 1 else jnp.concatenate(parts, axis=0)
    return w.T.astype(bf16)


def _proj_kernel(a_ref, w_ref, o_ref, wb_ref, *, gate, last_rows):
    j = pl.program_id(0)
    nj = pl.num_programs(0)

    @pl.when(pl.program_id(1) == 0)
    def _():
        if last_rows is None:
            wb_ref[...] = _load_wt(w_ref[0], [(0, w_ref.shape[1])])
        else:
            @pl.when(j < nj - 1)
            def _():
                wb_ref[...] = _load_wt(w_ref[0], [(0, w_ref.shape[1])])

            @pl.when(j == nj - 1)
            def _():
                wb_ref[...] = _load_wt(w_ref[0], last_rows)

    z = _mm(a_ref[...], wb_ref[...])
    o_ref[...] = (0.5 * jnp.tanh(0.5 * z) + 0.5 if gate else z).astype(o_ref.dtype)


def _proj(a, wt, layer, row0, n_tiles, tn, out_dtype=f32, gate=False, last_rows=None):
    T, K = a.shape
    tm = min(1024, T)
    return pl.pallas_call(
        functools.partial(_proj_kernel, gate=gate, last_rows=last_rows),
        grid=(n_tiles, T // tm),
        in_specs=[pl.BlockSpec((tm, K), lambda j, i: (i, 0)),
                  pl.BlockSpec((pl.Element(1), pl.Element(tn), pl.Element(K)),
                               lambda j, i: (layer, pl.multiple_of(row0 + tn * j, 8), 0))],
        out_specs=pl.BlockSpec((tm, tn), lambda j, i: (i, j)),
        out_shape=jax.ShapeDtypeStruct((T, n_tiles * tn), out_dtype),
        scratch_shapes=[pltpu.VMEM((K, tn), bf16)],
        compiler_params=_cp(("parallel", "arbitrary")),
        name="in_proj",
    )(a, wt)


def _proj_tail_kernel(a_ref, ws_ref, wg_ref, o_ref, wb_ref):
    @pl.when(pl.program_id(0) == 0)
    def _():
        pad = jnp.zeros((Z_TAIL - S5_W - wg_ref.shape[0], wg_ref.shape[1]), f32)
        wb_ref[...] = jnp.concatenate([ws_ref[0], wg_ref[...], pad], axis=0).T.astype(bf16)

    o_ref[...] = _mm(a_ref[...], wb_ref[...])


def _proj_tail(a, wt, layer):
    T, K = a.shape
    tm = min(1024, T)
    n_if = 2 * MLSTM_HEADS
    return pl.pallas_call(
        _proj_tail_kernel,
        grid=(T // tm,),
        in_specs=[pl.BlockSpec((tm, K), lambda i: (i, 0)),
                  pl.BlockSpec((pl.Element(1), pl.Element(S5_W), pl.Element(K)), lambda i: (layer, M_IN + RWKV_IN, 0)),
                  pl.BlockSpec((None, n_if, K), lambda i: (layer, 4 * MLSTM_W // n_if, 0))],
        out_specs=pl.BlockSpec((tm, Z_TAIL), lambda i: (i, 0)),
        out_shape=jax.ShapeDtypeStruct((T, Z_TAIL), f32),
        scratch_shapes=[pltpu.VMEM((K, Z_TAIL), bf16)],
        compiler_params=_cp(("arbitrary",)),
        name="in_proj_tail",
    )(a, wt, wt)


def _merge_kernel(ym_ref, yr_ref, ys_ref, gm_ref, gr_ref, gs_ref, um_ref, ur_ref, us_ref, o_ref, umb_ref, urb_ref, usb_ref):
    @pl.when(pl.program_id(1) == 0)
    def _():
        umb_ref[...] = um_ref[...].astype(bf16)
        urb_ref[...] = ur_ref[...].astype(bf16)
        usb_ref[...] = us_ref[...].astype(bf16)

    acc = gm_ref[...] * _mm(ym_ref[...], umb_ref[...])
    acc += gr_ref[...] * _mm(yr_ref[...], urb_ref[...])
    acc += gs_ref[...] * _mm(ys_ref[...], usb_ref[...])
    o_ref[...] = acc.astype(o_ref.dtype)


def _merge(ym, yr, ys, gates, um, ur, us, layer):
    T = ym.shape[0]
    tm, tn = min(512, T), 1024
    nj = D_MODEL // tn

    def gate_spec(b):
        return pl.BlockSpec((tm, tn), lambda j, i, b=b: (i, b * nj + j))

    def y_spec(w):
        return pl.BlockSpec((tm, w), lambda j, i: (i, 0))

    def u_spec(w):
        return pl.BlockSpec((None, w, tn), lambda j, i: (layer, 0, j))

    return pl.pallas_call(
        _merge_kernel,
        grid=(nj, T // tm),
        in_specs=[y_spec(MLSTM_W), y_spec(RWKV_W), y_spec(S5_W), gate_spec(0), gate_spec(1), gate_spec(2),
                  u_spec(MLSTM_W), u_spec(RWKV_W), u_spec(S5_W)],
        out_specs=pl.BlockSpec((tm, tn), lambda j, i: (i, j)),
        out_shape=jax.ShapeDtypeStruct((T, D_MODEL), bf16),
        scratch_shapes=[pltpu.VMEM((MLSTM_W, tn), bf16), pltpu.VMEM((RWKV_W, tn), bf16), pltpu.VMEM((S5_W, tn), bf16)],
        compiler_params=_cp(("parallel", "arbitrary")),
        name="gated_merge",
    )(ym, yr, ys, gates, gates, gates, um, ur, us)


def _resid_norm_kernel(a_ref, w_ref, h_ref, g_ref, ho_ref, no_ref, acc_ref):
    k = pl.program_id(1)

    @pl.when(k == 0)
    def _():
        acc_ref[...] = h_ref[...]

    acc_ref[...] += _mm(a_ref[...], w_ref[...])

    @pl.when(k == pl.num_programs(1) - 1)
    def _():
        hn = acc_ref[...]
        ho_ref[...] = hn
        no_ref[...] = _rms(hn, g_ref[...]).astype(no_ref.dtype)


def _resid_norm(a, w, h, g, tk, norm_dtype=bf16):
    T, K = a.shape
    tm = min(512, T)
    return pl.pallas_call(
        _resid_norm_kernel,
        grid=(T // tm, K // tk),
        in_specs=[pl.BlockSpec((tm, tk), lambda i, k: (i, k)), pl.BlockSpec((tk, D_MODEL), lambda i, k: (k, 0)),
                  pl.BlockSpec((tm, D_MODEL), lambda i, k: (i, 0)), pl.BlockSpec((1, D_MODEL), lambda i, k: (0, 0))],
        out_specs=[pl.BlockSpec((tm, D_MODEL), lambda i, k: (i, 0)), pl.BlockSpec((tm, D_MODEL), lambda i, k: (i, 0))],
        out_shape=[jax.ShapeDtypeStruct((T, D_MODEL), f32), jax.ShapeDtypeStruct((T, D_MODEL), norm_dtype)],
        scratch_shapes=[pltpu.VMEM((tm, D_MODEL), f32)],
        compiler_params=_cp(("parallel", "arbitrary")),
        name="resid_matmul_norm",
    )(a, w, h, g.reshape(1, D_MODEL))


def _ffn_up_kernel(a_ref, wg_ref, wu_ref, o_ref, wgb_ref, wub_ref):
    @pl.when(pl.program_id(1) == 0)
    def _():
        wgb_ref[...] = wg_ref[...].astype(bf16)
        wub_ref[...] = wu_ref[...].astype(bf16)

    a = a_ref[...]
    gt = _mm(a, wgb_ref[...])
    up = _mm(a, wub_ref[...])
    o_ref[...] = (gt * _sigmoid(gt) * up).astype(o_ref.dtype)


def _ffn_up(a, wg, wu, layer):
    T, K = a.shape
    N = wg.shape[2]
    tm, tn = min(1024, T), 512
    return pl.pallas_call(
        _ffn_up_kernel,
        grid=(N // tn, T // tm),
        in_specs=[pl.BlockSpec((tm, K), lambda j, i: (i, 0)), pl.BlockSpec((None, K, tn), lambda j, i: (layer, 0, j)),
                  pl.BlockSpec((None, K, tn), lambda j, i: (layer, 0, j))],
        out_specs=pl.BlockSpec((tm, tn), lambda j, i: (i, j)),
        out_shape=jax.ShapeDtypeStruct((T, N), bf16),
        scratch_shapes=[pltpu.VMEM((K, tn), bf16), pltpu.VMEM((K, tn), bf16)],
        compiler_params=_cp(("parallel", "arbitrary")),
        name="ffn_up",
    )(a, wg, wu)


def _resid_mm_kernel(a_ref, w_ref, h_ref, o_ref, wb_ref):
    @pl.when(pl.program_id(1) == 0)
    def _():
        wb_ref[...] = w_ref[...].astype(bf16)

    o_ref[...] = h_ref[...] + _mm(a_ref[...], wb_ref[...])


def _resid_mm(a, w, h, layer):
    T, K = a.shape
    N = w.shape[2]
    tm, tn = min(512, T), 512
    return pl.pallas_call(
        _resid_mm_kernel,
        grid=(N // tn, T // tm),
        in_specs=[pl.BlockSpec((tm, K), lambda j, i: (i, 0)), pl.BlockSpec((None, K, tn), lambda j, i: (layer, 0, j)),
                  pl.BlockSpec((tm, tn), lambda j, i: (i, j))],
        out_specs=pl.BlockSpec((tm, tn), lambda j, i: (i, j)),
        out_shape=jax.ShapeDtypeStruct((T, N), f32),
        scratch_shapes=[pltpu.VMEM((K, tn), bf16)],
        compiler_params=_cp(("parallel", "arbitrary")),
        name="resid_matmul",
    )(a, w, h)


def _ple_kernel(h_ref, p_ref, wg_ref, wp_ref, gp_ref, g_ref, ho_ref, no_ref):
    tm = h_ref.shape[0]
    parts = [slice(0, tm // 2), slice(tm // 2, tm)]
    hs = [h_ref[r, :] for r in parts]
    hps = [_rms(h, gp_ref[...]).astype(bf16) for h in hs]
    gates = [_sigmoid(_mm(hp, wg_ref[...])) for hp in hps]
    projs = [_mm(p_ref[r, :], wp_ref[...]) for r in parts]
    for r, h, gate, proj in zip(parts, hs, gates, projs):
        hn = h + proj * gate
        ho_ref[r, :] = hn
        no_ref[r, :] = _rms(hn, g_ref[...]).astype(no_ref.dtype)


def _ple(h, p, wg, wp, g_ple, g, norm_dtype):
    T = h.shape[0]
    tm = min(512, T)
    row = lambda w: pl.BlockSpec((tm, w), lambda i: (i, 0))
    full = lambda s: pl.BlockSpec(s, lambda i: (0, 0))
    return pl.pallas_call(
        _ple_kernel,
        grid=(T // tm,),
        in_specs=[row(D_MODEL), row(PLE_DIM), full((D_MODEL, D_MODEL)), full((PLE_DIM, D_MODEL)), full((1, D_MODEL)),
                  full((1, D_MODEL))],
        out_specs=[row(D_MODEL), row(D_MODEL)],
        out_shape=[jax.ShapeDtypeStruct((T, D_MODEL), f32), jax.ShapeDtypeStruct((T, D_MODEL), norm_dtype)],
        compiler_params=_cp(("parallel",)),
        name="ple_norm",
    )(h, p, wg, wp, g_ple.reshape(1, D_MODEL), g.reshape(1, D_MODEL))


def _mlstm_kernel(zq_ref, zk_ref, v_ref, o_ref, gz_ref, cq_ref, ck_ref, gb_ref, ng_ref, y_ref,
                  ct_ref, n_ref, qbuf_ref, kbuf_ref):
    c = pl.program_id(0)
    L, W = zq_ref.shape
    H, dh = MLSTM_HEADS, MLSTM_HEAD_DIM
    heads = range(H)

    @pl.when(c == 0)
    def _():
        ct_ref[...] = jnp.zeros_like(ct_ref)
        n_ref[...] = jnp.zeros_like(n_ref)
        qbuf_ref[0:8, :] = jnp.zeros((8, W), f32)
        kbuf_ref[0:8, :] = jnp.zeros((8, W), f32)

    def conv_silu(z_ref, w_ref, buf_ref):
        x = z_ref[...]
        buf_ref[8:, :] = x
        w = w_ref[...]
        acc = x * w[0:1, :]
        for j in range(1, w.shape[0]):
            acc = acc + buf_ref[pl.ds(8 - j, L), :] * w[j:j + 1, :]
        buf_ref[0:8, :] = x[L - 8:, :]
        return acc * _sigmoid(acc)

    q = conv_silu(zq_ref, cq_ref, qbuf_ref) * (dh ** -0.5)
    k = conv_silu(zk_ref, ck_ref, kbuf_ref)

    g = gz_ref[...] + gb_ref[...]
    sc = GATE_SOFTCAP * jnp.tanh(g / GATE_SOFTCAP)
    logf = -jnp.log(1.0 + jnp.exp(-sc))
    row = lax.broadcasted_iota(jnp.int32, (L, L), 0)
    col = lax.broadcasted_iota(jnp.int32, (L, L), 1)
    causal = row >= col
    bcs = _mm(causal.astype(f32), logf, precision=HI)
    b128 = pltpu.roll(bcs, 128 - H, 1)
    e128 = sc - b128
    e_t = e128.T

    sl = [slice(h * dh, (h + 1) * dh) for h in heads]
    b_col = [b128[:, h:h + 1] for h in heads]
    e_col = [e128[:, h:h + 1] for h in heads]
    wmat = [jnp.where(causal, jnp.exp(jnp.where(causal, b_col[h] + e_t[h:h + 1, :], 0.0)), 0.0) for h in heads]
    b_end = [b_col[h][L - 1:L, :] for h in heads]
    eb = [jnp.exp(b_col[h]) for h in heads]
    qh = [q[:, sl[h]] for h in heads]
    kh = [k[:, sl[h]] for h in heads]
    qb = [x.astype(bf16) for x in qh]
    kb = [x.astype(bf16) for x in kh]
    vb = [v_ref[:, sl[h]].astype(bf16) for h in heads]
    ct = [ct_ref[h] for h in heads]
    nn = [n_ref[:, sl[h]] for h in heads]
    s = [_nt(qb[h], kb[h]) * wmat[h] for h in heads]
    inter = [_mm(qb[h], ct[h].astype(bf16)) for h in heads]
    num = [_mm(s[h].astype(bf16), vb[h]) + eb[h] * inter[h] for h in heads]
    ng = ng_ref[...]
    for h in heads:
        den = jnp.sum(s[h], axis=1, keepdims=True) + eb[h] * jnp.sum(qh[h] * nn[h], axis=1, keepdims=True)
        hh = num[h] / jnp.maximum(jnp.abs(den), 1.0)
        hh = hh * lax.rsqrt(jnp.mean(hh * hh, axis=1, keepdims=True) + RMS_EPS) * ng[:, sl[h]]
        y_ref[:, sl[h]] = (_sigmoid(o_ref[:, sl[h]]) * hh).astype(y_ref.dtype)
    kw = [kh[h] * jnp.exp(b_end[h] + e_col[h]) for h in heads]
    upd = [_tn(kw[h].astype(bf16), vb[h]) for h in heads]
    for h in heads:
        decay = jnp.exp(b_end[h])
        ct_ref[h] = decay * ct[h] + upd[h]
        n_ref[:, sl[h]] = decay * nn[h] + jnp.sum(kw[h], axis=0, keepdims=True)


def _mlstm(z_main, z_tail, conv_w, gate_b, norm_g):
    T = z_main.shape[0]
    L, dh, H, W = min(MLSTM_CHUNK, T), MLSTM_HEAD_DIM, MLSTM_HEADS, MLSTM_W
    blk = lambda j: pl.BlockSpec((L, W), lambda c, j=j: (c, j))
    return pl.pallas_call(
        _mlstm_kernel,
        grid=(T // L,),
        in_specs=[blk(0), blk(1), blk(2), blk(3),
                  pl.BlockSpec((L, 128), lambda c: (c, S5_W // 128)),
                  pl.BlockSpec((4, W), lambda c: (0, 0)), pl.BlockSpec((4, W), lambda c: (0, 1)),
                  pl.BlockSpec((1, 128), lambda c: (0, 0)), pl.BlockSpec((1, W), lambda c: (0, 0))],
        out_specs=pl.BlockSpec((L, W), lambda c: (c, 0)),
        out_shape=jax.ShapeDtypeStruct((T, W), bf16),
        scratch_shapes=[pltpu.VMEM((H, dh, dh), f32), pltpu.VMEM((1, W), f32),
                        pltpu.VMEM((L + 8, W), f32), pltpu.VMEM((L + 8, W), f32)],
        compiler_params=_cp(("arbitrary",)),
        name="mlstm",
    )(z_main, z_main, z_main, z_main, z_tail, conv_w, conv_w, gate_b, norm_g.reshape(1, W))


def _iota_div(shape, dim, width):
    return lax.shift_right_logical(lax.broadcasted_iota(jnp.int32, shape, dim), int(np.log2(width)))


def _head_ones(n, width):
    return (_iota_div((n, n), 0, width) == _iota_div((n, n), 1, width)).astype(f32)


def _head_sum(x):
    blk = _head_ones(RWKV_W, RWKV_HEAD_DIM).astype(bf16)
    hi = x.astype(bf16)
    r1 = x - hi.astype(f32)
    mid = r1.astype(bf16)
    lo = (r1 - mid.astype(f32)).astype(bf16)
    return _mm(hi, blk) + _mm(mid, blk) + _mm(lo, blk)


def _rwkv_prep_kernel(z_ref, mu_ref, w0_ref, a0_ref, kkg_ref, ka_ref, rk_ref, w2_ref, a2_ref, g2_ref,
                      r_ref, ld_ref, k_ref, v_ref, kk_ref, b_ref, g_ref, bonus_ref, buf_ref):
    i = pl.program_id(0)
    tm = z_ref.shape[0]
    W = RWKV_W

    @pl.when(i == 0)
    def _():
        buf_ref[0:8, :] = jnp.zeros((8, buf_ref.shape[1]), f32)

    x = z_ref[...]
    buf_ref[8:, :] = x
    xs = x + (buf_ref[pl.ds(7, tm), :] - x) * mu_ref[...]
    buf_ref[0:8, :] = x[tm - 8:, :]
    r = xs[:, 0:W]
    k = xs[:, W:2 * W]
    v = xs[:, 2 * W:3 * W]
    wl = xs[:, 3 * W:3 * W + LORA_PAD]
    al = xs[:, 3 * W + LORA_PAD:3 * W + 2 * LORA_PAD]
    gl = xs[:, 3 * W + 2 * LORA_PAD:]
    t = w0_ref[...] + _bmm(jnp.tanh(wl), w2_ref[...])
    w = -(jnp.maximum(-t, 0.0) + jnp.log(1.0 + jnp.exp(-jnp.abs(t)))) - 0.5
    a = _sigmoid(a0_ref[...] + _bmm(al, a2_ref[...]))
    g = _bmm(_sigmoid(gl), g2_ref[...])
    kk = k * kkg_ref[...]
    ss = _head_sum(kk * kk)
    kk = kk / jnp.maximum(jnp.sqrt(ss), 1e-12)
    k2 = k * (1.0 + (a - 1.0) * ka_ref[...])
    bonus = _head_sum(r * k2 * rk_ref[...]) * v
    r_ref[...] = r
    ld_ref[...] = -jnp.exp(w)
    k_ref[...] = k2
    v_ref[...] = v
    kk_ref[...] = kk
    b_ref[...] = kk * a
    g_ref[...] = g
    bonus_ref[...] = bonus


def _rwkv_prep(z_main, mu, w0, a0, kkg, ka, rk, w2, a2, g2):
    T = z_main.shape[0]
    tm = min(256, T)
    W = RWKV_W
    vec = lambda n: pl.BlockSpec((1, n), lambda i: (0, 0))
    mat = lambda s: pl.BlockSpec(s, lambda i: (0, 0))
    out = pl.BlockSpec((tm, W), lambda i: (i, 0))
    return pl.pallas_call(
        _rwkv_prep_kernel,
        grid=(T // tm,),
        in_specs=[pl.BlockSpec((tm, RWKV_PAD), lambda i: (i, 0)), vec(RWKV_PAD), vec(W), vec(W),
                  vec(W), vec(W), vec(W), mat((LORA_PAD, W)), mat((LORA_PAD, W)), mat((GATE_LORA, W))],
        out_specs=[out] * 8,
        out_shape=[jax.ShapeDtypeStruct((T, W), f32)] * 8,
        scratch_shapes=[pltpu.VMEM((tm + 8, RWKV_PAD), f32)],
        compiler_params=_cp(("arbitrary",)),
        name="rwkv_prep",
    )(z_main, mu, w0, a0, kkg, ka, rk, w2, a2, g2)


def _rwkv_pre_kernel(r_ref, ld_ref, k_ref, v_ref, kk_ref, b_ref, rh_ref, oh_ref, nt_ref, ds_ref, ge_ref):
    L, dh = RWKV_CHUNK, RWKV_HEAD_DIM
    GW = 4 * dh
    PW = 2 * dh
    n_chunks = r_ref.shape[0] // L
    rr = lax.broadcasted_iota(jnp.int32, (GW, GW), 0)
    cc = lax.broadcasted_iota(jnp.int32, (GW, GW), 1)
    same = _iota_div((GW, GW), 0, L) == _iota_div((GW, GW), 1, dh)
    strict = same & (rr > cc)
    incl = same & (rr >= cc)
    eye = jnp.where(rr == cc, 1.0, 0.0)
    pair_blk = _head_ones(PW, dh)
    tr = lax.broadcasted_iota(jnp.int32, (L, L), 0)
    tc = lax.broadcasted_iota(jnp.int32, (L, L), 1)
    tril = (tr >= tc).astype(f32)

    def unstack(x):
        acc = x[0:L, :]
        for hh in range(1, GW // L):
            acc = acc + x[hh * L:(hh + 1) * L, :]
        return acc

    def tile(x):
        return jnp.concatenate([x] * (GW // L), axis=0)

    def stack(x):
        return jnp.where(same, tile(x), 0.0)

    tiles = [(ci, gi) for ci in range(n_chunks) for gi in range(RWKV_W // GW)]
    ch = []
    for ci in range(n_chunks):
        rows = slice(ci * L, (ci + 1) * L)
        ld = ld_ref[rows, :]
        lg = _mm(tril, ld, precision=HI)
        g_end = lg[L - 1:L, :]
        ge_ref[ci] = g_end
        dec_out = jnp.exp(-lg)
        dec_tail = jnp.exp(g_end - lg)
        k = k_ref[rows, :]
        b = b_ref[rows, :]
        v = v_ref[rows, :]
        ch.append(dict(rows=rows, v=v, kap=kk_ref[rows, :] * jnp.exp(lg - ld), rt=r_ref[rows, :] * jnp.exp(lg),
                       kt=k * dec_out, bt=b * dec_out, khat=(k * dec_tail).astype(bf16),
                       bhat=(b * dec_tail).astype(bf16), vb=v.astype(bf16)))
    ops = []
    for ci, gi in tiles:
        c = ch[ci]
        lanes = slice(gi * GW, (gi + 1) * GW)
        kap_s = stack(c["kap"][:, lanes])
        rt_s = stack(c["rt"][:, lanes])
        ops.append(dict(lanes=lanes, kap_s=kap_s, rt_s=rt_s, kap_b=kap_s.astype(bf16), rt_b=rt_s.astype(bf16),
                        v_s=stack(c["v"][:, lanes]).astype(bf16), kt_t=tile(c["kt"][:, lanes]).astype(bf16),
                        bt_t=tile(c["bt"][:, lanes]).astype(bf16)))
    for o in ops:
        o["a_kb"] = jnp.where(strict, _nt(o["kap_b"], o["bt_t"]), 0.0)
    for o in ops:
        o["a_kk"] = jnp.where(strict, _nt(o["kap_b"], o["kt_t"]), 0.0).astype(bf16)
        o["a_rk"] = jnp.where(incl, _nt(o["rt_b"], o["kt_t"]), 0.0).astype(bf16)
        o["a_rb"] = jnp.where(incl, _nt(o["rt_b"], o["bt_t"]), 0.0).astype(bf16)
    for o in ops:
        o["p"] = -o["a_kb"]
        o["tinv"] = eye + o["p"]
    step = 2
    while step < L:
        for o in ops:
            pb = o["p"].astype(bf16)
            o["p"] = _mm(pb, pb)
        for o in ops:
            o["tinv"] = o["tinv"] + _bmm(o["tinv"], o["p"])
        step *= 2
    for o in ops:
        o["w1"] = _mm(o["a_kk"], o["v_s"])
        o["w2"] = _mm(o["a_rk"], o["v_s"])
    for o in ops:
        o["ku"] = _bmm(o["tinv"], jnp.concatenate([o["kap_s"], o["w1"]], axis=1))
    for o in ops:
        o["ru"] = _mm(o["a_rb"], o["ku"].astype(bf16))
    for (ci, gi), o in zip(tiles, ops):
        c = ch[ci]
        rows, lanes = c["rows"], o["lanes"]
        rh_ref[rows, lanes] = unstack(o["rt_s"] - o["ru"][:, :GW]).astype(rh_ref.dtype)
        oh_ref[rows, lanes] = unstack(o["w2"] - o["ru"][:, GW:])
        kh = unstack(o["ku"][:, :GW]).astype(bf16)
        uh = unstack(o["ku"][:, GW:]).astype(bf16)
        for pi in range(GW // PW):
            q = gi * (GW // PW) + pi
            loc = slice(pi * PW, (pi + 1) * PW)
            glo = slice(q * PW, (q + 1) * PW)
            nt_ref[ci, q] = (_tn(kh[:, loc], c["bhat"][:, glo]) * pair_blk).astype(nt_ref.dtype)
            ds_ref[ci, q] = (_tn(c["vb"][:, glo], c["khat"][:, glo]) - _tn(uh[:, loc], c["bhat"][:, glo])) * pair_blk


def _rwkv_seq_kernel(rh_ref, oh_ref, nt_ref, ds_ref, ge_ref, o_ref, s_ref):
    L, PW = RWKV_CHUNK, 2 * RWKV_HEAD_DIM
    n_chunks = rh_ref.shape[0] // L

    @pl.when(pl.program_id(0) == 0)
    def _():
        s_ref[...] = jnp.zeros_like(s_ref)

    pairs = range(RWKV_W // PW)
    lanes = [slice(q * PW, (q + 1) * PW) for q in pairs]
    s_mat = [s_ref[q] for q in pairs]
    for ci in range(n_chunks):
        rows = slice(ci * L, (ci + 1) * L)
        s_b = [s.astype(bf16) for s in s_mat]
        decay = jnp.exp(ge_ref[ci])
        for q in pairs:
            s_mat[q] = s_mat[q] * decay[:, lanes[q]] - _mm(s_b[q], nt_ref[ci, q]) + ds_ref[ci, q]
        for q in pairs:
            o_ref[rows, lanes[q]] = _nt(rh_ref[rows, lanes[q]], s_b[q]) + oh_ref[rows, lanes[q]]
    for q in pairs:
        s_ref[q] = s_mat[q]


def _rwkv_post_kernel(o_ref, g_ref, bonus_ref, lng_ref, lnb_ref, y_ref):
    inv = 1.0 / RWKV_HEAD_DIM
    o = o_ref[...]
    d = o - _head_sum(o) * inv
    var = _head_sum(d * d) * inv
    yn = d * lax.rsqrt(var + RWKV_GN_EPS) * lng_ref[...] + lnb_ref[...] + bonus_ref[...]
    y_ref[...] = (yn * g_ref[...]).astype(y_ref.dtype)


def _rwkv_scan(parts, ln_g, ln_b):
    r, ld, k, v, kk, b, g, bonus = parts
    T, W = r.shape
    L, PW = RWKV_CHUNK, 2 * RWKV_HEAD_DIM
    nc, npair = T // L, W // PW
    pre_rows = min(2 * L, T)
    seq_rows = min(4 * L, T)

    def specs(rows):
        n = rows // L
        row = pl.BlockSpec((rows, W), lambda c: (c, 0))
        mat = pl.BlockSpec((n, npair, PW, PW), lambda c: (c, 0, 0, 0))
        vec = pl.BlockSpec((n, 1, W), lambda c: (c, 0, 0))
        return row, mat, vec

    row, mat, vec = specs(pre_rows)
    rh, oh, nt, ds, ge = pl.pallas_call(
        _rwkv_pre_kernel,
        grid=(T // pre_rows,),
        in_specs=[row] * 6,
        out_specs=[row, row, mat, mat, vec],
        out_shape=[jax.ShapeDtypeStruct((T, W), bf16), jax.ShapeDtypeStruct((T, W), f32),
                   jax.ShapeDtypeStruct((nc, npair, PW, PW), bf16), jax.ShapeDtypeStruct((nc, npair, PW, PW), f32),
                   jax.ShapeDtypeStruct((nc, 1, W), f32)],
        compiler_params=_cp(("parallel",)),
        name="rwkv_pre",
    )(r, ld, k, v, kk, b)
    row, mat, vec = specs(seq_rows)
    o = pl.pallas_call(
        _rwkv_seq_kernel,
        grid=(T // seq_rows,),
        in_specs=[row, row, mat, mat, vec],
        out_specs=row,
        out_shape=jax.ShapeDtypeStruct((T, W), f32),
        scratch_shapes=[pltpu.VMEM((npair, PW, PW), f32)],
        compiler_params=_cp(("arbitrary",)),
        name="rwkv_seq",
    )(rh, oh, nt, ds, ge)
    tm = min(512, T)
    blk = pl.BlockSpec((tm, W), lambda i: (i, 0))
    one = pl.BlockSpec((1, W), lambda i: (0, 0))
    return pl.pallas_call(
        _rwkv_post_kernel,
        grid=(T // tm,),
        in_specs=[blk, blk, blk, one, one],
        out_specs=blk,
        out_shape=jax.ShapeDtypeStruct((T, W), bf16),
        compiler_params=_cp(("parallel",)),
        name="rwkv_post",
    )(o, g, bonus, ln_g.reshape(1, W), ln_b.reshape(1, W))


def _s5_dot(a, b):
    return jnp.dot(a.astype(bf16), b.astype(bf16), preferred_element_type=f32)


def _transpose_pieces(w):
    w = list(w)
    lane = lax.broadcasted_iota(jnp.int32, w[0].shape, 1)
    for d in (4, 2, 1):
        width = S5_GROUP * d
        low = (lane & width) == 0
        for i in range(8):
            if i & d == 0:
                a, b = w[i], w[i + d]
                w[i] = jnp.where(low, a, pltpu.roll(b, width, 1))
                w[i + d] = jnp.where(low, pltpu.roll(a, 128 - width, 1), b)
    return w


def _s5_split_kernel(*refs):
    z_refs, u_ref = refs[:-1], refs[-1]
    ncb = u_ref.shape[1]
    for j, z_ref in enumerate(z_refs):
        for b in range(S5_CHUNK // 8):
            w = [z_ref[pl.ds(8 * b + i, ncb, stride=S5_CHUNK), :] for i in range(8)]
            o = _transpose_pieces(w)
            for a in range(8):
                u_ref[8 * j + a, :, 128 * b:128 * (b + 1)] = o[a]


def _toeplitz(krow):
    P, LP = krow.shape
    lane = lax.broadcasted_iota(jnp.int32, (P, LP), 1)
    blocks = [krow]
    for s in range(1, LP // P):
        blocks.append(jnp.where(lane >= P * s, pltpu.roll(krow, P * s, 1), 0.0))
    return jnp.concatenate(blocks, axis=0)


def _s5_in_kernel(u_ref, krow_ref, br_ref, bi_ref, y_ref, xr_ref, xi_ref):
    for j in range(2):
        y_ref[j] = _s5_dot(u_ref[j], _toeplitz(krow_ref[j]))
    u2 = jnp.concatenate([u_ref[0], u_ref[1]], axis=1)
    xr_ref[...] = _s5_dot(u2, br_ref[0])
    xi_ref[...] = _s5_dot(u2, bi_ref[0])


def _s5_scan_kernel(xr_ref, xi_ref, ar_ref, ai_ref, sr_ref, si_ref):
    nc = xr_ref.shape[0]
    ar = ar_ref[...]
    ai = ai_ref[...]

    def body(c, carry):
        sr, si = carry
        sr_ref[pl.ds(c, 1), :] = sr
        si_ref[pl.ds(c, 1), :] = si
        xr = xr_ref[pl.ds(c, 1), :]
        xi = xi_ref[pl.ds(c, 1), :]
        return ar * sr - ai * si + xr, ar * si + ai * sr + xi

    z = jnp.zeros_like(ar)
    lax.fori_loop(0, nc, body, (z, z))


def _s5_out_kernel(y_ref, sr_ref, si_ref, cr_ref, ci_ref, u_ref, d_ref, o_ref):
    sr = sr_ref[...]
    si = si_ref[...]
    for j in range(2):
        y = y_ref[j] + _s5_dot(sr, cr_ref[j]) + _s5_dot(si, ci_ref[j]) + d_ref[j] * u_ref[j]
        o_ref[j] = 0.5 * y * (1.0 + jnp.tanh(0.7978845608028654 * (y + 0.044715 * (y * y * y))))


def _glu_kernel(y_ref, w_ref, b_ref, o_ref, *nat_refs):
    ncb = y_ref.shape[1]
    for j, nat_ref in enumerate(nat_refs):
        for b in range(S5_CHUNK // 8):
            o = [y_ref[8 * j + a, :, 128 * b:128 * (b + 1)] for a in range(8)]
            w = _transpose_pieces(o)
            for i in range(8):
                nat_ref[pl.ds(8 * b + i, ncb, stride=S5_CHUNK), :] = w[i]
    y = jnp.concatenate([r[...] for r in nat_refs], axis=1)
    o_ref[...] = (y * _sigmoid(_bmm(y, w_ref[...]) + b_ref[...])).astype(o_ref.dtype)


def _s5_tables(a_re, a_im, log_dt, b_re, b_im, c_re, c_im, d):
    G, N, P, L = S5_GROUPS, S5_STATE, S5_GROUP, S5_CHUNK
    D = a_re.shape[0]
    dt = jnp.exp(log_dt)[:, None, :, None]
    lags = jnp.arange(L + 1, dtype=f32)[None, :, None, None]
    mag = jnp.exp(a_re[:, None] * dt * lags)
    ang = a_im[:, None] * dt * lags
    pw_re, pw_im = mag * jnp.cos(ang), mag * jnp.sin(ang)
    den = a_re * a_re + a_im * a_im
    nr, ni = pw_re[:, 1] - 1.0, pw_im[:, 1]
    coef_re = ((nr * a_re + ni * a_im) / den)[..., None]
    coef_im = ((ni * a_re - nr * a_im) / den)[..., None]
    bb_re = coef_re * b_re - coef_im * b_im
    bb_im = coef_re * b_im + coef_im * b_re
    rep = lambda a, axis: jnp.repeat(a, P, axis=axis)
    pl_re, pl_im = rep(pw_re[:, :L].transpose(0, 2, 1, 3), 2), rep(pw_im[:, :L].transpose(0, 2, 1, 3), 2)
    ct_re, ct_im = jnp.tile(c_re, (1, 1, L, 1)), jnp.tile(c_im, (1, 1, L, 1))
    cb_re, cb_im = ct_re * pl_re - ct_im * pl_im, ct_re * pl_im + ct_im * pl_re
    krow = (jnp.einsum('dgxn,dgnq->dgqx', cb_re, bb_re, precision=HI)
            - jnp.einsum('dgxn,dgnq->dgqx', cb_im, bb_im, precision=HI))
    e_re = rep(pw_re[:, L - 1::-1][:, :L].transpose(0, 2, 1, 3), 2)
    e_im = rep(pw_im[:, L - 1::-1][:, :L].transpose(0, 2, 1, 3), 2)
    bq_re = jnp.tile(bb_re.transpose(0, 1, 3, 2), (1, 1, L, 1))
    bq_im = jnp.tile(bb_im.transpose(0, 1, 3, 2), (1, 1, L, 1))
    bs_re = (e_re * bq_re - e_im * bq_im).reshape(D, G // 2, 2, L * P, N)
    bs_im = (e_re * bq_im + e_im * bq_re).reshape(D, G // 2, 2, L * P, N)

    def pair_in(bs):
        zz = jnp.zeros_like(bs[:, :, 0])
        return jnp.concatenate([jnp.concatenate([bs[:, :, 0], zz], axis=3), jnp.concatenate([zz, bs[:, :, 1]], axis=3)], axis=2)

    f_re = rep(pw_re[:, 1:].transpose(0, 2, 3, 1), 3)
    f_im = rep(pw_im[:, 1:].transpose(0, 2, 3, 1), 3)
    cn_re = jnp.tile(c_re.transpose(0, 1, 3, 2), (1, 1, 1, L))
    cn_im = jnp.tile(c_im.transpose(0, 1, 3, 2), (1, 1, 1, L))
    cs_re = (cn_re * f_re - cn_im * f_im).reshape(D, G // 2, 2, N, L * P)
    cs_im = (-(cn_re * f_im + cn_im * f_re)).reshape(D, G // 2, 2, N, L * P)

    def pair_out(cs):
        zz = jnp.zeros_like(cs[:, :, 0])
        top = jnp.concatenate([cs[:, :, 0], zz], axis=2)[:, :, None]
        bot = jnp.concatenate([zz, cs[:, :, 1]], axis=2)[:, :, None]
        return jnp.concatenate([top, bot], axis=2).reshape(D, G, 2 * N, L * P)

    dd = jnp.tile(d.reshape(D, G, 1, P), (1, 1, 1, L))
    return dict(krow=krow, bsr=pair_in(bs_re), bsi=pair_in(bs_im), cr=pair_out(cs_re), ci=pair_out(cs_im),
                al_re=pw_re[:, L].reshape(D, 1, G * N), al_im=pw_im[:, L].reshape(D, 1, G * N), d=dd)


def _s5(z_tail, tab, layer, glu_w, glu_b):
    T = z_tail.shape[0]
    G, N, P, L = S5_GROUPS, S5_STATE, S5_GROUP, S5_CHUNK
    nc, LP = T // L, L * P
    tm = min(1024, T)
    u = pl.pallas_call(
        _s5_split_kernel,
        grid=(T // tm,),
        in_specs=[pl.BlockSpec((tm, 128), lambda i, j=j: (i, j)) for j in range(S5_W // 128)],
        out_specs=pl.BlockSpec((G, tm // L, LP), lambda i: (0, i, 0)),
        out_shape=jax.ShapeDtypeStruct((G, nc, LP), f32),
        compiler_params=_cp(("parallel",)),
        name="s5_split",
    )(*[z_tail] * (S5_W // 128))
    grp = lambda s: pl.BlockSpec((2,) + s, lambda j: (j, 0, 0))
    tgrp = lambda s: pl.BlockSpec((None, 2) + s, lambda j: (layer, j, 0, 0))
    tone = lambda s: pl.BlockSpec((None, 1) + s, lambda j: (layer, j, 0, 0))
    lanes = pl.BlockSpec((nc, 2 * N), lambda j: (0, j))
    y_in, xr, xi = pl.pallas_call(
        _s5_in_kernel,
        grid=(G // 2,),
        in_specs=[grp((nc, LP)), tgrp((P, LP)), tone((2 * LP, 2 * N)), tone((2 * LP, 2 * N))],
        out_specs=[grp((nc, LP)), lanes, lanes],
        out_shape=[jax.ShapeDtypeStruct((G, nc, LP), f32), jax.ShapeDtypeStruct((nc, G * N), f32),
                   jax.ShapeDtypeStruct((nc, G * N), f32)],
        compiler_params=_cp(("parallel",)),
        name="s5_in",
    )(u, tab["krow"], tab["bsr"], tab["bsi"])
    whole = lambda s: pl.BlockSpec(s, lambda i: (0,) * len(s))
    sr, si = pl.pallas_call(
        _s5_scan_kernel,
        grid=(1,),
        in_specs=[whole((nc, G * N)), whole((nc, G * N)), pl.BlockSpec((None, 1, G * N), lambda i: (layer, 0, 0)),
                  pl.BlockSpec((None, 1, G * N), lambda i: (layer, 0, 0))],
        out_specs=[whole((nc, G * N)), whole((nc, G * N))],
        out_shape=[jax.ShapeDtypeStruct((nc, G * N), f32)] * 2,
        compiler_params=_cp(("arbitrary",)),
        name="s5_scan",
    )(xr, xi, tab["al_re"], tab["al_im"])
    y = pl.pallas_call(
        _s5_out_kernel,
        grid=(G // 2,),
        in_specs=[grp((nc, LP)), lanes, lanes, tgrp((2 * N, LP)), tgrp((2 * N, LP)), grp((nc, LP)), tgrp((1, LP))],
        out_specs=grp((nc, LP)),
        out_shape=jax.ShapeDtypeStruct((G, nc, LP), f32),
        compiler_params=_cp(("parallel",)),
        name="s5_out",
    )(y_in, sr, si, tab["cr"], tab["ci"], u, tab["d"])
    return pl.pallas_call(
        _glu_kernel,
        grid=(T // tm,),
        in_specs=[pl.BlockSpec((G, tm // L, LP), lambda i: (0, i, 0)), pl.BlockSpec((S5_W, S5_W), lambda i: (0, 0)),
                  pl.BlockSpec((1, S5_W), lambda i: (0, 0))],
        out_specs=pl.BlockSpec((tm, S5_W), lambda i: (i, 0)),
        out_shape=jax.ShapeDtypeStruct((T, S5_W), bf16),
        scratch_shapes=[pltpu.VMEM((tm, 128), f32)] * (S5_W // 128),
        compiler_params=_cp(("parallel",)),
        name="s5_glu",
    )(y, glu_w, glu_b.reshape(1, S5_W))


def _pad_rows(w, n):
    return jnp.pad(w, ((0, n - w.shape[0]), (0, 0)))


def _split_mu(mu):
    l0 = 3 * RWKV_W
    return jnp.concatenate([
        mu[:l0], jnp.pad(mu[l0:l0 + DECAY_LORA], (0, LORA_PAD - DECAY_LORA)),
        jnp.pad(mu[l0 + DECAY_LORA:l0 + DECAY_LORA + AAA_LORA], (0, LORA_PAD - AAA_LORA)),
        mu[l0 + DECAY_LORA + AAA_LORA:]]).reshape(1, RWKV_PAD)


def kernel(x, p, norm_mix_g, w_in, mlstm_conv, mlstm_ib, mlstm_fb, mlstm_norm_g, rwkv_mu, rwkv_w0, rwkv_w2, rwkv_a0, rwkv_a2, rwkv_g2, rwkv_kk, rwkv_ka, rwkv_rk, rwkv_ln_g, rwkv_ln_b, s5_a_re, s5_a_im, s5_log_dt, s5_b_re, s5_b_im, s5_c_re, s5_c_im, s5_d, s5_glu_w, s5_glu_b, w_up_m, w_up_r, w_up_s, w_out, norm_ffn_g, ffn_w_gate, ffn_w_up, ffn_w_down, norm_ple_g, ple_w_gate, ple_w_proj, final_norm_g):
    B, T, D = x.shape
    depth = w_in.shape[0]
    wt = jnp.swapaxes(w_in, 1, 2)
    lora0 = 3 * RWKV_W
    lora_rows = [(0, DECAY_LORA), LORA_PAD - DECAY_LORA, (DECAY_LORA, DECAY_LORA + AAA_LORA), LORA_PAD - AAA_LORA,
                 (DECAY_LORA + AAA_LORA, RWKV_IN - lora0)]
    s5_tab = _s5_tables(s5_a_re, s5_a_im, s5_log_dt, s5_b_re, s5_b_im, s5_c_re, s5_c_im, s5_d)
    outs = []
    for bi in range(B):
        h = x[bi]
        xn = _rmsnorm(h, norm_mix_g[0], bf16)
        for i in range(depth):
            z_main = _proj(xn, wt, i, 0, 4, MLSTM_W)
            z_rwkv = _proj(xn, wt, i, M_IN, RWKV_PAD // RWKV_W, RWKV_W, last_rows=lora_rows)
            z_gate = _proj(xn, wt, i, M_IN + RWKV_IN + S5_W, 3 * D_MODEL // 1024, 1024, bf16, gate=True)
            z_tail = _proj_tail(xn, wt, i)

            gate_b = jnp.pad(jnp.concatenate([mlstm_ib[i], mlstm_fb[i]]), (0, 128 - 2 * MLSTM_HEADS)).reshape(1, 128)
            y_m = _mlstm(z_main, z_tail, mlstm_conv[i], gate_b, mlstm_norm_g[i])

            vec = lambda a: a.reshape(1, RWKV_W)
            parts = _rwkv_prep(z_rwkv, _split_mu(rwkv_mu[i]), vec(rwkv_w0[i]), vec(rwkv_a0[i]), vec(rwkv_kk[i]),
                               vec(rwkv_ka[i]), vec(rwkv_rk[i]), _pad_rows(rwkv_w2[i], LORA_PAD).astype(bf16),
                               _pad_rows(rwkv_a2[i], LORA_PAD).astype(bf16), rwkv_g2[i].astype(bf16))
            y_r = _rwkv_scan(parts, rwkv_ln_g[i], rwkv_ln_b[i])

            y_s = _s5(z_tail, s5_tab, i, s5_glu_w[i].astype(bf16), s5_glu_b[i])

            mixed = _merge(y_m, y_r, y_s, z_gate, w_up_m, w_up_r, w_up_s, i)
            h, hn = _resid_norm(mixed, w_out[i].astype(bf16), h, norm_ffn_g[i], D_MODEL)
            act = _ffn_up(hn, ffn_w_gate, ffn_w_up, i)
            h = _resid_mm(act, ffn_w_down, h, i)
            last = i == depth - 1
            g_next = final_norm_g if last else norm_mix_g[i + 1]
            h, xn = _ple(h, p[i, bi].astype(bf16), ple_w_gate[i].astype(bf16), ple_w_proj[i].astype(bf16), norm_ple_g[i],
                         g_next, f32 if last else bf16)
        outs.append(xn)
    return jnp.stack(outs)
```

```python
import functools

import jax
import jax.numpy as jnp
import numpy as np
from jax import lax
from jax.experimental import pallas as pl
from jax.experimental.pallas import tpu as pltpu

f32 = jnp.float32
bf16 = jnp.bfloat16
HI = lax.Precision.HIGHEST

D_MODEL = 2048
PLE_DIM = 256
RMS_EPS = 1e-6
MLSTM_HEADS = 4
MLSTM_HEAD_DIM = 256
MLSTM_W = 1024
MLSTM_CHUNK = 256
GATE_SOFTCAP = 15.0
RWKV_HEADS = 8
RWKV_HEAD_DIM = 64
RWKV_W = 512
DECAY_LORA = 96
AAA_LORA = 96
GATE_LORA = 256
LORA_PAD = 128
RWKV_GN_EPS = 64e-5
RWKV_CHUNK = 64
S5_GROUP = 16
S5_GROUPS = 32
S5_W = 512
S5_STATE = 64
S5_CHUNK = 16
FFN_HIDDEN = 5632
M_IN = 4 * MLSTM_W + 2 * MLSTM_HEADS
RWKV_IN = 3 * RWKV_W + DECAY_LORA + AAA_LORA + GATE_LORA
RWKV_PAD = 3 * RWKV_W + 2 * LORA_PAD + GATE_LORA
Z_TAIL = S5_W + 128
VMEM_LIMIT = 56 * 1024 * 1024


def _cp(sem):
    return pltpu.CompilerParams(dimension_semantics=sem, vmem_limit_bytes=VMEM_LIMIT)


def _nt(a, b, **kw):
    return lax.dot_general(a, b, (((1,), (1,)), ((), ())), preferred_element_type=f32, **kw)


def _tn(a, b, **kw):
    return lax.dot_general(a, b, (((0,), (0,)), ((), ())), preferred_element_type=f32, **kw)


def _mm(a, b, **kw):
    return jnp.dot(a, b, preferred_element_type=f32, **kw)


def _bmm(a, b):
    return jnp.dot(a.astype(bf16), b.astype(bf16), preferred_element_type=f32)


def _sigmoid(x):
    return 1.0 / (1.0 + jnp.exp(-x))


def _rms(x, g):
    return x * lax.rsqrt(jnp.mean(x * x, axis=-1, keepdims=True) + RMS_EPS) * g


def _norm_kernel(x_ref, g_ref, o_ref):
    o_ref[...] = _rms(x_ref[...], g_ref[...]).astype(o_ref.dtype)


def _rmsnorm(x, g, out_dtype):
    T, D = x.shape
    tm = min(512, T)
    return pl.pallas_call(
        _norm_kernel,
        grid=(T // tm,),
        in_specs=[pl.BlockSpec((tm, D), lambda i: (i, 0)), pl.BlockSpec((1, D), lambda i: (0, 0))],
        out_specs=pl.BlockSpec((tm, D), lambda i: (i, 0)),
        out_shape=jax.ShapeDtypeStruct((T, D), out_dtype),
        compiler_params=_cp(("parallel",)),
        name="rmsnorm",
    )(x, g.reshape(1, D))


def _load_wt(w, rows):
    parts = [w[r[0]:r[1], :] if isinstance(r, tuple) else jnp.zeros((r, w.shape[1]), f32) for r in rows]
    w = parts[0] if len(parts) == 1 else jnp.concatenate(parts, axis=0)
    return w.T.astype(bf16)


def _proj_kernel(a_ref, w_ref, o_ref, wb_ref, *, gate):
    @pl.when(pl.program_id(1) == 0)
    def _():
        wb_ref[...] = _load_wt(w_ref[0], [(0, w_ref.shape[1])])

    z = _mm(a_ref[...], wb_ref[...])
    o_ref[...] = (0.5 * jnp.tanh(0.5 * z) + 0.5 if gate else z).astype(o_ref.dtype)


def _proj(a, wt, layer, row0, n_tiles, tn, out_dtype=f32, gate=False):
    T, K = a.shape
    tm = min(1024, T)
    return pl.pallas_call(
        functools.partial(_proj_kernel, gate=gate),
        grid=(n_tiles, T // tm),
        in_specs=[pl.BlockSpec((tm, K), lambda j, i: (i, 0)),
                  pl.BlockSpec((pl.Element(1), pl.Element(tn), pl.Element(K)),
                               lambda j, i: (layer, pl.multiple_of(row0 + tn * j, 8), 0))],
        out_specs=pl.BlockSpec((tm, tn), lambda j, i: (i, j)),
        out_shape=jax.ShapeDtypeStruct((T, n_tiles * tn), out_dtype),
        scratch_shapes=[pltpu.VMEM((K, tn), bf16)],
        compiler_params=_cp(("parallel", "arbitrary")),
        name="in_proj",
    )(a, wt)


def _proj_tail_kernel(a_ref, ws_ref, wg_ref, o_ref, wb_ref):
    @pl.when(pl.program_id(0) == 0)
    def _():
        pad = jnp.zeros((Z_TAIL - S5_W - wg_ref.shape[0], wg_ref.shape[1]), f32)
        wb_ref[...] = jnp.concatenate([ws_ref[0], wg_ref[...], pad], axis=0).T.astype(bf16)

    o_ref[...] = _mm(a_ref[...], wb_ref[...])


def _proj_tail(a, wt, layer):
    T, K = a.shape
    tm = min(1024, T)
    n_if = 2 * MLSTM_HEADS
    return pl.pallas_call(
        _proj_tail_kernel,
        grid=(T // tm,),
        in_specs=[pl.BlockSpec((tm, K), lambda i: (i, 0)),
                  pl.BlockSpec((pl.Element(1), pl.Element(S5_W), pl.Element(K)), lambda i: (layer, M_IN + RWKV_IN, 0)),
                  pl.BlockSpec((None, n_if, K), lambda i: (layer, 4 * MLSTM_W // n_if, 0))],
        out_specs=pl.BlockSpec((tm, Z_TAIL), lambda i: (i, 0)),
        out_shape=jax.ShapeDtypeStruct((T, Z_TAIL), f32),
        scratch_shapes=[pltpu.VMEM((K, Z_TAIL), bf16)],
        compiler_params=_cp(("arbitrary",)),
        name="in_proj_tail",
    )(a, wt, wt)


def _merge_kernel(ym_ref, yr_ref, ys_ref, gm_ref, gr_ref, gs_ref, um_ref, ur_ref, us_ref, o_ref, umb_ref, urb_ref, usb_ref):
    @pl.when(pl.program_id(1) == 0)
    def _():
        umb_ref[...] = um_ref[...].astype(bf16)
        urb_ref[...] = ur_ref[...].astype(bf16)
        usb_ref[...] = us_ref[...].astype(bf16)

    acc = gm_ref[...] * _mm(ym_ref[...], umb_ref[...])
    acc += gr_ref[...] * _mm(yr_ref[...], urb_ref[...])
    acc += gs_ref[...] * _mm(ys_ref[...], usb_ref[...])
    o_ref[...] = acc.astype(o_ref.dtype)


def _merge(ym, yr, ys, gates, um, ur, us, layer):
    T = ym.shape[0]
    tm, tn = min(512, T), 1024
    nj = D_MODEL // tn

    def gate_spec(b):
        return pl.BlockSpec((tm, tn), lambda j, i, b=b: (i, b * nj + j))

    def y_spec(w):
        return pl.BlockSpec((tm, w), lambda j, i: (i, 0))

    def u_spec(w):
        return pl.BlockSpec((None, w, tn), lambda j, i: (layer, 0, j))

    return pl.pallas_call(
        _merge_kernel,
        grid=(nj, T // tm),
        in_specs=[y_spec(MLSTM_W), y_spec(RWKV_W), y_spec(S5_W), gate_spec(0), gate_spec(1), gate_spec(2),
                  u_spec(MLSTM_W), u_spec(RWKV_W), u_spec(S5_W)],
        out_specs=pl.BlockSpec((tm, tn), lambda j, i: (i, j)),
        out_shape=jax.ShapeDtypeStruct((T, D_MODEL), bf16),
        scratch_shapes=[pltpu.VMEM((MLSTM_W, tn), bf16), pltpu.VMEM((RWKV_W, tn), bf16), pltpu.VMEM((S5_W, tn), bf16)],
        compiler_params=_cp(("parallel", "arbitrary")),
        name="gated_merge",
    )(ym, yr, ys, gates, gates, gates, um, ur, us)


def _resid_norm_kernel(a_ref, w_ref, h_ref, g_ref, ho_ref, no_ref, acc_ref):
    k = pl.program_id(1)

    @pl.when(k == 0)
    def _():
        acc_ref[...] = h_ref[...]

    acc_ref[...] += _mm(a_ref[...], w_ref[...])

    @pl.when(k == pl.num_programs(1) - 1)
    def _():
        hn = acc_ref[...]
        ho_ref[...] = hn
        no_ref[...] = _rms(hn, g_ref[...]).astype(no_ref.dtype)


def _resid_norm(a, w, h, g, tk, norm_dtype=bf16):
    T, K = a.shape
    tm = min(512, T)
    return pl.pallas_call(
        _resid_norm_kernel,
        grid=(T // tm, K // tk),
        in_specs=[pl.BlockSpec((tm, tk), lambda i, k: (i, k)), pl.BlockSpec((tk, D_MODEL), lambda i, k: (k, 0)),
                  pl.BlockSpec((tm, D_MODEL), lambda i, k: (i, 0)), pl.BlockSpec((1, D_MODEL), lambda i, k: (0, 0))],
        out_specs=[pl.BlockSpec((tm, D_MODEL), lambda i, k: (i, 0)), pl.BlockSpec((tm, D_MODEL), lambda i, k: (i, 0))],
        out_shape=[jax.ShapeDtypeStruct((T, D_MODEL), f32), jax.ShapeDtypeStruct((T, D_MODEL), norm_dtype)],
        scratch_shapes=[pltpu.VMEM((tm, D_MODEL), f32)],
        compiler_params=_cp(("parallel", "arbitrary")),
        name="resid_matmul_norm",
    )(a, w, h, g.reshape(1, D_MODEL))


def _ffn_up_kernel(a_ref, wg_ref, wu_ref, o_ref, wgb_ref, wub_ref):
    @pl.when(pl.program_id(1) == 0)
    def _():
        wgb_ref[...] = wg_ref[...].astype(bf16)
        wub_ref[...] = wu_ref[...].astype(bf16)

    a = a_ref[...]
    gt = _mm(a, wgb_ref[...])
    up = _mm(a, wub_ref[...])
    o_ref[...] = (gt * _sigmoid(gt) * up).astype(o_ref.dtype)


def _ffn_up(a, wg, wu, layer):
    T, K = a.shape
    N = wg.shape[2]
    tm, tn = min(1024, T), 512
    return pl.pallas_call(
        _ffn_up_kernel,
        grid=(N // tn, T // tm),
        in_specs=[pl.BlockSpec((tm, K), lambda j, i: (i, 0)), pl.BlockSpec((None, K, tn), lambda j, i: (layer, 0, j)),
                  pl.BlockSpec((None, K, tn), lambda j, i: (layer, 0, j))],
        out_specs=pl.BlockSpec((tm, tn), lambda j, i: (i, j)),
        out_shape=jax.ShapeDtypeStruct((T, N), bf16),
        scratch_shapes=[pltpu.VMEM((K, tn), bf16), pltpu.VMEM((K, tn), bf16)],
        compiler_params=_cp(("parallel", "arbitrary")),
        name="ffn_up",
    )(a, wg, wu)


def _resid_mm_kernel(a_ref, w_ref, h_ref, o_ref, wb_ref):
    @pl.when(pl.program_id(1) == 0)
    def _():
        wb_ref[...] = w_ref[...].astype(bf16)

    o_ref[...] = h_ref[...] + _mm(a_ref[...], wb_ref[...])


def _resid_mm(a, w, h, layer):
    T, K = a.shape
    N = w.shape[2]
    tm, tn = min(512, T), 512
    return pl.pallas_call(
        _resid_mm_kernel,
        grid=(N // tn, T // tm),
        in_specs=[pl.BlockSpec((tm, K), lambda j, i: (i, 0)), pl.BlockSpec((None, K, tn), lambda j, i: (layer, 0, j)),
                  pl.BlockSpec((tm, tn), lambda j, i: (i, j))],
        out_specs=pl.BlockSpec((tm, tn), lambda j, i: (i, j)),
        out_shape=jax.ShapeDtypeStruct((T, N), f32),
        scratch_shapes=[pltpu.VMEM((K, tn), bf16)],
        compiler_params=_cp(("parallel", "arbitrary")),
        name="resid_matmul",
    )(a, w, h)


def _ple_kernel(h_ref, p_ref, wg_ref, wp_ref, gp_ref, g_ref, ho_ref, no_ref):
    tm = h_ref.shape[0]
    parts = [slice(0, tm // 2), slice(tm // 2, tm)]
    hs = [h_ref[r, :] for r in parts]
    hps = [_rms(h, gp_ref[...]).astype(bf16) for h in hs]
    gates = [_sigmoid(_mm(hp, wg_ref[...])) for hp in hps]
    projs = [_mm(p_ref[r, :], wp_ref[...]) for r in parts]
    for r, h, gate, proj in zip(parts, hs, gates, projs):
        hn = h + proj * gate
        ho_ref[r, :] = hn
        no_ref[r, :] = _rms(hn, g_ref[...]).astype(no_ref.dtype)


def _ple(h, p, wg, wp, g_ple, g, norm_dtype):
    T = h.shape[0]
    tm = min(512, T)
    row = lambda w: pl.BlockSpec((tm, w), lambda i: (i, 0))
    full = lambda s: pl.BlockSpec(s, lambda i: (0, 0))
    return pl.pallas_call(
        _ple_kernel,
        grid=(T // tm,),
        in_specs=[row(D_MODEL), row(PLE_DIM), full((D_MODEL, D_MODEL)), full((PLE_DIM, D_MODEL)), full((1, D_MODEL)),
                  full((1, D_MODEL))],
        out_specs=[row(D_MODEL), row(D_MODEL)],
        out_shape=[jax.ShapeDtypeStruct((T, D_MODEL), f32), jax.ShapeDtypeStruct((T, D_MODEL), norm_dtype)],
        compiler_params=_cp(("parallel",)),
        name="ple_norm",
    )(h, p, wg, wp, g_ple.reshape(1, D_MODEL), g.reshape(1, D_MODEL))


def _mlstm_kernel(zq_ref, zk_ref, v_ref, o_ref, gz_ref, cq_ref, ck_ref, gb_ref, ng_ref, y_ref,
                  ct_ref, n_ref, qbuf_ref, kbuf_ref):
    c = pl.program_id(0)
    L, W = zq_ref.shape
    H, dh = MLSTM_HEADS, MLSTM_HEAD_DIM
    heads = range(H)

    @pl.when(c == 0)
    def _():
        ct_ref[...] = jnp.zeros_like(ct_ref)
        n_ref[...] = jnp.zeros_like(n_ref)
        qbuf_ref[...] = jnp.zeros_like(qbuf_ref)
        kbuf_ref[...] = jnp.zeros_like(kbuf_ref)

    def conv_silu(z_ref, w_ref, buf_ref):
        x = z_ref[...]
        xx = jnp.concatenate([buf_ref[...], x], axis=0)
        w = w_ref[...]
        acc = x * w[0:1, :]
        for j in range(1, w.shape[0]):
            acc = acc + pltpu.roll(xx, j, 0)[8:, :] * w[j:j + 1, :]
        buf_ref[...] = x[L - 8:, :]
        return acc * _sigmoid(acc)

    q = conv_silu(zq_ref, cq_ref, qbuf_ref) * (dh ** -0.5)
    k = conv_silu(zk_ref, ck_ref, kbuf_ref)

    g = gz_ref[...] + gb_ref[...]
    sc = GATE_SOFTCAP * jnp.tanh(g / GATE_SOFTCAP)
    logf = -jnp.log(1.0 + jnp.exp(-sc))
    row = lax.broadcasted_iota(jnp.int32, (L, L), 0)
    col = lax.broadcasted_iota(jnp.int32, (L, L), 1)
    causal = row >= col
    bcs = _mm(causal.astype(f32), logf, precision=HI)
    b128 = pltpu.roll(bcs, 128 - H, 1)
    e128 = sc - b128
    e_t = e128.T

    sl = [slice(h * dh, (h + 1) * dh) for h in heads]
    b_col = [b128[:, h:h + 1] for h in heads]
    e_col = [e128[:, h:h + 1] for h in heads]
    wmat = [jnp.where(causal, jnp.exp(jnp.where(causal, b_col[h] + e_t[h:h + 1, :], 0.0)), 0.0) for h in heads]
    b_end = [b_col[h][L - 1:L, :] for h in heads]
    eb = [jnp.exp(b_col[h]) for h in heads]
    qh = [q[:, sl[h]] for h in heads]
    kh = [k[:, sl[h]] for h in heads]
    qb = [x.astype(bf16) for x in qh]
    kb = [x.astype(bf16) for x in kh]
    vb = [v_ref[:, sl[h]].astype(bf16) for h in heads]
    ct = [ct_ref[h] for h in heads]
    nn = [n_ref[:, sl[h]] for h in heads]
    s = [_nt(qb[h], kb[h]) * wmat[h] for h in heads]
    inter = [_mm(qb[h], ct[h].astype(bf16)) for h in heads]
    num = [_mm(s[h].astype(bf16), vb[h]) + eb[h] * inter[h] for h in heads]
    ng = ng_ref[...]
    for h in heads:
        den = jnp.sum(s[h], axis=1, keepdims=True) + eb[h] * jnp.sum(qh[h] * nn[h], axis=1, keepdims=True)
        hh = num[h] / jnp.maximum(jnp.abs(den), 1.0)
        hh = hh * lax.rsqrt(jnp.mean(hh * hh, axis=1, keepdims=True) + RMS_EPS) * ng[:, sl[h]]
        y_ref[:, sl[h]] = (_sigmoid(o_ref[:, sl[h]]) * hh).astype(y_ref.dtype)
    kw = [kh[h] * jnp.exp(b_end[h] + e_col[h]) for h in heads]
    upd = [_tn(kw[h].astype(bf16), vb[h]) for h in heads]
    for h in heads:
        decay = jnp.exp(b_end[h])
        ct_ref[h] = decay * ct[h] + upd[h]
        n_ref[:, sl[h]] = decay * nn[h] + jnp.sum(kw[h], axis=0, keepdims=True)


def _mlstm(z_main, z_tail, conv_w, gate_b, norm_g):
    T = z_main.shape[0]
    L, dh, H, W = min(MLSTM_CHUNK, T), MLSTM_HEAD_DIM, MLSTM_HEADS, MLSTM_W
    blk = lambda j: pl.BlockSpec((L, W), lambda c, j=j: (c, j))
    return pl.pallas_call(
        _mlstm_kernel,
        grid=(T // L,),
        in_specs=[blk(0), blk(1), blk(2), blk(3),
                  pl.BlockSpec((L, 128), lambda c: (c, S5_W // 128)),
                  pl.BlockSpec((4, W), lambda c: (0, 0)), pl.BlockSpec((4, W), lambda c: (0, 1)),
                  pl.BlockSpec((1, 128), lambda c: (0, 0)), pl.BlockSpec((1, W), lambda c: (0, 0))],
        out_specs=pl.BlockSpec((L, W), lambda c: (c, 0)),
        out_shape=jax.ShapeDtypeStruct((T, W), bf16),
        scratch_shapes=[pltpu.VMEM((H, dh, dh), f32), pltpu.VMEM((1, W), f32),
                        pltpu.VMEM((8, W), f32), pltpu.VMEM((8, W), f32)],
        compiler_params=_cp(("arbitrary",)),
        name="mlstm",
    )(z_main, z_main, z_main, z_main, z_tail, conv_w, conv_w, gate_b, norm_g.reshape(1, W))


def _iota_div(shape, dim, width):
    return lax.shift_right_logical(lax.broadcasted_iota(jnp.int32, shape, dim), int(np.log2(width)))


def _head_ones(n, width):
    return (_iota_div((n, n), 0, width) == _iota_div((n, n), 1, width)).astype(f32)


def _head_sum(x):
    blk = _head_ones(RWKV_W, RWKV_HEAD_DIM).astype(bf16)
    hi = x.astype(bf16)
    r1 = x - hi.astype(f32)
    mid = r1.astype(bf16)
    lo = (r1 - mid.astype(f32)).astype(bf16)
    return _mm(hi, blk) + _mm(mid, blk) + _mm(lo, blk)


def _rwkv_w_kernel(w_ref, o_ref, *, last_rows):
    j = pl.program_id(0)
    nj = pl.num_programs(0)

    @pl.when(j < nj - 1)
    def _():
        o_ref[...] = _load_wt(w_ref[0], [(0, w_ref.shape[1])])

    @pl.when(j == nj - 1)
    def _():
        o_ref[...] = _load_wt(w_ref[0], last_rows)


def _rwkv_w(wt, layer):
    K = wt.shape[2]
    tn = RWKV_W
    lora_rows = [(0, DECAY_LORA), LORA_PAD - DECAY_LORA, (DECAY_LORA, DECAY_LORA + AAA_LORA), LORA_PAD - AAA_LORA,
                 (DECAY_LORA + AAA_LORA, RWKV_IN - 3 * RWKV_W)]
    return pl.pallas_call(
        functools.partial(_rwkv_w_kernel, last_rows=lora_rows),
        grid=(RWKV_PAD // tn,),
        in_specs=[pl.BlockSpec((pl.Element(1), pl.Element(tn), pl.Element(K)),
                               lambda j: (layer, pl.multiple_of(M_IN + tn * j, 8), 0))],
        out_specs=pl.BlockSpec((K, tn), lambda j: (0, j)),
        out_shape=jax.ShapeDtypeStruct((K, RWKV_PAD), bf16),
        compiler_params=_cp(("parallel",)),
        name="rwkv_w",
    )(wt)


def _rwkv_prep_kernel(a_ref, w_ref, mu_ref, w0_ref, a0_ref, kkg_ref, ka_ref, rk_ref, w2_ref, a2_ref, g2_ref,
                      r_ref, ld_ref, k_ref, v_ref, kk_ref, b_ref, g_ref, bonus_ref, *buf_refs):
    i = pl.program_id(0)
    nh = len(buf_refs)
    tm = a_ref.shape[0] // nh
    W = RWKV_W

    @pl.when(i == 0)
    def _():
        buf_refs[0][0:8, :] = jnp.zeros((8, buf_refs[0].shape[1]), f32)

    rows = [slice(h * tm, (h + 1) * tm) for h in range(nh)]
    xs_all = [_mm(a_ref[rw, :], w_ref[...]) for rw in rows]
    for h, (rw, x) in enumerate(zip(rows, xs_all)):
        buf_ref = buf_refs[h]
        buf_ref[8:, :] = x
        buf_refs[(h + 1) % nh][0:8, :] = x[tm - 8:, :]
        xs = x + (buf_ref[pl.ds(7, tm), :] - x) * mu_ref[...]
        r = xs[:, 0:W]
        k = xs[:, W:2 * W]
        v = xs[:, 2 * W:3 * W]
        wl = xs[:, 3 * W:3 * W + LORA_PAD]
        al = xs[:, 3 * W + LORA_PAD:3 * W + 2 * LORA_PAD]
        gl = xs[:, 3 * W + 2 * LORA_PAD:]
        t = w0_ref[...] + _bmm(jnp.tanh(wl), w2_ref[...])
        w = -(jnp.maximum(-t, 0.0) + jnp.log(1.0 + jnp.exp(-jnp.abs(t)))) - 0.5
        a = _sigmoid(a0_ref[...] + _bmm(al, a2_ref[...]))
        g = _bmm(_sigmoid(gl), g2_ref[...])
        kk = k * kkg_ref[...]
        ss = _head_sum(kk * kk)
        kk = kk / jnp.maximum(jnp.sqrt(ss), 1e-12)
        k2 = k * (1.0 + (a - 1.0) * ka_ref[...])
        bonus = _head_sum(r * k2 * rk_ref[...]) * v
        r_ref[rw, :] = r
        ld_ref[rw, :] = -jnp.exp(w)
        k_ref[rw, :] = k2
        v_ref[rw, :] = v
        kk_ref[rw, :] = kk
        b_ref[rw, :] = kk * a
        g_ref[rw, :] = g
        bonus_ref[rw, :] = bonus


def _rwkv_prep(xn, w_rwkv, mu, w0, a0, kkg, ka, rk, w2, a2, g2):
    T, K = xn.shape
    tm = min(512, T)
    n_sub = 2
    W = RWKV_W
    vec = lambda n: pl.BlockSpec((1, n), lambda i: (0, 0))
    mat = lambda s: pl.BlockSpec(s, lambda i: (0, 0))
    out = pl.BlockSpec((tm, W), lambda i: (i, 0))
    return pl.pallas_call(
        _rwkv_prep_kernel,
        grid=(T // tm,),
        in_specs=[pl.BlockSpec((tm, K), lambda i: (i, 0)), mat((K, RWKV_PAD)), vec(RWKV_PAD), vec(W), vec(W),
                  vec(W), vec(W), vec(W), mat((LORA_PAD, W)), mat((LORA_PAD, W)), mat((GATE_LORA, W))],
        out_specs=[out] * 8,
        out_shape=[jax.ShapeDtypeStruct((T, W), f32)] * 8,
        scratch_shapes=[pltpu.VMEM((tm // n_sub + 8, RWKV_PAD), f32)] * n_sub,
        compiler_params=_cp(("arbitrary",)),
        name="rwkv_prep",
    )(xn, w_rwkv, mu, w0, a0, kkg, ka, rk, w2, a2, g2)


def _rwkv_pre_kernel(r_ref, ld_ref, k_ref, v_ref, kk_ref, b_ref, rh_ref, oh_ref, nt_ref, ds_ref, ge_ref):
    L, dh = RWKV_CHUNK, RWKV_HEAD_DIM
    GW = 4 * dh
    PW = 2 * dh
    n_chunks = r_ref.shape[0] // L
    rr = lax.broadcasted_iota(jnp.int32, (GW, GW), 0)
    cc = lax.broadcasted_iota(jnp.int32, (GW, GW), 1)
    same = _iota_div((GW, GW), 0, L) == _iota_div((GW, GW), 1, dh)
    strict = same & (rr > cc)
    incl = same & (rr >= cc)
    eye = jnp.where(rr == cc, 1.0, 0.0)
    pair_blk = _head_ones(PW, dh)
    tr = lax.broadcasted_iota(jnp.int32, (L, L), 0)
    tc = lax.broadcasted_iota(jnp.int32, (L, L), 1)
    tril = (tr >= tc).astype(f32)

    def unstack(x):
        acc = x[0:L, :]
        for hh in range(1, GW // L):
            acc = acc + x[hh * L:(hh + 1) * L, :]
        return acc

    def tile(x):
        return jnp.concatenate([x] * (GW // L), axis=0)

    def stack(x):
        return jnp.where(same, tile(x), 0.0)

    tiles = [(ci, gi) for ci in range(n_chunks) for gi in range(RWKV_W // GW)]
    ch = []
    for ci in range(n_chunks):
        rows = slice(ci * L, (ci + 1) * L)
        ld = ld_ref[rows, :]
        lg = _mm(tril, ld, precision=HI)
        g_end = lg[L - 1:L, :]
        ge_ref[ci] = g_end
        dec_out = jnp.exp(-lg)
        dec_tail = jnp.exp(g_end - lg)
        k = k_ref[rows, :]
        b = b_ref[rows, :]
        v = v_ref[rows, :]
        ch.append(dict(rows=rows, v=v, kap=kk_ref[rows, :] * jnp.exp(lg - ld), rt=r_ref[rows, :] * jnp.exp(lg),
                       kt=k * dec_out, bt=b * dec_out, khat=(k * dec_tail).astype(bf16),
                       bhat=(b * dec_tail).astype(bf16), vb=v.astype(bf16)))
    ops = []
    for ci, gi in tiles:
        c = ch[ci]
        lanes = slice(gi * GW, (gi + 1) * GW)
        kap_s = stack(c["kap"][:, lanes])
        rt_s = stack(c["rt"][:, lanes])
        ops.append(dict(lanes=lanes, kap_s=kap_s, rt_s=rt_s, kap_b=kap_s.astype(bf16), rt_b=rt_s.astype(bf16),
                        v_s=stack(c["v"][:, lanes]).astype(bf16), kt_t=tile(c["kt"][:, lanes]).astype(bf16),
                        bt_t=tile(c["bt"][:, lanes]).astype(bf16)))
    for o in ops:
        o["a_kb"] = jnp.where(strict, _nt(o["kap_b"], o["bt_t"]), 0.0)
    for o in ops:
        o["a_kk"] = jnp.where(strict, _nt(o["kap_b"], o["kt_t"]), 0.0).astype(bf16)
        o["a_rk"] = jnp.where(incl, _nt(o["rt_b"], o["kt_t"]), 0.0).astype(bf16)
        o["a_rb"] = jnp.where(incl, _nt(o["rt_b"], o["bt_t"]), 0.0).astype(bf16)
    for o in ops:
        o["p"] = -o["a_kb"]
        o["tinv"] = eye + o["p"]
    step = 2
    while step < L:
        for o in ops:
            pb = o["p"].astype(bf16)
            o["p"] = _mm(pb, pb)
        for o in ops:
            o["tinv"] = o["tinv"] + _bmm(o["tinv"], o["p"])
        step *= 2
    for o in ops:
        o["w1"] = _mm(o["a_kk"], o["v_s"])
        o["w2"] = _mm(o["a_rk"], o["v_s"])
    for o in ops:
        o["ku"] = _bmm(o["tinv"], jnp.concatenate([o["kap_s"], o["w1"]], axis=1))
    for o in ops:
        o["ru"] = _mm(o["a_rb"], o["ku"].astype(bf16))
    for (ci, gi), o in zip(tiles, ops):
        c = ch[ci]
        rows, lanes = c["rows"], o["lanes"]
        rh_ref[rows, lanes] = unstack(o["rt_s"] - o["ru"][:, :GW]).astype(rh_ref.dtype)
        oh_ref[rows, lanes] = unstack(o["w2"] - o["ru"][:, GW:])
        kh = unstack(o["ku"][:, :GW]).astype(bf16)
        uh = unstack(o["ku"][:, GW:]).astype(bf16)
        for pi in range(GW // PW):
            q = gi * (GW // PW) + pi
            loc = slice(pi * PW, (pi + 1) * PW)
            glo = slice(q * PW, (q + 1) * PW)
            nt_ref[ci, q] = (_tn(kh[:, loc], c["bhat"][:, glo]) * pair_blk).astype(nt_ref.dtype)
            ds_ref[ci, q] = (_tn(c["vb"][:, glo], c["khat"][:, glo]) - _tn(uh[:, loc], c["bhat"][:, glo])) * pair_blk


def _rwkv_seq_kernel(rh_ref, oh_ref, nt_ref, ds_ref, ge_ref, o_ref, s_ref):
    L, PW = RWKV_CHUNK, 2 * RWKV_HEAD_DIM
    n_chunks = rh_ref.shape[0] // L

    @pl.when(pl.program_id(0) == 0)
    def _():
        s_ref[...] = jnp.zeros_like(s_ref)

    pairs = range(RWKV_W // PW)
    lanes = [slice(q * PW, (q + 1) * PW) for q in pairs]
    s_mat = [s_ref[q] for q in pairs]
    for ci in range(n_chunks):
        rows = slice(ci * L, (ci + 1) * L)
        s_b = [s.astype(bf16) for s in s_mat]
        decay = jnp.exp(ge_ref[ci])
        for q in pairs:
            s_mat[q] = s_mat[q] * decay[:, lanes[q]] - _mm(s_b[q], nt_ref[ci, q]) + ds_ref[ci, q]
        for q in pairs:
            o_ref[rows, lanes[q]] = _nt(rh_ref[rows, lanes[q]], s_b[q]) + oh_ref[rows, lanes[q]]
    for q in pairs:
        s_ref[q] = s_mat[q]


def _rwkv_post_kernel(o_ref, g_ref, bonus_ref, lng_ref, lnb_ref, y_ref):
    inv = 1.0 / RWKV_HEAD_DIM
    o = o_ref[...]
    d = o - _head_sum(o) * inv
    var = _head_sum(d * d) * inv
    yn = d * lax.rsqrt(var + RWKV_GN_EPS) * lng_ref[...] + lnb_ref[...] + bonus_ref[...]
    y_ref[...] = (yn * g_ref[...]).astype(y_ref.dtype)


def _rwkv_scan(parts, ln_g, ln_b):
    r, ld, k, v, kk, b, g, bonus = parts
    T, W = r.shape
    L, PW = RWKV_CHUNK, 2 * RWKV_HEAD_DIM
    nc, npair = T // L, W // PW
    pre_rows = min(2 * L, T)
    seq_rows = min(4 * L, T)

    def specs(rows):
        n = rows // L
        row = pl.BlockSpec((rows, W), lambda c: (c, 0))
        mat = pl.BlockSpec((n, npair, PW, PW), lambda c: (c, 0, 0, 0))
        vec = pl.BlockSpec((n, 1, W), lambda c: (c, 0, 0))
        return row, mat, vec

    row, mat, vec = specs(pre_rows)
    rh, oh, nt, ds, ge = pl.pallas_call(
        _rwkv_pre_kernel,
        grid=(T // pre_rows,),
        in_specs=[row] * 6,
        out_specs=[row, row, mat, mat, vec],
        out_shape=[jax.ShapeDtypeStruct((T, W), bf16), jax.ShapeDtypeStruct((T, W), f32),
                   jax.ShapeDtypeStruct((nc, npair, PW, PW), bf16), jax.ShapeDtypeStruct((nc, npair, PW, PW), f32),
                   jax.ShapeDtypeStruct((nc, 1, W), f32)],
        compiler_params=_cp(("parallel",)),
        name="rwkv_pre",
    )(r, ld, k, v, kk, b)
    row, mat, vec = specs(seq_rows)
    o = pl.pallas_call(
        _rwkv_seq_kernel,
        grid=(T // seq_rows,),
        in_specs=[row, row, mat, mat, vec],
        out_specs=row,
        out_shape=jax.ShapeDtypeStruct((T, W), f32),
        scratch_shapes=[pltpu.VMEM((npair, PW, PW), f32)],
        compiler_params=_cp(("arbitrary",)),
        name="rwkv_seq",
    )(rh, oh, nt, ds, ge)
    tm = min(512, T)
    blk = pl.BlockSpec((tm, W), lambda i: (i, 0))
    one = pl.BlockSpec((1, W), lambda i: (0, 0))
    return pl.pallas_call(
        _rwkv_post_kernel,
        grid=(T // tm,),
        in_specs=[blk, blk, blk, one, one],
        out_specs=blk,
        out_shape=jax.ShapeDtypeStruct((T, W), bf16),
        compiler_params=_cp(("parallel",)),
        name="rwkv_post",
    )(o, g, bonus, ln_g.reshape(1, W), ln_b.reshape(1, W))


def _s5_dot(a, b):
    return jnp.dot(a.astype(bf16), b.astype(bf16), preferred_element_type=f32)


def _transpose_pieces(w):
    w = list(w)
    lane = lax.broadcasted_iota(jnp.int32, w[0].shape, 1)
    for d in (4, 2, 1):
        width = S5_GROUP * d
        low = (lane & width) == 0
        for i in range(8):
            if i & d == 0:
                a, b = w[i], w[i + d]
                w[i] = jnp.where(low, a, pltpu.roll(b, width, 1))
                w[i + d] = jnp.where(low, pltpu.roll(a, 128 - width, 1), b)
    return w


def _s5_split_kernel(*refs):
    z_refs, u_ref = refs[:-1], refs[-1]
    ncb = u_ref.shape[1]
    for j, z_ref in enumerate(z_refs):
        for b in range(S5_CHUNK // 8):
            w = [z_ref[pl.ds(8 * b + i, ncb, stride=S5_CHUNK), :] for i in range(8)]
            o = _transpose_pieces(w)
            for a in range(8):
                u_ref[8 * j + a, :, 128 * b:128 * (b + 1)] = o[a]


def _toeplitz(krow):
    P, LP = krow.shape
    lane = lax.broadcasted_iota(jnp.int32, (P, LP), 1)
    blocks = [krow]
    for s in range(1, LP // P):
        blocks.append(jnp.where(lane >= P * s, pltpu.roll(krow, P * s, 1), 0.0))
    return jnp.concatenate(blocks, axis=0)


def _s5_in_kernel(u_ref, krow_ref, br_ref, bi_ref, y_ref, xr_ref, xi_ref):
    for j in range(2):
        y_ref[j] = _s5_dot(u_ref[j], _toeplitz(krow_ref[j]))
    u2 = jnp.concatenate([u_ref[0], u_ref[1]], axis=1)
    xr_ref[...] = _s5_dot(u2, br_ref[0])
    xi_ref[...] = _s5_dot(u2, bi_ref[0])


def _s5_scan_kernel(xr_ref, xi_ref, ar_ref, ai_ref, sr_ref, si_ref):
    nc = xr_ref.shape[0]
    ar = ar_ref[...]
    ai = ai_ref[...]

    def body(c, carry):
        sr, si = carry
        sr_ref[pl.ds(c, 1), :] = sr
        si_ref[pl.ds(c, 1), :] = si
        xr = xr_ref[pl.ds(c, 1), :]
        xi = xi_ref[pl.ds(c, 1), :]
        return ar * sr - ai * si + xr, ar * si + ai * sr + xi

    z = jnp.zeros_like(ar)
    lax.fori_loop(0, nc, body, (z, z))


def _s5_out_kernel(y_ref, sr_ref, si_ref, cr_ref, ci_ref, u_ref, d_ref, o_ref):
    sr = sr_ref[...]
    si = si_ref[...]
    for j in range(2):
        y = y_ref[j] + _s5_dot(sr, cr_ref[j]) + _s5_dot(si, ci_ref[j]) + d_ref[j] * u_ref[j]
        o_ref[j] = 0.5 * y * (1.0 + jnp.tanh(0.7978845608028654 * (y + 0.044715 * (y * y * y))))


def _glu_kernel(y_ref, w_ref, b_ref, o_ref, *nat_refs):
    ncb = y_ref.shape[1]
    for j, nat_ref in enumerate(nat_refs):
        for b in range(S5_CHUNK // 8):
            o = [y_ref[8 * j + a, :, 128 * b:128 * (b + 1)] for a in range(8)]
            w = _transpose_pieces(o)
            for i in range(8):
                nat_ref[pl.ds(8 * b + i, ncb, stride=S5_CHUNK), :] = w[i]
    y = jnp.concatenate([r[...] for r in nat_refs], axis=1)
    o_ref[...] = (y * _sigmoid(_bmm(y, w_ref[...]) + b_ref[...])).astype(o_ref.dtype)


def _s5_tables(a_re, a_im, log_dt, b_re, b_im, c_re, c_im, d):
    G, N, P, L = S5_GROUPS, S5_STATE, S5_GROUP, S5_CHUNK
    D = a_re.shape[0]
    dt = jnp.exp(log_dt)[:, None, :, None]
    lags = jnp.arange(L + 1, dtype=f32)[None, :, None, None]
    mag = jnp.exp(a_re[:, None] * dt * lags)
    ang = a_im[:, None] * dt * lags
    pw_re, pw_im = mag * jnp.cos(ang), mag * jnp.sin(ang)
    den = a_re * a_re + a_im * a_im
    nr, ni = pw_re[:, 1] - 1.0, pw_im[:, 1]
    coef_re = ((nr * a_re + ni * a_im) / den)[..., None]
    coef_im = ((ni * a_re - nr * a_im) / den)[..., None]
    bb_re = coef_re * b_re - coef_im * b_im
    bb_im = coef_re * b_im + coef_im * b_re
    rep = lambda a, axis: jnp.repeat(a, P, axis=axis)
    pl_re, pl_im = rep(pw_re[:, :L].transpose(0, 2, 1, 3), 2), rep(pw_im[:, :L].transpose(0, 2, 1, 3), 2)
    ct_re, ct_im = jnp.tile(c_re, (1, 1, L, 1)), jnp.tile(c_im, (1, 1, L, 1))
    cb_re, cb_im = ct_re * pl_re - ct_im * pl_im, ct_re * pl_im + ct_im * pl_re
    krow = (jnp.einsum('dgxn,dgnq->dgqx', cb_re, bb_re, precision=HI)
            - jnp.einsum('dgxn,dgnq->dgqx', cb_im, bb_im, precision=HI))
    e_re = rep(pw_re[:, L - 1::-1][:, :L].transpose(0, 2, 1, 3), 2)
    e_im = rep(pw_im[:, L - 1::-1][:, :L].transpose(0, 2, 1, 3), 2)
    bq_re = jnp.tile(bb_re.transpose(0, 1, 3, 2), (1, 1, L, 1))
    bq_im = jnp.tile(bb_im.transpose(0, 1, 3, 2), (1, 1, L, 1))
    bs_re = (e_re * bq_re - e_im * bq_im).reshape(D, G // 2, 2, L * P, N)
    bs_im = (e_re * bq_im + e_im * bq_re).reshape(D, G // 2, 2, L * P, N)

    def pair_in(bs):
        zz = jnp.zeros_like(bs[:, :, 0])
        return jnp.concatenate([jnp.concatenate([bs[:, :, 0], zz], axis=3), jnp.concatenate([zz, bs[:, :, 1]], axis=3)], axis=2)

    f_re = rep(pw_re[:, 1:].transpose(0, 2, 3, 1), 3)
    f_im = rep(pw_im[:, 1:].transpose(0, 2, 3, 1), 3)
    cn_re = jnp.tile(c_re.transpose(0, 1, 3, 2), (1, 1, 1, L))
    cn_im = jnp.tile(c_im.transpose(0, 1, 3, 2), (1, 1, 1, L))
    cs_re = (cn_re * f_re - cn_im * f_im).reshape(D, G // 2, 2, N, L * P)
    cs_im = (-(cn_re * f_im + cn_im * f_re)).reshape(D, G // 2, 2, N, L * P)

    def pair_out(cs):
        zz = jnp.zeros_like(cs[:, :, 0])
        top = jnp.concatenate([cs[:, :, 0], zz], axis=2)[:, :, None]
        bot = jnp.concatenate([zz, cs[:, :, 1]], axis=2)[:, :, None]
        return jnp.concatenate([top, bot], axis=2).reshape(D, G, 2 * N, L * P)

    dd = jnp.tile(d.reshape(D, G, 1, P), (1, 1, 1, L))
    return dict(krow=krow, bsr=pair_in(bs_re), bsi=pair_in(bs_im), cr=pair_out(cs_re), ci=pair_out(cs_im),
                al_re=pw_re[:, L].reshape(D, 1, G * N), al_im=pw_im[:, L].reshape(D, 1, G * N), d=dd)


def _s5(z_tail, tab, layer, glu_w, glu_b):
    T = z_tail.shape[0]
    G, N, P, L = S5_GROUPS, S5_STATE, S5_GROUP, S5_CHUNK
    nc, LP = T // L, L * P
    tm = min(1024, T)
    u = pl.pallas_call(
        _s5_split_kernel,
        grid=(T // tm,),
        in_specs=[pl.BlockSpec((tm, 128), lambda i, j=j: (i, j)) for j in range(S5_W // 128)],
        out_specs=pl.BlockSpec((G, tm // L, LP), lambda i: (0, i, 0)),
        out_shape=jax.ShapeDtypeStruct((G, nc, LP), f32),
        compiler_params=_cp(("parallel",)),
        name="s5_split",
    )(*[z_tail] * (S5_W // 128))
    grp = lambda s: pl.BlockSpec((2,) + s, lambda j: (j, 0, 0))
    tgrp = lambda s: pl.BlockSpec((None, 2) + s, lambda j: (layer, j, 0, 0))
    tone = lambda s: pl.BlockSpec((None, 1) + s, lambda j: (layer, j, 0, 0))
    lanes = pl.BlockSpec((nc, 2 * N), lambda j: (0, j))
    y_in, xr, xi = pl.pallas_call(
        _s5_in_kernel,
        grid=(G // 2,),
        in_specs=[grp((nc, LP)), tgrp((P, LP)), tone((2 * LP, 2 * N)), tone((2 * LP, 2 * N))],
        out_specs=[grp((nc, LP)), lanes, lanes],
        out_shape=[jax.ShapeDtypeStruct((G, nc, LP), f32), jax.ShapeDtypeStruct((nc, G * N), f32),
                   jax.ShapeDtypeStruct((nc, G * N), f32)],
        compiler_params=_cp(("parallel",)),
        name="s5_in",
    )(u, tab["krow"], tab["bsr"], tab["bsi"])
    whole = lambda s: pl.BlockSpec(s, lambda i: (0,) * len(s))
    sr, si = pl.pallas_call(
        _s5_scan_kernel,
        grid=(1,),
        in_specs=[whole((nc, G * N)), whole((nc, G * N)), pl.BlockSpec((None, 1, G * N), lambda i: (layer, 0, 0)),
                  pl.BlockSpec((None, 1, G * N), lambda i: (layer, 0, 0))],
        out_specs=[whole((nc, G * N)), whole((nc, G * N))],
        out_shape=[jax.ShapeDtypeStruct((nc, G * N), f32)] * 2,
        compiler_params=_cp(("arbitrary",)),
        name="s5_scan",
    )(xr, xi, tab["al_re"], tab["al_im"])
    y = pl.pallas_call(
        _s5_out_kernel,
        grid=(G // 2,),
        in_specs=[grp((nc, LP)), lanes, lanes, tgrp((2 * N, LP)), tgrp((2 * N, LP)), grp((nc, LP)), tgrp((1, LP))],
        out_specs=grp((nc, LP)),
        out_shape=jax.ShapeDtypeStruct((G, nc, LP), f32),
        compiler_params=_cp(("parallel",)),
        name="s5_out",
    )(y_in, sr, si, tab["cr"], tab["ci"], u, tab["d"])
    return pl.pallas_call(
        _glu_kernel,
        grid=(T // tm,),
        in_specs=[pl.BlockSpec((G, tm // L, LP), lambda i: (0, i, 0)), pl.BlockSpec((S5_W, S5_W), lambda i: (0, 0)),
                  pl.BlockSpec((1, S5_W), lambda i: (0, 0))],
        out_specs=pl.BlockSpec((tm, S5_W), lambda i: (i, 0)),
        out_shape=jax.ShapeDtypeStruct((T, S5_W), bf16),
        scratch_shapes=[pltpu.VMEM((tm, 128), f32)] * (S5_W // 128),
        compiler_params=_cp(("parallel",)),
        name="s5_glu",
    )(y, glu_w, glu_b.reshape(1, S5_W))


def _pad_rows(w, n):
    return jnp.pad(w, ((0, n - w.shape[0]), (0, 0)))


def _split_mu(mu):
    l0 = 3 * RWKV_W
    return jnp.concatenate([
        mu[:l0], jnp.pad(mu[l0:l0 + DECAY_LORA], (0, LORA_PAD - DECAY_LORA)),
        jnp.pad(mu[l0 + DECAY_LORA:l0 + DECAY_LORA + AAA_LORA], (0, LORA_PAD - AAA_LORA)),
        mu[l0 + DECAY_LORA + AAA_LORA:]]).reshape(1, RWKV_PAD)


def kernel(x, p, norm_mix_g, w_in, mlstm_conv, mlstm_ib, mlstm_fb, mlstm_norm_g, rwkv_mu, rwkv_w0, rwkv_w2, rwkv_a0, rwkv_a2, rwkv_g2, rwkv_kk, rwkv_ka, rwkv_rk, rwkv_ln_g, rwkv_ln_b, s5_a_re, s5_a_im, s5_log_dt, s5_b_re, s5_b_im, s5_c_re, s5_c_im, s5_d, s5_glu_w, s5_glu_b, w_up_m, w_up_r, w_up_s, w_out, norm_ffn_g, ffn_w_gate, ffn_w_up, ffn_w_down, norm_ple_g, ple_w_gate, ple_w_proj, final_norm_g):
    B, T, D = x.shape
    depth = w_in.shape[0]
    wt = jnp.swapaxes(w_in, 1, 2)
    s5_tab = _s5_tables(s5_a_re, s5_a_im, s5_log_dt, s5_b_re, s5_b_im, s5_c_re, s5_c_im, s5_d)
    outs = []
    for bi in range(B):
        h = x[bi]
        xn = _rmsnorm(h, norm_mix_g[0], bf16)
        for i in range(depth):
            z_main = _proj(xn, wt, i, 0, 4, MLSTM_W)
            z_gate = _proj(xn, wt, i, M_IN + RWKV_IN + S5_W, 3 * D_MODEL // 1024, 1024, bf16, gate=True)
            z_tail = _proj_tail(xn, wt, i)

            gate_b = jnp.pad(jnp.concatenate([mlstm_ib[i], mlstm_fb[i]]), (0, 128 - 2 * MLSTM_HEADS)).reshape(1, 128)
            y_m = _mlstm(z_main, z_tail, mlstm_conv[i], gate_b, mlstm_norm_g[i])

            vec = lambda a: a.reshape(1, RWKV_W)
            parts = _rwkv_prep(xn, _rwkv_w(wt, i), _split_mu(rwkv_mu[i]), vec(rwkv_w0[i]), vec(rwkv_a0[i]), vec(rwkv_kk[i]),
                               vec(rwkv_ka[i]), vec(rwkv_rk[i]), _pad_rows(rwkv_w2[i], LORA_PAD).astype(bf16),
                               _pad_rows(rwkv_a2[i], LORA_PAD).astype(bf16), rwkv_g2[i].astype(bf16))
            y_r = _rwkv_scan(parts, rwkv_ln_g[i], rwkv_ln_b[i])

            y_s = _s5(z_tail, s5_tab, i, s5_glu_w[i].astype(bf16), s5_glu_b[i])

            mixed = _merge(y_m, y_r, y_s, z_gate, w_up_m, w_up_r, w_up_s, i)
            h, hn = _resid_norm(mixed, w_out[i].astype(bf16), h, norm_ffn_g[i], D_MODEL)
            act = _ffn_up(hn, ffn_w_gate, ffn_w_up, i)
            h = _resid_mm(act, ffn_w_down, h, i)
            last = i == depth - 1
            g_next = final_norm_g if last else norm_mix_g[i + 1]
            h, xn = _ple(h, p[i, bi].astype(bf16), ple_w_gate[i].astype(bf16), ple_w_proj[i].astype(bf16), norm_ple_g[i],
                         g_next, f32 if last else bf16)
        outs.append(xn)
    return jnp.stack(outs)
```

```python
import functools

import jax
import jax.numpy as jnp
import numpy as np
from jax import lax
from jax.experimental import pallas as pl
from jax.experimental.pallas import tpu as pltpu

f32 = jnp.float32
bf16 = jnp.bfloat16
HI = lax.Precision.HIGHEST

D_MODEL = 2048
PLE_DIM = 256
RMS_EPS = 1e-6
MLSTM_HEADS = 4
MLSTM_HEAD_DIM = 256
MLSTM_W = 1024
MLSTM_CHUNK = 256
GATE_SOFTCAP = 15.0
RWKV_HEADS = 8
RWKV_HEAD_DIM = 64
RWKV_W = 512
DECAY_LORA = 96
AAA_LORA = 96
GATE_LORA = 256
LORA_PAD = 128
RWKV_GN_EPS = 64e-5
RWKV_CHUNK = 64
S5_GROUP = 16
S5_GROUPS = 32
S5_W = 512
S5_STATE = 64
S5_CHUNK = 16
FFN_HIDDEN = 5632
M_IN = 4 * MLSTM_W + 2 * MLSTM_HEADS
RWKV_IN = 3 * RWKV_W + DECAY_LORA + AAA_LORA + GATE_LORA
RWKV_PAD = 3 * RWKV_W + 2 * LORA_PAD + GATE_LORA
Z_TAIL = S5_W + 128
VMEM_LIMIT = 56 * 1024 * 1024


def _cp(sem):
    return pltpu.CompilerParams(dimension_semantics=sem, vmem_limit_bytes=VMEM_LIMIT)


def _nt(a, b, **kw):
    return lax.dot_general(a, b, (((1,), (1,)), ((), ())), preferred_element_type=f32, **kw)


def _tn(a, b, **kw):
    return lax.dot_general(a, b, (((0,), (0,)), ((), ())), preferred_element_type=f32, **kw)


def _mm(a, b, **kw):
    return jnp.dot(a, b, preferred_element_type=f32, **kw)


def _bmm(a, b):
    return jnp.dot(a.astype(bf16), b.astype(bf16), preferred_element_type=f32)


def _sigmoid(x):
    return 1.0 / (1.0 + jnp.exp(-x))


def _rms(x, g):
    return x * lax.rsqrt(jnp.mean(x * x, axis=-1, keepdims=True) + RMS_EPS) * g


def _norm_kernel(x_ref, g_ref, o_ref):
    o_ref[...] = _rms(x_ref[...], g_ref[...]).astype(o_ref.dtype)


def _rmsnorm(x, g, out_dtype):
    T, D = x.shape
    tm = min(512, T)
    return pl.pallas_call(
        _norm_kernel,
        grid=(T // tm,),
        in_specs=[pl.BlockSpec((tm, D), lambda i: (i, 0)), pl.BlockSpec((1, D), lambda i: (0, 0))],
        out_specs=pl.BlockSpec((tm, D), lambda i: (i, 0)),
        out_shape=jax.ShapeDtypeStruct((T, D), out_dtype),
        compiler_params=_cp(("parallel",)),
        name="rmsnorm",
    )(x, g.reshape(1, D))


def _load_wt(w, rows):
    parts = [w[r[0]:r[1], :] if isinstance(r, tuple) else jnp.zeros((r, w.shape[1]), f32) for r in rows]
    w = parts[0] if len(parts) == 1 else jnp.concatenate(parts, axis=0)
    return w.T.astype(bf16)


def _proj_kernel(a_ref, w_ref, o_ref, wb_ref, *, gate):
    @pl.when(pl.program_id(1) == 0)
    def _():
        wb_ref[...] = _load_wt(w_ref[0], [(0, w_ref.shape[1])])

    z = _mm(a_ref[...], wb_ref[...])
    o_ref[...] = (0.5 * jnp.tanh(0.5 * z) + 0.5 if gate else z).astype(o_ref.dtype)


def _proj(a, wt, layer, row0, n_tiles, tn, out_dtype=f32, gate=False):
    T, K = a.shape
    tm = min(1024, T)
    return pl.pallas_call(
        functools.partial(_proj_kernel, gate=gate),
        grid=(n_tiles, T // tm),
        in_specs=[pl.BlockSpec((tm, K), lambda j, i: (i, 0)),
                  pl.BlockSpec((pl.Element(1), pl.Element(tn), pl.Element(K)),
                               lambda j, i: (layer, pl.multiple_of(row0 + tn * j, 8), 0))],
        out_specs=pl.BlockSpec((tm, tn), lambda j, i: (i, j)),
        out_shape=jax.ShapeDtypeStruct((T, n_tiles * tn), out_dtype),
        scratch_shapes=[pltpu.VMEM((K, tn), bf16)],
        compiler_params=_cp(("parallel", "arbitrary")),
        name="in_proj",
    )(a, wt)


def _proj_tail_kernel(a_ref, ws_ref, wg_ref, o_ref, wb_ref):
    @pl.when(pl.program_id(0) == 0)
    def _():
        pad = jnp.zeros((Z_TAIL - S5_W - wg_ref.shape[0], wg_ref.shape[1]), f32)
        wb_ref[...] = jnp.concatenate([ws_ref[0], wg_ref[...], pad], axis=0).T.astype(bf16)

    o_ref[...] = _mm(a_ref[...], wb_ref[...])


def _proj_tail(a, wt, layer):
    T, K = a.shape
    tm = min(1024, T)
    n_if = 2 * MLSTM_HEADS
    return pl.pallas_call(
        _proj_tail_kernel,
        grid=(T // tm,),
        in_specs=[pl.BlockSpec((tm, K), lambda i: (i, 0)),
                  pl.BlockSpec((pl.Element(1), pl.Element(S5_W), pl.Element(K)), lambda i: (layer, M_IN + RWKV_IN, 0)),
                  pl.BlockSpec((None, n_if, K), lambda i: (layer, 4 * MLSTM_W // n_if, 0))],
        out_specs=pl.BlockSpec((tm, Z_TAIL), lambda i: (i, 0)),
        out_shape=jax.ShapeDtypeStruct((T, Z_TAIL), f32),
        scratch_shapes=[pltpu.VMEM((K, Z_TAIL), bf16)],
        compiler_params=_cp(("arbitrary",)),
        name="in_proj_tail",
    )(a, wt, wt)


def _merge_kernel(ym_ref, yr_ref, ys_ref, gm_ref, gr_ref, gs_ref, um_ref, ur_ref, us_ref, o_ref, umb_ref, urb_ref, usb_ref):
    @pl.when(pl.program_id(1) == 0)
    def _():
        umb_ref[...] = um_ref[...].astype(bf16)
        urb_ref[...] = ur_ref[...].astype(bf16)
        usb_ref[...] = us_ref[...].astype(bf16)

    acc = gm_ref[...] * _mm(ym_ref[...], umb_ref[...])
    acc += gr_ref[...] * _mm(yr_ref[...], urb_ref[...])
    acc += gs_ref[...] * _mm(ys_ref[...], usb_ref[...])
    o_ref[...] = acc.astype(o_ref.dtype)


def _merge(ym, yr, ys, gates, um, ur, us, layer):
    T = ym.shape[0]
    tm, tn = min(1024, T), 1024
    nj = D_MODEL // tn

    def gate_spec(b):
        return pl.BlockSpec((tm, tn), lambda j, i, b=b: (i, b * nj + j))

    def y_spec(w):
        return pl.BlockSpec((tm, w), lambda j, i: (i, 0))

    def u_spec(w):
        return pl.BlockSpec((None, w, tn), lambda j, i: (layer, 0, j))

    return pl.pallas_call(
        _merge_kernel,
        grid=(nj, T // tm),
        in_specs=[y_spec(MLSTM_W), y_spec(RWKV_W), y_spec(S5_W), gate_spec(0), gate_spec(1), gate_spec(2),
                  u_spec(MLSTM_W), u_spec(RWKV_W), u_spec(S5_W)],
        out_specs=pl.BlockSpec((tm, tn), lambda j, i: (i, j)),
        out_shape=jax.ShapeDtypeStruct((T, D_MODEL), bf16),
        scratch_shapes=[pltpu.VMEM((MLSTM_W, tn), bf16), pltpu.VMEM((RWKV_W, tn), bf16), pltpu.VMEM((S5_W, tn), bf16)],
        compiler_params=_cp(("parallel", "arbitrary")),
        name="gated_merge",
    )(ym, yr, ys, gates, gates, gates, um, ur, us)


def _resid_norm_kernel(a_ref, w_ref, h_ref, g_ref, ho_ref, no_ref):
    hn = h_ref[...] + _mm(a_ref[...], w_ref[...])
    ho_ref[...] = hn
    no_ref[...] = _rms(hn, g_ref[...]).astype(no_ref.dtype)


def _resid_norm(a, w, layer, h, g, norm_dtype=bf16):
    T, K = a.shape
    tm = min(512, T)
    row = lambda n: pl.BlockSpec((tm, n), lambda i: (i, 0))
    return pl.pallas_call(
        _resid_norm_kernel,
        grid=(T // tm,),
        in_specs=[row(K), pl.BlockSpec((None, K, D_MODEL), lambda i: (layer, 0, 0)), row(D_MODEL),
                  pl.BlockSpec((1, D_MODEL), lambda i: (0, 0))],
        out_specs=[row(D_MODEL), row(D_MODEL)],
        out_shape=[jax.ShapeDtypeStruct((T, D_MODEL), f32), jax.ShapeDtypeStruct((T, D_MODEL), norm_dtype)],
        compiler_params=_cp(("parallel",)),
        name="resid_matmul_norm",
    )(a, w, h, g.reshape(1, D_MODEL))


def _ffn_up_kernel(a_ref, wg_ref, wu_ref, o_ref, wgb_ref, wub_ref):
    @pl.when(pl.program_id(1) == 0)
    def _():
        wgb_ref[...] = wg_ref[...].astype(bf16)
        wub_ref[...] = wu_ref[...].astype(bf16)

    a = a_ref[...]
    gt = _mm(a, wgb_ref[...])
    up = _mm(a, wub_ref[...])
    o_ref[...] = (gt * _sigmoid(gt) * up).astype(o_ref.dtype)


def _ffn_up(a, wg, wu, layer):
    T, K = a.shape
    N = wg.shape[2]
    tm, tn = min(1024, T), 512
    return pl.pallas_call(
        _ffn_up_kernel,
        grid=(N // tn, T // tm),
        in_specs=[pl.BlockSpec((tm, K), lambda j, i: (i, 0)), pl.BlockSpec((None, K, tn), lambda j, i: (layer, 0, j)),
                  pl.BlockSpec((None, K, tn), lambda j, i: (layer, 0, j))],
        out_specs=pl.BlockSpec((tm, tn), lambda j, i: (i, j)),
        out_shape=jax.ShapeDtypeStruct((T, N), bf16),
        scratch_shapes=[pltpu.VMEM((K, tn), bf16), pltpu.VMEM((K, tn), bf16)],
        compiler_params=_cp(("parallel", "arbitrary")),
        name="ffn_up",
    )(a, wg, wu)


def _resid_mm_kernel(a_ref, w_ref, h_ref, o_ref, wb_ref):
    @pl.when(pl.program_id(1) == 0)
    def _():
        wb_ref[...] = w_ref[...].astype(bf16)

    o_ref[...] = h_ref[...] + _mm(a_ref[...], wb_ref[...])


def _resid_mm(a, w, h, layer):
    T, K = a.shape
    N = w.shape[2]
    tm, tn = min(512, T), 512
    return pl.pallas_call(
        _resid_mm_kernel,
        grid=(N // tn, T // tm),
        in_specs=[pl.BlockSpec((tm, K), lambda j, i: (i, 0)), pl.BlockSpec((None, K, tn), lambda j, i: (layer, 0, j)),
                  pl.BlockSpec((tm, tn), lambda j, i: (i, j))],
        out_specs=pl.BlockSpec((tm, tn), lambda j, i: (i, j)),
        out_shape=jax.ShapeDtypeStruct((T, N), f32),
        scratch_shapes=[pltpu.VMEM((K, tn), bf16)],
        compiler_params=_cp(("parallel", "arbitrary")),
        name="resid_matmul",
    )(a, w, h)


def _ple_kernel(h_ref, p_ref, wg_ref, wp_ref, gp_ref, g_ref, ho_ref, no_ref):
    tm = h_ref.shape[0]
    parts = [slice(0, tm // 2), slice(tm // 2, tm)]
    hs = [h_ref[r, :] for r in parts]
    hps = [_rms(h, gp_ref[...]).astype(bf16) for h in hs]
    gates = [_sigmoid(_mm(hp, wg_ref[...])) for hp in hps]
    projs = [_mm(p_ref[r, :], wp_ref[...]) for r in parts]
    for r, h, gate, proj in zip(parts, hs, gates, projs):
        hn = h + proj * gate
        ho_ref[r, :] = hn
        no_ref[r, :] = _rms(hn, g_ref[...]).astype(no_ref.dtype)


def _ple(h, p, wg, wp, layer, g_ple, g, norm_dtype):
    T = h.shape[0]
    tm = min(512, T)
    row = lambda w: pl.BlockSpec((tm, w), lambda i: (i, 0))
    full = lambda s: pl.BlockSpec(s, lambda i: (0, 0))
    return pl.pallas_call(
        _ple_kernel,
        grid=(T // tm,),
        in_specs=[row(D_MODEL), pl.BlockSpec((None, None, tm, PLE_DIM), lambda i: (layer, 0, i, 0)),
                  pl.BlockSpec((None, D_MODEL, D_MODEL), lambda i: (layer, 0, 0)),
                  pl.BlockSpec((None, PLE_DIM, D_MODEL), lambda i: (layer, 0, 0)), full((1, D_MODEL)), full((1, D_MODEL))],
        out_specs=[row(D_MODEL), row(D_MODEL)],
        out_shape=[jax.ShapeDtypeStruct((T, D_MODEL), f32), jax.ShapeDtypeStruct((T, D_MODEL), norm_dtype)],
        compiler_params=_cp(("parallel",)),
        name="ple_norm",
    )(h, p, wg, wp, g_ple.reshape(1, D_MODEL), g.reshape(1, D_MODEL))


def _mlstm_kernel(zq_ref, zk_ref, v_ref, o_ref, gz_ref, cq_ref, ck_ref, gb_ref, ng_ref, y_ref,
                  ct_ref, n_ref, qbuf_ref, kbuf_ref):
    c = pl.program_id(0)
    L, W = zq_ref.shape
    H, dh = MLSTM_HEADS, MLSTM_HEAD_DIM
    heads = range(H)

    @pl.when(c == 0)
    def _():
        ct_ref[...] = jnp.zeros_like(ct_ref)
        n_ref[...] = jnp.zeros_like(n_ref)
        qbuf_ref[...] = jnp.zeros_like(qbuf_ref)
        kbuf_ref[...] = jnp.zeros_like(kbuf_ref)

    def conv_silu(z_ref, w_ref, buf_ref):
        x = z_ref[...]
        xx = jnp.concatenate([buf_ref[...], x], axis=0)
        w = w_ref[...]
        acc = x * w[0:1, :]
        for j in range(1, w.shape[0]):
            acc = acc + pltpu.roll(xx, j, 0)[8:, :] * w[j:j + 1, :]
        buf_ref[...] = x[L - 8:, :]
        return acc * _sigmoid(acc)

    q = conv_silu(zq_ref, cq_ref, qbuf_ref) * (dh ** -0.5)
    k = conv_silu(zk_ref, ck_ref, kbuf_ref)

    g = gz_ref[...] + gb_ref[...]
    sc = GATE_SOFTCAP * jnp.tanh(g / GATE_SOFTCAP)
    logf = -jnp.log(1.0 + jnp.exp(-sc))
    row = lax.broadcasted_iota(jnp.int32, (L, L), 0)
    col = lax.broadcasted_iota(jnp.int32, (L, L), 1)
    causal = row >= col
    bcs = _mm(causal.astype(f32), logf, precision=HI)
    b128 = pltpu.roll(bcs, 128 - H, 1)
    e128 = sc - b128
    e_t = e128.T

    sl = [slice(h * dh, (h + 1) * dh) for h in heads]
    b_col = [b128[:, h:h + 1] for h in heads]
    e_col = [e128[:, h:h + 1] for h in heads]
    wmat = [jnp.where(causal, jnp.exp(jnp.where(causal, b_col[h] + e_t[h:h + 1, :], 0.0)), 0.0) for h in heads]
    b_end = [b_col[h][L - 1:L, :] for h in heads]
    eb = [jnp.exp(b_col[h]) for h in heads]
    qh = [q[:, sl[h]] for h in heads]
    kh = [k[:, sl[h]] for h in heads]
    qb = [x.astype(bf16) for x in qh]
    kb = [x.astype(bf16) for x in kh]
    vb = [v_ref[:, sl[h]].astype(bf16) for h in heads]
    ct = [ct_ref[h] for h in heads]
    nn = [n_ref[:, sl[h]] for h in heads]
    s = [_nt(qb[h], kb[h]) * wmat[h] for h in heads]
    inter = [_mm(qb[h], ct[h].astype(bf16)) for h in heads]
    num = [_mm(s[h].astype(bf16), vb[h]) + eb[h] * inter[h] for h in heads]
    ng = ng_ref[...]
    for h in heads:
        den = jnp.sum(s[h], axis=1, keepdims=True) + eb[h] * jnp.sum(qh[h] * nn[h], axis=1, keepdims=True)
        hh = num[h] / jnp.maximum(jnp.abs(den), 1.0)
        hh = hh * lax.rsqrt(jnp.mean(hh * hh, axis=1, keepdims=True) + RMS_EPS) * ng[:, sl[h]]
        y_ref[:, sl[h]] = (_sigmoid(o_ref[:, sl[h]]) * hh).astype(y_ref.dtype)
    kw = [kh[h] * jnp.exp(b_end[h] + e_col[h]) for h in heads]
    upd = [_tn(kw[h].astype(bf16), vb[h]) for h in heads]
    for h in heads:
        decay = jnp.exp(b_end[h])
        ct_ref[h] = decay * ct[h] + upd[h]
        n_ref[:, sl[h]] = decay * nn[h] + jnp.sum(kw[h], axis=0, keepdims=True)


def _mlstm(z_main, z_tail, conv_w, gate_b, norm_g):
    T = z_main.shape[0]
    L, dh, H, W = min(MLSTM_CHUNK, T), MLSTM_HEAD_DIM, MLSTM_HEADS, MLSTM_W
    blk = lambda j: pl.BlockSpec((L, W), lambda c, j=j: (c, j))
    return pl.pallas_call(
        _mlstm_kernel,
        grid=(T // L,),
        in_specs=[blk(0), blk(1), blk(2), blk(3),
                  pl.BlockSpec((L, 128), lambda c: (c, S5_W // 128)),
                  pl.BlockSpec((4, W), lambda c: (0, 0)), pl.BlockSpec((4, W), lambda c: (0, 1)),
                  pl.BlockSpec((1, 128), lambda c: (0, 0)), pl.BlockSpec((1, W), lambda c: (0, 0))],
        out_specs=pl.BlockSpec((L, W), lambda c: (c, 0)),
        out_shape=jax.ShapeDtypeStruct((T, W), bf16),
        scratch_shapes=[pltpu.VMEM((H, dh, dh), f32), pltpu.VMEM((1, W), f32),
                        pltpu.VMEM((8, W), f32), pltpu.VMEM((8, W), f32)],
        compiler_params=_cp(("arbitrary",)),
        name="mlstm",
    )(z_main, z_main, z_main, z_main, z_tail, conv_w, conv_w, gate_b, norm_g.reshape(1, W))


def _iota_div(shape, dim, width):
    return lax.shift_right_logical(lax.broadcasted_iota(jnp.int32, shape, dim), int(np.log2(width)))


def _head_ones(n, width):
    return (_iota_div((n, n), 0, width) == _iota_div((n, n), 1, width)).astype(f32)


def _head_sum(x):
    blk = _head_ones(RWKV_W, RWKV_HEAD_DIM).astype(bf16)
    hi = x.astype(bf16)
    r1 = x - hi.astype(f32)
    mid = r1.astype(bf16)
    lo = (r1 - mid.astype(f32)).astype(bf16)
    return _mm(hi, blk) + _mm(mid, blk) + _mm(lo, blk)


def _rwkv_w_kernel(w_ref, o_ref, *, last_rows):
    j = pl.program_id(0)
    nj = pl.num_programs(0)

    @pl.when(j < nj - 1)
    def _():
        o_ref[...] = _load_wt(w_ref[0], [(0, w_ref.shape[1])])

    @pl.when(j == nj - 1)
    def _():
        o_ref[...] = _load_wt(w_ref[0], last_rows)


def _rwkv_w(wt, layer):
    K = wt.shape[2]
    tn = RWKV_W
    lora_rows = [(0, DECAY_LORA), LORA_PAD - DECAY_LORA, (DECAY_LORA, DECAY_LORA + AAA_LORA), LORA_PAD - AAA_LORA,
                 (DECAY_LORA + AAA_LORA, RWKV_IN - 3 * RWKV_W)]
    return pl.pallas_call(
        functools.partial(_rwkv_w_kernel, last_rows=lora_rows),
        grid=(RWKV_PAD // tn,),
        in_specs=[pl.BlockSpec((pl.Element(1), pl.Element(tn), pl.Element(K)),
                               lambda j: (layer, pl.multiple_of(M_IN + tn * j, 8), 0))],
        out_specs=pl.BlockSpec((K, tn), lambda j: (0, j)),
        out_shape=jax.ShapeDtypeStruct((K, RWKV_PAD), bf16),
        compiler_params=_cp(("parallel",)),
        name="rwkv_w",
    )(wt)


def _rwkv_prep_kernel(a_ref, w_ref, mu_ref, w0_ref, a0_ref, kkg_ref, ka_ref, rk_ref, w2_ref, a2_ref, g2_ref,
                      r_ref, ld_ref, k_ref, v_ref, kk_ref, b_ref, g_ref, bonus_ref, *buf_refs):
    i = pl.program_id(0)
    nh = len(buf_refs)
    tm = a_ref.shape[0] // nh
    W = RWKV_W

    @pl.when(i == 0)
    def _():
        buf_refs[0][0:8, :] = jnp.zeros((8, buf_refs[0].shape[1]), f32)

    rows = [slice(h * tm, (h + 1) * tm) for h in range(nh)]
    xs_all = [_mm(a_ref[rw, :], w_ref[...]) for rw in rows]
    for h, (rw, x) in enumerate(zip(rows, xs_all)):
        buf_ref = buf_refs[h]
        buf_ref[8:, :] = x
        buf_refs[(h + 1) % nh][0:8, :] = x[tm - 8:, :]
        xs = x + (buf_ref[pl.ds(7, tm), :] - x) * mu_ref[...]
        r = xs[:, 0:W]
        k = xs[:, W:2 * W]
        v = xs[:, 2 * W:3 * W]
        wl = xs[:, 3 * W:3 * W + LORA_PAD]
        al = xs[:, 3 * W + LORA_PAD:3 * W + 2 * LORA_PAD]
        gl = xs[:, 3 * W + 2 * LORA_PAD:]
        t = w0_ref[...] + _bmm(jnp.tanh(wl), w2_ref[...])
        w = -(jnp.maximum(-t, 0.0) + jnp.log(1.0 + jnp.exp(-jnp.abs(t)))) - 0.5
        a = _sigmoid(a0_ref[...] + _bmm(al, a2_ref[...]))
        g = _bmm(_sigmoid(gl), g2_ref[...])
        kk = k * kkg_ref[...]
        ss = _head_sum(kk * kk)
        kk = kk / jnp.maximum(jnp.sqrt(ss), 1e-12)
        k2 = k * (1.0 + (a - 1.0) * ka_ref[...])
        bonus = _head_sum(r * k2 * rk_ref[...]) * v
        r_ref[rw, :] = r
        ld_ref[rw, :] = -jnp.exp(w)
        k_ref[rw, :] = k2
        v_ref[rw, :] = v
        kk_ref[rw, :] = kk
        b_ref[rw, :] = kk * a
        g_ref[rw, :] = g
        bonus_ref[rw, :] = bonus


def _rwkv_prep(xn, w_rwkv, mu, w0, a0, kkg, ka, rk, w2, a2, g2):
    T, K = xn.shape
    tm = min(512, T)
    n_sub = 2
    W = RWKV_W
    vec = lambda n: pl.BlockSpec((1, n), lambda i: (0, 0))
    mat = lambda s: pl.BlockSpec(s, lambda i: (0, 0))
    out = pl.BlockSpec((tm, W), lambda i: (i, 0))
    return pl.pallas_call(
        _rwkv_prep_kernel,
        grid=(T // tm,),
        in_specs=[pl.BlockSpec((tm, K), lambda i: (i, 0)), mat((K, RWKV_PAD)), vec(RWKV_PAD), vec(W), vec(W),
                  vec(W), vec(W), vec(W), mat((LORA_PAD, W)), mat((LORA_PAD, W)), mat((GATE_LORA, W))],
        out_specs=[out] * 8,
        out_shape=[jax.ShapeDtypeStruct((T, W), f32)] * 8,
        scratch_shapes=[pltpu.VMEM((tm // n_sub + 8, RWKV_PAD), f32)] * n_sub,
        compiler_params=_cp(("arbitrary",)),
        name="rwkv_prep",
    )(xn, w_rwkv, mu, w0, a0, kkg, ka, rk, w2, a2, g2)


def _rwkv_pre_kernel(r_ref, ld_ref, k_ref, v_ref, kk_ref, b_ref, rh_ref, oh_ref, nt_ref, ds_ref, ge_ref):
    L, dh = RWKV_CHUNK, RWKV_HEAD_DIM
    GW = 4 * dh
    PW = 2 * dh
    n_chunks = r_ref.shape[0] // L
    rr = lax.broadcasted_iota(jnp.int32, (GW, GW), 0)
    cc = lax.broadcasted_iota(jnp.int32, (GW, GW), 1)
    same = _iota_div((GW, GW), 0, L) == _iota_div((GW, GW), 1, dh)
    strict = same & (rr > cc)
    incl = same & (rr >= cc)
    eye = jnp.where(rr == cc, 1.0, 0.0)
    pair_blk = _head_ones(PW, dh)
    tr = lax.broadcasted_iota(jnp.int32, (L, L), 0)
    tc = lax.broadcasted_iota(jnp.int32, (L, L), 1)
    tril = (tr >= tc).astype(f32)

    def unstack(x):
        acc = x[0:L, :]
        for hh in range(1, GW // L):
            acc = acc + x[hh * L:(hh + 1) * L, :]
        return acc

    def tile(x):
        return jnp.concatenate([x] * (GW // L), axis=0)

    def stack(x):
        return jnp.where(same, tile(x), 0.0)

    tiles = [(ci, gi) for ci in range(n_chunks) for gi in range(RWKV_W // GW)]
    ch = []
    for ci in range(n_chunks):
        rows = slice(ci * L, (ci + 1) * L)
        ld = ld_ref[rows, :]
        lg = _mm(tril, ld, precision=HI)
        g_end = lg[L - 1:L, :]
        ge_ref[ci] = g_end
        dec_out = jnp.exp(-lg)
        dec_tail = jnp.exp(g_end - lg)
        k = k_ref[rows, :]
        b = b_ref[rows, :]
        v = v_ref[rows, :]
        ch.append(dict(rows=rows, v=v, kap=kk_ref[rows, :] * jnp.exp(lg - ld), rt=r_ref[rows, :] * jnp.exp(lg),
                       kt=k * dec_out, bt=b * dec_out, khat=(k * dec_tail).astype(bf16),
                       bhat=(b * dec_tail).astype(bf16), vb=v.astype(bf16)))
    ops = []
    for ci, gi in tiles:
        c = ch[ci]
        lanes = slice(gi * GW, (gi + 1) * GW)
        kap_s = stack(c["kap"][:, lanes])
        rt_s = stack(c["rt"][:, lanes])
        ops.append(dict(lanes=lanes, kap_s=kap_s, rt_s=rt_s, kap_b=kap_s.astype(bf16), rt_b=rt_s.astype(bf16),
                        v_s=stack(c["v"][:, lanes]).astype(bf16), kt_t=tile(c["kt"][:, lanes]).astype(bf16),
                        bt_t=tile(c["bt"][:, lanes]).astype(bf16)))
    for o in ops:
        o["a_kb"] = jnp.where(strict, _nt(o["kap_b"], o["bt_t"]), 0.0)
    for o in ops:
        o["a_kk"] = jnp.where(strict, _nt(o["kap_b"], o["kt_t"]), 0.0).astype(bf16)
        o["a_rk"] = jnp.where(incl, _nt(o["rt_b"], o["kt_t"]), 0.0).astype(bf16)
        o["a_rb"] = jnp.where(incl, _nt(o["rt_b"], o["bt_t"]), 0.0).astype(bf16)
    for o in ops:
        o["p"] = -o["a_kb"]
        o["tinv"] = eye + o["p"]
    step = 2
    while step < L:
        for o in ops:
            pb = o["p"].astype(bf16)
            o["p"] = _mm(pb, pb)
        for o in ops:
            o["tinv"] = o["tinv"] + _bmm(o["tinv"], o["p"])
        step *= 2
    for o in ops:
        o["w1"] = _mm(o["a_kk"], o["v_s"])
        o["w2"] = _mm(o["a_rk"], o["v_s"])
    for o in ops:
        o["ku"] = _bmm(o["tinv"], jnp.concatenate([o["kap_s"], o["w1"]], axis=1))
    for o in ops:
        o["ru"] = _mm(o["a_rb"], o["ku"].astype(bf16))
    for (ci, gi), o in zip(tiles, ops):
        c = ch[ci]
        rows, lanes = c["rows"], o["lanes"]
        rh_ref[rows, lanes] = unstack(o["rt_s"] - o["ru"][:, :GW]).astype(rh_ref.dtype)
        oh_ref[rows, lanes] = unstack(o["w2"] - o["ru"][:, GW:])
        kh = unstack(o["ku"][:, :GW]).astype(bf16)
        uh = unstack(o["ku"][:, GW:]).astype(bf16)
        for pi in range(GW // PW):
            q = gi * (GW // PW) + pi
            loc = slice(pi * PW, (pi + 1) * PW)
            glo = slice(q * PW, (q + 1) * PW)
            nt_ref[ci, q] = (_tn(kh[:, loc], c["bhat"][:, glo]) * pair_blk).astype(nt_ref.dtype)
            ds_ref[ci, q] = (_tn(c["vb"][:, glo], c["khat"][:, glo]) - _tn(uh[:, loc], c["bhat"][:, glo])) * pair_blk


def _rwkv_seq_kernel(rh_ref, oh_ref, nt_ref, ds_ref, ge_ref, o_ref, s_ref):
    L, PW = RWKV_CHUNK, 2 * RWKV_HEAD_DIM
    n_chunks = rh_ref.shape[0] // L

    @pl.when(pl.program_id(0) == 0)
    def _():
        s_ref[...] = jnp.zeros_like(s_ref)

    pairs = range(RWKV_W // PW)
    lanes = [slice(q * PW, (q + 1) * PW) for q in pairs]
    s_mat = [s_ref[q] for q in pairs]
    for ci in range(n_chunks):
        rows = slice(ci * L, (ci + 1) * L)
        s_b = [s.astype(bf16) for s in s_mat]
        decay = jnp.exp(ge_ref[ci])
        for q in pairs:
            s_mat[q] = s_mat[q] * decay[:, lanes[q]] - _mm(s_b[q], nt_ref[ci, q]) + ds_ref[ci, q]
        for q in pairs:
            o_ref[rows, lanes[q]] = _nt(rh_ref[rows, lanes[q]], s_b[q]) + oh_ref[rows, lanes[q]]
    for q in pairs:
        s_ref[q] = s_mat[q]


def _rwkv_post_kernel(o_ref, g_ref, bonus_ref, lng_ref, lnb_ref, y_ref):
    inv = 1.0 / RWKV_HEAD_DIM
    o = o_ref[...]
    d = o - _head_sum(o) * inv
    var = _head_sum(d * d) * inv
    yn = d * lax.rsqrt(var + RWKV_GN_EPS) * lng_ref[...] + lnb_ref[...] + bonus_ref[...]
    y_ref[...] = (yn * g_ref[...]).astype(y_ref.dtype)


def _rwkv_scan(parts, ln_g, ln_b):
    r, ld, k, v, kk, b, g, bonus = parts
    T, W = r.shape
    L, PW = RWKV_CHUNK, 2 * RWKV_HEAD_DIM
    nc, npair = T // L, W // PW
    pre_rows = min(2 * L, T)
    seq_rows = min(4 * L, T)

    def specs(rows):
        n = rows // L
        row = pl.BlockSpec((rows, W), lambda c: (c, 0))
        mat = pl.BlockSpec((n, npair, PW, PW), lambda c: (c, 0, 0, 0))
        vec = pl.BlockSpec((n, 1, W), lambda c: (c, 0, 0))
        return row, mat, vec

    row, mat, vec = specs(pre_rows)
    rh, oh, nt, ds, ge = pl.pallas_call(
        _rwkv_pre_kernel,
        grid=(T // pre_rows,),
        in_specs=[row] * 6,
        out_specs=[row, row, mat, mat, vec],
        out_shape=[jax.ShapeDtypeStruct((T, W), bf16), jax.ShapeDtypeStruct((T, W), f32),
                   jax.ShapeDtypeStruct((nc, npair, PW, PW), bf16), jax.ShapeDtypeStruct((nc, npair, PW, PW), f32),
                   jax.ShapeDtypeStruct((nc, 1, W), f32)],
        compiler_params=_cp(("parallel",)),
        name="rwkv_pre",
    )(r, ld, k, v, kk, b)
    row, mat, vec = specs(seq_rows)
    o = pl.pallas_call(
        _rwkv_seq_kernel,
        grid=(T // seq_rows,),
        in_specs=[row, row, mat, mat, vec],
        out_specs=row,
        out_shape=jax.ShapeDtypeStruct((T, W), f32),
        scratch_shapes=[pltpu.VMEM((npair, PW, PW), f32)],
        compiler_params=_cp(("arbitrary",)),
        name="rwkv_seq",
    )(rh, oh, nt, ds, ge)
    tm = min(512, T)
    blk = pl.BlockSpec((tm, W), lambda i: (i, 0))
    one = pl.BlockSpec((1, W), lambda i: (0, 0))
    return pl.pallas_call(
        _rwkv_post_kernel,
        grid=(T // tm,),
        in_specs=[blk, blk, blk, one, one],
        out_specs=blk,
        out_shape=jax.ShapeDtypeStruct((T, W), bf16),
        compiler_params=_cp(("parallel",)),
        name="rwkv_post",
    )(o, g, bonus, ln_g.reshape(1, W), ln_b.reshape(1, W))


def _s5_dot(a, b):
    return jnp.dot(a.astype(bf16), b.astype(bf16), preferred_element_type=f32)


def _transpose_pieces(w):
    w = list(w)
    lane = lax.broadcasted_iota(jnp.int32, w[0].shape, 1)
    for d in (4, 2, 1):
        width = S5_GROUP * d
        low = (lane & width) == 0
        for i in range(8):
            if i & d == 0:
                a, b = w[i], w[i + d]
                w[i] = jnp.where(low, a, pltpu.roll(b, width, 1))
                w[i + d] = jnp.where(low, pltpu.roll(a, 128 - width, 1), b)
    return w


def _s5_split_kernel(*refs):
    z_refs, u_ref = refs[:-1], refs[-1]
    ncb = u_ref.shape[1]
    for j, z_ref in enumerate(z_refs):
        for b in range(S5_CHUNK // 8):
            w = [z_ref[pl.ds(8 * b + i, ncb, stride=S5_CHUNK), :] for i in range(8)]
            o = _transpose_pieces(w)
            for a in range(8):
                u_ref[8 * j + a, :, 128 * b:128 * (b + 1)] = o[a]


def _toeplitz(krow):
    P, LP = krow.shape
    lane = lax.broadcasted_iota(jnp.int32, (P, LP), 1)
    blocks = [krow]
    for s in range(1, LP // P):
        blocks.append(jnp.where(lane >= P * s, pltpu.roll(krow, P * s, 1), 0.0))
    return jnp.concatenate(blocks, axis=0)


def _s5_in_kernel(u_ref, br_ref, bi_ref, xr_ref, xi_ref):
    u2 = jnp.concatenate([u_ref[0], u_ref[1]], axis=1)
    xr_ref[...] = _s5_dot(u2, br_ref[0])
    xi_ref[...] = _s5_dot(u2, bi_ref[0])


def _s5_scan_kernel(xr_ref, xi_ref, ar_ref, ai_ref, sr_ref, si_ref):
    nc = xr_ref.shape[0]
    ar = ar_ref[...]
    ai = ai_ref[...]

    def body(c, carry):
        sr, si = carry
        sr_ref[pl.ds(c, 1), :] = sr
        si_ref[pl.ds(c, 1), :] = si
        xr = xr_ref[pl.ds(c, 1), :]
        xi = xi_ref[pl.ds(c, 1), :]
        return ar * sr - ai * si + xr, ar * si + ai * sr + xi

    z = jnp.zeros_like(ar)
    lax.fori_loop(0, nc, body, (z, z))


def _s5_out_kernel(u_ref, krow_ref, sr_ref, si_ref, cr_ref, ci_ref, d_ref, o_ref):
    sr = sr_ref[...]
    si = si_ref[...]
    for j in range(2):
        u = u_ref[j]
        y = _s5_dot(u, _toeplitz(krow_ref[j])) + _s5_dot(sr, cr_ref[j]) + _s5_dot(si, ci_ref[j]) + d_ref[j] * u
        o_ref[j] = 0.5 * y * (1.0 + jnp.tanh(0.7978845608028654 * (y + 0.044715 * (y * y * y))))


def _glu_kernel(y_ref, w_ref, b_ref, o_ref, *nat_refs):
    ncb = y_ref.shape[1]
    for j, nat_ref in enumerate(nat_refs):
        for b in range(S5_CHUNK // 8):
            o = [y_ref[8 * j + a, :, 128 * b:128 * (b + 1)] for a in range(8)]
            w = _transpose_pieces(o)
            for i in range(8):
                nat_ref[pl.ds(8 * b + i, ncb, stride=S5_CHUNK), :] = w[i]
    y = jnp.concatenate([r[...] for r in nat_refs], axis=1)
    o_ref[...] = (y * _sigmoid(_bmm(y, w_ref[...]) + b_ref[...])).astype(o_ref.dtype)


def _s5_tables(a_re, a_im, log_dt, b_re, b_im, c_re, c_im, d):
    G, N, P, L = S5_GROUPS, S5_STATE, S5_GROUP, S5_CHUNK
    D = a_re.shape[0]
    dt = jnp.exp(log_dt)[:, None, :, None]
    lags = jnp.arange(L + 1, dtype=f32)[None, :, None, None]
    mag = jnp.exp(a_re[:, None] * dt * lags)
    ang = a_im[:, None] * dt * lags
    pw_re, pw_im = mag * jnp.cos(ang), mag * jnp.sin(ang)
    den = a_re * a_re + a_im * a_im
    nr, ni = pw_re[:, 1] - 1.0, pw_im[:, 1]
    coef_re = ((nr * a_re + ni * a_im) / den)[..., None]
    coef_im = ((ni * a_re - nr * a_im) / den)[..., None]
    bb_re = coef_re * b_re - coef_im * b_im
    bb_im = coef_re * b_im + coef_im * b_re
    rep = lambda a, axis: jnp.repeat(a, P, axis=axis)
    pl_re, pl_im = rep(pw_re[:, :L].transpose(0, 2, 1, 3), 2), rep(pw_im[:, :L].transpose(0, 2, 1, 3), 2)
    ct_re, ct_im = jnp.tile(c_re, (1, 1, L, 1)), jnp.tile(c_im, (1, 1, L, 1))
    cb_re, cb_im = ct_re * pl_re - ct_im * pl_im, ct_re * pl_im + ct_im * pl_re
    krow = (jnp.einsum('dgxn,dgnq->dgqx', cb_re, bb_re, precision=HI)
            - jnp.einsum('dgxn,dgnq->dgqx', cb_im, bb_im, precision=HI))
    e_re = rep(pw_re[:, L - 1::-1][:, :L].transpose(0, 2, 1, 3), 2)
    e_im = rep(pw_im[:, L - 1::-1][:, :L].transpose(0, 2, 1, 3), 2)
    bq_re = jnp.tile(bb_re.transpose(0, 1, 3, 2), (1, 1, L, 1))
    bq_im = jnp.tile(bb_im.transpose(0, 1, 3, 2), (1, 1, L, 1))
    bs_re = (e_re * bq_re - e_im * bq_im).reshape(D, G // 2, 2, L * P, N)
    bs_im = (e_re * bq_im + e_im * bq_re).reshape(D, G // 2, 2, L * P, N)

    def pair_in(bs):
        zz = jnp.zeros_like(bs[:, :, 0])
        return jnp.concatenate([jnp.concatenate([bs[:, :, 0], zz], axis=3), jnp.concatenate([zz, bs[:, :, 1]], axis=3)], axis=2)

    f_re = rep(pw_re[:, 1:].transpose(0, 2, 3, 1), 3)
    f_im = rep(pw_im[:, 1:].transpose(0, 2, 3, 1), 3)
    cn_re = jnp.tile(c_re.transpose(0, 1, 3, 2), (1, 1, 1, L))
    cn_im = jnp.tile(c_im.transpose(0, 1, 3, 2), (1, 1, 1, L))
    cs_re = (cn_re * f_re - cn_im * f_im).reshape(D, G // 2, 2, N, L * P)
    cs_im = (-(cn_re * f_im + cn_im * f_re)).reshape(D, G // 2, 2, N, L * P)

    def pair_out(cs):
        zz = jnp.zeros_like(cs[:, :, 0])
        top = jnp.concatenate([cs[:, :, 0], zz], axis=2)[:, :, None]
        bot = jnp.concatenate([zz, cs[:, :, 1]], axis=2)[:, :, None]
        return jnp.concatenate([top, bot], axis=2).reshape(D, G, 2 * N, L * P)

    dd = jnp.tile(d.reshape(D, G, 1, P), (1, 1, 1, L))
    return dict(krow=krow, bsr=pair_in(bs_re), bsi=pair_in(bs_im), cr=pair_out(cs_re), ci=pair_out(cs_im),
                al_re=pw_re[:, L].reshape(D, 1, G * N), al_im=pw_im[:, L].reshape(D, 1, G * N), d=dd)


def _s5(z_tail, tab, layer, glu_w, glu_b):
    T = z_tail.shape[0]
    G, N, P, L = S5_GROUPS, S5_STATE, S5_GROUP, S5_CHUNK
    nc, LP = T // L, L * P
    tm = min(1024, T)
    u = pl.pallas_call(
        _s5_split_kernel,
        grid=(T // tm,),
        in_specs=[pl.BlockSpec((tm, 128), lambda i, j=j: (i, j)) for j in range(S5_W // 128)],
        out_specs=pl.BlockSpec((G, tm // L, LP), lambda i: (0, i, 0)),
        out_shape=jax.ShapeDtypeStruct((G, nc, LP), f32),
        compiler_params=_cp(("parallel",)),
        name="s5_split",
    )(*[z_tail] * (S5_W // 128))
    grp = lambda s: pl.BlockSpec((2,) + s, lambda j: (j, 0, 0))
    tgrp = lambda s: pl.BlockSpec((None, 2) + s, lambda j: (layer, j, 0, 0))
    tone = lambda s: pl.BlockSpec((None, 1) + s, lambda j: (layer, j, 0, 0))
    lanes = pl.BlockSpec((nc, 2 * N), lambda j: (0, j))
    xr, xi = pl.pallas_call(
        _s5_in_kernel,
        grid=(G // 2,),
        in_specs=[grp((nc, LP)), tone((2 * LP, 2 * N)), tone((2 * LP, 2 * N))],
        out_specs=[lanes, lanes],
        out_shape=[jax.ShapeDtypeStruct((nc, G * N), f32), jax.ShapeDtypeStruct((nc, G * N), f32)],
        compiler_params=_cp(("parallel",)),
        name="s5_in",
    )(u, tab["bsr"], tab["bsi"])
    whole = lambda s: pl.BlockSpec(s, lambda i: (0,) * len(s))
    sr, si = pl.pallas_call(
        _s5_scan_kernel,
        grid=(1,),
        in_specs=[whole((nc, G * N)), whole((nc, G * N)), pl.BlockSpec((None, 1, G * N), lambda i: (layer, 0, 0)),
                  pl.BlockSpec((None, 1, G * N), lambda i: (layer, 0, 0))],
        out_specs=[whole((nc, G * N)), whole((nc, G * N))],
        out_shape=[jax.ShapeDtypeStruct((nc, G * N), f32)] * 2,
        compiler_params=_cp(("arbitrary",)),
        name="s5_scan",
    )(xr, xi, tab["al_re"], tab["al_im"])
    y = pl.pallas_call(
        _s5_out_kernel,
        grid=(G // 2,),
        in_specs=[grp((nc, LP)), tgrp((P, LP)), lanes, lanes, tgrp((2 * N, LP)), tgrp((2 * N, LP)), tgrp((1, LP))],
        out_specs=grp((nc, LP)),
        out_shape=jax.ShapeDtypeStruct((G, nc, LP), f32),
        compiler_params=_cp(("parallel",)),
        name="s5_out",
    )(u, tab["krow"], sr, si, tab["cr"], tab["ci"], tab["d"])
    return pl.pallas_call(
        _glu_kernel,
        grid=(T // tm,),
        in_specs=[pl.BlockSpec((G, tm // L, LP), lambda i: (0, i, 0)), pl.BlockSpec((S5_W, S5_W), lambda i: (0, 0)),
                  pl.BlockSpec((1, S5_W), lambda i: (0, 0))],
        out_specs=pl.BlockSpec((tm, S5_W), lambda i: (i, 0)),
        out_shape=jax.ShapeDtypeStruct((T, S5_W), bf16),
        scratch_shapes=[pltpu.VMEM((tm, 128), f32)] * (S5_W // 128),
        compiler_params=_cp(("parallel",)),
        name="s5_glu",
    )(y, glu_w, glu_b.reshape(1, S5_W))


def _pad_rows(w, n):
    return jnp.pad(w, ((0, n - w.shape[0]), (0, 0)))


def _split_mu(mu):
    l0 = 3 * RWKV_W
    return jnp.concatenate([
        mu[:l0], jnp.pad(mu[l0:l0 + DECAY_LORA], (0, LORA_PAD - DECAY_LORA)),
        jnp.pad(mu[l0 + DECAY_LORA:l0 + DECAY_LORA + AAA_LORA], (0, LORA_PAD - AAA_LORA)),
        mu[l0 + DECAY_LORA + AAA_LORA:]]).reshape(1, RWKV_PAD)


def kernel(x, p, norm_mix_g, w_in, mlstm_conv, mlstm_ib, mlstm_fb, mlstm_norm_g, rwkv_mu, rwkv_w0, rwkv_w2, rwkv_a0, rwkv_a2, rwkv_g2, rwkv_kk, rwkv_ka, rwkv_rk, rwkv_ln_g, rwkv_ln_b, s5_a_re, s5_a_im, s5_log_dt, s5_b_re, s5_b_im, s5_c_re, s5_c_im, s5_d, s5_glu_w, s5_glu_b, w_up_m, w_up_r, w_up_s, w_out, norm_ffn_g, ffn_w_gate, ffn_w_up, ffn_w_down, norm_ple_g, ple_w_gate, ple_w_proj, final_norm_g):
    B, T, D = x.shape
    depth = w_in.shape[0]
    w_out_b, ple_wg_b, ple_wp_b, p_b = (a.astype(bf16) for a in (w_out, ple_w_gate, ple_w_proj, p))
    wt = jnp.swapaxes(w_in, 1, 2)
    s5_tab = _s5_tables(s5_a_re, s5_a_im, s5_log_dt, s5_b_re, s5_b_im, s5_c_re, s5_c_im, s5_d)
    outs = []
    for bi in range(B):
        h = x[bi]
        xn = _rmsnorm(h, norm_mix_g[0], bf16)
        for i in range(depth):
            z_main = _proj(xn, wt, i, 0, 4, MLSTM_W)
            z_gate = _proj(xn, wt, i, M_IN + RWKV_IN + S5_W, 3 * D_MODEL // 1024, 1024, bf16, gate=True)
            z_tail = _proj_tail(xn, wt, i)

            gate_b = jnp.pad(jnp.concatenate([mlstm_ib[i], mlstm_fb[i]]), (0, 128 - 2 * MLSTM_HEADS)).reshape(1, 128)
            y_m = _mlstm(z_main, z_tail, mlstm_conv[i], gate_b, mlstm_norm_g[i])

            vec = lambda a: a.reshape(1, RWKV_W)
            parts = _rwkv_prep(xn, _rwkv_w(wt, i), _split_mu(rwkv_mu[i]), vec(rwkv_w0[i]), vec(rwkv_a0[i]), vec(rwkv_kk[i]),
                               vec(rwkv_ka[i]), vec(rwkv_rk[i]), _pad_rows(rwkv_w2[i], LORA_PAD).astype(bf16),
                               _pad_rows(rwkv_a2[i], LORA_PAD).astype(bf16), rwkv_g2[i].astype(bf16))
            y_r = _rwkv_scan(parts, rwkv_ln_g[i], rwkv_ln_b[i])

            y_s = _s5(z_tail, s5_tab, i, s5_glu_w[i].astype(bf16), s5_glu_b[i])

            mixed = _merge(y_m, y_r, y_s, z_gate, w_up_m, w_up_r, w_up_s, i)
            h, hn = _resid_norm(mixed, w_out_b, i, h, norm_ffn_g[i])
            act = _ffn_up(hn, ffn_w_gate, ffn_w_up, i)
            h = _resid_mm(act, ffn_w_down, h, i)
            last = i == depth - 1
            g_next = final_norm_g if last else norm_mix_g[i + 1]
            h, xn = _ple(h, p_b[:, bi:bi + 1], ple_wg_b, ple_wp_b, i, norm_ple_g[i], g_next, f32 if last else bf16)
        outs.append(xn)
    return jnp.stack(outs)
```

```python
import functools

import jax
import jax.numpy as jnp
import numpy as np
from jax import lax
from jax.experimental import pallas as pl
from jax.experimental.pallas import tpu as pltpu

f32 = jnp.float32
bf16 = jnp.bfloat16
HI = lax.Precision.HIGHEST

D_MODEL = 2048
PLE_DIM = 256
RMS_EPS = 1e-6
MLSTM_HEADS = 4
MLSTM_HEAD_DIM = 256
MLSTM_W = 1024
MLSTM_CHUNK = 256
GATE_SOFTCAP = 15.0
RWKV_HEAD_DIM = 64
RWKV_W = 512
DECAY_LORA = 96
AAA_LORA = 96
GATE_LORA = 256
LORA_PAD = 128
RWKV_GN_EPS = 64e-5
RWKV_CHUNK = 64
S5_GROUP = 16
S5_GROUPS = 32
S5_W = 512
S5_STATE = 64
S5_CHUNK = 16
FFN_HIDDEN = 5632
M_IN = 4 * MLSTM_W + 2 * MLSTM_HEADS
RWKV_IN = 3 * RWKV_W + DECAY_LORA + AAA_LORA + GATE_LORA
RWKV_PAD = 3 * RWKV_W + 2 * LORA_PAD + GATE_LORA
Z_TAIL = S5_W + 128
VMEM_LIMIT = 56 * 1024 * 1024


def _cp(sem):
    return pltpu.CompilerParams(dimension_semantics=sem, vmem_limit_bytes=VMEM_LIMIT)


def _nt(a, b, **kw):
    return lax.dot_general(a, b, (((1,), (1,)), ((), ())), preferred_element_type=f32, **kw)


def _tn(a, b, **kw):
    return lax.dot_general(a, b, (((0,), (0,)), ((), ())), preferred_element_type=f32, **kw)


def _mm(a, b, **kw):
    return jnp.dot(a, b, preferred_element_type=f32, **kw)


def _bmm(a, b):
    return jnp.dot(a.astype(bf16), b.astype(bf16), preferred_element_type=f32)


def _sigmoid(x):
    return 1.0 / (1.0 + jnp.exp(-x))


def _rms(x, g):
    return x * lax.rsqrt(jnp.mean(x * x, axis=-1, keepdims=True) + RMS_EPS) * g


def _norm_kernel(x_ref, g_ref, o_ref):
    o_ref[...] = _rms(x_ref[...], g_ref[...]).astype(o_ref.dtype)


def _rmsnorm(x, g, out_dtype):
    T, D = x.shape
    tm = min(512, T)
    return pl.pallas_call(
        _norm_kernel,
        grid=(T // tm,),
        in_specs=[pl.BlockSpec((tm, D), lambda i: (i, 0)), pl.BlockSpec((1, D), lambda i: (0, 0))],
        out_specs=pl.BlockSpec((tm, D), lambda i: (i, 0)),
        out_shape=jax.ShapeDtypeStruct((T, D), out_dtype),
        compiler_params=_cp(("parallel",)),
        name="rmsnorm",
    )(x, g.reshape(1, D))


def _load_wt(w, rows):
    parts = [w[r[0]:r[1], :] if isinstance(r, tuple) else jnp.zeros((r, w.shape[1]), f32) for r in rows]
    w = parts[0] if len(parts) == 1 else jnp.concatenate(parts, axis=0)
    return w.T.astype(bf16)


def _proj_kernel(a_ref, w_ref, o_ref, wb_ref, *, gate):
    @pl.when(pl.program_id(1) == 0)
    def _():
        wb_ref[...] = _load_wt(w_ref[0], [(0, w_ref.shape[1])])

    z = _mm(a_ref[...], wb_ref[...])
    o_ref[...] = (0.5 * jnp.tanh(0.5 * z) + 0.5 if gate else z).astype(o_ref.dtype)


def _proj(a, wt, layer, row0, n_tiles, tn, out_dtype=f32, gate=False):
    T, K = a.shape
    tm = min(1024, T)
    return pl.pallas_call(
        functools.partial(_proj_kernel, gate=gate),
        grid=(n_tiles, T // tm),
        in_specs=[pl.BlockSpec((tm, K), lambda j, i: (i, 0)),
                  pl.BlockSpec((pl.Element(1), pl.Element(tn), pl.Element(K)),
                               lambda j, i: (layer, pl.multiple_of(row0 + tn * j, 8), 0))],
        out_specs=pl.BlockSpec((tm, tn), lambda j, i: (i, j)),
        out_shape=jax.ShapeDtypeStruct((T, n_tiles * tn), out_dtype),
        scratch_shapes=[pltpu.VMEM((K, tn), bf16)],
        compiler_params=_cp(("parallel", "arbitrary")),
        name="in_proj",
    )(a, wt)


def _proj_tail_kernel(a_ref, ws_ref, wg_ref, u_ref, gz_ref, wb_ref, *z_refs):
    @pl.when(pl.program_id(0) == 0)
    def _():
        pad = jnp.zeros((Z_TAIL - S5_W - wg_ref.shape[0], wg_ref.shape[1]), f32)
        wb_ref[...] = jnp.concatenate([ws_ref[0], wg_ref[...], pad], axis=0).T.astype(bf16)

    z = _mm(a_ref[...], wb_ref[...])
    gz_ref[...] = z[:, S5_W:]
    for j, z_ref in enumerate(z_refs):
        z_ref[...] = z[:, 128 * j:128 * (j + 1)]
    _s5_split(z_refs, u_ref)


def _proj_tail(a, wt, layer):
    T, K = a.shape
    tm = min(1024, T)
    n_if = 2 * MLSTM_HEADS
    L = S5_CHUNK
    return pl.pallas_call(
        _proj_tail_kernel,
        grid=(T // tm,),
        in_specs=[pl.BlockSpec((tm, K), lambda i: (i, 0)),
                  pl.BlockSpec((pl.Element(1), pl.Element(S5_W), pl.Element(K)), lambda i: (layer, M_IN + RWKV_IN, 0)),
                  pl.BlockSpec((None, n_if, K), lambda i: (layer, 4 * MLSTM_W // n_if, 0))],
        out_specs=[pl.BlockSpec((S5_GROUPS, tm // L, L * S5_GROUP), lambda i: (0, i, 0)),
                   pl.BlockSpec((tm, Z_TAIL - S5_W), lambda i: (i, 0))],
        out_shape=[jax.ShapeDtypeStruct((S5_GROUPS, T // L, L * S5_GROUP), f32),
                   jax.ShapeDtypeStruct((T, Z_TAIL - S5_W), f32)],
        scratch_shapes=[pltpu.VMEM((K, Z_TAIL), bf16)] + [pltpu.VMEM((tm, 128), f32)] * (S5_W // 128),
        compiler_params=_cp(("arbitrary",)),
        name="in_proj_tail",
    )(a, wt, wt)


def _merge_kernel(ym_ref, yr_ref, ys_ref, gm_ref, gr_ref, gs_ref, um_ref, ur_ref, us_ref, o_ref, umb_ref, urb_ref, usb_ref):
    @pl.when(pl.program_id(1) == 0)
    def _():
        umb_ref[...] = um_ref[...].astype(bf16)
        urb_ref[...] = ur_ref[...].astype(bf16)
        usb_ref[...] = us_ref[...].astype(bf16)

    acc = gm_ref[...] * _mm(ym_ref[...], umb_ref[...])
    acc += gr_ref[...] * _mm(yr_ref[...], urb_ref[...])
    acc += gs_ref[...] * _mm(ys_ref[...], usb_ref[...])
    o_ref[...] = acc.astype(o_ref.dtype)


def _merge(ym, yr, ys, gates, um, ur, us, layer):
    T = ym.shape[0]
    tm, tn = min(1024, T), 1024
    nj = D_MODEL // tn

    def gate_spec(b):
        return pl.BlockSpec((tm, tn), lambda j, i, b=b: (i, b * nj + j))

    def y_spec(w):
        return pl.BlockSpec((tm, w), lambda j, i: (i, 0))

    def u_spec(w):
        return pl.BlockSpec((None, w, tn), lambda j, i: (layer, 0, j))

    return pl.pallas_call(
        _merge_kernel,
        grid=(nj, T // tm),
        in_specs=[y_spec(MLSTM_W), y_spec(RWKV_W), y_spec(S5_W), gate_spec(0), gate_spec(1), gate_spec(2),
                  u_spec(MLSTM_W), u_spec(RWKV_W), u_spec(S5_W)],
        out_specs=pl.BlockSpec((tm, tn), lambda j, i: (i, j)),
        out_shape=jax.ShapeDtypeStruct((T, D_MODEL), bf16),
        scratch_shapes=[pltpu.VMEM((MLSTM_W, tn), bf16), pltpu.VMEM((RWKV_W, tn), bf16), pltpu.VMEM((S5_W, tn), bf16)],
        compiler_params=_cp(("parallel", "arbitrary")),
        name="gated_merge",
    )(ym, yr, ys, gates, gates, gates, um, ur, us)


def _resid_norm_kernel(a_ref, w_ref, h_ref, g_ref, ho_ref, no_ref):
    hn = h_ref[...] + _mm(a_ref[...], w_ref[...])
    ho_ref[...] = hn
    no_ref[...] = _rms(hn, g_ref[...]).astype(no_ref.dtype)


def _resid_norm(a, w, layer, h, g, norm_dtype=bf16):
    T, K = a.shape
    tm = min(512, T)
    row = lambda n: pl.BlockSpec((tm, n), lambda i: (i, 0))
    return pl.pallas_call(
        _resid_norm_kernel,
        grid=(T // tm,),
        in_specs=[row(K), pl.BlockSpec((None, K, D_MODEL), lambda i: (layer, 0, 0)), row(D_MODEL),
                  pl.BlockSpec((1, D_MODEL), lambda i: (0, 0))],
        out_specs=[row(D_MODEL), row(D_MODEL)],
        out_shape=[jax.ShapeDtypeStruct((T, D_MODEL), f32), jax.ShapeDtypeStruct((T, D_MODEL), norm_dtype)],
        compiler_params=_cp(("parallel",)),
        name="resid_matmul_norm",
    )(a, w, h, g.reshape(1, D_MODEL))


def _ffn_up_kernel(a_ref, wg_ref, wu_ref, o_ref, wgb_ref, wub_ref):
    @pl.when(pl.program_id(1) == 0)
    def _():
        wgb_ref[...] = wg_ref[...].astype(bf16)
        wub_ref[...] = wu_ref[...].astype(bf16)

    a = a_ref[...]
    gt = _mm(a, wgb_ref[...])
    up = _mm(a, wub_ref[...])
    o_ref[...] = (gt * _sigmoid(gt) * up).astype(o_ref.dtype)


def _ffn_up(a, wg, wu, layer):
    T, K = a.shape
    N = wg.shape[2]
    tm, tn = min(1024, T), 512
    return pl.pallas_call(
        _ffn_up_kernel,
        grid=(N // tn, T // tm),
        in_specs=[pl.BlockSpec((tm, K), lambda j, i: (i, 0)), pl.BlockSpec((None, K, tn), lambda j, i: (layer, 0, j)),
                  pl.BlockSpec((None, K, tn), lambda j, i: (layer, 0, j))],
        out_specs=pl.BlockSpec((tm, tn), lambda j, i: (i, j)),
        out_shape=jax.ShapeDtypeStruct((T, N), bf16),
        scratch_shapes=[pltpu.VMEM((K, tn), bf16), pltpu.VMEM((K, tn), bf16)],
        compiler_params=_cp(("parallel", "arbitrary")),
        name="ffn_up",
    )(a, wg, wu)


def _resid_mm_kernel(a_ref, w_ref, h_ref, o_ref, wb_ref):
    @pl.when(pl.program_id(1) == 0)
    def _():
        wb_ref[...] = w_ref[...].astype(bf16)

    o_ref[...] = h_ref[...] + _mm(a_ref[...], wb_ref[...])


def _resid_mm(a, w, h, layer):
    T, K = a.shape
    N = w.shape[2]
    tm, tn = min(512, T), 512
    return pl.pallas_call(
        _resid_mm_kernel,
        grid=(N // tn, T // tm),
        in_specs=[pl.BlockSpec((tm, K), lambda j, i: (i, 0)), pl.BlockSpec((None, K, tn), lambda j, i: (layer, 0, j)),
                  pl.BlockSpec((tm, tn), lambda j, i: (i, j))],
        out_specs=pl.BlockSpec((tm, tn), lambda j, i: (i, j)),
        out_shape=jax.ShapeDtypeStruct((T, N), f32),
        scratch_shapes=[pltpu.VMEM((K, tn), bf16)],
        compiler_params=_cp(("parallel", "arbitrary")),
        name="resid_matmul",
    )(a, w, h)


def _ple_kernel(h_ref, p_ref, wg_ref, wp_ref, gp_ref, g_ref, ho_ref, no_ref):
    tm = h_ref.shape[0]
    parts = [slice(0, tm // 2), slice(tm // 2, tm)]
    hs = [h_ref[r, :] for r in parts]
    hps = [_rms(h, gp_ref[...]).astype(bf16) for h in hs]
    gates = [_sigmoid(_mm(hp, wg_ref[...])) for hp in hps]
    projs = [_mm(p_ref[r, :], wp_ref[...]) for r in parts]
    for r, h, gate, proj in zip(parts, hs, gates, projs):
        hn = h + proj * gate
        ho_ref[r, :] = hn
        no_ref[r, :] = _rms(hn, g_ref[...]).astype(no_ref.dtype)


def _ple(h, p, wg, wp, layer, g_ple, g, norm_dtype):
    T = h.shape[0]
    tm = min(512, T)
    row = lambda w: pl.BlockSpec((tm, w), lambda i: (i, 0))
    full = lambda s: pl.BlockSpec(s, lambda i: (0, 0))
    return pl.pallas_call(
        _ple_kernel,
        grid=(T // tm,),
        in_specs=[row(D_MODEL), pl.BlockSpec((None, None, tm, PLE_DIM), lambda i: (layer, 0, i, 0)),
                  pl.BlockSpec((None, D_MODEL, D_MODEL), lambda i: (layer, 0, 0)),
                  pl.BlockSpec((None, PLE_DIM, D_MODEL), lambda i: (layer, 0, 0)), full((1, D_MODEL)), full((1, D_MODEL))],
        out_specs=[row(D_MODEL), row(D_MODEL)],
        out_shape=[jax.ShapeDtypeStruct((T, D_MODEL), f32), jax.ShapeDtypeStruct((T, D_MODEL), norm_dtype)],
        compiler_params=_cp(("parallel",)),
        name="ple_norm",
    )(h, p, wg, wp, g_ple.reshape(1, D_MODEL), g.reshape(1, D_MODEL))


def _mlstm_kernel(zq_ref, zk_ref, v_ref, o_ref, gz_ref, cq_ref, ck_ref, gb_ref, ng_ref, y_ref,
                  ct_ref, n_ref, qbuf_ref, kbuf_ref):
    c = pl.program_id(0)
    L, W = zq_ref.shape
    H, dh = MLSTM_HEADS, MLSTM_HEAD_DIM
    heads = range(H)

    @pl.when(c == 0)
    def _():
        ct_ref[...] = jnp.zeros_like(ct_ref)
        n_ref[...] = jnp.zeros_like(n_ref)
        qbuf_ref[...] = jnp.zeros_like(qbuf_ref)
        kbuf_ref[...] = jnp.zeros_like(kbuf_ref)

    def conv_silu(z_ref, w_ref, buf_ref):
        x = z_ref[...]
        xx = jnp.concatenate([buf_ref[...], x], axis=0)
        w = w_ref[...]
        acc = x * w[0:1, :]
        for j in range(1, w.shape[0]):
            acc = acc + pltpu.roll(xx, j, 0)[8:, :] * w[j:j + 1, :]
        buf_ref[...] = x[L - 8:, :]
        return acc * _sigmoid(acc)

    q = conv_silu(zq_ref, cq_ref, qbuf_ref) * (dh ** -0.5)
    k = conv_silu(zk_ref, ck_ref, kbuf_ref)

    g = gz_ref[...] + gb_ref[...]
    sc = GATE_SOFTCAP * jnp.tanh(g / GATE_SOFTCAP)
    logf = -jnp.log(1.0 + jnp.exp(-sc))
    row = lax.broadcasted_iota(jnp.int32, (L, L), 0)
    col = lax.broadcasted_iota(jnp.int32, (L, L), 1)
    causal = row >= col
    bcs = _mm(causal.astype(f32), logf, precision=HI)
    b128 = pltpu.roll(bcs, 128 - H, 1)
    e128 = sc - b128
    e_t = e128.T

    sl = [slice(h * dh, (h + 1) * dh) for h in heads]
    b_col = [b128[:, h:h + 1] for h in heads]
    e_col = [e128[:, h:h + 1] for h in heads]
    wmat = [jnp.where(causal, jnp.exp(jnp.where(causal, b_col[h] + e_t[h:h + 1, :], 0.0)), 0.0) for h in heads]
    b_end = [b_col[h][L - 1:L, :] for h in heads]
    eb = [jnp.exp(b_col[h]) for h in heads]
    qh = [q[:, sl[h]] for h in heads]
    kh = [k[:, sl[h]] for h in heads]
    qb = [x.astype(bf16) for x in qh]
    kb = [x.astype(bf16) for x in kh]
    vb = [v_ref[:, sl[h]].astype(bf16) for h in heads]
    ct = [ct_ref[h] for h in heads]
    nn = [n_ref[:, sl[h]] for h in heads]
    s = [_nt(qb[h], kb[h]) * wmat[h] for h in heads]
    inter = [_mm(qb[h], ct[h].astype(bf16)) for h in heads]
    num = [_mm(s[h].astype(bf16), vb[h]) + eb[h] * inter[h] for h in heads]
    ng = ng_ref[...]
    for h in heads:
        den = jnp.sum(s[h], axis=1, keepdims=True) + eb[h] * jnp.sum(qh[h] * nn[h], axis=1, keepdims=True)
        hh = num[h] / jnp.maximum(jnp.abs(den), 1.0)
        hh = hh * lax.rsqrt(jnp.mean(hh * hh, axis=1, keepdims=True) + RMS_EPS) * ng[:, sl[h]]
        y_ref[:, sl[h]] = (_sigmoid(o_ref[:, sl[h]]) * hh).astype(y_ref.dtype)
    kw = [kh[h] * jnp.exp(b_end[h] + e_col[h]) for h in heads]
    upd = [_tn(kw[h].astype(bf16), vb[h]) for h in heads]
    for h in heads:
        decay = jnp.exp(b_end[h])
        ct_ref[h] = decay * ct[h] + upd[h]
        n_ref[:, sl[h]] = decay * nn[h] + jnp.sum(kw[h], axis=0, keepdims=True)


def _mlstm(z_main, z_if, conv_w, gate_b, norm_g):
    T = z_main.shape[0]
    L, dh, H, W = min(MLSTM_CHUNK, T), MLSTM_HEAD_DIM, MLSTM_HEADS, MLSTM_W
    blk = lambda j: pl.BlockSpec((L, W), lambda c, j=j: (c, j))
    return pl.pallas_call(
        _mlstm_kernel,
        grid=(T // L,),
        in_specs=[blk(0), blk(1), blk(2), blk(3),
                  pl.BlockSpec((L, 128), lambda c: (c, 0)),
                  pl.BlockSpec((4, W), lambda c: (0, 0)), pl.BlockSpec((4, W), lambda c: (0, 1)),
                  pl.BlockSpec((1, 128), lambda c: (0, 0)), pl.BlockSpec((1, W), lambda c: (0, 0))],
        out_specs=pl.BlockSpec((L, W), lambda c: (c, 0)),
        out_shape=jax.ShapeDtypeStruct((T, W), bf16),
        scratch_shapes=[pltpu.VMEM((H, dh, dh), f32), pltpu.VMEM((1, W), f32),
                        pltpu.VMEM((8, W), f32), pltpu.VMEM((8, W), f32)],
        compiler_params=_cp(("arbitrary",)),
        name="mlstm",
    )(z_main, z_main, z_main, z_main, z_if, conv_w, conv_w, gate_b, norm_g.reshape(1, W))


def _iota_div(shape, dim, width):
    return lax.shift_right_logical(lax.broadcasted_iota(jnp.int32, shape, dim), int(np.log2(width)))


def _head_ones(n, width):
    return (_iota_div((n, n), 0, width) == _iota_div((n, n), 1, width)).astype(f32)


def _head_sum(x):
    blk = _head_ones(RWKV_W, RWKV_HEAD_DIM).astype(bf16)
    hi = x.astype(bf16)
    lo = (x - hi.astype(f32)).astype(bf16)
    return _mm(hi, blk) + _mm(lo, blk)


def _rwkv_w_kernel(w_ref, o_ref, *, last_rows):
    j = pl.program_id(0)
    nj = pl.num_programs(0)

    @pl.when(j < nj - 1)
    def _():
        o_ref[...] = _load_wt(w_ref[0], [(0, w_ref.shape[1])])

    @pl.when(j == nj - 1)
    def _():
        o_ref[...] = _load_wt(w_ref[0], last_rows)


def _rwkv_w(wt, layer):
    K = wt.shape[2]
    tn = RWKV_W
    lora_rows = [(0, DECAY_LORA), LORA_PAD - DECAY_LORA, (DECAY_LORA, DECAY_LORA + AAA_LORA), LORA_PAD - AAA_LORA,
                 (DECAY_LORA + AAA_LORA, RWKV_IN - 3 * RWKV_W)]
    return pl.pallas_call(
        functools.partial(_rwkv_w_kernel, last_rows=lora_rows),
        grid=(RWKV_PAD // tn,),
        in_specs=[pl.BlockSpec((pl.Element(1), pl.Element(tn), pl.Element(K)),
                               lambda j: (layer, pl.multiple_of(M_IN + tn * j, 8), 0))],
        out_specs=pl.BlockSpec((K, tn), lambda j: (0, j)),
        out_shape=jax.ShapeDtypeStruct((K, RWKV_PAD), bf16),
        compiler_params=_cp(("parallel",)),
        name="rwkv_w",
    )(wt)


def _rwkv_prep_kernel(a_ref, w_ref, mu_ref, w0_ref, a0_ref, kkg_ref, ka_ref, rk_ref, w2_ref, a2_ref, g2_ref,
                      r_ref, ld_ref, k_ref, v_ref, kk_ref, b_ref, g_ref, bonus_ref, *buf_refs):
    i = pl.program_id(0)
    nh = len(buf_refs)
    tm = a_ref.shape[0] // nh
    W = RWKV_W

    @pl.when(i == 0)
    def _():
        buf_refs[0][0:8, :] = jnp.zeros((8, buf_refs[0].shape[1]), f32)

    rows = [slice(h * tm, (h + 1) * tm) for h in range(nh)]
    xs_all = [_mm(a_ref[rw, :], w_ref[...]) for rw in rows]
    for h, (rw, x) in enumerate(zip(rows, xs_all)):
        buf_ref = buf_refs[h]
        buf_ref[8:, :] = x
        buf_refs[(h + 1) % nh][0:8, :] = x[tm - 8:, :]
        xs = x + (buf_ref[pl.ds(7, tm), :] - x) * mu_ref[...]
        r = xs[:, 0:W]
        k = xs[:, W:2 * W]
        v = xs[:, 2 * W:3 * W]
        wl = xs[:, 3 * W:3 * W + LORA_PAD]
        al = xs[:, 3 * W + LORA_PAD:3 * W + 2 * LORA_PAD]
        gl = xs[:, 3 * W + 2 * LORA_PAD:]
        t = w0_ref[...] + _bmm(jnp.tanh(wl), w2_ref[...])
        w = -(jnp.maximum(-t, 0.0) + jnp.log(1.0 + jnp.exp(-jnp.abs(t)))) - 0.5
        a = _sigmoid(a0_ref[...] + _bmm(al, a2_ref[...]))
        g = _bmm(_sigmoid(gl), g2_ref[...])
        kk = k * kkg_ref[...]
        ss = _head_sum(kk * kk)
        kk = kk / jnp.maximum(jnp.sqrt(ss), 1e-12)
        k2 = k * (1.0 + (a - 1.0) * ka_ref[...])
        bonus = _head_sum(r * k2 * rk_ref[...]) * v
        r_ref[rw, :] = r
        ld_ref[rw, :] = -jnp.exp(w)
        k_ref[rw, :] = k2
        v_ref[rw, :] = v
        kk_ref[rw, :] = kk
        b_ref[rw, :] = kk * a
        g_ref[rw, :] = g
        bonus_ref[rw, :] = bonus


def _rwkv_prep(xn, w_rwkv, mu, w0, a0, kkg, ka, rk, w2, a2, g2):
    T, K = xn.shape
    tm = min(512, T)
    n_sub = 2
    W = RWKV_W
    vec = lambda n: pl.BlockSpec((1, n), lambda i: (0, 0))
    mat = lambda s: pl.BlockSpec(s, lambda i: (0, 0))
    out = pl.BlockSpec((tm, W), lambda i: (i, 0))
    return pl.pallas_call(
        _rwkv_prep_kernel,
        grid=(T // tm,),
        in_specs=[pl.BlockSpec((tm, K), lambda i: (i, 0)), mat((K, RWKV_PAD)), vec(RWKV_PAD), vec(W), vec(W),
                  vec(W), vec(W), vec(W), mat((LORA_PAD, W)), mat((LORA_PAD, W)), mat((GATE_LORA, W))],
        out_specs=[out] * 8,
        out_shape=[jax.ShapeDtypeStruct((T, W), f32)] * 8,
        scratch_shapes=[pltpu.VMEM((tm // n_sub + 8, RWKV_PAD), f32)] * n_sub,
        compiler_params=_cp(("arbitrary",)),
        name="rwkv_prep",
    )(xn, w_rwkv, mu, w0, a0, kkg, ka, rk, w2, a2, g2)


def _rwkv_pre_kernel(r_ref, ld_ref, k_ref, v_ref, kk_ref, b_ref, rh_ref, oh_ref, nt_ref, ds_ref, ge_ref):
    L, dh = RWKV_CHUNK, RWKV_HEAD_DIM
    GW = 4 * dh
    PW = 2 * dh
    n_chunks = r_ref.shape[0] // L
    rr = lax.broadcasted_iota(jnp.int32, (GW, GW), 0)
    cc = lax.broadcasted_iota(jnp.int32, (GW, GW), 1)
    same = _iota_div((GW, GW), 0, L) == _iota_div((GW, GW), 1, dh)
    strict = same & (rr > cc)
    incl = same & (rr >= cc)
    eye = jnp.where(rr == cc, 1.0, 0.0)
    pair_blk = _head_ones(PW, dh)
    tr = lax.broadcasted_iota(jnp.int32, (L, L), 0)
    tc = lax.broadcasted_iota(jnp.int32, (L, L), 1)
    tril = (tr >= tc).astype(f32)

    def unstack(x):
        acc = x[0:L, :]
        for hh in range(1, GW // L):
            acc = acc + x[hh * L:(hh + 1) * L, :]
        return acc

    def tile(x):
        return jnp.concatenate([x] * (GW // L), axis=0)

    def stack(x):
        return jnp.where(same, tile(x), 0.0)

    tiles = [(ci, gi) for ci in range(n_chunks) for gi in range(RWKV_W // GW)]
    ch = []
    for ci in range(n_chunks):
        rows = slice(ci * L, (ci + 1) * L)
        ld = ld_ref[rows, :]
        lg = _mm(tril, ld, precision=HI)
        g_end = lg[L - 1:L, :]
        ge_ref[ci] = g_end
        dec_out = jnp.exp(-lg)
        dec_tail = jnp.exp(g_end - lg)
        k = k_ref[rows, :]
        b = b_ref[rows, :]
        v = v_ref[rows, :]
        ch.append(dict(rows=rows, v=v, kap=kk_ref[rows, :] * jnp.exp(lg - ld), rt=r_ref[rows, :] * jnp.exp(lg),
                       kt=k * dec_out, bt=b * dec_out, khat=(k * dec_tail).astype(bf16),
                       bhat=(b * dec_tail).astype(bf16), vb=v.astype(bf16)))
    ops = []
    for ci, gi in tiles:
        c = ch[ci]
        lanes = slice(gi * GW, (gi + 1) * GW)
        kap_s = stack(c["kap"][:, lanes])
        rt_s = stack(c["rt"][:, lanes])
        ops.append(dict(lanes=lanes, kap_s=kap_s, rt_s=rt_s, kap_b=kap_s.astype(bf16), rt_b=rt_s.astype(bf16),
                        v_s=stack(c["v"][:, lanes]).astype(bf16), kt_t=tile(c["kt"][:, lanes]).astype(bf16),
                        bt_t=tile(c["bt"][:, lanes]).astype(bf16)))
    for o in ops:
        o["a_kb"] = jnp.where(strict, _nt(o["kap_b"], o["bt_t"]), 0.0)
    for o in ops:
        o["a_kk"] = jnp.where(strict, _nt(o["kap_b"], o["kt_t"]), 0.0).astype(bf16)
        o["a_rk"] = jnp.where(incl, _nt(o["rt_b"], o["kt_t"]), 0.0).astype(bf16)
        o["a_rb"] = jnp.where(incl, _nt(o["rt_b"], o["bt_t"]), 0.0).astype(bf16)
    for o in ops:
        o["p"] = -o["a_kb"]
        o["tinv"] = eye + o["p"]
    step = 2
    while step < L:
        for o in ops:
            pb = o["p"].astype(bf16)
            o["p"] = _mm(pb, pb)
        for o in ops:
            o["tinv"] = o["tinv"] + _bmm(o["tinv"], o["p"])
        step *= 2
    for o in ops:
        o["w1"] = _mm(o["a_kk"], o["v_s"])
        o["w2"] = _mm(o["a_rk"], o["v_s"])
    for o in ops:
        o["ku"] = _bmm(o["tinv"], jnp.concatenate([o["kap_s"], o["w1"]], axis=1))
    for o in ops:
        o["ru"] = _mm(o["a_rb"], o["ku"].astype(bf16))
    for (ci, gi), o in zip(tiles, ops):
        c = ch[ci]
        rows, lanes = c["rows"], o["lanes"]
        rh_ref[rows, lanes] = unstack(o["rt_s"] - o["ru"][:, :GW]).astype(rh_ref.dtype)
        oh_ref[rows, lanes] = unstack(o["w2"] - o["ru"][:, GW:])
        kh = unstack(o["ku"][:, :GW]).astype(bf16)
        uh = unstack(o["ku"][:, GW:]).astype(bf16)
        for pi in range(GW // PW):
            q = gi * (GW // PW) + pi
            loc = slice(pi * PW, (pi + 1) * PW)
            glo = slice(q * PW, (q + 1) * PW)
            nt_ref[ci, q] = (_tn(kh[:, loc], c["bhat"][:, glo]) * pair_blk).astype(nt_ref.dtype)
            ds_ref[ci, q] = (_tn(c["vb"][:, glo], c["khat"][:, glo]) - _tn(uh[:, loc], c["bhat"][:, glo])) * pair_blk


def _rwkv_seq_kernel(rh_ref, oh_ref, nt_ref, ds_ref, ge_ref, o_ref, s_ref):
    L, PW = RWKV_CHUNK, 2 * RWKV_HEAD_DIM
    n_chunks = rh_ref.shape[0] // L

    @pl.when(pl.program_id(0) == 0)
    def _():
        s_ref[...] = jnp.zeros_like(s_ref)

    pairs = range(RWKV_W // PW)
    lanes = [slice(q * PW, (q + 1) * PW) for q in pairs]
    s_mat = [s_ref[q] for q in pairs]
    for ci in range(n_chunks):
        rows = slice(ci * L, (ci + 1) * L)
        s_b = [s.astype(bf16) for s in s_mat]
        decay = jnp.exp(ge_ref[ci])
        for q in pairs:
            s_mat[q] = s_mat[q] * decay[:, lanes[q]] - _mm(s_b[q], nt_ref[ci, q]) + ds_ref[ci, q]
        for q in pairs:
            o_ref[rows, lanes[q]] = _nt(rh_ref[rows, lanes[q]], s_b[q]) + oh_ref[rows, lanes[q]]
    for q in pairs:
        s_ref[q] = s_mat[q]


def _rwkv_post_kernel(o_ref, g_ref, bonus_ref, lng_ref, lnb_ref, y_ref):
    inv = 1.0 / RWKV_HEAD_DIM
    o = o_ref[...]
    d = o - _head_sum(o) * inv
    var = _head_sum(d * d) * inv
    yn = d * lax.rsqrt(var + RWKV_GN_EPS) * lng_ref[...] + lnb_ref[...] + bonus_ref[...]
    y_ref[...] = (yn * g_ref[...]).astype(y_ref.dtype)


def _rwkv_scan(parts, ln_g, ln_b):
    r, ld, k, v, kk, b, g, bonus = parts
    T, W = r.shape
    L, PW = RWKV_CHUNK, 2 * RWKV_HEAD_DIM
    nc, npair = T // L, W // PW
    pre_rows = min(2 * L, T)
    seq_rows = min(4 * L, T)

    def specs(rows):
        n = rows // L
        row = pl.BlockSpec((rows, W), lambda c: (c, 0))
        mat = pl.BlockSpec((n, npair, PW, PW), lambda c: (c, 0, 0, 0))
        vec = pl.BlockSpec((n, 1, W), lambda c: (c, 0, 0))
        return row, mat, vec

    row, mat, vec = specs(pre_rows)
    rh, oh, nt, ds, ge = pl.pallas_call(
        _rwkv_pre_kernel,
        grid=(T // pre_rows,),
        in_specs=[row] * 6,
        out_specs=[row, row, mat, mat, vec],
        out_shape=[jax.ShapeDtypeStruct((T, W), bf16), jax.ShapeDtypeStruct((T, W), f32),
                   jax.ShapeDtypeStruct((nc, npair, PW, PW), bf16), jax.ShapeDtypeStruct((nc, npair, PW, PW), f32),
                   jax.ShapeDtypeStruct((nc, 1, W), f32)],
        compiler_params=_cp(("parallel",)),
        name="rwkv_pre",
    )(r, ld, k, v, kk, b)
    row, mat, vec = specs(seq_rows)
    o = pl.pallas_call(
        _rwkv_seq_kernel,
        grid=(T // seq_rows,),
        in_specs=[row, row, mat, mat, vec],
        out_specs=row,
        out_shape=jax.ShapeDtypeStruct((T, W), f32),
        scratch_shapes=[pltpu.VMEM((npair, PW, PW), f32)],
        compiler_params=_cp(("arbitrary",)),
        name="rwkv_seq",
    )(rh, oh, nt, ds, ge)
    tm = min(512, T)
    blk = pl.BlockSpec((tm, W), lambda i: (i, 0))
    one = pl.BlockSpec((1, W), lambda i: (0, 0))
    return pl.pallas_call(
        _rwkv_post_kernel,
        grid=(T // tm,),
        in_specs=[blk, blk, blk, one, one],
        out_specs=blk,
        out_shape=jax.ShapeDtypeStruct((T, W), bf16),
        compiler_params=_cp(("parallel",)),
        name="rwkv_post",
    )(o, g, bonus, ln_g.reshape(1, W), ln_b.reshape(1, W))


def _s5_dot(a, b):
    return jnp.dot(a.astype(bf16), b.astype(bf16), preferred_element_type=f32)


def _transpose_pieces(w):
    w = list(w)
    lane = lax.broadcasted_iota(jnp.int32, w[0].shape, 1)
    for d in (4, 2, 1):
        width = S5_GROUP * d
        low = (lane & width) == 0
        for i in range(8):
            if i & d == 0:
                a, b = w[i], w[i + d]
                w[i] = jnp.where(low, a, pltpu.roll(b, width, 1))
                w[i + d] = jnp.where(low, pltpu.roll(a, 128 - width, 1), b)
    return w


def _s5_split(z_refs, u_ref):
    ncb = u_ref.shape[1]
    for j, z_ref in enumerate(z_refs):
        for b in range(S5_CHUNK // 8):
            w = [z_ref[pl.ds(8 * b + i, ncb, stride=S5_CHUNK), :] for i in range(8)]
            o = _transpose_pieces(w)
            for a in range(8):
                u_ref[8 * j + a, :, 128 * b:128 * (b + 1)] = o[a]


def _toeplitz(krow):
    P, LP = krow.shape
    lane = lax.broadcasted_iota(jnp.int32, (P, LP), 1)
    blocks = [krow]
    for s in range(1, LP // P):
        blocks.append(jnp.where(lane >= P * s, pltpu.roll(krow, P * s, 1), 0.0))
    return jnp.concatenate(blocks, axis=0)


def _s5_in_kernel(u_ref, br_ref, bi_ref, xr_ref, xi_ref):
    u2 = jnp.concatenate([u_ref[0], u_ref[1]], axis=1)
    xr_ref[...] = _s5_dot(u2, br_ref[0])
    xi_ref[...] = _s5_dot(u2, bi_ref[0])


def _s5_scan_kernel(xr_ref, xi_ref, ar_ref, ai_ref, sr_ref, si_ref):
    nc = xr_ref.shape[0]
    ar = ar_ref[...]
    ai = ai_ref[...]

    def body(c, carry):
        sr, si = carry
        sr_ref[pl.ds(c, 1), :] = sr
        si_ref[pl.ds(c, 1), :] = si
        xr = xr_ref[pl.ds(c, 1), :]
        xi = xi_ref[pl.ds(c, 1), :]
        return ar * sr - ai * si + xr, ar * si + ai * sr + xi

    z = jnp.zeros_like(ar)
    lax.fori_loop(0, nc, body, (z, z))


def _s5_out_kernel(u_ref, krow_ref, sr_ref, si_ref, cr_ref, ci_ref, d_ref, o_ref):
    sr = sr_ref[...]
    si = si_ref[...]
    for j in range(2):
        u = u_ref[j]
        y = _s5_dot(u, _toeplitz(krow_ref[j])) + _s5_dot(sr, cr_ref[j]) + _s5_dot(si, ci_ref[j]) + d_ref[j] * u
        o_ref[j] = 0.5 * y * (1.0 + jnp.tanh(0.7978845608028654 * (y + 0.044715 * (y * y * y))))


def _glu_kernel(y_ref, w_ref, b_ref, o_ref, *nat_refs):
    ncb = y_ref.shape[1]
    for j, nat_ref in enumerate(nat_refs):
        for b in range(S5_CHUNK // 8):
            o = [y_ref[8 * j + a, :, 128 * b:128 * (b + 1)] for a in range(8)]
            w = _transpose_pieces(o)
            for i in range(8):
                nat_ref[pl.ds(8 * b + i, ncb, stride=S5_CHUNK), :] = w[i]
    y = jnp.concatenate([r[...] for r in nat_refs], axis=1)
    o_ref[...] = (y * _sigmoid(_bmm(y, w_ref[...]) + b_ref[...])).astype(o_ref.dtype)


def _s5_tables(a_re, a_im, log_dt, b_re, b_im, c_re, c_im, d):
    G, N, P, L = S5_GROUPS, S5_STATE, S5_GROUP, S5_CHUNK
    D = a_re.shape[0]
    dt = jnp.exp(log_dt)[:, None, :, None]
    lags = jnp.arange(L + 1, dtype=f32)[None, :, None, None]
    mag = jnp.exp(a_re[:, None] * dt * lags)
    ang = a_im[:, None] * dt * lags
    pw_re, pw_im = mag * jnp.cos(ang), mag * jnp.sin(ang)
    den = a_re * a_re + a_im * a_im
    nr, ni = pw_re[:, 1] - 1.0, pw_im[:, 1]
    coef_re = ((nr * a_re + ni * a_im) / den)[..., None]
    coef_im = ((ni * a_re - nr * a_im) / den)[..., None]
    bb_re = coef_re * b_re - coef_im * b_im
    bb_im = coef_re * b_im + coef_im * b_re
    rep = lambda a, axis: jnp.repeat(a, P, axis=axis)
    pl_re, pl_im = rep(pw_re[:, :L].transpose(0, 2, 1, 3), 2), rep(pw_im[:, :L].transpose(0, 2, 1, 3), 2)
    ct_re, ct_im = jnp.tile(c_re, (1, 1, L, 1)), jnp.tile(c_im, (1, 1, L, 1))
    cb_re, cb_im = ct_re * pl_re - ct_im * pl_im, ct_re * pl_im + ct_im * pl_re
    krow = (jnp.einsum('dgxn,dgnq->dgqx', cb_re, bb_re, precision=HI)
            - jnp.einsum('dgxn,dgnq->dgqx', cb_im, bb_im, precision=HI))
    e_re = rep(pw_re[:, L - 1::-1][:, :L].transpose(0, 2, 1, 3), 2)
    e_im = rep(pw_im[:, L - 1::-1][:, :L].transpose(0, 2, 1, 3), 2)
    bq_re = jnp.tile(bb_re.transpose(0, 1, 3, 2), (1, 1, L, 1))
    bq_im = jnp.tile(bb_im.transpose(0, 1, 3, 2), (1, 1, L, 1))
    bs_re = (e_re * bq_re - e_im * bq_im).reshape(D, G // 2, 2, L * P, N)
    bs_im = (e_re * bq_im + e_im * bq_re).reshape(D, G // 2, 2, L * P, N)

    def pair_in(bs):
        zz = jnp.zeros_like(bs[:, :, 0])
        return jnp.concatenate([jnp.concatenate([bs[:, :, 0], zz], axis=3), jnp.concatenate([zz, bs[:, :, 1]], axis=3)], axis=2)

    f_re = rep(pw_re[:, 1:].transpose(0, 2, 3, 1), 3)
    f_im = rep(pw_im[:, 1:].transpose(0, 2, 3, 1), 3)
    cn_re = jnp.tile(c_re.transpose(0, 1, 3, 2), (1, 1, 1, L))
    cn_im = jnp.tile(c_im.transpose(0, 1, 3, 2), (1, 1, 1, L))
    cs_re = (cn_re * f_re - cn_im * f_im).reshape(D, G // 2, 2, N, L * P)
    cs_im = (-(cn_re * f_im + cn_im * f_re)).reshape(D, G // 2, 2, N, L * P)

    def pair_out(cs):
        zz = jnp.zeros_like(cs[:, :, 0])
        top = jnp.concatenate([cs[:, :, 0], zz], axis=2)[:, :, None]
        bot = jnp.concatenate([zz, cs[:, :, 1]], axis=2)[:, :, None]
        return jnp.concatenate([top, bot], axis=2).reshape(D, G, 2 * N, L * P)

    dd = jnp.tile(d.reshape(D, G, 1, P), (1, 1, 1, L))
    return dict(krow=krow, bsr=pair_in(bs_re), bsi=pair_in(bs_im), cr=pair_out(cs_re), ci=pair_out(cs_im),
                al_re=pw_re[:, L].reshape(D, 1, G * N), al_im=pw_im[:, L].reshape(D, 1, G * N), d=dd)


def _s5(u, tab, layer, glu_w, glu_b):
    G, N, P, L = S5_GROUPS, S5_STATE, S5_GROUP, S5_CHUNK
    nc, LP = u.shape[1:]
    T = nc * L
    tm = min(1024, T)
    grp = lambda s: pl.BlockSpec((2,) + s, lambda j: (j, 0, 0))
    tgrp = lambda s: pl.BlockSpec((None, 2) + s, lambda j: (layer, j, 0, 0))
    tone = lambda s: pl.BlockSpec((None, 1) + s, lambda j: (layer, j, 0, 0))
    lanes = pl.BlockSpec((nc, 2 * N), lambda j: (0, j))
    xr, xi = pl.pallas_call(
        _s5_in_kernel,
        grid=(G // 2,),
        in_specs=[grp((nc, LP)), tone((2 * LP, 2 * N)), tone((2 * LP, 2 * N))],
        out_specs=[lanes, lanes],
        out_shape=[jax.ShapeDtypeStruct((nc, G * N), f32), jax.ShapeDtypeStruct((nc, G * N), f32)],
        compiler_params=_cp(("parallel",)),
        name="s5_in",
    )(u, tab["bsr"], tab["bsi"])
    whole = lambda s: pl.BlockSpec(s, lambda i: (0,) * len(s))
    sr, si = pl.pallas_call(
        _s5_scan_kernel,
        grid=(1,),
        in_specs=[whole((nc, G * N)), whole((nc, G * N)), pl.BlockSpec((None, 1, G * N), lambda i: (layer, 0, 0)),
                  pl.BlockSpec((None, 1, G * N), lambda i: (layer, 0, 0))],
        out_specs=[whole((nc, G * N)), whole((nc, G * N))],
        out_shape=[jax.ShapeDtypeStruct((nc, G * N), f32)] * 2,
        compiler_params=_cp(("arbitrary",)),
        name="s5_scan",
    )(xr, xi, tab["al_re"], tab["al_im"])
    y = pl.pallas_call(
        _s5_out_kernel,
        grid=(G // 2,),
        in_specs=[grp((nc, LP)), tgrp((P, LP)), lanes, lanes, tgrp((2 * N, LP)), tgrp((2 * N, LP)), tgrp((1, LP))],
        out_specs=grp((nc, LP)),
        out_shape=jax.ShapeDtypeStruct((G, nc, LP), f32),
        compiler_params=_cp(("parallel",)),
        name="s5_out",
    )(u, tab["krow"], sr, si, tab["cr"], tab["ci"], tab["d"])
    return pl.pallas_call(
        _glu_kernel,
        grid=(T // tm,),
        in_specs=[pl.BlockSpec((G, tm // L, LP), lambda i: (0, i, 0)), pl.BlockSpec((S5_W, S5_W), lambda i: (0, 0)),
                  pl.BlockSpec((1, S5_W), lambda i: (0, 0))],
        out_specs=pl.BlockSpec((tm, S5_W), lambda i: (i, 0)),
        out_shape=jax.ShapeDtypeStruct((T, S5_W), bf16),
        scratch_shapes=[pltpu.VMEM((tm, 128), f32)] * (S5_W // 128),
        compiler_params=_cp(("parallel",)),
        name="s5_glu",
    )(y, glu_w, glu_b.reshape(1, S5_W))


def _pad_rows(w, n):
    return jnp.pad(w, ((0, n - w.shape[0]), (0, 0)))


def _split_mu(mu):
    l0 = 3 * RWKV_W
    return jnp.concatenate([
        mu[:l0], jnp.pad(mu[l0:l0 + DECAY_LORA], (0, LORA_PAD - DECAY_LORA)),
        jnp.pad(mu[l0 + DECAY_LORA:l0 + DECAY_LORA + AAA_LORA], (0, LORA_PAD - AAA_LORA)),
        mu[l0 + DECAY_LORA + AAA_LORA:]]).reshape(1, RWKV_PAD)


def kernel(x, p, norm_mix_g, w_in, mlstm_conv, mlstm_ib, mlstm_fb, mlstm_norm_g, rwkv_mu, rwkv_w0, rwkv_w2, rwkv_a0, rwkv_a2, rwkv_g2, rwkv_kk, rwkv_ka, rwkv_rk, rwkv_ln_g, rwkv_ln_b, s5_a_re, s5_a_im, s5_log_dt, s5_b_re, s5_b_im, s5_c_re, s5_c_im, s5_d, s5_glu_w, s5_glu_b, w_up_m, w_up_r, w_up_s, w_out, norm_ffn_g, ffn_w_gate, ffn_w_up, ffn_w_down, norm_ple_g, ple_w_gate, ple_w_proj, final_norm_g):
    B, T, D = x.shape
    depth = w_in.shape[0]
    w_out_b, ple_wg_b, ple_wp_b, p_b = (a.astype(bf16) for a in (w_out, ple_w_gate, ple_w_proj, p))
    wt = jnp.swapaxes(w_in, 1, 2)
    s5_tab = _s5_tables(s5_a_re, s5_a_im, s5_log_dt, s5_b_re, s5_b_im, s5_c_re, s5_c_im, s5_d)
    outs = []
    for bi in range(B):
        h = x[bi]
        xn = _rmsnorm(h, norm_mix_g[0], bf16)
        for i in range(depth):
            z_main = _proj(xn, wt, i, 0, 4, MLSTM_W)
            z_gate = _proj(xn, wt, i, M_IN + RWKV_IN + S5_W, 3 * D_MODEL // 1024, 1024, bf16, gate=True)
            u_s5, z_if = _proj_tail(xn, wt, i)

            gate_b = jnp.pad(jnp.concatenate([mlstm_ib[i], mlstm_fb[i]]), (0, 128 - 2 * MLSTM_HEADS)).reshape(1, 128)
            y_m = _mlstm(z_main, z_if, mlstm_conv[i], gate_b, mlstm_norm_g[i])

            vec = lambda a: a.reshape(1, RWKV_W)
            parts = _rwkv_prep(xn, _rwkv_w(wt, i), _split_mu(rwkv_mu[i]), vec(rwkv_w0[i]), vec(rwkv_a0[i]), vec(rwkv_kk[i]),
                               vec(rwkv_ka[i]), vec(rwkv_rk[i]), _pad_rows(rwkv_w2[i], LORA_PAD).astype(bf16),
                               _pad_rows(rwkv_a2[i], LORA_PAD).astype(bf16), rwkv_g2[i].astype(bf16))
            y_r = _rwkv_scan(parts, rwkv_ln_g[i], rwkv_ln_b[i])

            y_s = _s5(u_s5, s5_tab, i, s5_glu_w[i].astype(bf16), s5_glu_b[i])

            mixed = _merge(y_m, y_r, y_s, z_gate, w_up_m, w_up_r, w_up_s, i)
            h, hn = _resid_norm(mixed, w_out_b, i, h, norm_ffn_g[i])
            act = _ffn_up(hn, ffn_w_gate, ffn_w_up, i)
            h = _resid_mm(act, ffn_w_down, h, i)
            last = i == depth - 1
            g_next = final_norm_g if last else norm_mix_g[i + 1]
            h, xn = _ple(h, p_b[:, bi:bi + 1], ple_wg_b, ple_wp_b, i, norm_ple_g[i], g_next, f32 if last else bf16)
        outs.append(xn)
    return jnp.stack(outs)
```

```python
import functools

import jax
import jax.numpy as jnp
import numpy as np
from jax import lax
from jax.experimental import pallas as pl
from jax.experimental.pallas import tpu as pltpu

f32 = jnp.float32
bf16 = jnp.bfloat16
HI = lax.Precision.HIGHEST

D_MODEL = 2048
PLE_DIM = 256
RMS_EPS = 1e-6
MLSTM_HEADS = 4
MLSTM_HEAD_DIM = 256
MLSTM_W = 1024
MLSTM_CHUNK = 256
GATE_SOFTCAP = 15.0
RWKV_HEAD_DIM = 64
RWKV_W = 512
DECAY_LORA = 96
AAA_LORA = 96
GATE_LORA = 256
LORA_PAD = 128
RWKV_GN_EPS = 64e-5
RWKV_CHUNK = 64
S5_GROUP = 16
S5_GROUPS = 32
S5_W = 512
S5_STATE = 64
S5_CHUNK = 16
FFN_HIDDEN = 5632
M_IN = 4 * MLSTM_W + 2 * MLSTM_HEADS
RWKV_IN = 3 * RWKV_W + DECAY_LORA + AAA_LORA + GATE_LORA
RWKV_PAD = 3 * RWKV_W + 2 * LORA_PAD + GATE_LORA
Z_TAIL = S5_W + 128
VMEM_LIMIT = 56 * 1024 * 1024


def _cp(sem):
    return pltpu.CompilerParams(dimension_semantics=sem, vmem_limit_bytes=VMEM_LIMIT)


def _nt(a, b, **kw):
    return lax.dot_general(a, b, (((1,), (1,)), ((), ())), preferred_element_type=f32, **kw)


def _tn(a, b, **kw):
    return lax.dot_general(a, b, (((0,), (0,)), ((), ())), preferred_element_type=f32, **kw)


def _mm(a, b, **kw):
    return jnp.dot(a, b, preferred_element_type=f32, **kw)


def _bmm(a, b):
    return jnp.dot(a.astype(bf16), b.astype(bf16), preferred_element_type=f32)


def _sigmoid(x):
    return 1.0 / (1.0 + jnp.exp(-x))


def _rms(x, g):
    return x * lax.rsqrt(jnp.mean(x * x, axis=-1, keepdims=True) + RMS_EPS) * g


def _norm_kernel(x_ref, g_ref, o_ref):
    o_ref[...] = _rms(x_ref[...], g_ref[...]).astype(o_ref.dtype)


def _rmsnorm(x, g, out_dtype):
    T, D = x.shape
    tm = min(512, T)
    return pl.pallas_call(
        _norm_kernel,
        grid=(T // tm,),
        in_specs=[pl.BlockSpec((tm, D), lambda i: (i, 0)), pl.BlockSpec((1, D), lambda i: (0, 0))],
        out_specs=pl.BlockSpec((tm, D), lambda i: (i, 0)),
        out_shape=jax.ShapeDtypeStruct((T, D), out_dtype),
        compiler_params=_cp(("parallel",)),
        name="rmsnorm",
    )(x, g.reshape(1, D))


def _pack_rows(w, rows):
    parts = [w[r[0]:r[1], :] if isinstance(r, tuple) else jnp.zeros((r, w.shape[1]), f32) for r in rows]
    return jnp.concatenate(parts, axis=0).astype(bf16)


def _proj_kernel(a_ref, w_ref, o_ref, wb_ref, *, gate):
    @pl.when(pl.program_id(1) == 0)
    def _():
        wb_ref[...] = w_ref[0].astype(bf16)

    z = _nt(a_ref[...], wb_ref[...])
    o_ref[...] = (0.5 * jnp.tanh(0.5 * z) + 0.5 if gate else z).astype(o_ref.dtype)


def _proj(a, wt, layer, row0, n_tiles, tn, out_dtype=f32, gate=False):
    T, K = a.shape
    tm = min(1024, T)
    return pl.pallas_call(
        functools.partial(_proj_kernel, gate=gate),
        grid=(n_tiles, T // tm),
        in_specs=[pl.BlockSpec((tm, K), lambda j, i: (i, 0)),
                  pl.BlockSpec((pl.Element(1), pl.Element(tn), pl.Element(K)),
                               lambda j, i: (layer, pl.multiple_of(row0 + tn * j, 8), 0))],
        out_specs=pl.BlockSpec((tm, tn), lambda j, i: (i, j)),
        out_shape=jax.ShapeDtypeStruct((T, n_tiles * tn), out_dtype),
        scratch_shapes=[pltpu.VMEM((tn, K), bf16)],
        compiler_params=_cp(("parallel", "arbitrary")),
        name="in_proj",
    )(a, wt)


def _proj_tail_kernel(a_ref, ws_ref, wg_ref, u_ref, gz_ref, wb_ref, *z_refs):
    @pl.when(pl.program_id(0) == 0)
    def _():
        pad = jnp.zeros((Z_TAIL - S5_W - wg_ref.shape[0], wg_ref.shape[1]), f32)
        wb_ref[...] = jnp.concatenate([ws_ref[0], wg_ref[...], pad], axis=0).astype(bf16)

    z = _nt(a_ref[...], wb_ref[...])
    gz_ref[...] = z[:, S5_W:]
    for j, z_ref in enumerate(z_refs):
        z_ref[...] = z[:, 128 * j:128 * (j + 1)]
    _s5_split(z_refs, u_ref)


def _proj_tail(a, wt, layer):
    T, K = a.shape
    tm = min(1024, T)
    n_if = 2 * MLSTM_HEADS
    L = S5_CHUNK
    return pl.pallas_call(
        _proj_tail_kernel,
        grid=(T // tm,),
        in_specs=[pl.BlockSpec((tm, K), lambda i: (i, 0)),
                  pl.BlockSpec((pl.Element(1), pl.Element(S5_W), pl.Element(K)), lambda i: (layer, M_IN + RWKV_IN, 0)),
                  pl.BlockSpec((None, n_if, K), lambda i: (layer, 4 * MLSTM_W // n_if, 0))],
        out_specs=[pl.BlockSpec((S5_GROUPS, tm // L, L * S5_GROUP), lambda i: (0, i, 0)),
                   pl.BlockSpec((tm, Z_TAIL - S5_W), lambda i: (i, 0))],
        out_shape=[jax.ShapeDtypeStruct((S5_GROUPS, T // L, L * S5_GROUP), f32),
                   jax.ShapeDtypeStruct((T, Z_TAIL - S5_W), f32)],
        scratch_shapes=[pltpu.VMEM((Z_TAIL, K), bf16)] + [pltpu.VMEM((tm, 128), f32)] * (S5_W // 128),
        compiler_params=_cp(("arbitrary",)),
        name="in_proj_tail",
    )(a, wt, wt)


def _merge_kernel(ym_ref, yr_ref, ys_ref, gm_ref, gr_ref, gs_ref, um_ref, ur_ref, us_ref, o_ref, umb_ref, urb_ref, usb_ref):
    @pl.when(pl.program_id(1) == 0)
    def _():
        umb_ref[...] = um_ref[...].astype(bf16)
        urb_ref[...] = ur_ref[...].astype(bf16)
        usb_ref[...] = us_ref[...].astype(bf16)

    acc = gm_ref[...] * _mm(ym_ref[...], umb_ref[...])
    acc += gr_ref[...] * _mm(yr_ref[...], urb_ref[...])
    acc += gs_ref[...] * _mm(ys_ref[...], usb_ref[...])
    o_ref[...] = acc.astype(o_ref.dtype)


def _merge(ym, yr, ys, gates, um, ur, us, layer):
    T = ym.shape[0]
    tm, tn = min(1024, T), 1024
    nj = D_MODEL // tn

    def gate_spec(b):
        return pl.BlockSpec((tm, tn), lambda j, i, b=b: (i, b * nj + j))

    def y_spec(w):
        return pl.BlockSpec((tm, w), lambda j, i: (i, 0))

    def u_spec(w):
        return pl.BlockSpec((None, w, tn), lambda j, i: (layer, 0, j))

    return pl.pallas_call(
        _merge_kernel,
        grid=(nj, T // tm),
        in_specs=[y_spec(MLSTM_W), y_spec(RWKV_W), y_spec(S5_W), gate_spec(0), gate_spec(1), gate_spec(2),
                  u_spec(MLSTM_W), u_spec(RWKV_W), u_spec(S5_W)],
        out_specs=pl.BlockSpec((tm, tn), lambda j, i: (i, j)),
        out_shape=jax.ShapeDtypeStruct((T, D_MODEL), bf16),
        scratch_shapes=[pltpu.VMEM((MLSTM_W, tn), bf16), pltpu.VMEM((RWKV_W, tn), bf16), pltpu.VMEM((S5_W, tn), bf16)],
        compiler_params=_cp(("parallel", "arbitrary")),
        name="gated_merge",
    )(ym, yr, ys, gates, gates, gates, um, ur, us)


def _resid_norm_kernel(a_ref, w_ref, h_ref, g_ref, ho_ref, no_ref):
    hn = h_ref[...] + _mm(a_ref[...], w_ref[...])
    ho_ref[...] = hn
    no_ref[...] = _rms(hn, g_ref[...]).astype(no_ref.dtype)


def _resid_norm(a, w, layer, h, g, norm_dtype=bf16):
    T, K = a.shape
    tm = min(512, T)
    row = lambda n: pl.BlockSpec((tm, n), lambda i: (i, 0))
    return pl.pallas_call(
        _resid_norm_kernel,
        grid=(T // tm,),
        in_specs=[row(K), pl.BlockSpec((None, K, D_MODEL), lambda i: (layer, 0, 0)), row(D_MODEL),
                  pl.BlockSpec((1, D_MODEL), lambda i: (0, 0))],
        out_specs=[row(D_MODEL), row(D_MODEL)],
        out_shape=[jax.ShapeDtypeStruct((T, D_MODEL), f32), jax.ShapeDtypeStruct((T, D_MODEL), norm_dtype)],
        compiler_params=_cp(("parallel",)),
        name="resid_matmul_norm",
    )(a, w, h, g.reshape(1, D_MODEL))


def _ffn_up_kernel(a_ref, wg_ref, wu_ref, o_ref, wgb_ref, wub_ref):
    @pl.when(pl.program_id(1) == 0)
    def _():
        wgb_ref[...] = wg_ref[...].astype(bf16)
        wub_ref[...] = wu_ref[...].astype(bf16)

    a = a_ref[...]
    gt = _mm(a, wgb_ref[...])
    up = _mm(a, wub_ref[...])
    o_ref[...] = (gt * _sigmoid(gt) * up).astype(o_ref.dtype)


def _ffn_up(a, wg, wu, layer):
    T, K = a.shape
    N = wg.shape[2]
    tm, tn = min(1024, T), 512
    return pl.pallas_call(
        _ffn_up_kernel,
        grid=(N // tn, T // tm),
        in_specs=[pl.BlockSpec((tm, K), lambda j, i: (i, 0)), pl.BlockSpec((None, K, tn), lambda j, i: (layer, 0, j)),
                  pl.BlockSpec((None, K, tn), lambda j, i: (layer, 0, j))],
        out_specs=pl.BlockSpec((tm, tn), lambda j, i: (i, j)),
        out_shape=jax.ShapeDtypeStruct((T, N), bf16),
        scratch_shapes=[pltpu.VMEM((K, tn), bf16), pltpu.VMEM((K, tn), bf16)],
        compiler_params=_cp(("parallel", "arbitrary")),
        name="ffn_up",
    )(a, wg, wu)


def _resid_mm_kernel(a_ref, w_ref, h_ref, o_ref, wb_ref):
    @pl.when(pl.program_id(1) == 0)
    def _():
        wb_ref[...] = w_ref[...].astype(bf16)

    o_ref[...] = h_ref[...] + _mm(a_ref[...], wb_ref[...])


def _resid_mm(a, w, h, layer):
    T, K = a.shape
    N = w.shape[2]
    tm, tn = min(512, T), 512
    return pl.pallas_call(
        _resid_mm_kernel,
        grid=(N // tn, T // tm),
        in_specs=[pl.BlockSpec((tm, K), lambda j, i: (i, 0)), pl.BlockSpec((None, K, tn), lambda j, i: (layer, 0, j)),
                  pl.BlockSpec((tm, tn), lambda j, i: (i, j))],
        out_specs=pl.BlockSpec((tm, tn), lambda j, i: (i, j)),
        out_shape=jax.ShapeDtypeStruct((T, N), f32),
        scratch_shapes=[pltpu.VMEM((K, tn), bf16)],
        compiler_params=_cp(("parallel", "arbitrary")),
        name="resid_matmul",
    )(a, w, h)


def _ple_kernel(h_ref, p_ref, wg_ref, wp_ref, gp_ref, g_ref, ho_ref, no_ref):
    tm = h_ref.shape[0]
    parts = [slice(0, tm // 2), slice(tm // 2, tm)]
    hs = [h_ref[r, :] for r in parts]
    hps = [_rms(h, gp_ref[...]).astype(bf16) for h in hs]
    gates = [_sigmoid(_mm(hp, wg_ref[...])) for hp in hps]
    projs = [_mm(p_ref[r, :], wp_ref[...]) for r in parts]
    for r, h, gate, proj in zip(parts, hs, gates, projs):
        hn = h + proj * gate
        ho_ref[r, :] = hn
        no_ref[r, :] = _rms(hn, g_ref[...]).astype(no_ref.dtype)


def _ple(h, p, wg, wp, layer, g_ple, g, norm_dtype):
    T = h.shape[0]
    tm = min(512, T)
    row = lambda w: pl.BlockSpec((tm, w), lambda i: (i, 0))
    full = lambda s: pl.BlockSpec(s, lambda i: (0, 0))
    return pl.pallas_call(
        _ple_kernel,
        grid=(T // tm,),
        in_specs=[row(D_MODEL), pl.BlockSpec((None, None, tm, PLE_DIM), lambda i: (layer, 0, i, 0)),
                  pl.BlockSpec((None, D_MODEL, D_MODEL), lambda i: (layer, 0, 0)),
                  pl.BlockSpec((None, PLE_DIM, D_MODEL), lambda i: (layer, 0, 0)), full((1, D_MODEL)), full((1, D_MODEL))],
        out_specs=[row(D_MODEL), row(D_MODEL)],
        out_shape=[jax.ShapeDtypeStruct((T, D_MODEL), f32), jax.ShapeDtypeStruct((T, D_MODEL), norm_dtype)],
        compiler_params=_cp(("parallel",)),
        name="ple_norm",
    )(h, p, wg, wp, g_ple.reshape(1, D_MODEL), g.reshape(1, D_MODEL))


def _mlstm_kernel(zq_ref, zk_ref, v_ref, o_ref, gz_ref, cq_ref, ck_ref, gb_ref, ng_ref, y_ref,
                  ct_ref, n_ref, qbuf_ref, kbuf_ref):
    c = pl.program_id(0)
    L, W = zq_ref.shape
    H, dh = MLSTM_HEADS, MLSTM_HEAD_DIM
    heads = range(H)

    @pl.when(c == 0)
    def _():
        ct_ref[...] = jnp.zeros_like(ct_ref)
        n_ref[...] = jnp.zeros_like(n_ref)
        qbuf_ref[...] = jnp.zeros_like(qbuf_ref)
        kbuf_ref[...] = jnp.zeros_like(kbuf_ref)

    def conv_silu(z_ref, w_ref, buf_ref):
        x = z_ref[...]
        xx = jnp.concatenate([buf_ref[...], x], axis=0)
        w = w_ref[...]
        acc = x * w[0:1, :]
        for j in range(1, w.shape[0]):
            acc = acc + pltpu.roll(xx, j, 0)[8:, :] * w[j:j + 1, :]
        buf_ref[...] = x[L - 8:, :]
        return acc * _sigmoid(acc)

    q = conv_silu(zq_ref, cq_ref, qbuf_ref) * (dh ** -0.5)
    k = conv_silu(zk_ref, ck_ref, kbuf_ref)

    g = gz_ref[...] + gb_ref[...]
    sc = GATE_SOFTCAP * jnp.tanh(g / GATE_SOFTCAP)
    logf = -jnp.log(1.0 + jnp.exp(-sc))
    row = lax.broadcasted_iota(jnp.int32, (L, L), 0)
    col = lax.broadcasted_iota(jnp.int32, (L, L), 1)
    causal = row >= col
    bcs = _mm(causal.astype(f32), logf, precision=HI)
    b128 = pltpu.roll(bcs, 128 - H, 1)
    e128 = sc - b128
    e_t = e128.T

    sl = [slice(h * dh, (h + 1) * dh) for h in heads]
    b_col = [b128[:, h:h + 1] for h in heads]
    e_col = [e128[:, h:h + 1] for h in heads]
    wmat = [jnp.where(causal, jnp.exp(jnp.where(causal, b_col[h] + e_t[h:h + 1, :], 0.0)), 0.0) for h in heads]
    b_end = [b_col[h][L - 1:L, :] for h in heads]
    eb = [jnp.exp(b_col[h]) for h in heads]
    qh = [q[:, sl[h]] for h in heads]
    kh = [k[:, sl[h]] for h in heads]
    qb = [x.astype(bf16) for x in qh]
    kb = [x.astype(bf16) for x in kh]
    vb = [v_ref[:, sl[h]].astype(bf16) for h in heads]
    ct = [ct_ref[h] for h in heads]
    nn = [n_ref[:, sl[h]] for h in heads]
    s = [_nt(qb[h], kb[h]) * wmat[h] for h in heads]
    inter = [_mm(qb[h], ct[h].astype(bf16)) for h in heads]
    num = [_mm(s[h].astype(bf16), vb[h]) + eb[h] * inter[h] for h in heads]
    ng = ng_ref[...]
    for h in heads:
        den = jnp.sum(s[h], axis=1, keepdims=True) + eb[h] * jnp.sum(qh[h] * nn[h], axis=1, keepdims=True)
        hh = num[h] / jnp.maximum(jnp.abs(den), 1.0)
        hh = hh * lax.rsqrt(jnp.mean(hh * hh, axis=1, keepdims=True) + RMS_EPS) * ng[:, sl[h]]
        y_ref[:, sl[h]] = (_sigmoid(o_ref[:, sl[h]]) * hh).astype(y_ref.dtype)
    kw = [kh[h] * jnp.exp(b_end[h] + e_col[h]) for h in heads]
    upd = [_tn(kw[h].astype(bf16), vb[h]) for h in heads]
    for h in heads:
        decay = jnp.exp(b_end[h])
        ct_ref[h] = decay * ct[h] + upd[h]
        n_ref[:, sl[h]] = decay * nn[h] + jnp.sum(kw[h], axis=0, keepdims=True)


def _mlstm(z_main, z_if, conv_w, gate_b, norm_g):
    T = z_main.shape[0]
    L, dh, H, W = min(MLSTM_CHUNK, T), MLSTM_HEAD_DIM, MLSTM_HEADS, MLSTM_W
    blk = lambda j: pl.BlockSpec((L, W), lambda c, j=j: (c, j))
    return pl.pallas_call(
        _mlstm_kernel,
        grid=(T // L,),
        in_specs=[blk(0), blk(1), blk(2), blk(3),
                  pl.BlockSpec((L, 128), lambda c: (c, 0)),
                  pl.BlockSpec((4, W), lambda c: (0, 0)), pl.BlockSpec((4, W), lambda c: (0, 1)),
                  pl.BlockSpec((1, 128), lambda c: (0, 0)), pl.BlockSpec((1, W), lambda c: (0, 0))],
        out_specs=pl.BlockSpec((L, W), lambda c: (c, 0)),
        out_shape=jax.ShapeDtypeStruct((T, W), bf16),
        scratch_shapes=[pltpu.VMEM((H, dh, dh), f32), pltpu.VMEM((1, W), f32),
                        pltpu.VMEM((8, W), f32), pltpu.VMEM((8, W), f32)],
        compiler_params=_cp(("arbitrary",)),
        name="mlstm",
    )(z_main, z_main, z_main, z_main, z_if, conv_w, conv_w, gate_b, norm_g.reshape(1, W))


def _iota_div(shape, dim, width):
    return lax.shift_right_logical(lax.broadcasted_iota(jnp.int32, shape, dim), int(np.log2(width)))


def _head_ones(n, width):
    return (_iota_div((n, n), 0, width) == _iota_div((n, n), 1, width)).astype(f32)


def _head_sum(x):
    blk = _head_ones(RWKV_W, RWKV_HEAD_DIM).astype(bf16)
    hi = x.astype(bf16)
    lo = (x - hi.astype(f32)).astype(bf16)
    return _mm(hi, blk) + _mm(lo, blk)


def _rwkv_w_kernel(w_ref, o_ref, *, last_rows):
    j = pl.program_id(0)
    nj = pl.num_programs(0)

    @pl.when(j < nj - 1)
    def _():
        o_ref[...] = w_ref[0].astype(bf16)

    @pl.when(j == nj - 1)
    def _():
        o_ref[...] = _pack_rows(w_ref[0], last_rows)


def _rwkv_w(wt, layer):
    K = wt.shape[2]
    tn = RWKV_W
    lora_rows = [(0, DECAY_LORA), LORA_PAD - DECAY_LORA, (DECAY_LORA, DECAY_LORA + AAA_LORA), LORA_PAD - AAA_LORA,
                 (DECAY_LORA + AAA_LORA, RWKV_IN - 3 * RWKV_W)]
    return pl.pallas_call(
        functools.partial(_rwkv_w_kernel, last_rows=lora_rows),
        grid=(RWKV_PAD // tn,),
        in_specs=[pl.BlockSpec((pl.Element(1), pl.Element(tn), pl.Element(K)),
                               lambda j: (layer, pl.multiple_of(M_IN + tn * j, 8), 0))],
        out_specs=pl.BlockSpec((tn, K), lambda j: (j, 0)),
        out_shape=jax.ShapeDtypeStruct((RWKV_PAD, K), bf16),
        compiler_params=_cp(("parallel",)),
        name="rwkv_w",
    )(wt)


def _rwkv_prep_kernel(a_ref, w_ref, mu_ref, w0_ref, a0_ref, kkg_ref, ka_ref, rk_ref, w2_ref, a2_ref, g2_ref,
                      r_ref, ld_ref, k_ref, v_ref, kk_ref, b_ref, g_ref, bonus_ref, *buf_refs):
    i = pl.program_id(0)
    nh = len(buf_refs)
    tm = a_ref.shape[0] // nh
    W = RWKV_W

    @pl.when(i == 0)
    def _():
        buf_refs[0][0:8, :] = jnp.zeros((8, buf_refs[0].shape[1]), f32)

    rows = [slice(h * tm, (h + 1) * tm) for h in range(nh)]
    xs_all = [_nt(a_ref[rw, :], w_ref[...]) for rw in rows]
    for h, (rw, x) in enumerate(zip(rows, xs_all)):
        buf_ref = buf_refs[h]
        buf_ref[8:, :] = x
        buf_refs[(h + 1) % nh][0:8, :] = x[tm - 8:, :]
        xs = x + (buf_ref[pl.ds(7, tm), :] - x) * mu_ref[...]
        r = xs[:, 0:W]
        k = xs[:, W:2 * W]
        v = xs[:, 2 * W:3 * W]
        wl = xs[:, 3 * W:3 * W + LORA_PAD]
        al = xs[:, 3 * W + LORA_PAD:3 * W + 2 * LORA_PAD]
        gl = xs[:, 3 * W + 2 * LORA_PAD:]
        t = w0_ref[...] + _bmm(jnp.tanh(wl), w2_ref[...])
        w = -(jnp.maximum(-t, 0.0) + jnp.log(1.0 + jnp.exp(-jnp.abs(t)))) - 0.5
        a = _sigmoid(a0_ref[...] + _bmm(al, a2_ref[...]))
        g = _bmm(_sigmoid(gl), g2_ref[...])
        kk = k * kkg_ref[...]
        ss = _head_sum(kk * kk)
        kk = kk / jnp.maximum(jnp.sqrt(ss), 1e-12)
        k2 = k * (1.0 + (a - 1.0) * ka_ref[...])
        bonus = _head_sum(r * k2 * rk_ref[...]) * v
        r_ref[rw, :] = r
        ld_ref[rw, :] = -jnp.exp(w)
        k_ref[rw, :] = k2
        v_ref[rw, :] = v
        kk_ref[rw, :] = kk
        b_ref[rw, :] = kk * a
        g_ref[rw, :] = g
        bonus_ref[rw, :] = bonus


def _rwkv_prep(xn, w_rwkv, mu, w0, a0, kkg, ka, rk, w2, a2, g2):
    T, K = xn.shape
    tm = min(512, T)
    n_sub = 2
    W = RWKV_W
    vec = lambda n: pl.BlockSpec((1, n), lambda i: (0, 0))
    mat = lambda s: pl.BlockSpec(s, lambda i: (0, 0))
    out = pl.BlockSpec((tm, W), lambda i: (i, 0))
    return pl.pallas_call(
        _rwkv_prep_kernel,
        grid=(T // tm,),
        in_specs=[pl.BlockSpec((tm, K), lambda i: (i, 0)), mat((RWKV_PAD, K)), vec(RWKV_PAD), vec(W), vec(W),
                  vec(W), vec(W), vec(W), mat((LORA_PAD, W)), mat((LORA_PAD, W)), mat((GATE_LORA, W))],
        out_specs=[out] * 8,
        out_shape=[jax.ShapeDtypeStruct((T, W), f32)] * 8,
        scratch_shapes=[pltpu.VMEM((tm // n_sub + 8, RWKV_PAD), f32)] * n_sub,
        compiler_params=_cp(("arbitrary",)),
        name="rwkv_prep",
    )(xn, w_rwkv, mu, w0, a0, kkg, ka, rk, w2, a2, g2)


def _rwkv_pre_kernel(r_ref, ld_ref, k_ref, v_ref, kk_ref, b_ref, rh_ref, oh_ref, nt_ref, ds_ref, ge_ref):
    L, dh = RWKV_CHUNK, RWKV_HEAD_DIM
    GW = 4 * dh
    PW = 2 * dh
    n_chunks = r_ref.shape[0] // L
    rr = lax.broadcasted_iota(jnp.int32, (GW, GW), 0)
    cc = lax.broadcasted_iota(jnp.int32, (GW, GW), 1)
    same = _iota_div((GW, GW), 0, L) == _iota_div((GW, GW), 1, dh)
    strict = same & (rr > cc)
    incl = same & (rr >= cc)
    eye = jnp.where(rr == cc, 1.0, 0.0)
    pair_blk = _head_ones(PW, dh)
    tr = lax.broadcasted_iota(jnp.int32, (L, L), 0)
    tc = lax.broadcasted_iota(jnp.int32, (L, L), 1)
    tril = (tr >= tc).astype(f32)

    def unstack(x):
        acc = x[0:L, :]
        for hh in range(1, GW // L):
            acc = acc + x[hh * L:(hh + 1) * L, :]
        return acc

    def tile(x):
        return jnp.concatenate([x] * (GW // L), axis=0)

    def stack(x):
        return jnp.where(same, tile(x), 0.0)

    tiles = [(ci, gi) for ci in range(n_chunks) for gi in range(RWKV_W // GW)]
    ch = []
    for ci in range(n_chunks):
        rows = slice(ci * L, (ci + 1) * L)
        ld = ld_ref[rows, :]
        lg = _mm(tril, ld, precision=HI)
        g_end = lg[L - 1:L, :]
        ge_ref[ci] = g_end
        dec_out = jnp.exp(-lg)
        dec_tail = jnp.exp(g_end - lg)
        k = k_ref[rows, :]
        b = b_ref[rows, :]
        v = v_ref[rows, :]
        ch.append(dict(rows=rows, v=v, kap=kk_ref[rows, :] * jnp.exp(lg - ld), rt=r_ref[rows, :] * jnp.exp(lg),
                       kt=k * dec_out, bt=b * dec_out, khat=(k * dec_tail).astype(bf16),
                       bhat=(b * dec_tail).astype(bf16), vb=v.astype(bf16)))
    ops = []
    for ci, gi in tiles:
        c = ch[ci]
        lanes = slice(gi * GW, (gi + 1) * GW)
        kap_s = stack(c["kap"][:, lanes])
        rt_s = stack(c["rt"][:, lanes])
        ops.append(dict(lanes=lanes, kap_s=kap_s, rt_s=rt_s, kap_b=kap_s.astype(bf16), rt_b=rt_s.astype(bf16),
                        v_s=stack(c["v"][:, lanes]).astype(bf16), kt_t=tile(c["kt"][:, lanes]).astype(bf16),
                        bt_t=tile(c["bt"][:, lanes]).astype(bf16)))
    for o in ops:
        o["a_kb"] = jnp.where(strict, _nt(o["kap_b"], o["bt_t"]), 0.0)
    for o in ops:
        o["a_kk"] = jnp.where(strict, _nt(o["kap_b"], o["kt_t"]), 0.0).astype(bf16)
        o["a_rk"] = jnp.where(incl, _nt(o["rt_b"], o["kt_t"]), 0.0).astype(bf16)
        o["a_rb"] = jnp.where(incl, _nt(o["rt_b"], o["bt_t"]), 0.0).astype(bf16)
    for o in ops:
        o["p"] = -o["a_kb"]
        o["tinv"] = eye + o["p"]
    step = 2
    while step < L:
        for o in ops:
            pb = o["p"].astype(bf16)
            o["p"] = _mm(pb, pb)
        for o in ops:
            o["tinv"] = o["tinv"] + _bmm(o["tinv"], o["p"])
        step *= 2
    for o in ops:
        o["w1"] = _mm(o["a_kk"], o["v_s"])
        o["w2"] = _mm(o["a_rk"], o["v_s"])
    for o in ops:
        o["ku"] = _bmm(o["tinv"], jnp.concatenate([o["kap_s"], o["w1"]], axis=1))
    for o in ops:
        o["ru"] = _mm(o["a_rb"], o["ku"].astype(bf16))
    for (ci, gi), o in zip(tiles, ops):
        c = ch[ci]
        rows, lanes = c["rows"], o["lanes"]
        rh_ref[rows, lanes] = unstack(o["rt_s"] - o["ru"][:, :GW]).astype(rh_ref.dtype)
        oh_ref[rows, lanes] = unstack(o["w2"] - o["ru"][:, GW:])
        kh = unstack(o["ku"][:, :GW]).astype(bf16)
        uh = unstack(o["ku"][:, GW:]).astype(bf16)
        for pi in range(GW // PW):
            q = gi * (GW // PW) + pi
            loc = slice(pi * PW, (pi + 1) * PW)
            glo = slice(q * PW, (q + 1) * PW)
            nt_ref[ci, q] = (_tn(kh[:, loc], c["bhat"][:, glo]) * pair_blk).astype(nt_ref.dtype)
            ds_ref[ci, q] = (_tn(c["vb"][:, glo], c["khat"][:, glo]) - _tn(uh[:, loc], c["bhat"][:, glo])) * pair_blk


def _rwkv_seq_kernel(rh_ref, oh_ref, nt_ref, ds_ref, ge_ref, o_ref, s_ref):
    L, PW = RWKV_CHUNK, 2 * RWKV_HEAD_DIM
    n_chunks = rh_ref.shape[0] // L

    @pl.when(pl.program_id(0) == 0)
    def _():
        s_ref[...] = jnp.zeros_like(s_ref)

    pairs = range(RWKV_W // PW)
    lanes = [slice(q * PW, (q + 1) * PW) for q in pairs]
    s_mat = [s_ref[q] for q in pairs]
    for ci in range(n_chunks):
        rows = slice(ci * L, (ci + 1) * L)
        s_b = [s.astype(bf16) for s in s_mat]
        decay = jnp.exp(ge_ref[ci])
        for q in pairs:
            s_mat[q] = s_mat[q] * decay[:, lanes[q]] - _mm(s_b[q], nt_ref[ci, q]) + ds_ref[ci, q]
        for q in pairs:
            o_ref[rows, lanes[q]] = _nt(rh_ref[rows, lanes[q]], s_b[q]) + oh_ref[rows, lanes[q]]
    for q in pairs:
        s_ref[q] = s_mat[q]


def _rwkv_post_kernel(o_ref, g_ref, bonus_ref, lng_ref, lnb_ref, y_ref):
    inv = 1.0 / RWKV_HEAD_DIM
    o = o_ref[...]
    d = o - _head_sum(o) * inv
    var = _head_sum(d * d) * inv
    yn = d * lax.rsqrt(var + RWKV_GN_EPS) * lng_ref[...] + lnb_ref[...] + bonus_ref[...]
    y_ref[...] = (yn * g_ref[...]).astype(y_ref.dtype)


def _rwkv_scan(parts, ln_g, ln_b):
    r, ld, k, v, kk, b, g, bonus = parts
    T, W = r.shape
    L, PW = RWKV_CHUNK, 2 * RWKV_HEAD_DIM
    nc, npair = T // L, W // PW
    pre_rows = min(2 * L, T)
    seq_rows = min(4 * L, T)

    def specs(rows):
        n = rows // L
        row = pl.BlockSpec((rows, W), lambda c: (c, 0))
        mat = pl.BlockSpec((n, npair, PW, PW), lambda c: (c, 0, 0, 0))
        vec = pl.BlockSpec((n, 1, W), lambda c: (c, 0, 0))
        return row, mat, vec

    row, mat, vec = specs(pre_rows)
    rh, oh, nt, ds, ge = pl.pallas_call(
        _rwkv_pre_kernel,
        grid=(T // pre_rows,),
        in_specs=[row] * 6,
        out_specs=[row, row, mat, mat, vec],
        out_shape=[jax.ShapeDtypeStruct((T, W), bf16), jax.ShapeDtypeStruct((T, W), f32),
                   jax.ShapeDtypeStruct((nc, npair, PW, PW), bf16), jax.ShapeDtypeStruct((nc, npair, PW, PW), f32),
                   jax.ShapeDtypeStruct((nc, 1, W), f32)],
        compiler_params=_cp(("parallel",)),
        name="rwkv_pre",
    )(r, ld, k, v, kk, b)
    row, mat, vec = specs(seq_rows)
    o = pl.pallas_call(
        _rwkv_seq_kernel,
        grid=(T // seq_rows,),
        in_specs=[row, row, mat, mat, vec],
        out_specs=row,
        out_shape=jax.ShapeDtypeStruct((T, W), f32),
        scratch_shapes=[pltpu.VMEM((npair, PW, PW), f32)],
        compiler_params=_cp(("arbitrary",)),
        name="rwkv_seq",
    )(rh, oh, nt, ds, ge)
    tm = min(512, T)
    blk = pl.BlockSpec((tm, W), lambda i: (i, 0))
    one = pl.BlockSpec((1, W), lambda i: (0, 0))
    return pl.pallas_call(
        _rwkv_post_kernel,
        grid=(T // tm,),
        in_specs=[blk, blk, blk, one, one],
        out_specs=blk,
        out_shape=jax.ShapeDtypeStruct((T, W), bf16),
        compiler_params=_cp(("parallel",)),
        name="rwkv_post",
    )(o, g, bonus, ln_g.reshape(1, W), ln_b.reshape(1, W))


def _s5_dot(a, b):
    return jnp.dot(a.astype(bf16), b.astype(bf16), preferred_element_type=f32)


def _transpose_pieces(w):
    w = list(w)
    lane = lax.broadcasted_iota(jnp.int32, w[0].shape, 1)
    for d in (4, 2, 1):
        width = S5_GROUP * d
        low = (lane & width) == 0
        for i in range(8):
            if i & d == 0:
                a, b = w[i], w[i + d]
                w[i] = jnp.where(low, a, pltpu.roll(b, width, 1))
                w[i + d] = jnp.where(low, pltpu.roll(a, 128 - width, 1), b)
    return w


def _s5_split(z_refs, u_ref):
    ncb = u_ref.shape[1]
    for j, z_ref in enumerate(z_refs):
        for b in range(S5_CHUNK // 8):
            w = [z_ref[pl.ds(8 * b + i, ncb, stride=S5_CHUNK), :] for i in range(8)]
            o = _transpose_pieces(w)
            for a in range(8):
                u_ref[8 * j + a, :, 128 * b:128 * (b + 1)] = o[a]


def _toeplitz(krow):
    P, LP = krow.shape
    lane = lax.broadcasted_iota(jnp.int32, (P, LP), 1)
    blocks = [krow]
    for s in range(1, LP // P):
        blocks.append(jnp.where(lane >= P * s, pltpu.roll(krow, P * s, 1), 0.0))
    return jnp.concatenate(blocks, axis=0)


def _s5_in_kernel(u_ref, br_ref, bi_ref, xr_ref, xi_ref):
    u2 = jnp.concatenate([u_ref[0], u_ref[1]], axis=1)
    xr_ref[...] = _s5_dot(u2, br_ref[0])
    xi_ref[...] = _s5_dot(u2, bi_ref[0])


def _s5_scan_kernel(xr_ref, xi_ref, ar_ref, ai_ref, sr_ref, si_ref):
    nc = xr_ref.shape[0]
    ar = ar_ref[...]
    ai = ai_ref[...]

    def body(c, carry):
        sr, si = carry
        sr_ref[pl.ds(c, 1), :] = sr
        si_ref[pl.ds(c, 1), :] = si
        xr = xr_ref[pl.ds(c, 1), :]
        xi = xi_ref[pl.ds(c, 1), :]
        return ar * sr - ai * si + xr, ar * si + ai * sr + xi

    z = jnp.zeros_like(ar)
    lax.fori_loop(0, nc, body, (z, z))


def _s5_out_kernel(u_ref, krow_ref, sr_ref, si_ref, cr_ref, ci_ref, d_ref, o_ref):
    sr = sr_ref[...]
    si = si_ref[...]
    for j in range(2):
        u = u_ref[j]
        y = _s5_dot(u, _toeplitz(krow_ref[j])) + _s5_dot(sr, cr_ref[j]) + _s5_dot(si, ci_ref[j]) + d_ref[j] * u
        o_ref[j] = 0.5 * y * (1.0 + jnp.tanh(0.7978845608028654 * (y + 0.044715 * (y * y * y))))


def _glu_kernel(y_ref, w_ref, b_ref, o_ref, *nat_refs):
    ncb = y_ref.shape[1]
    for j, nat_ref in enumerate(nat_refs):
        for b in range(S5_CHUNK // 8):
            o = [y_ref[8 * j + a, :, 128 * b:128 * (b + 1)] for a in range(8)]
            w = _transpose_pieces(o)
            for i in range(8):
                nat_ref[pl.ds(8 * b + i, ncb, stride=S5_CHUNK), :] = w[i]
    y = jnp.concatenate([r[...] for r in nat_refs], axis=1)
    o_ref[...] = (y * _sigmoid(_bmm(y, w_ref[...]) + b_ref[...])).astype(o_ref.dtype)


def _s5_tables(a_re, a_im, log_dt, b_re, b_im, c_re, c_im, d):
    G, N, P, L = S5_GROUPS, S5_STATE, S5_GROUP, S5_CHUNK
    D = a_re.shape[0]
    dt = jnp.exp(log_dt)[:, None, :, None]
    lags = jnp.arange(L + 1, dtype=f32)[None, :, None, None]
    mag = jnp.exp(a_re[:, None] * dt * lags)
    ang = a_im[:, None] * dt * lags
    pw_re, pw_im = mag * jnp.cos(ang), mag * jnp.sin(ang)
    den = a_re * a_re + a_im * a_im
    nr, ni = pw_re[:, 1] - 1.0, pw_im[:, 1]
    coef_re = ((nr * a_re + ni * a_im) / den)[..., None]
    coef_im = ((ni * a_re - nr * a_im) / den)[..., None]
    bb_re = coef_re * b_re - coef_im * b_im
    bb_im = coef_re * b_im + coef_im * b_re
    rep = lambda a, axis: jnp.repeat(a, P, axis=axis)
    pl_re, pl_im = rep(pw_re[:, :L].transpose(0, 2, 1, 3), 2), rep(pw_im[:, :L].transpose(0, 2, 1, 3), 2)
    ct_re, ct_im = jnp.tile(c_re, (1, 1, L, 1)), jnp.tile(c_im, (1, 1, L, 1))
    cb_re, cb_im = ct_re * pl_re - ct_im * pl_im, ct_re * pl_im + ct_im * pl_re
    krow = (jnp.einsum('dgxn,dgnq->dgqx', cb_re, bb_re, precision=HI)
            - jnp.einsum('dgxn,dgnq->dgqx', cb_im, bb_im, precision=HI))
    e_re = rep(pw_re[:, L - 1::-1][:, :L].transpose(0, 2, 1, 3), 2)
    e_im = rep(pw_im[:, L - 1::-1][:, :L].transpose(0, 2, 1, 3), 2)
    bq_re = jnp.tile(bb_re.transpose(0, 1, 3, 2), (1, 1, L, 1))
    bq_im = jnp.tile(bb_im.transpose(0, 1, 3, 2), (1, 1, L, 1))
    bs_re = (e_re * bq_re - e_im * bq_im).reshape(D, G // 2, 2, L * P, N)
    bs_im = (e_re * bq_im + e_im * bq_re).reshape(D, G // 2, 2, L * P, N)

    def pair_in(bs):
        zz = jnp.zeros_like(bs[:, :, 0])
        return jnp.concatenate([jnp.concatenate([bs[:, :, 0], zz], axis=3), jnp.concatenate([zz, bs[:, :, 1]], axis=3)], axis=2)

    f_re = rep(pw_re[:, 1:].transpose(0, 2, 3, 1), 3)
    f_im = rep(pw_im[:, 1:].transpose(0, 2, 3, 1), 3)
    cn_re = jnp.tile(c_re.transpose(0, 1, 3, 2), (1, 1, 1, L))
    cn_im = jnp.tile(c_im.transpose(0, 1, 3, 2), (1, 1, 1, L))
    cs_re = (cn_re * f_re - cn_im * f_im).reshape(D, G // 2, 2, N, L * P)
    cs_im = (-(cn_re * f_im + cn_im * f_re)).reshape(D, G // 2, 2, N, L * P)

    def pair_out(cs):
        zz = jnp.zeros_like(cs[:, :, 0])
        top = jnp.concatenate([cs[:, :, 0], zz], axis=2)[:, :, None]
        bot = jnp.concatenate([zz, cs[:, :, 1]], axis=2)[:, :, None]
        return jnp.concatenate([top, bot], axis=2).reshape(D, G, 2 * N, L * P)

    dd = jnp.tile(d.reshape(D, G, 1, P), (1, 1, 1, L))
    return dict(krow=krow, bsr=pair_in(bs_re), bsi=pair_in(bs_im), cr=pair_out(cs_re), ci=pair_out(cs_im),
                al_re=pw_re[:, L].reshape(D, 1, G * N), al_im=pw_im[:, L].reshape(D, 1, G * N), d=dd)


def _s5(u, tab, layer, glu_w, glu_b):
    G, N, P, L = S5_GROUPS, S5_STATE, S5_GROUP, S5_CHUNK
    nc, LP = u.shape[1:]
    T = nc * L
    tm = min(1024, T)
    grp = lambda s: pl.BlockSpec((2,) + s, lambda j: (j, 0, 0))
    tgrp = lambda s: pl.BlockSpec((None, 2) + s, lambda j: (layer, j, 0, 0))
    tone = lambda s: pl.BlockSpec((None, 1) + s, lambda j: (layer, j, 0, 0))
    lanes = pl.BlockSpec((nc, 2 * N), lambda j: (0, j))
    xr, xi = pl.pallas_call(
        _s5_in_kernel,
        grid=(G // 2,),
        in_specs=[grp((nc, LP)), tone((2 * LP, 2 * N)), tone((2 * LP, 2 * N))],
        out_specs=[lanes, lanes],
        out_shape=[jax.ShapeDtypeStruct((nc, G * N), f32), jax.ShapeDtypeStruct((nc, G * N), f32)],
        compiler_params=_cp(("parallel",)),
        name="s5_in",
    )(u, tab["bsr"], tab["bsi"])
    whole = lambda s: pl.BlockSpec(s, lambda i: (0,) * len(s))
    sr, si = pl.pallas_call(
        _s5_scan_kernel,
        grid=(1,),
        in_specs=[whole((nc, G * N)), whole((nc, G * N)), pl.BlockSpec((None, 1, G * N), lambda i: (layer, 0, 0)),
                  pl.BlockSpec((None, 1, G * N), lambda i: (layer, 0, 0))],
        out_specs=[whole((nc, G * N)), whole((nc, G * N))],
        out_shape=[jax.ShapeDtypeStruct((nc, G * N), f32)] * 2,
        compiler_params=_cp(("arbitrary",)),
        name="s5_scan",
    )(xr, xi, tab["al_re"], tab["al_im"])
    y = pl.pallas_call(
        _s5_out_kernel,
        grid=(G // 2,),
        in_specs=[grp((nc, LP)), tgrp((P, LP)), lanes, lanes, tgrp((2 * N, LP)), tgrp((2 * N, LP)), tgrp((1, LP))],
        out_specs=grp((nc, LP)),
        out_shape=jax.ShapeDtypeStruct((G, nc, LP), f32),
        compiler_params=_cp(("parallel",)),
        name="s5_out",
    )(u, tab["krow"], sr, si, tab["cr"], tab["ci"], tab["d"])
    return pl.pallas_call(
        _glu_kernel,
        grid=(T // tm,),
        in_specs=[pl.BlockSpec((G, tm // L, LP), lambda i: (0, i, 0)), pl.BlockSpec((S5_W, S5_W), lambda i: (0, 0)),
                  pl.BlockSpec((1, S5_W), lambda i: (0, 0))],
        out_specs=pl.BlockSpec((tm, S5_W), lambda i: (i, 0)),
        out_shape=jax.ShapeDtypeStruct((T, S5_W), bf16),
        scratch_shapes=[pltpu.VMEM((tm, 128), f32)] * (S5_W // 128),
        compiler_params=_cp(("parallel",)),
        name="s5_glu",
    )(y, glu_w, glu_b.reshape(1, S5_W))


def _pad_rows(w, n):
    return jnp.pad(w, ((0, n - w.shape[0]), (0, 0)))


def _split_mu(mu):
    l0 = 3 * RWKV_W
    return jnp.concatenate([
        mu[:l0], jnp.pad(mu[l0:l0 + DECAY_LORA], (0, LORA_PAD - DECAY_LORA)),
        jnp.pad(mu[l0 + DECAY_LORA:l0 + DECAY_LORA + AAA_LORA], (0, LORA_PAD - AAA_LORA)),
        mu[l0 + DECAY_LORA + AAA_LORA:]]).reshape(1, RWKV_PAD)


def kernel(x, p, norm_mix_g, w_in, mlstm_conv, mlstm_ib, mlstm_fb, mlstm_norm_g, rwkv_mu, rwkv_w0, rwkv_w2, rwkv_a0, rwkv_a2, rwkv_g2, rwkv_kk, rwkv_ka, rwkv_rk, rwkv_ln_g, rwkv_ln_b, s5_a_re, s5_a_im, s5_log_dt, s5_b_re, s5_b_im, s5_c_re, s5_c_im, s5_d, s5_glu_w, s5_glu_b, w_up_m, w_up_r, w_up_s, w_out, norm_ffn_g, ffn_w_gate, ffn_w_up, ffn_w_down, norm_ple_g, ple_w_gate, ple_w_proj, final_norm_g):
    B, T, D = x.shape
    depth = w_in.shape[0]
    w_out_b, ple_wg_b, ple_wp_b, p_b = (a.astype(bf16) for a in (w_out, ple_w_gate, ple_w_proj, p))
    wt = jnp.swapaxes(w_in, 1, 2)
    s5_tab = _s5_tables(s5_a_re, s5_a_im, s5_log_dt, s5_b_re, s5_b_im, s5_c_re, s5_c_im, s5_d)
    outs = []
    for bi in range(B):
        h = x[bi]
        xn = _rmsnorm(h, norm_mix_g[0], bf16)
        for i in range(depth):
            z_main = _proj(xn, wt, i, 0, 4, MLSTM_W)
            z_gate = _proj(xn, wt, i, M_IN + RWKV_IN + S5_W, 3 * D_MODEL // 1024, 1024, bf16, gate=True)
            u_s5, z_if = _proj_tail(xn, wt, i)

            gate_b = jnp.pad(jnp.concatenate([mlstm_ib[i], mlstm_fb[i]]), (0, 128 - 2 * MLSTM_HEADS)).reshape(1, 128)
            y_m = _mlstm(z_main, z_if, mlstm_conv[i], gate_b, mlstm_norm_g[i])

            vec = lambda a: a.reshape(1, RWKV_W)
            parts = _rwkv_prep(xn, _rwkv_w(wt, i), _split_mu(rwkv_mu[i]), vec(rwkv_w0[i]), vec(rwkv_a0[i]), vec(rwkv_kk[i]),
                               vec(rwkv_ka[i]), vec(rwkv_rk[i]), _pad_rows(rwkv_w2[i], LORA_PAD).astype(bf16),
                               _pad_rows(rwkv_a2[i], LORA_PAD).astype(bf16), rwkv_g2[i].astype(bf16))
            y_r = _rwkv_scan(parts, rwkv_ln_g[i], rwkv_ln_b[i])

            y_s = _s5(u_s5, s5_tab, i, s5_glu_w[i].astype(bf16), s5_glu_b[i])

            mixed = _merge(y_m, y_r, y_s, z_gate, w_up_m, w_up_r, w_up_s, i)
            h, hn = _resid_norm(mixed, w_out_b, i, h, norm_ffn_g[i])
            act = _ffn_up(hn, ffn_w_gate, ffn_w_up, i)
            h = _resid_mm(act, ffn_w_down, h, i)
            last = i == depth - 1
            g_next = final_norm_g if last else norm_mix_g[i + 1]
            h, xn = _ple(h, p_b[:, bi:bi + 1], ple_wg_b, ple_wp_b, i, norm_ple_g[i], g_next, f32 if last else bf16)
        outs.append(xn)
    return jnp.stack(outs)
```

```python
import functools

import jax
import jax.numpy as jnp
import numpy as np
from jax import lax
from jax.experimental import pallas as pl
from jax.experimental.pallas import tpu as pltpu

f32 = jnp.float32
bf16 = jnp.bfloat16
HI = lax.Precision.HIGHEST

D_MODEL = 2048
PLE_DIM = 256
RMS_EPS = 1e-6
MLSTM_HEADS = 4
MLSTM_HEAD_DIM = 256
MLSTM_W = 1024
MLSTM_CHUNK = 256
GATE_SOFTCAP = 15.0
RWKV_HEAD_DIM = 64
RWKV_W = 512
DECAY_LORA = 96
AAA_LORA = 96
GATE_LORA = 256
LORA_PAD = 128
RWKV_GN_EPS = 64e-5
RWKV_CHUNK = 64
S5_GROUP = 16
S5_GROUPS = 32
S5_W = 512
S5_STATE = 64
S5_CHUNK = 16
FFN_HIDDEN = 5632
M_IN = 4 * MLSTM_W + 2 * MLSTM_HEADS
RWKV_IN = 3 * RWKV_W + DECAY_LORA + AAA_LORA + GATE_LORA
RWKV_PAD = 3 * RWKV_W + 2 * LORA_PAD + GATE_LORA
Z_TAIL = S5_W + 128
VMEM_LIMIT = 56 * 1024 * 1024


def _cp(sem):
    return pltpu.CompilerParams(dimension_semantics=sem, vmem_limit_bytes=VMEM_LIMIT)


def _nt(a, b, **kw):
    return lax.dot_general(a, b, (((1,), (1,)), ((), ())), preferred_element_type=f32, **kw)


def _tn(a, b, **kw):
    return lax.dot_general(a, b, (((0,), (0,)), ((), ())), preferred_element_type=f32, **kw)


def _mm(a, b, **kw):
    return jnp.dot(a, b, preferred_element_type=f32, **kw)


def _bmm(a, b):
    return jnp.dot(a.astype(bf16), b.astype(bf16), preferred_element_type=f32)


def _sigmoid(x):
    return 1.0 / (1.0 + jnp.exp(-x))


def _rms(x, g):
    return x * lax.rsqrt(jnp.mean(x * x, axis=-1, keepdims=True) + RMS_EPS) * g


def _norm_kernel(x_ref, g_ref, o_ref):
    o_ref[...] = _rms(x_ref[...], g_ref[...]).astype(o_ref.dtype)


def _rmsnorm(x, g, out_dtype):
    T, D = x.shape
    tm = min(512, T)
    return pl.pallas_call(
        _norm_kernel,
        grid=(T // tm,),
        in_specs=[pl.BlockSpec((tm, D), lambda i: (i, 0)), pl.BlockSpec((1, D), lambda i: (0, 0))],
        out_specs=pl.BlockSpec((tm, D), lambda i: (i, 0)),
        out_shape=jax.ShapeDtypeStruct((T, D), out_dtype),
        compiler_params=_cp(("parallel",)),
        name="rmsnorm",
    )(x, g.reshape(1, D))


def _pack_rows(w, rows):
    parts = [w[r[0]:r[1], :] if isinstance(r, tuple) else jnp.zeros((r, w.shape[1]), f32) for r in rows]
    return jnp.concatenate(parts, axis=0).astype(bf16)


def _proj_kernel(a_ref, w_ref, o_ref, wb_ref, *, gate):
    @pl.when(pl.program_id(1) == 0)
    def _():
        wb_ref[...] = w_ref[0].astype(bf16)

    z = _nt(a_ref[...], wb_ref[...])
    o_ref[...] = (0.5 * jnp.tanh(0.5 * z) + 0.5 if gate else z).astype(o_ref.dtype)


def _proj(a, wt, layer, row0, n_tiles, tn, out_dtype=f32, gate=False):
    T, K = a.shape
    tm = min(1024, T)
    return pl.pallas_call(
        functools.partial(_proj_kernel, gate=gate),
        grid=(n_tiles, T // tm),
        in_specs=[pl.BlockSpec((tm, K), lambda j, i: (i, 0)),
                  pl.BlockSpec((pl.Element(1), pl.Element(tn), pl.Element(K)),
                               lambda j, i: (layer, pl.multiple_of(row0 + tn * j, 8), 0))],
        out_specs=pl.BlockSpec((tm, tn), lambda j, i: (i, j)),
        out_shape=jax.ShapeDtypeStruct((T, n_tiles * tn), out_dtype),
        scratch_shapes=[pltpu.VMEM((tn, K), bf16)],
        compiler_params=_cp(("parallel", "arbitrary")),
        name="in_proj",
    )(a, wt)


def _proj_tail_kernel(a_ref, ws_ref, wg_ref, u_ref, gz_ref, wb_ref, *z_refs):
    @pl.when(pl.program_id(0) == 0)
    def _():
        pad = jnp.zeros((Z_TAIL - S5_W - wg_ref.shape[0], wg_ref.shape[1]), f32)
        wb_ref[...] = jnp.concatenate([ws_ref[0], wg_ref[...], pad], axis=0).astype(bf16)

    z = _nt(a_ref[...], wb_ref[...])
    gz_ref[...] = z[:, S5_W:]
    for j, z_ref in enumerate(z_refs):
        z_ref[...] = z[:, 128 * j:128 * (j + 1)]
    _s5_split(z_refs, u_ref)


def _proj_tail(a, wt, layer):
    T, K = a.shape
    tm = min(1024, T)
    n_if = 2 * MLSTM_HEADS
    L = S5_CHUNK
    return pl.pallas_call(
        _proj_tail_kernel,
        grid=(T // tm,),
        in_specs=[pl.BlockSpec((tm, K), lambda i: (i, 0)),
                  pl.BlockSpec((pl.Element(1), pl.Element(S5_W), pl.Element(K)), lambda i: (layer, M_IN + RWKV_IN, 0)),
                  pl.BlockSpec((None, n_if, K), lambda i: (layer, 4 * MLSTM_W // n_if, 0))],
        out_specs=[pl.BlockSpec((S5_GROUPS, tm // L, L * S5_GROUP), lambda i: (0, i, 0)),
                   pl.BlockSpec((tm, Z_TAIL - S5_W), lambda i: (i, 0))],
        out_shape=[jax.ShapeDtypeStruct((S5_GROUPS, T // L, L * S5_GROUP), f32),
                   jax.ShapeDtypeStruct((T, Z_TAIL - S5_W), f32)],
        scratch_shapes=[pltpu.VMEM((Z_TAIL, K), bf16)] + [pltpu.VMEM((tm, 128), f32)] * (S5_W // 128),
        compiler_params=_cp(("arbitrary",)),
        name="in_proj_tail",
    )(a, wt, wt)


def _merge_kernel(ym_ref, yr_ref, ys_ref, gm_ref, gr_ref, gs_ref, um_ref, ur_ref, us_ref, o_ref, umb_ref, urb_ref, usb_ref):
    @pl.when(pl.program_id(1) == 0)
    def _():
        umb_ref[...] = um_ref[...].astype(bf16)
        urb_ref[...] = ur_ref[...].astype(bf16)
        usb_ref[...] = us_ref[...].astype(bf16)

    acc = gm_ref[...] * _mm(ym_ref[...], umb_ref[...])
    acc += gr_ref[...] * _mm(yr_ref[...], urb_ref[...])
    acc += gs_ref[...] * _mm(ys_ref[...], usb_ref[...])
    o_ref[...] = acc.astype(o_ref.dtype)


def _merge(ym, yr, ys, gates, um, ur, us, layer):
    T = ym.shape[0]
    tm, tn = min(1024, T), 1024
    nj = D_MODEL // tn

    def gate_spec(b):
        return pl.BlockSpec((tm, tn), lambda j, i, b=b: (i, b * nj + j))

    def y_spec(w):
        return pl.BlockSpec((tm, w), lambda j, i: (i, 0))

    def u_spec(w):
        return pl.BlockSpec((None, w, tn), lambda j, i: (layer, 0, j))

    return pl.pallas_call(
        _merge_kernel,
        grid=(nj, T // tm),
        in_specs=[y_spec(MLSTM_W), y_spec(RWKV_W), y_spec(S5_W), gate_spec(0), gate_spec(1), gate_spec(2),
                  u_spec(MLSTM_W), u_spec(RWKV_W), u_spec(S5_W)],
        out_specs=pl.BlockSpec((tm, tn), lambda j, i: (i, j)),
        out_shape=jax.ShapeDtypeStruct((T, D_MODEL), bf16),
        scratch_shapes=[pltpu.VMEM((MLSTM_W, tn), bf16), pltpu.VMEM((RWKV_W, tn), bf16), pltpu.VMEM((S5_W, tn), bf16)],
        compiler_params=_cp(("parallel", "arbitrary")),
        name="gated_merge",
    )(ym, yr, ys, gates, gates, gates, um, ur, us)


def _resid_norm_kernel(a_ref, w_ref, h_ref, g_ref, ho_ref, no_ref):
    hn = h_ref[...] + _mm(a_ref[...], w_ref[...])
    ho_ref[...] = hn
    no_ref[...] = _rms(hn, g_ref[...]).astype(no_ref.dtype)


def _resid_norm(a, w, layer, h, g, norm_dtype=bf16):
    T, K = a.shape
    tm = min(512, T)
    row = lambda n: pl.BlockSpec((tm, n), lambda i: (i, 0))
    return pl.pallas_call(
        _resid_norm_kernel,
        grid=(T // tm,),
        in_specs=[row(K), pl.BlockSpec((None, K, D_MODEL), lambda i: (layer, 0, 0)), row(D_MODEL),
                  pl.BlockSpec((1, D_MODEL), lambda i: (0, 0))],
        out_specs=[row(D_MODEL), row(D_MODEL)],
        out_shape=[jax.ShapeDtypeStruct((T, D_MODEL), f32), jax.ShapeDtypeStruct((T, D_MODEL), norm_dtype)],
        compiler_params=_cp(("parallel",)),
        name="resid_matmul_norm",
    )(a, w, h, g.reshape(1, D_MODEL))


def _ffn_up_kernel(a_ref, wg_ref, wu_ref, o_ref, wgb_ref, wub_ref):
    @pl.when(pl.program_id(1) == 0)
    def _():
        wgb_ref[...] = wg_ref[...].astype(bf16)
        wub_ref[...] = wu_ref[...].astype(bf16)

    a = a_ref[...]
    gt = _mm(a, wgb_ref[...])
    up = _mm(a, wub_ref[...])
    o_ref[...] = (gt * _sigmoid(gt) * up).astype(o_ref.dtype)


def _ffn_up(a, wg, wu, layer):
    T, K = a.shape
    N = wg.shape[2]
    tm, tn = min(2048, T), 512
    return pl.pallas_call(
        _ffn_up_kernel,
        grid=(N // tn, T // tm),
        in_specs=[pl.BlockSpec((tm, K), lambda j, i: (i, 0)), pl.BlockSpec((None, K, tn), lambda j, i: (layer, 0, j)),
                  pl.BlockSpec((None, K, tn), lambda j, i: (layer, 0, j))],
        out_specs=pl.BlockSpec((tm, tn), lambda j, i: (i, j)),
        out_shape=jax.ShapeDtypeStruct((T, N), bf16),
        scratch_shapes=[pltpu.VMEM((K, tn), bf16), pltpu.VMEM((K, tn), bf16)],
        compiler_params=_cp(("parallel", "arbitrary")),
        name="ffn_up",
    )(a, wg, wu)


def _resid_mm_kernel(a_ref, w_ref, h_ref, o_ref, wb_ref):
    @pl.when(pl.program_id(1) == 0)
    def _():
        wb_ref[...] = w_ref[...].astype(bf16)

    o_ref[...] = h_ref[...] + _mm(a_ref[...], wb_ref[...])


def _resid_mm(a, w, h, layer):
    T, K = a.shape
    N = w.shape[2]
    tm, tn = min(512, T), 512
    return pl.pallas_call(
        _resid_mm_kernel,
        grid=(N // tn, T // tm),
        in_specs=[pl.BlockSpec((tm, K), lambda j, i: (i, 0)), pl.BlockSpec((None, K, tn), lambda j, i: (layer, 0, j)),
                  pl.BlockSpec((tm, tn), lambda j, i: (i, j))],
        out_specs=pl.BlockSpec((tm, tn), lambda j, i: (i, j)),
        out_shape=jax.ShapeDtypeStruct((T, N), f32),
        scratch_shapes=[pltpu.VMEM((K, tn), bf16)],
        compiler_params=_cp(("parallel", "arbitrary")),
        name="resid_matmul",
    )(a, w, h)


def _ple_kernel(h_ref, p_ref, wg_ref, wp_ref, gp_ref, g_ref, ho_ref, no_ref):
    tm = h_ref.shape[0]
    parts = [slice(0, tm // 2), slice(tm // 2, tm)]
    hs = [h_ref[r, :] for r in parts]
    hps = [_rms(h, gp_ref[...]).astype(bf16) for h in hs]
    gates = [_sigmoid(_mm(hp, wg_ref[...])) for hp in hps]
    projs = [_mm(p_ref[r, :], wp_ref[...]) for r in parts]
    for r, h, gate, proj in zip(parts, hs, gates, projs):
        hn = h + proj * gate
        ho_ref[r, :] = hn
        no_ref[r, :] = _rms(hn, g_ref[...]).astype(no_ref.dtype)


def _ple(h, p, wg, wp, layer, g_ple, g, norm_dtype):
    T = h.shape[0]
    tm = min(512, T)
    row = lambda w: pl.BlockSpec((tm, w), lambda i: (i, 0))
    full = lambda s: pl.BlockSpec(s, lambda i: (0, 0))
    return pl.pallas_call(
        _ple_kernel,
        grid=(T // tm,),
        in_specs=[row(D_MODEL), pl.BlockSpec((None, None, tm, PLE_DIM), lambda i: (layer, 0, i, 0)),
                  pl.BlockSpec((None, D_MODEL, D_MODEL), lambda i: (layer, 0, 0)),
                  pl.BlockSpec((None, PLE_DIM, D_MODEL), lambda i: (layer, 0, 0)), full((1, D_MODEL)), full((1, D_MODEL))],
        out_specs=[row(D_MODEL), row(D_MODEL)],
        out_shape=[jax.ShapeDtypeStruct((T, D_MODEL), f32), jax.ShapeDtypeStruct((T, D_MODEL), norm_dtype)],
        compiler_params=_cp(("parallel",)),
        name="ple_norm",
    )(h, p, wg, wp, g_ple.reshape(1, D_MODEL), g.reshape(1, D_MODEL))


def _mlstm_kernel(zq_ref, zk_ref, v_ref, o_ref, gz_ref, cq_ref, ck_ref, gb_ref, ng_ref, y_ref,
                  ct_ref, n_ref, qbuf_ref, kbuf_ref):
    c = pl.program_id(0)
    L, W = zq_ref.shape
    H, dh = MLSTM_HEADS, MLSTM_HEAD_DIM
    heads = range(H)

    @pl.when(c == 0)
    def _():
        ct_ref[...] = jnp.zeros_like(ct_ref)
        n_ref[...] = jnp.zeros_like(n_ref)
        qbuf_ref[...] = jnp.zeros_like(qbuf_ref)
        kbuf_ref[...] = jnp.zeros_like(kbuf_ref)

    def conv_silu(z_ref, w_ref, buf_ref):
        x = z_ref[...]
        xx = jnp.concatenate([buf_ref[...], x], axis=0)
        w = w_ref[...]
        acc = x * w[0:1, :]
        for j in range(1, w.shape[0]):
            acc = acc + pltpu.roll(xx, j, 0)[8:, :] * w[j:j + 1, :]
        buf_ref[...] = x[L - 8:, :]
        return acc * _sigmoid(acc)

    q = conv_silu(zq_ref, cq_ref, qbuf_ref) * (dh ** -0.5)
    k = conv_silu(zk_ref, ck_ref, kbuf_ref)

    g = gz_ref[...] + gb_ref[...]
    sc = GATE_SOFTCAP * jnp.tanh(g / GATE_SOFTCAP)
    logf = -jnp.log(1.0 + jnp.exp(-sc))
    row = lax.broadcasted_iota(jnp.int32, (L, L), 0)
    col = lax.broadcasted_iota(jnp.int32, (L, L), 1)
    causal = row >= col
    bcs = _mm(causal.astype(f32), logf, precision=HI)
    b128 = pltpu.roll(bcs, 128 - H, 1)
    e128 = sc - b128
    e_t = e128.T

    sl = [slice(h * dh, (h + 1) * dh) for h in heads]
    b_col = [b128[:, h:h + 1] for h in heads]
    e_col = [e128[:, h:h + 1] for h in heads]
    wmat = [jnp.where(causal, jnp.exp(jnp.where(causal, b_col[h] + e_t[h:h + 1, :], 0.0)), 0.0) for h in heads]
    b_end = [b_col[h][L - 1:L, :] for h in heads]
    eb = [jnp.exp(b_col[h]) for h in heads]
    qh = [q[:, sl[h]] for h in heads]
    kh = [k[:, sl[h]] for h in heads]
    qb = [x.astype(bf16) for x in qh]
    kb = [x.astype(bf16) for x in kh]
    vb = [v_ref[:, sl[h]].astype(bf16) for h in heads]
    ct = [ct_ref[h] for h in heads]
    nn = [n_ref[:, sl[h]] for h in heads]
    s = [_nt(qb[h], kb[h]) * wmat[h] for h in heads]
    inter = [_mm(qb[h], ct[h].astype(bf16)) for h in heads]
    num = [_mm(s[h].astype(bf16), vb[h]) + eb[h] * inter[h] for h in heads]
    ng = ng_ref[...]
    for h in heads:
        den = jnp.sum(s[h], axis=1, keepdims=True) + eb[h] * jnp.sum(qh[h] * nn[h], axis=1, keepdims=True)
        hh = num[h] / jnp.maximum(jnp.abs(den), 1.0)
        hh = hh * lax.rsqrt(jnp.mean(hh * hh, axis=1, keepdims=True) + RMS_EPS) * ng[:, sl[h]]
        y_ref[:, sl[h]] = (_sigmoid(o_ref[:, sl[h]]) * hh).astype(y_ref.dtype)
    kw = [kh[h] * jnp.exp(b_end[h] + e_col[h]) for h in heads]
    upd = [_tn(kw[h].astype(bf16), vb[h]) for h in heads]
    for h in heads:
        decay = jnp.exp(b_end[h])
        ct_ref[h] = decay * ct[h] + upd[h]
        n_ref[:, sl[h]] = decay * nn[h] + jnp.sum(kw[h], axis=0, keepdims=True)


def _mlstm(z_main, z_if, conv_w, gate_b, norm_g):
    T = z_main.shape[0]
    L, dh, H, W = min(MLSTM_CHUNK, T), MLSTM_HEAD_DIM, MLSTM_HEADS, MLSTM_W
    blk = lambda j: pl.BlockSpec((L, W), lambda c, j=j: (c, j))
    return pl.pallas_call(
        _mlstm_kernel,
        grid=(T // L,),
        in_specs=[blk(0), blk(1), blk(2), blk(3),
                  pl.BlockSpec((L, 128), lambda c: (c, 0)),
                  pl.BlockSpec((4, W), lambda c: (0, 0)), pl.BlockSpec((4, W), lambda c: (0, 1)),
                  pl.BlockSpec((1, 128), lambda c: (0, 0)), pl.BlockSpec((1, W), lambda c: (0, 0))],
        out_specs=pl.BlockSpec((L, W), lambda c: (c, 0)),
        out_shape=jax.ShapeDtypeStruct((T, W), bf16),
        scratch_shapes=[pltpu.VMEM((H, dh, dh), f32), pltpu.VMEM((1, W), f32),
                        pltpu.VMEM((8, W), f32), pltpu.VMEM((8, W), f32)],
        compiler_params=_cp(("arbitrary",)),
        name="mlstm",
    )(z_main, z_main, z_main, z_main, z_if, conv_w, conv_w, gate_b, norm_g.reshape(1, W))


def _iota_div(shape, dim, width):
    return lax.shift_right_logical(lax.broadcasted_iota(jnp.int32, shape, dim), int(np.log2(width)))


def _head_ones(n, width):
    return (_iota_div((n, n), 0, width) == _iota_div((n, n), 1, width)).astype(f32)


def _head_sum(x):
    blk = _head_ones(RWKV_W, RWKV_HEAD_DIM).astype(bf16)
    hi = x.astype(bf16)
    lo = (x - hi.astype(f32)).astype(bf16)
    return _mm(hi, blk) + _mm(lo, blk)


def _rwkv_w_kernel(w_ref, o_ref, *, last_rows):
    j = pl.program_id(0)
    nj = pl.num_programs(0)

    @pl.when(j < nj - 1)
    def _():
        o_ref[...] = w_ref[0].astype(bf16)

    @pl.when(j == nj - 1)
    def _():
        o_ref[...] = _pack_rows(w_ref[0], last_rows)


def _rwkv_w(wt, layer):
    K = wt.shape[2]
    tn = RWKV_W
    lora_rows = [(0, DECAY_LORA), LORA_PAD - DECAY_LORA, (DECAY_LORA, DECAY_LORA + AAA_LORA), LORA_PAD - AAA_LORA,
                 (DECAY_LORA + AAA_LORA, RWKV_IN - 3 * RWKV_W)]
    return pl.pallas_call(
        functools.partial(_rwkv_w_kernel, last_rows=lora_rows),
        grid=(RWKV_PAD // tn,),
        in_specs=[pl.BlockSpec((pl.Element(1), pl.Element(tn), pl.Element(K)),
                               lambda j: (layer, pl.multiple_of(M_IN + tn * j, 8), 0))],
        out_specs=pl.BlockSpec((tn, K), lambda j: (j, 0)),
        out_shape=jax.ShapeDtypeStruct((RWKV_PAD, K), bf16),
        compiler_params=_cp(("parallel",)),
        name="rwkv_w",
    )(wt)


def _rwkv_prep_kernel(a_ref, w_ref, mu_ref, w0_ref, a0_ref, kkg_ref, ka_ref, rk_ref, w2_ref, a2_ref, g2_ref,
                      r_ref, ld_ref, k_ref, v_ref, kk_ref, b_ref, g_ref, bonus_ref, *buf_refs):
    i = pl.program_id(0)
    nh = len(buf_refs)
    tm = a_ref.shape[0] // nh
    W = RWKV_W

    @pl.when(i == 0)
    def _():
        buf_refs[0][0:8, :] = jnp.zeros((8, buf_refs[0].shape[1]), f32)

    rows = [slice(h * tm, (h + 1) * tm) for h in range(nh)]
    xs_all = [_nt(a_ref[rw, :], w_ref[...]) for rw in rows]
    for h, (rw, x) in enumerate(zip(rows, xs_all)):
        buf_ref = buf_refs[h]
        buf_ref[8:, :] = x
        buf_refs[(h + 1) % nh][0:8, :] = x[tm - 8:, :]
        xs = x + (buf_ref[pl.ds(7, tm), :] - x) * mu_ref[...]
        r = xs[:, 0:W]
        k = xs[:, W:2 * W]
        v = xs[:, 2 * W:3 * W]
        wl = xs[:, 3 * W:3 * W + LORA_PAD]
        al = xs[:, 3 * W + LORA_PAD:3 * W + 2 * LORA_PAD]
        gl = xs[:, 3 * W + 2 * LORA_PAD:]
        t = w0_ref[...] + _bmm(jnp.tanh(wl), w2_ref[...])
        w = -(jnp.maximum(-t, 0.0) + jnp.log(1.0 + jnp.exp(-jnp.abs(t)))) - 0.5
        a = _sigmoid(a0_ref[...] + _bmm(al, a2_ref[...]))
        g = _bmm(_sigmoid(gl), g2_ref[...])
        kk = k * kkg_ref[...]
        ss = _head_sum(kk * kk)
        kk = kk / jnp.maximum(jnp.sqrt(ss), 1e-12)
        k2 = k * (1.0 + (a - 1.0) * ka_ref[...])
        bonus = _head_sum(r * k2 * rk_ref[...]) * v
        r_ref[rw, :] = r
        ld_ref[rw, :] = -jnp.exp(w)
        k_ref[rw, :] = k2
        v_ref[rw, :] = v
        kk_ref[rw, :] = kk
        b_ref[rw, :] = kk * a
        g_ref[rw, :] = g
        bonus_ref[rw, :] = bonus


def _rwkv_prep(xn, w_rwkv, mu, w0, a0, kkg, ka, rk, w2, a2, g2):
    T, K = xn.shape
    tm = min(512, T)
    n_sub = 2
    W = RWKV_W
    vec = lambda n: pl.BlockSpec((1, n), lambda i: (0, 0))
    mat = lambda s: pl.BlockSpec(s, lambda i: (0, 0))
    out = pl.BlockSpec((tm, W), lambda i: (i, 0))
    return pl.pallas_call(
        _rwkv_prep_kernel,
        grid=(T // tm,),
        in_specs=[pl.BlockSpec((tm, K), lambda i: (i, 0)), mat((RWKV_PAD, K)), vec(RWKV_PAD), vec(W), vec(W),
                  vec(W), vec(W), vec(W), mat((LORA_PAD, W)), mat((LORA_PAD, W)), mat((GATE_LORA, W))],
        out_specs=[out] * 8,
        out_shape=[jax.ShapeDtypeStruct((T, W), f32)] * 8,
        scratch_shapes=[pltpu.VMEM((tm // n_sub + 8, RWKV_PAD), f32)] * n_sub,
        compiler_params=_cp(("arbitrary",)),
        name="rwkv_prep",
    )(xn, w_rwkv, mu, w0, a0, kkg, ka, rk, w2, a2, g2)


def _rwkv_pre_kernel(r_ref, ld_ref, k_ref, v_ref, kk_ref, b_ref, rh_ref, oh_ref, nt_ref, ds_ref, ge_ref):
    L, dh = RWKV_CHUNK, RWKV_HEAD_DIM
    GW = 4 * dh
    PW = 2 * dh
    n_chunks = r_ref.shape[0] // L
    rr = lax.broadcasted_iota(jnp.int32, (GW, GW), 0)
    cc = lax.broadcasted_iota(jnp.int32, (GW, GW), 1)
    same = _iota_div((GW, GW), 0, L) == _iota_div((GW, GW), 1, dh)
    strict = same & (rr > cc)
    incl = same & (rr >= cc)
    eye = jnp.where(rr == cc, 1.0, 0.0)
    pair_blk = _head_ones(PW, dh)
    tr = lax.broadcasted_iota(jnp.int32, (L, L), 0)
    tc = lax.broadcasted_iota(jnp.int32, (L, L), 1)
    tril = (tr >= tc).astype(f32)

    def unstack(x):
        acc = x[0:L, :]
        for hh in range(1, GW // L):
            acc = acc + x[hh * L:(hh + 1) * L, :]
        return acc

    def tile(x):
        return jnp.concatenate([x] * (GW // L), axis=0)

    def stack(x):
        return jnp.where(same, tile(x), 0.0)

    tiles = [(ci, gi) for ci in range(n_chunks) for gi in range(RWKV_W // GW)]
    ch = []
    for ci in range(n_chunks):
        rows = slice(ci * L, (ci + 1) * L)
        ld = ld_ref[rows, :]
        lg = _mm(tril, ld, precision=HI)
        g_end = lg[L - 1:L, :]
        ge_ref[ci] = g_end
        dec_out = jnp.exp(-lg)
        dec_tail = jnp.exp(g_end - lg)
        k = k_ref[rows, :]
        b = b_ref[rows, :]
        v = v_ref[rows, :]
        ch.append(dict(rows=rows, v=v, kap=kk_ref[rows, :] * jnp.exp(lg - ld), rt=r_ref[rows, :] * jnp.exp(lg),
                       kt=k * dec_out, bt=b * dec_out, khat=(k * dec_tail).astype(bf16),
                       bhat=(b * dec_tail).astype(bf16), vb=v.astype(bf16)))
    ops = []
    for ci, gi in tiles:
        c = ch[ci]
        lanes = slice(gi * GW, (gi + 1) * GW)
        kap_s = stack(c["kap"][:, lanes])
        rt_s = stack(c["rt"][:, lanes])
        ops.append(dict(lanes=lanes, kap_s=kap_s, rt_s=rt_s, kap_b=kap_s.astype(bf16), rt_b=rt_s.astype(bf16),
                        v_s=stack(c["v"][:, lanes]).astype(bf16), kt_t=tile(c["kt"][:, lanes]).astype(bf16),
                        bt_t=tile(c["bt"][:, lanes]).astype(bf16)))
    for o in ops:
        o["a_kb"] = jnp.where(strict, _nt(o["kap_b"], o["bt_t"]), 0.0)
    for o in ops:
        o["a_kk"] = jnp.where(strict, _nt(o["kap_b"], o["kt_t"]), 0.0).astype(bf16)
        o["a_rk"] = jnp.where(incl, _nt(o["rt_b"], o["kt_t"]), 0.0).astype(bf16)
        o["a_rb"] = jnp.where(incl, _nt(o["rt_b"], o["bt_t"]), 0.0).astype(bf16)
    for o in ops:
        o["p"] = -o["a_kb"]
        o["tinv"] = eye + o["p"]
    step = 2
    while step < L:
        for o in ops:
            pb = o["p"].astype(bf16)
            o["p"] = _mm(pb, pb)
        for o in ops:
            o["tinv"] = o["tinv"] + _bmm(o["tinv"], o["p"])
        step *= 2
    for o in ops:
        o["w1"] = _mm(o["a_kk"], o["v_s"])
        o["w2"] = _mm(o["a_rk"], o["v_s"])
    for o in ops:
        o["ku"] = _bmm(o["tinv"], jnp.concatenate([o["kap_s"], o["w1"]], axis=1))
    for o in ops:
        o["ru"] = _mm(o["a_rb"], o["ku"].astype(bf16))
    for (ci, gi), o in zip(tiles, ops):
        c = ch[ci]
        rows, lanes = c["rows"], o["lanes"]
        rh_ref[rows, lanes] = unstack(o["rt_s"] - o["ru"][:, :GW]).astype(rh_ref.dtype)
        oh_ref[rows, lanes] = unstack(o["w2"] - o["ru"][:, GW:])
        kh = unstack(o["ku"][:, :GW]).astype(bf16)
        uh = unstack(o["ku"][:, GW:]).astype(bf16)
        for pi in range(GW // PW):
            q = gi * (GW // PW) + pi
            loc = slice(pi * PW, (pi + 1) * PW)
            glo = slice(q * PW, (q + 1) * PW)
            nt_ref[ci, q] = (_tn(kh[:, loc], c["bhat"][:, glo]) * pair_blk).astype(nt_ref.dtype)
            ds_ref[ci, q] = (_tn(c["vb"][:, glo], c["khat"][:, glo]) - _tn(uh[:, loc], c["bhat"][:, glo])) * pair_blk


def _rwkv_seq_kernel(rh_ref, oh_ref, nt_ref, ds_ref, ge_ref, o_ref, s_ref):
    L, PW = RWKV_CHUNK, 2 * RWKV_HEAD_DIM
    n_chunks = rh_ref.shape[0] // L

    @pl.when(pl.program_id(0) == 0)
    def _():
        s_ref[...] = jnp.zeros_like(s_ref)

    pairs = range(RWKV_W // PW)
    lanes = [slice(q * PW, (q + 1) * PW) for q in pairs]
    s_mat = [s_ref[q] for q in pairs]
    for ci in range(n_chunks):
        rows = slice(ci * L, (ci + 1) * L)
        s_b = [s.astype(bf16) for s in s_mat]
        decay = jnp.exp(ge_ref[ci])
        for q in pairs:
            s_mat[q] = s_mat[q] * decay[:, lanes[q]] - _mm(s_b[q], nt_ref[ci, q]) + ds_ref[ci, q]
        for q in pairs:
            o_ref[rows, lanes[q]] = _nt(rh_ref[rows, lanes[q]], s_b[q]) + oh_ref[rows, lanes[q]]
    for q in pairs:
        s_ref[q] = s_mat[q]


def _rwkv_post_kernel(o_ref, g_ref, bonus_ref, lng_ref, lnb_ref, y_ref):
    inv = 1.0 / RWKV_HEAD_DIM
    o = o_ref[...]
    d = o - _head_sum(o) * inv
    var = _head_sum(d * d) * inv
    yn = d * lax.rsqrt(var + RWKV_GN_EPS) * lng_ref[...] + lnb_ref[...] + bonus_ref[...]
    y_ref[...] = (yn * g_ref[...]).astype(y_ref.dtype)


def _rwkv_scan(parts, ln_g, ln_b):
    r, ld, k, v, kk, b, g, bonus = parts
    T, W = r.shape
    L, PW = RWKV_CHUNK, 2 * RWKV_HEAD_DIM
    nc, npair = T // L, W // PW
    pre_rows = min(2 * L, T)
    seq_rows = min(4 * L, T)

    def specs(rows):
        n = rows // L
        row = pl.BlockSpec((rows, W), lambda c: (c, 0))
        mat = pl.BlockSpec((n, npair, PW, PW), lambda c: (c, 0, 0, 0))
        vec = pl.BlockSpec((n, 1, W), lambda c: (c, 0, 0))
        return row, mat, vec

    row, mat, vec = specs(pre_rows)
    rh, oh, nt, ds, ge = pl.pallas_call(
        _rwkv_pre_kernel,
        grid=(T // pre_rows,),
        in_specs=[row] * 6,
        out_specs=[row, row, mat, mat, vec],
        out_shape=[jax.ShapeDtypeStruct((T, W), bf16), jax.ShapeDtypeStruct((T, W), f32),
                   jax.ShapeDtypeStruct((nc, npair, PW, PW), bf16), jax.ShapeDtypeStruct((nc, npair, PW, PW), f32),
                   jax.ShapeDtypeStruct((nc, 1, W), f32)],
        compiler_params=_cp(("parallel",)),
        name="rwkv_pre",
    )(r, ld, k, v, kk, b)
    row, mat, vec = specs(seq_rows)
    o = pl.pallas_call(
        _rwkv_seq_kernel,
        grid=(T // seq_rows,),
        in_specs=[row, row, mat, mat, vec],
        out_specs=row,
        out_shape=jax.ShapeDtypeStruct((T, W), f32),
        scratch_shapes=[pltpu.VMEM((npair, PW, PW), f32)],
        compiler_params=_cp(("arbitrary",)),
        name="rwkv_seq",
    )(rh, oh, nt, ds, ge)
    tm = min(512, T)
    blk = pl.BlockSpec((tm, W), lambda i: (i, 0))
    one = pl.BlockSpec((1, W), lambda i: (0, 0))
    return pl.pallas_call(
        _rwkv_post_kernel,
        grid=(T // tm,),
        in_specs=[blk, blk, blk, one, one],
        out_specs=blk,
        out_shape=jax.ShapeDtypeStruct((T, W), bf16),
        compiler_params=_cp(("parallel",)),
        name="rwkv_post",
    )(o, g, bonus, ln_g.reshape(1, W), ln_b.reshape(1, W))


def _s5_dot(a, b):
    return jnp.dot(a.astype(bf16), b.astype(bf16), preferred_element_type=f32)


def _transpose_pieces(w):
    w = list(w)
    lane = lax.broadcasted_iota(jnp.int32, w[0].shape, 1)
    for d in (4, 2, 1):
        width = S5_GROUP * d
        low = (lane & width) == 0
        for i in range(8):
            if i & d == 0:
                a, b = w[i], w[i + d]
                w[i] = jnp.where(low, a, pltpu.roll(b, width, 1))
                w[i + d] = jnp.where(low, pltpu.roll(a, 128 - width, 1), b)
    return w


def _s5_split(z_refs, u_ref):
    ncb = u_ref.shape[1]
    for j, z_ref in enumerate(z_refs):
        for b in range(S5_CHUNK // 8):
            w = [z_ref[pl.ds(8 * b + i, ncb, stride=S5_CHUNK), :] for i in range(8)]
            o = _transpose_pieces(w)
            for a in range(8):
                u_ref[8 * j + a, :, 128 * b:128 * (b + 1)] = o[a]


def _toeplitz(krow):
    P, LP = krow.shape
    lane = lax.broadcasted_iota(jnp.int32, (P, LP), 1)
    blocks = [krow]
    for s in range(1, LP // P):
        blocks.append(jnp.where(lane >= P * s, pltpu.roll(krow, P * s, 1), 0.0))
    return jnp.concatenate(blocks, axis=0)


def _s5_in_kernel(u_ref, br_ref, bi_ref, xr_ref, xi_ref):
    u2 = jnp.concatenate([u_ref[0], u_ref[1]], axis=1)
    xr_ref[...] = _s5_dot(u2, br_ref[0])
    xi_ref[...] = _s5_dot(u2, bi_ref[0])


def _s5_scan_kernel(xr_ref, xi_ref, ar_ref, ai_ref, sr_ref, si_ref):
    nc = xr_ref.shape[0]
    ar = ar_ref[...]
    ai = ai_ref[...]

    def body(c, carry):
        sr, si = carry
        sr_ref[pl.ds(c, 1), :] = sr
        si_ref[pl.ds(c, 1), :] = si
        xr = xr_ref[pl.ds(c, 1), :]
        xi = xi_ref[pl.ds(c, 1), :]
        return ar * sr - ai * si + xr, ar * si + ai * sr + xi

    z = jnp.zeros_like(ar)
    lax.fori_loop(0, nc, body, (z, z))


def _s5_out_kernel(u_ref, krow_ref, sr_ref, si_ref, cr_ref, ci_ref, d_ref, o_ref):
    sr = sr_ref[...]
    si = si_ref[...]
    for j in range(2):
        u = u_ref[j]
        y = _s5_dot(u, _toeplitz(krow_ref[j])) + _s5_dot(sr, cr_ref[j]) + _s5_dot(si, ci_ref[j]) + d_ref[j] * u
        o_ref[j] = 0.5 * y * (1.0 + jnp.tanh(0.7978845608028654 * (y + 0.044715 * (y * y * y))))


def _glu_kernel(y_ref, w_ref, b_ref, o_ref, *nat_refs):
    ncb = y_ref.shape[1]
    for j, nat_ref in enumerate(nat_refs):
        for b in range(S5_CHUNK // 8):
            o = [y_ref[8 * j + a, :, 128 * b:128 * (b + 1)] for a in range(8)]
            w = _transpose_pieces(o)
            for i in range(8):
                nat_ref[pl.ds(8 * b + i, ncb, stride=S5_CHUNK), :] = w[i]
    y = jnp.concatenate([r[...] for r in nat_refs], axis=1)
    o_ref[...] = (y * _sigmoid(_bmm(y, w_ref[...]) + b_ref[...])).astype(o_ref.dtype)


def _s5_tables(a_re, a_im, log_dt, b_re, b_im, c_re, c_im, d):
    G, N, P, L = S5_GROUPS, S5_STATE, S5_GROUP, S5_CHUNK
    D = a_re.shape[0]
    dt = jnp.exp(log_dt)[:, None, :, None]
    lags = jnp.arange(L + 1, dtype=f32)[None, :, None, None]
    mag = jnp.exp(a_re[:, None] * dt * lags)
    ang = a_im[:, None] * dt * lags
    pw_re, pw_im = mag * jnp.cos(ang), mag * jnp.sin(ang)
    den = a_re * a_re + a_im * a_im
    nr, ni = pw_re[:, 1] - 1.0, pw_im[:, 1]
    coef_re = ((nr * a_re + ni * a_im) / den)[..., None]
    coef_im = ((ni * a_re - nr * a_im) / den)[..., None]
    bb_re = coef_re * b_re - coef_im * b_im
    bb_im = coef_re * b_im + coef_im * b_re
    rep = lambda a, axis: jnp.repeat(a, P, axis=axis)
    pl_re, pl_im = rep(pw_re[:, :L].transpose(0, 2, 1, 3), 2), rep(pw_im[:, :L].transpose(0, 2, 1, 3), 2)
    ct_re, ct_im = jnp.tile(c_re, (1, 1, L, 1)), jnp.tile(c_im, (1, 1, L, 1))
    cb_re, cb_im = ct_re * pl_re - ct_im * pl_im, ct_re * pl_im + ct_im * pl_re
    krow = (jnp.einsum('dgxn,dgnq->dgqx', cb_re, bb_re, precision=HI)
            - jnp.einsum('dgxn,dgnq->dgqx', cb_im, bb_im, precision=HI))
    e_re = rep(pw_re[:, L - 1::-1][:, :L].transpose(0, 2, 1, 3), 2)
    e_im = rep(pw_im[:, L - 1::-1][:, :L].transpose(0, 2, 1, 3), 2)
    bq_re = jnp.tile(bb_re.transpose(0, 1, 3, 2), (1, 1, L, 1))
    bq_im = jnp.tile(bb_im.transpose(0, 1, 3, 2), (1, 1, L, 1))
    bs_re = (e_re * bq_re - e_im * bq_im).reshape(D, G // 2, 2, L * P, N)
    bs_im = (e_re * bq_im + e_im * bq_re).reshape(D, G // 2, 2, L * P, N)

    def pair_in(bs):
        zz = jnp.zeros_like(bs[:, :, 0])
        return jnp.concatenate([jnp.concatenate([bs[:, :, 0], zz], axis=3), jnp.concatenate([zz, bs[:, :, 1]], axis=3)], axis=2)

    f_re = rep(pw_re[:, 1:].transpose(0, 2, 3, 1), 3)
    f_im = rep(pw_im[:, 1:].transpose(0, 2, 3, 1), 3)
    cn_re = jnp.tile(c_re.transpose(0, 1, 3, 2), (1, 1, 1, L))
    cn_im = jnp.tile(c_im.transpose(0, 1, 3, 2), (1, 1, 1, L))
    cs_re = (cn_re * f_re - cn_im * f_im).reshape(D, G // 2, 2, N, L * P)
    cs_im = (-(cn_re * f_im + cn_im * f_re)).reshape(D, G // 2, 2, N, L * P)

    def pair_out(cs):
        zz = jnp.zeros_like(cs[:, :, 0])
        top = jnp.concatenate([cs[:, :, 0], zz], axis=2)[:, :, None]
        bot = jnp.concatenate([zz, cs[:, :, 1]], axis=2)[:, :, None]
        return jnp.concatenate([top, bot], axis=2).reshape(D, G, 2 * N, L * P)

    dd = jnp.tile(d.reshape(D, G, 1, P), (1, 1, 1, L))
    return dict(krow=krow, bsr=pair_in(bs_re), bsi=pair_in(bs_im), cr=pair_out(cs_re), ci=pair_out(cs_im),
                al_re=pw_re[:, L].reshape(D, 1, G * N), al_im=pw_im[:, L].reshape(D, 1, G * N), d=dd)


def _s5(u, tab, layer, glu_w, glu_b):
    G, N, P, L = S5_GROUPS, S5_STATE, S5_GROUP, S5_CHUNK
    nc, LP = u.shape[1:]
    T = nc * L
    tm = min(1024, T)
    grp = lambda s: pl.BlockSpec((2,) + s, lambda j: (j, 0, 0))
    tgrp = lambda s: pl.BlockSpec((None, 2) + s, lambda j: (layer, j, 0, 0))
    tone = lambda s: pl.BlockSpec((None, 1) + s, lambda j: (layer, j, 0, 0))
    lanes = pl.BlockSpec((nc, 2 * N), lambda j: (0, j))
    xr, xi = pl.pallas_call(
        _s5_in_kernel,
        grid=(G // 2,),
        in_specs=[grp((nc, LP)), tone((2 * LP, 2 * N)), tone((2 * LP, 2 * N))],
        out_specs=[lanes, lanes],
        out_shape=[jax.ShapeDtypeStruct((nc, G * N), f32), jax.ShapeDtypeStruct((nc, G * N), f32)],
        compiler_params=_cp(("parallel",)),
        name="s5_in",
    )(u, tab["bsr"], tab["bsi"])
    whole = lambda s: pl.BlockSpec(s, lambda i: (0,) * len(s))
    sr, si = pl.pallas_call(
        _s5_scan_kernel,
        grid=(1,),
        in_specs=[whole((nc, G * N)), whole((nc, G * N)), pl.BlockSpec((None, 1, G * N), lambda i: (layer, 0, 0)),
                  pl.BlockSpec((None, 1, G * N), lambda i: (layer, 0, 0))],
        out_specs=[whole((nc, G * N)), whole((nc, G * N))],
        out_shape=[jax.ShapeDtypeStruct((nc, G * N), f32)] * 2,
        compiler_params=_cp(("arbitrary",)),
        name="s5_scan",
    )(xr, xi, tab["al_re"], tab["al_im"])
    y = pl.pallas_call(
        _s5_out_kernel,
        grid=(G // 2,),
        in_specs=[grp((nc, LP)), tgrp((P, LP)), lanes, lanes, tgrp((2 * N, LP)), tgrp((2 * N, LP)), tgrp((1, LP))],
        out_specs=grp((nc, LP)),
        out_shape=jax.ShapeDtypeStruct((G, nc, LP), f32),
        compiler_params=_cp(("parallel",)),
        name="s5_out",
    )(u, tab["krow"], sr, si, tab["cr"], tab["ci"], tab["d"])
    return pl.pallas_call(
        _glu_kernel,
        grid=(T // tm,),
        in_specs=[pl.BlockSpec((G, tm // L, LP), lambda i: (0, i, 0)), pl.BlockSpec((S5_W, S5_W), lambda i: (0, 0)),
                  pl.BlockSpec((1, S5_W), lambda i: (0, 0))],
        out_specs=pl.BlockSpec((tm, S5_W), lambda i: (i, 0)),
        out_shape=jax.ShapeDtypeStruct((T, S5_W), bf16),
        scratch_shapes=[pltpu.VMEM((tm, 128), f32)] * (S5_W // 128),
        compiler_params=_cp(("parallel",)),
        name="s5_glu",
    )(y, glu_w, glu_b.reshape(1, S5_W))


def _pad_rows(w, n):
    return jnp.pad(w, ((0, n - w.shape[0]), (0, 0)))


def _split_mu(mu):
    l0 = 3 * RWKV_W
    return jnp.concatenate([
        mu[:l0], jnp.pad(mu[l0:l0 + DECAY_LORA], (0, LORA_PAD - DECAY_LORA)),
        jnp.pad(mu[l0 + DECAY_LORA:l0 + DECAY_LORA + AAA_LORA], (0, LORA_PAD - AAA_LORA)),
        mu[l0 + DECAY_LORA + AAA_LORA:]]).reshape(1, RWKV_PAD)


def kernel(x, p, norm_mix_g, w_in, mlstm_conv, mlstm_ib, mlstm_fb, mlstm_norm_g, rwkv_mu, rwkv_w0, rwkv_w2, rwkv_a0, rwkv_a2, rwkv_g2, rwkv_kk, rwkv_ka, rwkv_rk, rwkv_ln_g, rwkv_ln_b, s5_a_re, s5_a_im, s5_log_dt, s5_b_re, s5_b_im, s5_c_re, s5_c_im, s5_d, s5_glu_w, s5_glu_b, w_up_m, w_up_r, w_up_s, w_out, norm_ffn_g, ffn_w_gate, ffn_w_up, ffn_w_down, norm_ple_g, ple_w_gate, ple_w_proj, final_norm_g):
    B, T, D = x.shape
    depth = w_in.shape[0]
    w_out_b, ple_wg_b, ple_wp_b, p_b = (a.astype(bf16) for a in (w_out, ple_w_gate, ple_w_proj, p))
    wt = jnp.swapaxes(w_in, 1, 2)
    s5_tab = _s5_tables(s5_a_re, s5_a_im, s5_log_dt, s5_b_re, s5_b_im, s5_c_re, s5_c_im, s5_d)
    outs = []
    for bi in range(B):
        h = x[bi]
        xn = _rmsnorm(h, norm_mix_g[0], bf16)
        for i in range(depth):
            z_main = _proj(xn, wt, i, 0, 4, MLSTM_W)
            z_gate = _proj(xn, wt, i, M_IN + RWKV_IN + S5_W, 3 * D_MODEL // 1024, 1024, bf16, gate=True)
            u_s5, z_if = _proj_tail(xn, wt, i)

            gate_b = jnp.pad(jnp.concatenate([mlstm_ib[i], mlstm_fb[i]]), (0, 128 - 2 * MLSTM_HEADS)).reshape(1, 128)
            y_m = _mlstm(z_main, z_if, mlstm_conv[i], gate_b, mlstm_norm_g[i])

            vec = lambda a: a.reshape(1, RWKV_W)
            parts = _rwkv_prep(xn, _rwkv_w(wt, i), _split_mu(rwkv_mu[i]), vec(rwkv_w0[i]), vec(rwkv_a0[i]), vec(rwkv_kk[i]),
                               vec(rwkv_ka[i]), vec(rwkv_rk[i]), _pad_rows(rwkv_w2[i], LORA_PAD).astype(bf16),
                               _pad_rows(rwkv_a2[i], LORA_PAD).astype(bf16), rwkv_g2[i].astype(bf16))
            y_r = _rwkv_scan(parts, rwkv_ln_g[i], rwkv_ln_b[i])

            y_s = _s5(u_s5, s5_tab, i, s5_glu_w[i].astype(bf16), s5_glu_b[i])

            mixed = _merge(y_m, y_r, y_s, z_gate, w_up_m, w_up_r, w_up_s, i)
            h, hn = _resid_norm(mixed, w_out_b, i, h, norm_ffn_g[i])
            act = _ffn_up(hn, ffn_w_gate, ffn_w_up, i)
            h = _resid_mm(act, ffn_w_down, h, i)
            last = i == depth - 1
            g_next = final_norm_g if last else norm_mix_g[i + 1]
            h, xn = _ple(h, p_b[:, bi:bi + 1], ple_wg_b, ple_wp_b, i, norm_ple_g[i], g_next, f32 if last else bf16)
        outs.append(xn)
    return jnp.stack(outs)
```

```python
import functools

import jax
import jax.numpy as jnp
import numpy as np
from jax import lax
from jax.experimental import pallas as pl
from jax.experimental.pallas import tpu as pltpu

f32 = jnp.float32
bf16 = jnp.bfloat16
HI = lax.Precision.HIGHEST

D_MODEL = 2048
PLE_DIM = 256
RMS_EPS = 1e-6
MLSTM_HEADS = 4
MLSTM_HEAD_DIM = 256
MLSTM_W = 1024
MLSTM_CHUNK = 256
GATE_SOFTCAP = 15.0
RWKV_HEAD_DIM = 64
RWKV_W = 512
DECAY_LORA = 96
AAA_LORA = 96
GATE_LORA = 256
LORA_PAD = 128
RWKV_GN_EPS = 64e-5
RWKV_CHUNK = 64
S5_GROUP = 16
S5_GROUPS = 32
S5_W = 512
S5_STATE = 64
S5_CHUNK = 16
FFN_HIDDEN = 5632
M_IN = 4 * MLSTM_W + 2 * MLSTM_HEADS
RWKV_IN = 3 * RWKV_W + DECAY_LORA + AAA_LORA + GATE_LORA
RWKV_PAD = 3 * RWKV_W + 2 * LORA_PAD + GATE_LORA
Z_TAIL = S5_W + 128
VMEM_LIMIT = 56 * 1024 * 1024


def _cp(sem):
    return pltpu.CompilerParams(dimension_semantics=sem, vmem_limit_bytes=VMEM_LIMIT)


def _nt(a, b, **kw):
    return lax.dot_general(a, b, (((1,), (1,)), ((), ())), preferred_element_type=f32, **kw)


def _tn(a, b, **kw):
    return lax.dot_general(a, b, (((0,), (0,)), ((), ())), preferred_element_type=f32, **kw)


def _mm(a, b, **kw):
    return jnp.dot(a, b, preferred_element_type=f32, **kw)


def _bmm(a, b):
    return jnp.dot(a.astype(bf16), b.astype(bf16), preferred_element_type=f32)


def _sigmoid(x):
    return 1.0 / (1.0 + jnp.exp(-x))


def _rms(x, g):
    return x * lax.rsqrt(jnp.mean(x * x, axis=-1, keepdims=True) + RMS_EPS) * g


def _norm_kernel(x_ref, g_ref, o_ref):
    o_ref[...] = _rms(x_ref[...], g_ref[...]).astype(o_ref.dtype)


def _rmsnorm(x, g, out_dtype):
    T, D = x.shape
    tm = min(512, T)
    return pl.pallas_call(
        _norm_kernel,
        grid=(T // tm,),
        in_specs=[pl.BlockSpec((tm, D), lambda i: (i, 0)), pl.BlockSpec((1, D), lambda i: (0, 0))],
        out_specs=pl.BlockSpec((tm, D), lambda i: (i, 0)),
        out_shape=jax.ShapeDtypeStruct((T, D), out_dtype),
        compiler_params=_cp(("parallel",)),
        name="rmsnorm",
    )(x, g.reshape(1, D))


def _pack_rows(w, rows):
    parts = [w[r[0]:r[1], :] if isinstance(r, tuple) else jnp.zeros((r, w.shape[1]), f32) for r in rows]
    return jnp.concatenate(parts, axis=0).astype(bf16)


def _proj_kernel(a_ref, w_ref, o_ref, wb_ref, *, gate):
    @pl.when(pl.program_id(1) == 0)
    def _():
        wb_ref[...] = w_ref[0].astype(bf16)

    z = _nt(a_ref[...], wb_ref[...])
    o_ref[...] = (0.5 * jnp.tanh(0.5 * z) + 0.5 if gate else z).astype(o_ref.dtype)


def _proj(a, wt, layer, row0, n_tiles, tn, out_dtype=f32, gate=False):
    T, K = a.shape
    tm = min(1024, T)
    return pl.pallas_call(
        functools.partial(_proj_kernel, gate=gate),
        grid=(n_tiles, T // tm),
        in_specs=[pl.BlockSpec((tm, K), lambda j, i: (i, 0)),
                  pl.BlockSpec((pl.Element(1), pl.Element(tn), pl.Element(K)),
                               lambda j, i: (layer, pl.multiple_of(row0 + tn * j, 8), 0))],
        out_specs=pl.BlockSpec((tm, tn), lambda j, i: (i, j)),
        out_shape=jax.ShapeDtypeStruct((T, n_tiles * tn), out_dtype),
        scratch_shapes=[pltpu.VMEM((tn, K), bf16)],
        compiler_params=_cp(("parallel", "arbitrary")),
        name="in_proj",
    )(a, wt)


def _proj_tail_kernel(a_ref, ws_ref, wg_ref, u_ref, gz_ref, wb_ref, *z_refs):
    @pl.when(pl.program_id(0) == 0)
    def _():
        pad = jnp.zeros((Z_TAIL - S5_W - wg_ref.shape[0], wg_ref.shape[1]), f32)
        wb_ref[...] = jnp.concatenate([ws_ref[0], wg_ref[...], pad], axis=0).astype(bf16)

    z = _nt(a_ref[...], wb_ref[...])
    gz_ref[...] = z[:, S5_W:]
    for j, z_ref in enumerate(z_refs):
        z_ref[...] = z[:, 128 * j:128 * (j + 1)]
    _s5_split(z_refs, u_ref)


def _proj_tail(a, wt, layer):
    T, K = a.shape
    tm = min(1024, T)
    n_if = 2 * MLSTM_HEADS
    L = S5_CHUNK
    return pl.pallas_call(
        _proj_tail_kernel,
        grid=(T // tm,),
        in_specs=[pl.BlockSpec((tm, K), lambda i: (i, 0)),
                  pl.BlockSpec((pl.Element(1), pl.Element(S5_W), pl.Element(K)), lambda i: (layer, M_IN + RWKV_IN, 0)),
                  pl.BlockSpec((None, n_if, K), lambda i: (layer, 4 * MLSTM_W // n_if, 0))],
        out_specs=[pl.BlockSpec((S5_GROUPS, tm // L, L * S5_GROUP), lambda i: (0, i, 0)),
                   pl.BlockSpec((tm, Z_TAIL - S5_W), lambda i: (i, 0))],
        out_shape=[jax.ShapeDtypeStruct((S5_GROUPS, T // L, L * S5_GROUP), f32),
                   jax.ShapeDtypeStruct((T, Z_TAIL - S5_W), f32)],
        scratch_shapes=[pltpu.VMEM((Z_TAIL, K), bf16)] + [pltpu.VMEM((tm, 128), f32)] * (S5_W // 128),
        compiler_params=_cp(("arbitrary",)),
        name="in_proj_tail",
    )(a, wt, wt)


def _merge_kernel(ym_ref, yr_ref, ys_ref, gm_ref, gr_ref, gs_ref, um_ref, ur_ref, us_ref, o_ref, umb_ref, urb_ref, usb_ref):
    @pl.when(pl.program_id(1) == 0)
    def _():
        umb_ref[...] = um_ref[...].astype(bf16)
        urb_ref[...] = ur_ref[...].astype(bf16)
        usb_ref[...] = us_ref[...].astype(bf16)

    acc = gm_ref[...] * _mm(ym_ref[...], umb_ref[...])
    acc += gr_ref[...] * _mm(yr_ref[...], urb_ref[...])
    acc += gs_ref[...] * _mm(ys_ref[...], usb_ref[...])
    o_ref[...] = acc.astype(o_ref.dtype)


def _merge(ym, yr, ys, gates, um, ur, us, layer):
    T = ym.shape[0]
    tm, tn = min(1024, T), 1024
    nj = D_MODEL // tn

    def gate_spec(b):
        return pl.BlockSpec((tm, tn), lambda j, i, b=b: (i, b * nj + j))

    def y_spec(w):
        return pl.BlockSpec((tm, w), lambda j, i: (i, 0))

    def u_spec(w):
        return pl.BlockSpec((None, w, tn), lambda j, i: (layer, 0, j))

    return pl.pallas_call(
        _merge_kernel,
        grid=(nj, T // tm),
        in_specs=[y_spec(MLSTM_W), y_spec(RWKV_W), y_spec(S5_W), gate_spec(0), gate_spec(1), gate_spec(2),
                  u_spec(MLSTM_W), u_spec(RWKV_W), u_spec(S5_W)],
        out_specs=pl.BlockSpec((tm, tn), lambda j, i: (i, j)),
        out_shape=jax.ShapeDtypeStruct((T, D_MODEL), bf16),
        scratch_shapes=[pltpu.VMEM((MLSTM_W, tn), bf16), pltpu.VMEM((RWKV_W, tn), bf16), pltpu.VMEM((S5_W, tn), bf16)],
        compiler_params=_cp(("parallel", "arbitrary")),
        name="gated_merge",
    )(ym, yr, ys, gates, gates, gates, um, ur, us)


def _resid_norm_kernel(a_ref, w_ref, h_ref, g_ref, ho_ref, no_ref):
    hn = h_ref[...] + _mm(a_ref[...], w_ref[...])
    ho_ref[...] = hn
    no_ref[...] = _rms(hn, g_ref[...]).astype(no_ref.dtype)


def _resid_norm(a, w, layer, h, g, norm_dtype=bf16):
    T, K = a.shape
    tm = min(512, T)
    row = lambda n: pl.BlockSpec((tm, n), lambda i: (i, 0))
    return pl.pallas_call(
        _resid_norm_kernel,
        grid=(T // tm,),
        in_specs=[row(K), pl.BlockSpec((None, K, D_MODEL), lambda i: (layer, 0, 0)), row(D_MODEL),
                  pl.BlockSpec((1, D_MODEL), lambda i: (0, 0))],
        out_specs=[row(D_MODEL), row(D_MODEL)],
        out_shape=[jax.ShapeDtypeStruct((T, D_MODEL), f32), jax.ShapeDtypeStruct((T, D_MODEL), norm_dtype)],
        compiler_params=_cp(("parallel",)),
        name="resid_matmul_norm",
    )(a, w, h, g.reshape(1, D_MODEL))


def _ffn_up_kernel(a_ref, wg_ref, wu_ref, o_ref, wgb_ref, wub_ref):
    @pl.when(pl.program_id(1) == 0)
    def _():
        wgb_ref[...] = wg_ref[...].astype(bf16)
        wub_ref[...] = wu_ref[...].astype(bf16)

    a = a_ref[...]
    gt = _mm(a, wgb_ref[...])
    up = _mm(a, wub_ref[...])
    o_ref[...] = (gt * _sigmoid(gt) * up).astype(o_ref.dtype)


def _ffn_up(a, wg, wu, layer):
    T, K = a.shape
    N = wg.shape[2]
    tm, tn = min(2048, T), 512
    return pl.pallas_call(
        _ffn_up_kernel,
        grid=(N // tn, T // tm),
        in_specs=[pl.BlockSpec((tm, K), lambda j, i: (i, 0)), pl.BlockSpec((None, K, tn), lambda j, i: (layer, 0, j)),
                  pl.BlockSpec((None, K, tn), lambda j, i: (layer, 0, j))],
        out_specs=pl.BlockSpec((tm, tn), lambda j, i: (i, j)),
        out_shape=jax.ShapeDtypeStruct((T, N), bf16),
        scratch_shapes=[pltpu.VMEM((K, tn), bf16), pltpu.VMEM((K, tn), bf16)],
        compiler_params=_cp(("parallel", "arbitrary")),
        name="ffn_up",
    )(a, wg, wu)


def _resid_mm_kernel(a_ref, w_ref, h_ref, o_ref, wb_ref):
    @pl.when(pl.program_id(1) == 0)
    def _():
        wb_ref[...] = w_ref[...].astype(bf16)

    o_ref[...] = h_ref[...] + _mm(a_ref[...], wb_ref[...])


def _resid_mm(a, w, h, layer):
    T, K = a.shape
    N = w.shape[2]
    tm, tn = min(512, T), 512
    return pl.pallas_call(
        _resid_mm_kernel,
        grid=(N // tn, T // tm),
        in_specs=[pl.BlockSpec((tm, K), lambda j, i: (i, 0)), pl.BlockSpec((None, K, tn), lambda j, i: (layer, 0, j)),
                  pl.BlockSpec((tm, tn), lambda j, i: (i, j))],
        out_specs=pl.BlockSpec((tm, tn), lambda j, i: (i, j)),
        out_shape=jax.ShapeDtypeStruct((T, N), f32),
        scratch_shapes=[pltpu.VMEM((K, tn), bf16)],
        compiler_params=_cp(("parallel", "arbitrary")),
        name="resid_matmul",
    )(a, w, h)


def _ple_kernel(h_ref, p_ref, wg_ref, wp_ref, gp_ref, g_ref, ho_ref, no_ref):
    tm = h_ref.shape[0]
    parts = [slice(0, tm // 2), slice(tm // 2, tm)]
    hs = [h_ref[r, :] for r in parts]
    hps = [_rms(h, gp_ref[...]).astype(bf16) for h in hs]
    gates = [_sigmoid(_mm(hp, wg_ref[...])) for hp in hps]
    projs = [_mm(p_ref[r, :], wp_ref[...]) for r in parts]
    for r, h, gate, proj in zip(parts, hs, gates, projs):
        hn = h + proj * gate
        ho_ref[r, :] = hn
        no_ref[r, :] = _rms(hn, g_ref[...]).astype(no_ref.dtype)


def _ple(h, p, wg, wp, layer, g_ple, g, norm_dtype):
    T = h.shape[0]
    tm = min(512, T)
    row = lambda w: pl.BlockSpec((tm, w), lambda i: (i, 0))
    full = lambda s: pl.BlockSpec(s, lambda i: (0, 0))
    return pl.pallas_call(
        _ple_kernel,
        grid=(T // tm,),
        in_specs=[row(D_MODEL), pl.BlockSpec((None, None, tm, PLE_DIM), lambda i: (layer, 0, i, 0)),
                  pl.BlockSpec((None, D_MODEL, D_MODEL), lambda i: (layer, 0, 0)),
                  pl.BlockSpec((None, PLE_DIM, D_MODEL), lambda i: (layer, 0, 0)), full((1, D_MODEL)), full((1, D_MODEL))],
        out_specs=[row(D_MODEL), row(D_MODEL)],
        out_shape=[jax.ShapeDtypeStruct((T, D_MODEL), f32), jax.ShapeDtypeStruct((T, D_MODEL), norm_dtype)],
        compiler_params=_cp(("parallel",)),
        name="ple_norm",
    )(h, p, wg, wp, g_ple.reshape(1, D_MODEL), g.reshape(1, D_MODEL))


def _mlstm_kernel(zq_ref, zk_ref, v_ref, o_ref, gz_ref, cq_ref, ck_ref, gb_ref, ng_ref, y_ref,
                  ct_ref, n_ref, qbuf_ref, kbuf_ref):
    c = pl.program_id(0)
    L, W = zq_ref.shape
    H, dh = MLSTM_HEADS, MLSTM_HEAD_DIM
    heads = range(H)

    @pl.when(c == 0)
    def _():
        ct_ref[...] = jnp.zeros_like(ct_ref)
        n_ref[...] = jnp.zeros_like(n_ref)
        qbuf_ref[...] = jnp.zeros_like(qbuf_ref)
        kbuf_ref[...] = jnp.zeros_like(kbuf_ref)

    def conv_silu(z_ref, w_ref, buf_ref):
        x = z_ref[...]
        xx = jnp.concatenate([buf_ref[...], x], axis=0)
        w = w_ref[...]
        acc = x * w[0:1, :]
        for j in range(1, w.shape[0]):
            acc = acc + pltpu.roll(xx, j, 0)[8:, :] * w[j:j + 1, :]
        buf_ref[...] = x[L - 8:, :]
        return acc * _sigmoid(acc)

    q = conv_silu(zq_ref, cq_ref, qbuf_ref) * (dh ** -0.5)
    k = conv_silu(zk_ref, ck_ref, kbuf_ref)

    g = gz_ref[...] + gb_ref[...]
    sc = GATE_SOFTCAP * jnp.tanh(g / GATE_SOFTCAP)
    logf = -jnp.log(1.0 + jnp.exp(-sc))
    row = lax.broadcasted_iota(jnp.int32, (L, L), 0)
    col = lax.broadcasted_iota(jnp.int32, (L, L), 1)
    causal = row >= col
    bcs = _mm(causal.astype(f32), logf, precision=HI)
    b128 = pltpu.roll(bcs, 128 - H, 1)
    e128 = sc - b128
    e_t = e128.T

    sl = [slice(h * dh, (h + 1) * dh) for h in heads]
    b_col = [b128[:, h:h + 1] for h in heads]
    e_col = [e128[:, h:h + 1] for h in heads]
    wmat = [jnp.where(causal, jnp.exp(jnp.where(causal, b_col[h] + e_t[h:h + 1, :], 0.0)), 0.0) for h in heads]
    b_end = [b_col[h][L - 1:L, :] for h in heads]
    eb = [jnp.exp(b_col[h]) for h in heads]
    qh = [q[:, sl[h]] for h in heads]
    kh = [k[:, sl[h]] for h in heads]
    qb = [x.astype(bf16) for x in qh]
    kb = [x.astype(bf16) for x in kh]
    vb = [v_ref[:, sl[h]].astype(bf16) for h in heads]
    ct = [ct_ref[h] for h in heads]
    nn = [n_ref[:, sl[h]] for h in heads]
    s = [_nt(qb[h], kb[h]) * wmat[h] for h in heads]
    inter = [_mm(qb[h], ct[h].astype(bf16)) for h in heads]
    num = [_mm(s[h].astype(bf16), vb[h]) + eb[h] * inter[h] for h in heads]
    ng = ng_ref[...]
    for h in heads:
        den = jnp.sum(s[h], axis=1, keepdims=True) + eb[h] * jnp.sum(qh[h] * nn[h], axis=1, keepdims=True)
        hh = num[h] / jnp.maximum(jnp.abs(den), 1.0)
        hh = hh * lax.rsqrt(jnp.mean(hh * hh, axis=1, keepdims=True) + RMS_EPS) * ng[:, sl[h]]
        y_ref[:, sl[h]] = (_sigmoid(o_ref[:, sl[h]]) * hh).astype(y_ref.dtype)
    kw = [kh[h] * jnp.exp(b_end[h] + e_col[h]) for h in heads]
    upd = [_tn(kw[h].astype(bf16), vb[h]) for h in heads]
    for h in heads:
        decay = jnp.exp(b_end[h])
        ct_ref[h] = decay * ct[h] + upd[h]
        n_ref[:, sl[h]] = decay * nn[h] + jnp.sum(kw[h], axis=0, keepdims=True)


def _mlstm(z_main, z_if, conv_w, gate_b, norm_g):
    T = z_main.shape[0]
    L, dh, H, W = min(MLSTM_CHUNK, T), MLSTM_HEAD_DIM, MLSTM_HEADS, MLSTM_W
    blk = lambda j: pl.BlockSpec((L, W), lambda c, j=j: (c, j))
    return pl.pallas_call(
        _mlstm_kernel,
        grid=(T // L,),
        in_specs=[blk(0), blk(1), blk(2), blk(3),
                  pl.BlockSpec((L, 128), lambda c: (c, 0)),
                  pl.BlockSpec((4, W), lambda c: (0, 0)), pl.BlockSpec((4, W), lambda c: (0, 1)),
                  pl.BlockSpec((1, 128), lambda c: (0, 0)), pl.BlockSpec((1, W), lambda c: (0, 0))],
        out_specs=pl.BlockSpec((L, W), lambda c: (c, 0)),
        out_shape=jax.ShapeDtypeStruct((T, W), bf16),
        scratch_shapes=[pltpu.VMEM((H, dh, dh), f32), pltpu.VMEM((1, W), f32),
                        pltpu.VMEM((8, W), f32), pltpu.VMEM((8, W), f32)],
        compiler_params=_cp(("arbitrary",)),
        name="mlstm",
    )(z_main, z_main, z_main, z_main, z_if, conv_w, conv_w, gate_b, norm_g.reshape(1, W))


def _iota_div(shape, dim, width):
    return lax.shift_right_logical(lax.broadcasted_iota(jnp.int32, shape, dim), int(np.log2(width)))


def _head_ones(n, width):
    return (_iota_div((n, n), 0, width) == _iota_div((n, n), 1, width)).astype(f32)


def _head_sum(x):
    blk = _head_ones(RWKV_W, RWKV_HEAD_DIM).astype(bf16)
    hi = x.astype(bf16)
    lo = (x - hi.astype(f32)).astype(bf16)
    return _mm(hi, blk) + _mm(lo, blk)


def _rwkv_w_kernel(w_ref, o_ref, *, last_rows):
    j = pl.program_id(0)
    nj = pl.num_programs(0)

    @pl.when(j < nj - 1)
    def _():
        o_ref[...] = w_ref[0].astype(bf16)

    @pl.when(j == nj - 1)
    def _():
        o_ref[...] = _pack_rows(w_ref[0], last_rows)


def _rwkv_w(wt, layer):
    K = wt.shape[2]
    tn = RWKV_W
    lora_rows = [(0, DECAY_LORA), LORA_PAD - DECAY_LORA, (DECAY_LORA, DECAY_LORA + AAA_LORA), LORA_PAD - AAA_LORA,
                 (DECAY_LORA + AAA_LORA, RWKV_IN - 3 * RWKV_W)]
    return pl.pallas_call(
        functools.partial(_rwkv_w_kernel, last_rows=lora_rows),
        grid=(RWKV_PAD // tn,),
        in_specs=[pl.BlockSpec((pl.Element(1), pl.Element(tn), pl.Element(K)),
                               lambda j: (layer, pl.multiple_of(M_IN + tn * j, 8), 0))],
        out_specs=pl.BlockSpec((tn, K), lambda j: (j, 0)),
        out_shape=jax.ShapeDtypeStruct((RWKV_PAD, K), bf16),
        compiler_params=_cp(("parallel",)),
        name="rwkv_w",
    )(wt)


def _rwkv_prep_kernel(a_ref, w_ref, mu_ref, w0_ref, a0_ref, kkg_ref, ka_ref, rk_ref, w2_ref, a2_ref, g2_ref,
                      r_ref, ld_ref, k_ref, v_ref, kk_ref, b_ref, g_ref, bonus_ref, *buf_refs):
    i = pl.program_id(0)
    nh = len(buf_refs)
    tm = a_ref.shape[0] // nh
    W = RWKV_W

    @pl.when(i == 0)
    def _():
        buf_refs[0][0:8, :] = jnp.zeros((8, buf_refs[0].shape[1]), f32)

    rows = [slice(h * tm, (h + 1) * tm) for h in range(nh)]
    xs_all = [_nt(a_ref[rw, :], w_ref[...]) for rw in rows]
    for h, (rw, x) in enumerate(zip(rows, xs_all)):
        buf_ref = buf_refs[h]
        buf_ref[8:, :] = x
        buf_refs[(h + 1) % nh][0:8, :] = x[tm - 8:, :]
        xs = x + (buf_ref[pl.ds(7, tm), :] - x) * mu_ref[...]
        r = xs[:, 0:W]
        k = xs[:, W:2 * W]
        v = xs[:, 2 * W:3 * W]
        wl = xs[:, 3 * W:3 * W + LORA_PAD]
        al = xs[:, 3 * W + LORA_PAD:3 * W + 2 * LORA_PAD]
        gl = xs[:, 3 * W + 2 * LORA_PAD:]
        t = w0_ref[...] + _bmm(jnp.tanh(wl), w2_ref[...])
        w = -(jnp.maximum(-t, 0.0) + jnp.log(1.0 + jnp.exp(-jnp.abs(t)))) - 0.5
        a = _sigmoid(a0_ref[...] + _bmm(al, a2_ref[...]))
        g = _bmm(_sigmoid(gl), g2_ref[...])
        kk = k * kkg_ref[...]
        ss = _head_sum(kk * kk)
        kk = kk / jnp.maximum(jnp.sqrt(ss), 1e-12)
        k2 = k * (1.0 + (a - 1.0) * ka_ref[...])
        bonus = _head_sum(r * k2 * rk_ref[...]) * v
        r_ref[rw, :] = r
        ld_ref[rw, :] = -jnp.exp(w)
        k_ref[rw, :] = k2
        v_ref[rw, :] = v
        kk_ref[rw, :] = kk
        b_ref[rw, :] = kk * a
        g_ref[rw, :] = g
        bonus_ref[rw, :] = bonus


def _rwkv_prep(xn, w_rwkv, mu, w0, a0, kkg, ka, rk, w2, a2, g2):
    T, K = xn.shape
    tm = min(512, T)
    n_sub = 2
    W = RWKV_W
    vec = lambda n: pl.BlockSpec((1, n), lambda i: (0, 0))
    mat = lambda s: pl.BlockSpec(s, lambda i: (0, 0))
    out = pl.BlockSpec((tm, W), lambda i: (i, 0))
    return pl.pallas_call(
        _rwkv_prep_kernel,
        grid=(T // tm,),
        in_specs=[pl.BlockSpec((tm, K), lambda i: (i, 0)), mat((RWKV_PAD, K)), vec(RWKV_PAD), vec(W), vec(W),
                  vec(W), vec(W), vec(W), mat((LORA_PAD, W)), mat((LORA_PAD, W)), mat((GATE_LORA, W))],
        out_specs=[out] * 8,
        out_shape=[jax.ShapeDtypeStruct((T, W), f32)] * 8,
        scratch_shapes=[pltpu.VMEM((tm // n_sub + 8, RWKV_PAD), f32)] * n_sub,
        compiler_params=_cp(("arbitrary",)),
        name="rwkv_prep",
    )(xn, w_rwkv, mu, w0, a0, kkg, ka, rk, w2, a2, g2)


def _rwkv_pre_kernel(r_ref, ld_ref, k_ref, v_ref, kk_ref, b_ref, rh_ref, oh_ref, nt_ref, ds_ref, ge_ref):
    L, dh = RWKV_CHUNK, RWKV_HEAD_DIM
    GW = 4 * dh
    PW = 2 * dh
    n_chunks = r_ref.shape[0] // L
    rr = lax.broadcasted_iota(jnp.int32, (GW, GW), 0)
    cc = lax.broadcasted_iota(jnp.int32, (GW, GW), 1)
    same = _iota_div((GW, GW), 0, L) == _iota_div((GW, GW), 1, dh)
    strict = same & (rr > cc)
    incl = same & (rr >= cc)
    eye = jnp.where(rr == cc, 1.0, 0.0)
    pair_blk = _head_ones(PW, dh)
    tr = lax.broadcasted_iota(jnp.int32, (L, L), 0)
    tc = lax.broadcasted_iota(jnp.int32, (L, L), 1)
    tril = (tr >= tc).astype(f32)

    def unstack(x):
        acc = x[0:L, :]
        for hh in range(1, GW // L):
            acc = acc + x[hh * L:(hh + 1) * L, :]
        return acc

    def tile(x):
        return jnp.concatenate([x] * (GW // L), axis=0)

    def stack(x):
        return jnp.where(same, tile(x), 0.0)

    tiles = [(ci, gi) for ci in range(n_chunks) for gi in range(RWKV_W // GW)]
    ch = []
    for ci in range(n_chunks):
        rows = slice(ci * L, (ci + 1) * L)
        ld = ld_ref[rows, :]
        lg = _mm(tril, ld, precision=HI)
        g_end = lg[L - 1:L, :]
        ge_ref[ci] = g_end
        dec_out = jnp.exp(-lg)
        dec_tail = jnp.exp(g_end - lg)
        k = k_ref[rows, :]
        b = b_ref[rows, :]
        v = v_ref[rows, :]
        ch.append(dict(rows=rows, v=v, kap=kk_ref[rows, :] * jnp.exp(lg - ld), rt=r_ref[rows, :] * jnp.exp(lg),
                       kt=k * dec_out, bt=b * dec_out, khat=(k * dec_tail).astype(bf16),
                       bhat=(b * dec_tail).astype(bf16), vb=v.astype(bf16)))
    ops = []
    for ci, gi in tiles:
        c = ch[ci]
        lanes = slice(gi * GW, (gi + 1) * GW)
        kap_s = stack(c["kap"][:, lanes])
        rt_s = stack(c["rt"][:, lanes])
        ops.append(dict(lanes=lanes, kap_s=kap_s, rt_s=rt_s, kap_b=kap_s.astype(bf16), rt_b=rt_s.astype(bf16),
                        v_s=stack(c["v"][:, lanes]).astype(bf16), kt_t=tile(c["kt"][:, lanes]).astype(bf16),
                        bt_t=tile(c["bt"][:, lanes]).astype(bf16)))
    for o in ops:
        o["a_kb"] = jnp.where(strict, _nt(o["kap_b"], o["bt_t"]), 0.0)
    for o in ops:
        o["a_kk"] = jnp.where(strict, _nt(o["kap_b"], o["kt_t"]), 0.0).astype(bf16)
        o["a_rk"] = jnp.where(incl, _nt(o["rt_b"], o["kt_t"]), 0.0).astype(bf16)
        o["a_rb"] = jnp.where(incl, _nt(o["rt_b"], o["bt_t"]), 0.0).astype(bf16)
    for o in ops:
        o["p"] = -o["a_kb"]
        o["tinv"] = eye + o["p"]
    step = 2
    while step < L:
        for o in ops:
            pb = o["p"].astype(bf16)
            o["p"] = _mm(pb, pb)
        for o in ops:
            o["tinv"] = o["tinv"] + _bmm(o["tinv"], o["p"])
        step *= 2
    for o in ops:
        o["w1"] = _mm(o["a_kk"], o["v_s"])
        o["w2"] = _mm(o["a_rk"], o["v_s"])
    for o in ops:
        o["ku"] = _bmm(o["tinv"], jnp.concatenate([o["kap_s"], o["w1"]], axis=1))
    for o in ops:
        o["ru"] = _mm(o["a_rb"], o["ku"].astype(bf16))
    for (ci, gi), o in zip(tiles, ops):
        c = ch[ci]
        rows, lanes = c["rows"], o["lanes"]
        rh_ref[rows, lanes] = unstack(o["rt_s"] - o["ru"][:, :GW]).astype(rh_ref.dtype)
        oh_ref[rows, lanes] = unstack(o["w2"] - o["ru"][:, GW:])
        kh = unstack(o["ku"][:, :GW]).astype(bf16)
        uh = unstack(o["ku"][:, GW:]).astype(bf16)
        for pi in range(GW // PW):
            q = gi * (GW // PW) + pi
            loc = slice(pi * PW, (pi + 1) * PW)
            glo = slice(q * PW, (q + 1) * PW)
            nt_ref[ci, q] = (_tn(kh[:, loc], c["bhat"][:, glo]) * pair_blk).astype(nt_ref.dtype)
            ds_ref[ci, q] = (_tn(c["vb"][:, glo], c["khat"][:, glo]) - _tn(uh[:, loc], c["bhat"][:, glo])) * pair_blk


def _rwkv_seq_kernel(rh_ref, oh_ref, nt_ref, ds_ref, ge_ref, o_ref, s_ref):
    L, PW = RWKV_CHUNK, 2 * RWKV_HEAD_DIM
    n_chunks = rh_ref.shape[0] // L

    @pl.when(pl.program_id(0) == 0)
    def _():
        s_ref[...] = jnp.zeros_like(s_ref)

    pairs = range(RWKV_W // PW)
    lanes = [slice(q * PW, (q + 1) * PW) for q in pairs]
    s_mat = [s_ref[q] for q in pairs]
    for ci in range(n_chunks):
        rows = slice(ci * L, (ci + 1) * L)
        s_b = [s.astype(bf16) for s in s_mat]
        decay = jnp.exp(ge_ref[ci])
        for q in pairs:
            s_mat[q] = s_mat[q] * decay[:, lanes[q]] - _mm(s_b[q], nt_ref[ci, q]) + ds_ref[ci, q]
        for q in pairs:
            o_ref[rows, lanes[q]] = _nt(rh_ref[rows, lanes[q]], s_b[q]) + oh_ref[rows, lanes[q]]
    for q in pairs:
        s_ref[q] = s_mat[q]


def _rwkv_post_kernel(o_ref, g_ref, bonus_ref, lng_ref, lnb_ref, y_ref):
    inv = 1.0 / RWKV_HEAD_DIM
    o = o_ref[...]
    d = o - _head_sum(o) * inv
    var = _head_sum(d * d) * inv
    yn = d * lax.rsqrt(var + RWKV_GN_EPS) * lng_ref[...] + lnb_ref[...] + bonus_ref[...]
    y_ref[...] = (yn * g_ref[...]).astype(y_ref.dtype)


def _rwkv_scan(parts, ln_g, ln_b):
    r, ld, k, v, kk, b, g, bonus = parts
    T, W = r.shape
    L, PW = RWKV_CHUNK, 2 * RWKV_HEAD_DIM
    nc, npair = T // L, W // PW
    pre_rows = min(4 * L, T)
    seq_rows = min(4 * L, T)

    def specs(rows):
        n = rows // L
        row = pl.BlockSpec((rows, W), lambda c: (c, 0))
        mat = pl.BlockSpec((n, npair, PW, PW), lambda c: (c, 0, 0, 0))
        vec = pl.BlockSpec((n, 1, W), lambda c: (c, 0, 0))
        return row, mat, vec

    row, mat, vec = specs(pre_rows)
    rh, oh, nt, ds, ge = pl.pallas_call(
        _rwkv_pre_kernel,
        grid=(T // pre_rows,),
        in_specs=[row] * 6,
        out_specs=[row, row, mat, mat, vec],
        out_shape=[jax.ShapeDtypeStruct((T, W), bf16), jax.ShapeDtypeStruct((T, W), f32),
                   jax.ShapeDtypeStruct((nc, npair, PW, PW), bf16), jax.ShapeDtypeStruct((nc, npair, PW, PW), f32),
                   jax.ShapeDtypeStruct((nc, 1, W), f32)],
        compiler_params=_cp(("parallel",)),
        name="rwkv_pre",
    )(r, ld, k, v, kk, b)
    row, mat, vec = specs(seq_rows)
    o = pl.pallas_call(
        _rwkv_seq_kernel,
        grid=(T // seq_rows,),
        in_specs=[row, row, mat, mat, vec],
        out_specs=row,
        out_shape=jax.ShapeDtypeStruct((T, W), f32),
        scratch_shapes=[pltpu.VMEM((npair, PW, PW), f32)],
        compiler_params=_cp(("arbitrary",)),
        name="rwkv_seq",
    )(rh, oh, nt, ds, ge)
    tm = min(512, T)
    blk = pl.BlockSpec((tm, W), lambda i: (i, 0))
    one = pl.BlockSpec((1, W), lambda i: (0, 0))
    return pl.pallas_call(
        _rwkv_post_kernel,
        grid=(T // tm,),
        in_specs=[blk, blk, blk, one, one],
        out_specs=blk,
        out_shape=jax.ShapeDtypeStruct((T, W), bf16),
        compiler_params=_cp(("parallel",)),
        name="rwkv_post",
    )(o, g, bonus, ln_g.reshape(1, W), ln_b.reshape(1, W))


def _s5_dot(a, b):
    return jnp.dot(a.astype(bf16), b.astype(bf16), preferred_element_type=f32)


def _transpose_pieces(w):
    w = list(w)
    lane = lax.broadcasted_iota(jnp.int32, w[0].shape, 1)
    for d in (4, 2, 1):
        width = S5_GROUP * d
        low = (lane & width) == 0
        for i in range(8):
            if i & d == 0:
                a, b = w[i], w[i + d]
                w[i] = jnp.where(low, a, pltpu.roll(b, width, 1))
                w[i + d] = jnp.where(low, pltpu.roll(a, 128 - width, 1), b)
    return w


def _s5_split(z_refs, u_ref):
    ncb = u_ref.shape[1]
    for j, z_ref in enumerate(z_refs):
        for b in range(S5_CHUNK // 8):
            w = [z_ref[pl.ds(8 * b + i, ncb, stride=S5_CHUNK), :] for i in range(8)]
            o = _transpose_pieces(w)
            for a in range(8):
                u_ref[8 * j + a, :, 128 * b:128 * (b + 1)] = o[a]


def _toeplitz(krow):
    P, LP = krow.shape
    lane = lax.broadcasted_iota(jnp.int32, (P, LP), 1)
    blocks = [krow]
    for s in range(1, LP // P):
        blocks.append(jnp.where(lane >= P * s, pltpu.roll(krow, P * s, 1), 0.0))
    return jnp.concatenate(blocks, axis=0)


def _s5_in_kernel(u_ref, br_ref, bi_ref, xr_ref, xi_ref):
    u2 = jnp.concatenate([u_ref[0], u_ref[1]], axis=1)
    xr_ref[...] = _s5_dot(u2, br_ref[0])
    xi_ref[...] = _s5_dot(u2, bi_ref[0])


def _s5_scan_kernel(xr_ref, xi_ref, ar_ref, ai_ref, sr_ref, si_ref):
    nc = xr_ref.shape[0]
    ar = ar_ref[...]
    ai = ai_ref[...]

    def body(c, carry):
        sr, si = carry
        sr_ref[pl.ds(c, 1), :] = sr
        si_ref[pl.ds(c, 1), :] = si
        xr = xr_ref[pl.ds(c, 1), :]
        xi = xi_ref[pl.ds(c, 1), :]
        return ar * sr - ai * si + xr, ar * si + ai * sr + xi

    z = jnp.zeros_like(ar)
    lax.fori_loop(0, nc, body, (z, z))


def _s5_out_kernel(u_ref, krow_ref, sr_ref, si_ref, cr_ref, ci_ref, d_ref, o_ref):
    sr = sr_ref[...]
    si = si_ref[...]
    for j in range(2):
        u = u_ref[j]
        y = _s5_dot(u, _toeplitz(krow_ref[j])) + _s5_dot(sr, cr_ref[j]) + _s5_dot(si, ci_ref[j]) + d_ref[j] * u
        o_ref[j] = 0.5 * y * (1.0 + jnp.tanh(0.7978845608028654 * (y + 0.044715 * (y * y * y))))


def _glu_kernel(y_ref, w_ref, b_ref, o_ref, *nat_refs):
    ncb = y_ref.shape[1]
    for j, nat_ref in enumerate(nat_refs):
        for b in range(S5_CHUNK // 8):
            o = [y_ref[8 * j + a, :, 128 * b:128 * (b + 1)] for a in range(8)]
            w = _transpose_pieces(o)
            for i in range(8):
                nat_ref[pl.ds(8 * b + i, ncb, stride=S5_CHUNK), :] = w[i]
    y = jnp.concatenate([r[...] for r in nat_refs], axis=1)
    o_ref[...] = (y * _sigmoid(_bmm(y, w_ref[...]) + b_ref[...])).astype(o_ref.dtype)


def _s5_tables(a_re, a_im, log_dt, b_re, b_im, c_re, c_im, d):
    G, N, P, L = S5_GROUPS, S5_STATE, S5_GROUP, S5_CHUNK
    D = a_re.shape[0]
    dt = jnp.exp(log_dt)[:, None, :, None]
    lags = jnp.arange(L + 1, dtype=f32)[None, :, None, None]
    mag = jnp.exp(a_re[:, None] * dt * lags)
    ang = a_im[:, None] * dt * lags
    pw_re, pw_im = mag * jnp.cos(ang), mag * jnp.sin(ang)
    den = a_re * a_re + a_im * a_im
    nr, ni = pw_re[:, 1] - 1.0, pw_im[:, 1]
    coef_re = ((nr * a_re + ni * a_im) / den)[..., None]
    coef_im = ((ni * a_re - nr * a_im) / den)[..., None]
    bb_re = coef_re * b_re - coef_im * b_im
    bb_im = coef_re * b_im + coef_im * b_re
    rep = lambda a, axis: jnp.repeat(a, P, axis=axis)
    pl_re, pl_im = rep(pw_re[:, :L].transpose(0, 2, 1, 3), 2), rep(pw_im[:, :L].transpose(0, 2, 1, 3), 2)
    ct_re, ct_im = jnp.tile(c_re, (1, 1, L, 1)), jnp.tile(c_im, (1, 1, L, 1))
    cb_re, cb_im = ct_re * pl_re - ct_im * pl_im, ct_re * pl_im + ct_im * pl_re
    krow = (jnp.einsum('dgxn,dgnq->dgqx', cb_re, bb_re, precision=HI)
            - jnp.einsum('dgxn,dgnq->dgqx', cb_im, bb_im, precision=HI))
    e_re = rep(pw_re[:, L - 1::-1][:, :L].transpose(0, 2, 1, 3), 2)
    e_im = rep(pw_im[:, L - 1::-1][:, :L].transpose(0, 2, 1, 3), 2)
    bq_re = jnp.tile(bb_re.transpose(0, 1, 3, 2), (1, 1, L, 1))
    bq_im = jnp.tile(bb_im.transpose(0, 1, 3, 2), (1, 1, L, 1))
    bs_re = (e_re * bq_re - e_im * bq_im).reshape(D, G // 2, 2, L * P, N)
    bs_im = (e_re * bq_im + e_im * bq_re).reshape(D, G // 2, 2, L * P, N)

    def pair_in(bs):
        zz = jnp.zeros_like(bs[:, :, 0])
        return jnp.concatenate([jnp.concatenate([bs[:, :, 0], zz], axis=3), jnp.concatenate([zz, bs[:, :, 1]], axis=3)], axis=2)

    f_re = rep(pw_re[:, 1:].transpose(0, 2, 3, 1), 3)
    f_im = rep(pw_im[:, 1:].transpose(0, 2, 3, 1), 3)
    cn_re = jnp.tile(c_re.transpose(0, 1, 3, 2), (1, 1, 1, L))
    cn_im = jnp.tile(c_im.transpose(0, 1, 3, 2), (1, 1, 1, L))
    cs_re = (cn_re * f_re - cn_im * f_im).reshape(D, G // 2, 2, N, L * P)
    cs_im = (-(cn_re * f_im + cn_im * f_re)).reshape(D, G // 2, 2, N, L * P)

    def pair_out(cs):
        zz = jnp.zeros_like(cs[:, :, 0])
        top = jnp.concatenate([cs[:, :, 0], zz], axis=2)[:, :, None]
        bot = jnp.concatenate([zz, cs[:, :, 1]], axis=2)[:, :, None]
        return jnp.concatenate([top, bot], axis=2).reshape(D, G, 2 * N, L * P)

    dd = jnp.tile(d.reshape(D, G, 1, P), (1, 1, 1, L))
    return dict(krow=krow, bsr=pair_in(bs_re), bsi=pair_in(bs_im), cr=pair_out(cs_re), ci=pair_out(cs_im),
                al_re=pw_re[:, L].reshape(D, 1, G * N), al_im=pw_im[:, L].reshape(D, 1, G * N), d=dd)


def _s5(u, tab, layer, glu_w, glu_b):
    G, N, P, L = S5_GROUPS, S5_STATE, S5_GROUP, S5_CHUNK
    nc, LP = u.shape[1:]
    T = nc * L
    tm = min(1024, T)
    grp = lambda s: pl.BlockSpec((2,) + s, lambda j: (j, 0, 0))
    tgrp = lambda s: pl.BlockSpec((None, 2) + s, lambda j: (layer, j, 0, 0))
    tone = lambda s: pl.BlockSpec((None, 1) + s, lambda j: (layer, j, 0, 0))
    lanes = pl.BlockSpec((nc, 2 * N), lambda j: (0, j))
    xr, xi = pl.pallas_call(
        _s5_in_kernel,
        grid=(G // 2,),
        in_specs=[grp((nc, LP)), tone((2 * LP, 2 * N)), tone((2 * LP, 2 * N))],
        out_specs=[lanes, lanes],
        out_shape=[jax.ShapeDtypeStruct((nc, G * N), f32), jax.ShapeDtypeStruct((nc, G * N), f32)],
        compiler_params=_cp(("parallel",)),
        name="s5_in",
    )(u, tab["bsr"], tab["bsi"])
    whole = lambda s: pl.BlockSpec(s, lambda i: (0,) * len(s))
    sr, si = pl.pallas_call(
        _s5_scan_kernel,
        grid=(1,),
        in_specs=[whole((nc, G * N)), whole((nc, G * N)), pl.BlockSpec((None, 1, G * N), lambda i: (layer, 0, 0)),
                  pl.BlockSpec((None, 1, G * N), lambda i: (layer, 0, 0))],
        out_specs=[whole((nc, G * N)), whole((nc, G * N))],
        out_shape=[jax.ShapeDtypeStruct((nc, G * N), f32)] * 2,
        compiler_params=_cp(("arbitrary",)),
        name="s5_scan",
    )(xr, xi, tab["al_re"], tab["al_im"])
    y = pl.pallas_call(
        _s5_out_kernel,
        grid=(G // 2,),
        in_specs=[grp((nc, LP)), tgrp((P, LP)), lanes, lanes, tgrp((2 * N, LP)), tgrp((2 * N, LP)), tgrp((1, LP))],
        out_specs=grp((nc, LP)),
        out_shape=jax.ShapeDtypeStruct((G, nc, LP), f32),
        compiler_params=_cp(("parallel",)),
        name="s5_out",
    )(u, tab["krow"], sr, si, tab["cr"], tab["ci"], tab["d"])
    return pl.pallas_call(
        _glu_kernel,
        grid=(T // tm,),
        in_specs=[pl.BlockSpec((G, tm // L, LP), lambda i: (0, i, 0)), pl.BlockSpec((S5_W, S5_W), lambda i: (0, 0)),
                  pl.BlockSpec((1, S5_W), lambda i: (0, 0))],
        out_specs=pl.BlockSpec((tm, S5_W), lambda i: (i, 0)),
        out_shape=jax.ShapeDtypeStruct((T, S5_W), bf16),
        scratch_shapes=[pltpu.VMEM((tm, 128), f32)] * (S5_W // 128),
        compiler_params=_cp(("parallel",)),
        name="s5_glu",
    )(y, glu_w, glu_b.reshape(1, S5_W))


def _pad_rows(w, n):
    return jnp.pad(w, ((0, n - w.shape[0]), (0, 0)))


def _split_mu(mu):
    l0 = 3 * RWKV_W
    return jnp.concatenate([
        mu[:l0], jnp.pad(mu[l0:l0 + DECAY_LORA], (0, LORA_PAD - DECAY_LORA)),
        jnp.pad(mu[l0 + DECAY_LORA:l0 + DECAY_LORA + AAA_LORA], (0, LORA_PAD - AAA_LORA)),
        mu[l0 + DECAY_LORA + AAA_LORA:]]).reshape(1, RWKV_PAD)


def kernel(x, p, norm_mix_g, w_in, mlstm_conv, mlstm_ib, mlstm_fb, mlstm_norm_g, rwkv_mu, rwkv_w0, rwkv_w2, rwkv_a0, rwkv_a2, rwkv_g2, rwkv_kk, rwkv_ka, rwkv_rk, rwkv_ln_g, rwkv_ln_b, s5_a_re, s5_a_im, s5_log_dt, s5_b_re, s5_b_im, s5_c_re, s5_c_im, s5_d, s5_glu_w, s5_glu_b, w_up_m, w_up_r, w_up_s, w_out, norm_ffn_g, ffn_w_gate, ffn_w_up, ffn_w_down, norm_ple_g, ple_w_gate, ple_w_proj, final_norm_g):
    B, T, D = x.shape
    depth = w_in.shape[0]
    w_out_b, ple_wg_b, ple_wp_b, p_b = (a.astype(bf16) for a in (w_out, ple_w_gate, ple_w_proj, p))
    wt = jnp.swapaxes(w_in, 1, 2)
    s5_tab = _s5_tables(s5_a_re, s5_a_im, s5_log_dt, s5_b_re, s5_b_im, s5_c_re, s5_c_im, s5_d)
    outs = []
    for bi in range(B):
        h = x[bi]
        xn = _rmsnorm(h, norm_mix_g[0], bf16)
        for i in range(depth):
            z_main = _proj(xn, wt, i, 0, 4, MLSTM_W)
            z_gate = _proj(xn, wt, i, M_IN + RWKV_IN + S5_W, 3 * D_MODEL // 1024, 1024, bf16, gate=True)
            u_s5, z_if = _proj_tail(xn, wt, i)

            gate_b = jnp.pad(jnp.concatenate([mlstm_ib[i], mlstm_fb[i]]), (0, 128 - 2 * MLSTM_HEADS)).reshape(1, 128)
            y_m = _mlstm(z_main, z_if, mlstm_conv[i], gate_b, mlstm_norm_g[i])

            vec = lambda a: a.reshape(1, RWKV_W)
            parts = _rwkv_prep(xn, _rwkv_w(wt, i), _split_mu(rwkv_mu[i]), vec(rwkv_w0[i]), vec(rwkv_a0[i]), vec(rwkv_kk[i]),
                               vec(rwkv_ka[i]), vec(rwkv_rk[i]), _pad_rows(rwkv_w2[i], LORA_PAD).astype(bf16),
                               _pad_rows(rwkv_a2[i], LORA_PAD).astype(bf16), rwkv_g2[i].astype(bf16))
            y_r = _rwkv_scan(parts, rwkv_ln_g[i], rwkv_ln_b[i])

            y_s = _s5(u_s5, s5_tab, i, s5_glu_w[i].astype(bf16), s5_glu_b[i])

            mixed = _merge(y_m, y_r, y_s, z_gate, w_up_m, w_up_r, w_up_s, i)
            h, hn = _resid_norm(mixed, w_out_b, i, h, norm_ffn_g[i])
            act = _ffn_up(hn, ffn_w_gate, ffn_w_up, i)
            h = _resid_mm(act, ffn_w_down, h, i)
            last = i == depth - 1
            g_next = final_norm_g if last else norm_mix_g[i + 1]
            h, xn = _ple(h, p_b[:, bi:bi + 1], ple_wg_b, ple_wp_b, i, norm_ple_g[i], g_next, f32 if last else bf16)
        outs.append(xn)
    return jnp.stack(outs)
```
